```python
import jax, jax.numpy as jnp
from jax import lax
import numpy as np

D_MODEL = 1024
BATCH = 16
SEQ = 2048
DEPTH = 1

D_MIX = 2 * D_MODEL
A_HEADS = 16
A_HEAD_DIM = 64
A_WIDTH = A_HEADS * A_HEAD_DIM
A_ROT_DIM = A_HEAD_DIM // 4
DILATED_PATTERNS = ((128, 1), (512, 4), (2048, 16))
MLA_HEADS = 8
MLA_Q_RANK = 256
MLA_KV_RANK = 128
MLA_NOPE_DIM = 64
MLA_ROPE_DIM = 32
MLA_V_DIM = 64
MLA_WIDTH = MLA_HEADS * MLA_V_DIM
N_MEM = 256
MEM_HEADS = 4
MEM_HEAD_DIM = 128
MEM_WIDTH = MEM_HEADS * MEM_HEAD_DIM

ROPE_THETA = 500000.0
Q_BLOCK = 128
NORM_EPS = 1e-5
NEG_INF = -1e30
DEEPNORM_ALPHA = (2 * DEPTH) ** 0.25
DEEPNORM_BETA = (8 * DEPTH) ** -0.25

IN_SPLITS = (A_WIDTH, A_WIDTH, A_WIDTH, A_WIDTH,
             MLA_Q_RANK, MLA_KV_RANK, MLA_ROPE_DIM, MLA_WIDTH,
             MEM_WIDTH, MEM_WIDTH)
D_IN = sum(IN_SPLITS)

kernel_name = "hymba_dilated_mla_memory_deepnorm"


def _layer_norm(x, g, b):
    xf = x.astype(jnp.float32)
    mu = jnp.mean(xf, axis=-1, keepdims=True)
    var = jnp.mean(jnp.square(xf - mu), axis=-1, keepdims=True)
    return ((xf - mu) * lax.rsqrt(var + NORM_EPS) * g.astype(jnp.float32) + b.astype(jnp.float32)).astype(x.dtype)


def _rms_norm(x, g, out_dtype):
    xf = x.astype(jnp.float32)
    ms = jnp.mean(jnp.square(xf), axis=-1, keepdims=True)
    return (xf * lax.rsqrt(ms + NORM_EPS) * g.astype(jnp.float32)).astype(out_dtype)


def _rope(x, pos):
    r = x.shape[-1]
    inv_freq = ROPE_THETA ** (-(jnp.arange(0, r, 2, dtype=jnp.float32) / r))
    ang = pos.astype(jnp.float32)[..., None] * inv_freq
    cos, sin = jnp.cos(ang)[:, :, None, :], jnp.sin(ang)[:, :, None, :]
    xf = x.astype(jnp.float32)
    x1, x2 = xf[..., : r // 2], xf[..., r // 2:]
    return jnp.concatenate([x1 * cos - x2 * sin, x2 * cos + x1 * sin], axis=-1).astype(x.dtype)


def _partial_rope(x, pos):
    return jnp.concatenate([_rope(x[..., :A_ROT_DIM], pos), x[..., A_ROT_DIM:]], axis=-1)


def _window_attn(q, k, v, n_side):
    n, length, h, e = q.shape
    blk = n_side
    nb = -(-length // blk)
    pad = nb * blk - length
    qb = jnp.pad(q, ((0, 0), (0, pad), (0, 0), (0, 0))).reshape(n, nb, blk, h, e).astype(jnp.float32)

    def bands(t):
        tb = jnp.pad(t, ((0, 0), (blk, pad + blk), (0, 0), (0, 0))).reshape(n, nb + 2, blk, h, t.shape[-1])
        return jnp.concatenate([tb[:, :-2], tb[:, 1:-1], tb[:, 2:]], axis=2).astype(jnp.float32)

    kb, vb = bands(k), bands(v)
    qpos = jnp.arange(nb)[:, None] * blk + jnp.arange(blk)[None, :]
    kpos = (jnp.arange(nb)[:, None] - 1) * blk + jnp.arange(3 * blk)[None, :]
    off = kpos[:, None, :] - qpos[:, :, None]
    valid = (jnp.abs(off) <= n_side) & (kpos[:, None, :] >= 0) & (kpos[:, None, :] < length)
    s = jnp.einsum('nbqhe,nbkhe->nbhqk', qb, kb) * (e ** -0.5)
    s = jnp.where(valid[None, :, None], s, NEG_INF)
    m = jnp.max(s, axis=-1, keepdims=True)
    p = jnp.exp(s - m)
    den = jnp.sum(p, axis=-1, keepdims=True)
    o = jnp.einsum('nbhqk,nbkhe->nbqhe', p / den, vb).reshape(n, nb * blk, h, vb.shape[-1])[:, :length]
    lse = (m + jnp.log(den))[..., 0]
    lse = lse.transpose(0, 1, 3, 2).reshape(n, nb * blk, h)[:, :length]
    return o, lse


def _dilated_attention(q, k, v):
    b, s, h, e = q.shape
    outs, lses = [], []
    for window, dil in DILATED_PATTERNS:
        n_side = window // (2 * dil)
        length = s // dil

        def to_sub(t):
            return t.reshape(b, length, dil, h, t.shape[-1]).transpose(0, 2, 1, 3, 4).reshape(b * dil, length, h, t.shape[-1])

        o, lse = _window_attn(to_sub(q), to_sub(k), to_sub(v), n_side)
        outs.append(o.reshape(b, dil, length, h, e).transpose(0, 2, 1, 3, 4).reshape(b, s, h, e))
        lses.append(lse.reshape(b, dil, length, h).transpose(0, 2, 1, 3).reshape(b, s, h))
    w = jax.nn.softmax(jnp.stack(lses, axis=0), axis=0)
    return jnp.einsum('gbsh,gbshe->bshe', w, jnp.stack(outs, axis=0))


def _mla_attention(q_nope, q_rope, k_nope, k_rope, v):
    b, s, h, _ = q_nope.shape
    scale = (MLA_NOPE_DIM + MLA_ROPE_DIM) ** -0.5
    nq = s // Q_BLOCK
    kn, kr, vf = k_nope.astype(jnp.float32), k_rope.astype(jnp.float32), v.astype(jnp.float32)

    def blocks(t):
        return t.reshape((b, nq, Q_BLOCK) + t.shape[2:]).swapaxes(0, 1)

    def one_block(args):
        qn, qr = args
        sc = (jnp.einsum('bqhe,bkhe->bhqk', qn.astype(jnp.float32), kn)
              + jnp.einsum('bqhr,bkr->bhqk', qr.astype(jnp.float32), kr)) * scale
        p = jax.nn.softmax(sc, axis=-1)
        return jnp.einsum('bhqk,bkhe->bqhe', p, vf)

    o = lax.map(one_block, (blocks(q_nope), blocks(q_rope)))
    return o.swapaxes(0, 1).reshape(b, s, h, v.shape[-1])


def _memory_attention(q, k, v):
    sc = jnp.einsum('bshe,bmhe->bhsm', q.astype(jnp.float32), k.astype(jnp.float32)) * (q.shape[-1] ** -0.5)
    p = jax.nn.softmax(sc, axis=-1)
    return jnp.einsum('bhsm,bmhe->bshe', p, v.astype(jnp.float32))


def _hybrid_layer(h, pos, mem, w_in, g_cq, g_ckv, w_uq, w_ukv, w_mem_kv,
                  g_out_a, g_out_b, g_out_m, w_out, g_post, b_post):
    b, s, _ = h.shape
    dt = h.dtype
    idx = [int(i) for i in np.cumsum(IN_SPLITS)[:-1]]
    proj = h @ w_in
    a_q, a_k, a_v, a_g, c_q, c_kv, b_kr, b_g, m_q, m_g = jnp.split(proj, idx, axis=-1)

    hd = (b, s, A_HEADS, A_HEAD_DIM)
    y_a = _dilated_attention(_partial_rope(a_q.reshape(hd), pos),
                             _partial_rope(a_k.reshape(hd), pos),
                             a_v.reshape(hd)).reshape(b, s, A_WIDTH)

    q = (_rms_norm(c_q, g_cq, dt) @ w_uq).reshape(b, s, MLA_HEADS, MLA_NOPE_DIM + MLA_ROPE_DIM)
    q_nope, q_rope = q[..., :MLA_NOPE_DIM], _rope(q[..., MLA_NOPE_DIM:], pos)
    kv = (_rms_norm(c_kv, g_ckv, dt) @ w_ukv).reshape(b, s, MLA_HEADS, MLA_NOPE_DIM + MLA_V_DIM)
    k_nope, v = kv[..., :MLA_NOPE_DIM], kv[..., MLA_NOPE_DIM:]
    k_rope = _rope(b_kr[:, :, None, :], pos)[:, :, 0]
    y_b = _mla_attention(q_nope, q_rope, k_nope, k_rope, v).reshape(b, s, MLA_WIDTH)

    mkv = mem @ w_mem_kv
    mk = mkv[..., :MEM_WIDTH].reshape(b, -1, MEM_HEADS, MEM_HEAD_DIM)
    mv = mkv[..., MEM_WIDTH:].reshape(b, -1, MEM_HEADS, MEM_HEAD_DIM)
    y_m = _memory_attention(m_q.reshape(b, s, MEM_HEADS, MEM_HEAD_DIM), mk, mv).reshape(b, s, MEM_WIDTH)

    y = jnp.concatenate([_rms_norm(y_a, g_out_a, dt) * jax.nn.silu(a_g),
                         _rms_norm(y_b, g_out_b, dt) * jax.nn.silu(b_g),
                         _rms_norm(y_m, g_out_m, dt) * jax.nn.silu(m_g)], axis=-1)
    sub = y @ w_out
    return _layer_norm(DEEPNORM_ALPHA * h + sub, g_post, b_post)


def setup_inputs(seed: int = 0) -> dict:
    key = jax.random.key(seed)
    ks = jax.random.split(key, 20)
    f32 = jnp.float32

    def nrm(k, shape, fan_in, scale=1.0):
        return jax.random.normal(k, shape, f32) * (fan_in ** -0.5) * scale

    def gain(k, shape):
        return 1.0 + 0.02 * jax.random.normal(k, shape, f32)

    x = jax.random.normal(ks[0], (BATCH, SEQ, D_MODEL), f32)
    mem = jax.random.normal(ks[1], (BATCH, N_MEM, D_MODEL), f32)
    offsets = jax.random.randint(ks[2], (BATCH, 1), 0, 4096, dtype=jnp.int32)
    positions = offsets + jnp.arange(SEQ, dtype=jnp.int32)[None, :]
    return {
        "x": x,
        "mem": mem,
        "positions": positions,
        "g_emb": gain(ks[3], (D_MODEL,)),
        "b_emb": 0.02 * jax.random.normal(ks[4], (D_MODEL,), f32),
        "w_in": nrm(ks[5], (DEPTH, D_MODEL, D_IN), D_MODEL),
        "g_cq": gain(ks[6], (DEPTH, MLA_Q_RANK)),
        "g_ckv": gain(ks[7], (DEPTH, MLA_KV_RANK)),
        "w_uq": nrm(ks[8], (DEPTH, MLA_Q_RANK, MLA_HEADS * (MLA_NOPE_DIM + MLA_ROPE_DIM)), MLA_Q_RANK),
        "w_ukv": nrm(ks[9], (DEPTH, MLA_KV_RANK, MLA_HEADS * (MLA_NOPE_DIM + MLA_V_DIM)), MLA_KV_RANK),
        "w_mem_kv": nrm(ks[10], (DEPTH, D_MODEL, 2 * MEM_WIDTH), D_MODEL),
        "g_out_a": gain(ks[11], (DEPTH, A_WIDTH)),
        "g_out_b": gain(ks[12], (DEPTH, MLA_WIDTH)),
        "g_out_m": gain(ks[13], (DEPTH, MEM_WIDTH)),
        "w_out": nrm(ks[14], (DEPTH, D_MIX, D_MODEL), D_MIX, DEEPNORM_BETA),
        "g_post": gain(ks[15], (DEPTH, D_MODEL)),
        "b_post": 0.02 * jax.random.normal(ks[16], (DEPTH, D_MODEL), f32),
    }


def reference(x, mem, positions, g_emb, b_emb, w_in, g_cq, g_ckv, w_uq, w_ukv, w_mem_kv,
              g_out_a, g_out_b, g_out_m, w_out, g_post, b_post):
    h = _layer_norm(x, g_emb, b_emb)
    for l in range(DEPTH):
        h = _hybrid_layer(h, positions, mem, w_in[l], g_cq[l], g_ckv[l], w_uq[l], w_ukv[l], w_mem_kv[l],
                          g_out_a[l], g_out_b[l], g_out_m[l], w_out[l], g_post[l], b_post[l])
    return h
```

```python
import functools

import jax
import jax.numpy as jnp
import numpy as np
from jax import lax
from jax.experimental import pallas as pl
from jax.experimental.pallas import tpu as pltpu

D_MODEL = 1024
SEQ = 2048
A_HEADS = 16
A_HEAD_DIM = 64
A_WIDTH = A_HEADS * A_HEAD_DIM
A_ROT_DIM = A_HEAD_DIM // 4
A_SIDE = 64
DILATIONS = (1, 4, 16)
MLA_HEADS = 8
MLA_Q_RANK = 256
MLA_KV_RANK = 128
MLA_NOPE_DIM = 64
MLA_ROPE_DIM = 32
MLA_V_DIM = 64
MLA_WIDTH = MLA_HEADS * MLA_V_DIM
N_MEM = 256
MEM_HEADS = 4
MEM_HEAD_DIM = 128
MEM_WIDTH = MEM_HEADS * MEM_HEAD_DIM
D_MIX = A_WIDTH + MLA_WIDTH + MEM_WIDTH
ROPE_THETA = 500000.0
NORM_EPS = 1e-5
NEG_INF = -1e30
DEPTH = 1
DEEPNORM_ALPHA = (2 * DEPTH) ** 0.25
IN_SPLITS = (A_WIDTH, A_WIDTH, A_WIDTH, A_WIDTH, MLA_Q_RANK, MLA_KV_RANK, MLA_ROPE_DIM, MLA_WIDTH,
             MEM_WIDTH, MEM_WIDTH)

LANES = 128
VMEM_BYTES_V7X = 64 * 1024 * 1024

N_RES = DILATIONS[-1]
RES_LEN = SEQ // N_RES
RES_PER_TILE = 4
ROW_TILE = RES_PER_TILE * RES_LEN
N_ROW_TILES = N_RES // RES_PER_TILE
A_PAIRS = A_WIDTH // LANES
MLA_SLOT = LANES
MLA_Q_TILE = 256

F32 = jnp.float32
BF16 = jnp.bfloat16


def _vmem_limit(nbytes):
    return int(min(VMEM_BYTES_V7X - (4 << 20), max(32 << 20, nbytes + (8 << 20))))


def _layer_norm_rows(x, g, b):
    mu = jnp.mean(x, axis=-1, keepdims=True)
    xc = x - mu
    var = jnp.mean(xc * xc, axis=-1, keepdims=True)
    return xc * lax.rsqrt(var + NORM_EPS) * g + b


def _rms_rows(x, g):
    ms = jnp.mean(x * x, axis=-1, keepdims=True)
    return x * lax.rsqrt(ms + NORM_EPS) * g


def _normed_tile_to_scratch(x_ref, g_ref, b_ref, h_scr):
    for j in range(RES_PER_TILE):
        xj = x_ref[:, j * D_MODEL:(j + 1) * D_MODEL]
        h_scr[j * RES_LEN:(j + 1) * RES_LEN, :] = _layer_norm_rows(xj, g_ref[...], b_ref[...]).astype(BF16)


def _rope_tables(pos_ref, tab_ref, cos_scr, sin1_scr, sin2_scr):
    inv_freq = tab_ref[0:1, :]
    rot = tab_ref[1:2, :] > 0.5
    first = tab_ref[2:3, :] > 0.5
    second = tab_ref[3:4, :] > 0.5
    for j in range(RES_PER_TILE):
        pos_col = jnp.broadcast_to(pos_ref[j:j + 1, :], (RES_LEN, LANES)).T
        ang = pos_col * inv_freq
        c = jnp.cos(ang)
        s = jnp.sin(ang)
        rows = slice(j * RES_LEN, (j + 1) * RES_LEN)
        cos_scr[rows, :] = jnp.where(rot, c, 1.0)
        sin1_scr[rows, :] = jnp.where(first, -s, 0.0)
        sin2_scr[rows, :] = jnp.where(second, s, 0.0)


def _rope_block(x, rows, half, cos_scr, sin1_scr, sin2_scr):
    return (x * cos_scr[rows, :]
            + pltpu.roll(x, LANES - half, 1) * sin1_scr[rows, :]
            + pltpu.roll(x, half, 1) * sin2_scr[rows, :])


def _proj_a_kernel(x_ref, pos_ref, tab_ref, g_ref, b_ref, w_ref,
                   qn_ref, kn_ref, vn_ref, q4_ref, k4_ref, v4_ref, q16_ref, k16_ref, v16_ref,
                   h_scr, acc_scr, cos_scr, sin1_scr, sin2_scr):
    _normed_tile_to_scratch(x_ref, g_ref, b_ref, h_scr)
    _rope_tables(pos_ref, tab_ref, cos_scr, sin1_scr, sin2_scr)
    outs = ((qn_ref, q4_ref, q16_ref), (kn_ref, k4_ref, k16_ref), (vn_ref, v4_ref, v16_ref))
    for sec, (nat_ref, r4_ref, r16_ref) in enumerate(outs):
        acc_scr[...] = jnp.dot(h_scr[...], w_ref[:, sec * A_WIDTH:(sec + 1) * A_WIDTH],
                               preferred_element_type=F32)
        for j in range(RES_PER_TILE):
            rows = slice(j * RES_LEN, (j + 1) * RES_LEN)
            for cg in range(A_PAIRS):
                cols = slice(cg * LANES, (cg + 1) * LANES)
                blk = acc_scr[rows, cols]
                if sec < 2:
                    blk = _rope_block(blk, rows, A_ROT_DIM // 2, cos_scr, sin1_scr, sin2_scr)
                if sec == 0:
                    blk = blk * (A_HEAD_DIM ** -0.5)
                val = blk.astype(BF16)
                nat_ref[:, j * A_WIDTH + cg * LANES:j * A_WIDTH + (cg + 1) * LANES] = val
                r4_ref[j, :, cols] = val
                r16_ref[j, :, cols] = val


def _proj_a(x_view, pos_view, tab_a, g_emb, b_emb, w_qkv):
    batch = x_view.shape[0]
    nat = jax.ShapeDtypeStruct((batch, RES_LEN, N_RES * A_WIDTH), BF16)
    r4 = jax.ShapeDtypeStruct((batch, 4, RES_LEN, N_ROW_TILES * A_WIDTH), BF16)
    r16 = jax.ShapeDtypeStruct((batch, N_RES, RES_LEN, A_WIDTH), BF16)
    nat_spec = pl.BlockSpec((None, RES_LEN, RES_PER_TILE * A_WIDTH), lambda b, q: (b, 0, q))
    r4_spec = pl.BlockSpec((None, 4, RES_LEN, A_WIDTH), lambda b, q: (b, 0, 0, q))
    r16_spec = pl.BlockSpec((None, RES_PER_TILE, RES_LEN, A_WIDTH), lambda b, q: (b, q, 0, 0))
    tile_out = ROW_TILE * A_WIDTH * 2
    est = (2 * RES_LEN * RES_PER_TILE * D_MODEL * 4 + 2 * w_qkv.size * 2 + 2 * 9 * tile_out
           + ROW_TILE * D_MODEL * 2 + ROW_TILE * A_WIDTH * 4 + 3 * ROW_TILE * LANES * 4)
    return pl.pallas_call(
        _proj_a_kernel,
        grid=(batch, N_ROW_TILES),
        in_specs=[
            pl.BlockSpec((None, RES_LEN, RES_PER_TILE * D_MODEL), lambda b, q: (b, 0, q)),
            pl.BlockSpec((None, None, RES_PER_TILE, RES_LEN), lambda b, q: (b, q, 0, 0)),
            pl.BlockSpec((8, LANES), lambda b, q: (0, 0)),
            pl.BlockSpec((1, D_MODEL), lambda b, q: (0, 0)),
            pl.BlockSpec((1, D_MODEL), lambda b, q: (0, 0)),
            pl.BlockSpec(w_qkv.shape, lambda b, q: (0, 0)),
        ],
        out_specs=[nat_spec] * 3 + [r4_spec] * 3 + [r16_spec] * 3,
        out_shape=[nat] * 3 + [r4] * 3 + [r16] * 3,
        scratch_shapes=[
            pltpu.VMEM((ROW_TILE, D_MODEL), BF16),
            pltpu.VMEM((ROW_TILE, A_WIDTH), F32),
            pltpu.VMEM((ROW_TILE, LANES), F32),
            pltpu.VMEM((ROW_TILE, LANES), F32),
            pltpu.VMEM((ROW_TILE, LANES), F32),
        ],
        compiler_params=pltpu.CompilerParams(
            dimension_semantics=("arbitrary", "arbitrary"), vmem_limit_bytes=_vmem_limit(est)),
        name="proj_a",
    )(x_view, pos_view, tab_a, g_emb, b_emb, w_qkv)


_GATE_W = D_MIX
_OFF_MQ = _GATE_W
_OFF_CQ = _OFF_MQ + MEM_WIDTH
_OFF_CKV = _OFF_CQ + MLA_Q_RANK
_OFF_KR = _OFF_CKV + MLA_KV_RANK
_W_B_COLS = _OFF_KR + MLA_SLOT
_MLA_QK_W = MLA_HEADS * MLA_SLOT


def _proj_b_kernel(x_ref, pos_ref, tab_ref, g_ref, b_ref, w_ref, gcq_ref, gckv_ref, wuq_ref, wukv_ref,
                   gate_ref, mq_ref, qb_ref, kb_ref, vb_ref,
                   h_scr, acc_scr, cos_scr, sin1_scr, sin2_scr):
    _normed_tile_to_scratch(x_ref, g_ref, b_ref, h_scr)
    _rope_tables(pos_ref, tab_ref, cos_scr, sin1_scr, sin2_scr)
    h = h_scr[...]

    for n in range(_GATE_W // A_WIDTH):
        acc_scr[...] = jnp.dot(h, w_ref[:, n * A_WIDTH:(n + 1) * A_WIDTH], preferred_element_type=F32)
        for j in range(RES_PER_TILE):
            rows = slice(j * RES_LEN, (j + 1) * RES_LEN)
            g = acc_scr[rows, :]
            gate_ref[rows, n * A_WIDTH:(n + 1) * A_WIDTH] = (g / (1.0 + jnp.exp(-g))).astype(BF16)

    mq = jnp.dot(h, w_ref[:, _OFF_MQ:_OFF_MQ + MEM_WIDTH], preferred_element_type=F32)
    mq_ref[...] = (mq * (MEM_HEAD_DIM ** -0.5)).astype(BF16)

    cq = jnp.dot(h, w_ref[:, _OFF_CQ:_OFF_CQ + MLA_Q_RANK], preferred_element_type=F32)
    cqn = _rms_rows(cq, gcq_ref[...]).astype(BF16)
    acc_scr[...] = jnp.dot(cqn, wuq_ref[...], preferred_element_type=F32)
    mla_scale = (MLA_NOPE_DIM + MLA_ROPE_DIM) ** -0.5
    for j in range(RES_PER_TILE):
        rows = slice(j * RES_LEN, (j + 1) * RES_LEN)
        for hd in range(MLA_HEADS):
            cols = slice(hd * MLA_SLOT, (hd + 1) * MLA_SLOT)
            blk = _rope_block(acc_scr[rows, cols], rows, MLA_ROPE_DIM // 2, cos_scr, sin1_scr, sin2_scr)
            qb_ref[rows, cols] = (blk * mla_scale).astype(BF16)

    ckv = jnp.dot(h, w_ref[:, _OFF_CKV:_OFF_CKV + MLA_KV_RANK], preferred_element_type=F32)
    ckvn = _rms_rows(ckv, gckv_ref[...]).astype(BF16)
    acc_scr[...] = jnp.dot(ckvn, wukv_ref[:, :_MLA_QK_W], preferred_element_type=F32)
    vb = jnp.dot(ckvn, wukv_ref[:, _MLA_QK_W:], preferred_element_type=F32)
    vb_ref[...] = vb.astype(BF16)
    kr = jnp.dot(h, w_ref[:, _OFF_KR:_OFF_KR + MLA_SLOT], preferred_element_type=F32)
    for j in range(RES_PER_TILE):
        rows = slice(j * RES_LEN, (j + 1) * RES_LEN)
        krj = kr[j * RES_LEN:(j + 1) * RES_LEN, :]
        for hd in range(MLA_HEADS):
            cols = slice(hd * MLA_SLOT, (hd + 1) * MLA_SLOT)
            blk = _rope_block(acc_scr[rows, cols] + krj, rows, MLA_ROPE_DIM // 2,
                              cos_scr, sin1_scr, sin2_scr)
            kb_ref[rows, cols] = blk.astype(BF16)


def _proj_b(x_view, pos_view, tab_b, g_emb, b_emb, w_b, g_cq, g_ckv, w_uq_p, w_ukv_p):
    batch = x_view.shape[0]

    def rows_spec(width):
        return pl.BlockSpec((None, ROW_TILE, width), lambda b, q: (b, q, 0))

    def full_spec(arr):
        return pl.BlockSpec(arr.shape, lambda b, q: (0,) * arr.ndim)

    widths = (_GATE_W, MEM_WIDTH, _MLA_QK_W, _MLA_QK_W, MLA_WIDTH)
    est = (2 * RES_LEN * RES_PER_TILE * D_MODEL * 4 + 2 * (w_b.size + w_uq_p.size + w_ukv_p.size) * 2
           + 2 * ROW_TILE * sum(widths) * 2 + ROW_TILE * D_MODEL * 2 + ROW_TILE * A_WIDTH * 4
           + 3 * ROW_TILE * LANES * 4 + 2 * ROW_TILE * A_WIDTH * 4)
    return pl.pallas_call(
        _proj_b_kernel,
        grid=(batch, N_ROW_TILES),
        in_specs=[
            pl.BlockSpec((None, RES_LEN, RES_PER_TILE * D_MODEL), lambda b, q: (b, 0, q)),
            pl.BlockSpec((None, None, RES_PER_TILE, RES_LEN), lambda b, q: (b, q, 0, 0)),
            pl.BlockSpec((8, LANES), lambda b, q: (0, 0)),
            full_spec(g_emb), full_spec(b_emb), full_spec(w_b), full_spec(g_cq), full_spec(g_ckv),
            full_spec(w_uq_p), full_spec(w_ukv_p),
        ],
        out_specs=[rows_spec(w) for w in widths],
        out_shape=[jax.ShapeDtypeStruct((batch, SEQ, w), BF16) for w in widths],
        scratch_shapes=[
            pltpu.VMEM((ROW_TILE, D_MODEL), BF16),
            pltpu.VMEM((ROW_TILE, A_WIDTH), F32),
            pltpu.VMEM((ROW_TILE, LANES), F32),
            pltpu.VMEM((ROW_TILE, LANES), F32),
            pltpu.VMEM((ROW_TILE, LANES), F32),
        ],
        compiler_params=pltpu.CompilerParams(
            dimension_semantics=("arbitrary", "arbitrary"), vmem_limit_bytes=_vmem_limit(est)),
        name="proj_b",
    )(x_view, pos_view, tab_b, g_emb, b_emb, w_b, g_cq, g_ckv, w_uq_p, w_ukv_p)


_A_Q_TILE = 128


def _band_tile(q, k, v, q_start, k_start):
    n_q, n_k = q.shape[0], k.shape[0]
    lane = lax.broadcasted_iota(jnp.int32, (n_q, LANES), 1)
    first_head = lane < A_HEAD_DIM
    zero = jnp.zeros_like(q)
    q2 = jnp.concatenate([jnp.where(first_head, q, zero), jnp.where(first_head, zero, q)], axis=0)
    s = lax.dot_general(q2, k, (((1,), (1,)), ((), ())), preferred_element_type=F32)
    row = lax.broadcasted_iota(jnp.int32, (2 * n_q, n_k), 0)
    col = lax.broadcasted_iota(jnp.int32, (2 * n_q, n_k), 1)
    q_idx = jnp.where(row >= n_q, row - n_q, row) + q_start
    off = (col + k_start) - q_idx
    s = jnp.where(jnp.abs(off) <= A_SIDE, s, NEG_INF)
    m = jnp.max(s, axis=-1, keepdims=True)
    p = jnp.exp(s - m)
    l = jnp.sum(p, axis=-1, keepdims=True)
    o = jnp.dot(p.astype(BF16), v, preferred_element_type=F32)
    acc = jnp.where(first_head, o[:n_q], o[n_q:])
    m_b = jnp.where(first_head, m[:n_q], m[n_q:])
    l_b = jnp.where(first_head, l[:n_q], l[n_q:])
    return acc, m_b, l_b


def _dilated_kernel(qn_ref, kn_ref, vn_ref, q4_ref, k4_ref, v4_ref, q16_ref, k16_ref, v16_ref,
                    o_ref, acc1, mx1, sm1, acc4, mx4, sm4, acc16, mx16, sm16):
    def body1(i, carry):
        qs = pl.multiple_of(i * _A_Q_TILE, _A_Q_TILE)
        ks = pl.multiple_of(jnp.clip(qs - A_SIDE, 0, SEQ - 2 * _A_Q_TILE), A_SIDE)
        acc, m_b, l_b = _band_tile(qn_ref[pl.ds(qs, _A_Q_TILE), :], kn_ref[pl.ds(ks, 2 * _A_Q_TILE), :],
                                   vn_ref[pl.ds(ks, 2 * _A_Q_TILE), :], qs, ks)
        acc1[pl.ds(qs, _A_Q_TILE), :] = acc
        mx1[pl.ds(qs, _A_Q_TILE), :] = m_b
        sm1[pl.ds(qs, _A_Q_TILE), :] = l_b
        return carry

    lax.fori_loop(0, SEQ // _A_Q_TILE, body1, 0)

    len4 = SEQ // 4
    tiles4 = len4 // _A_Q_TILE

    def body4(i, carry):
        res = i // tiles4
        qs = pl.multiple_of((i % tiles4) * _A_Q_TILE, _A_Q_TILE)
        ks = pl.multiple_of(jnp.clip(qs - A_SIDE, 0, len4 - 2 * _A_Q_TILE), A_SIDE)
        acc, m_b, l_b = _band_tile(q4_ref[res, pl.ds(qs, _A_Q_TILE), :],
                                   k4_ref[res, pl.ds(ks, 2 * _A_Q_TILE), :],
                                   v4_ref[res, pl.ds(ks, 2 * _A_Q_TILE), :], qs, ks)
        acc4[res, pl.ds(qs, _A_Q_TILE), :] = acc
        mx4[res, pl.ds(qs, _A_Q_TILE), :] = m_b
        sm4[res, pl.ds(qs, _A_Q_TILE), :] = l_b
        return carry

    lax.fori_loop(0, 4 * tiles4, body4, 0)

    def body16(r, carry):
        acc, m_b, l_b = _band_tile(q16_ref[r], k16_ref[r], v16_ref[r], 0, 0)
        acc16[r] = acc
        mx16[r] = m_b
        sm16[r] = l_b
        return carry

    lax.fori_loop(0, N_RES, body16, 0)

    for r in range(N_RES):
        nat_rows = pl.ds(r, RES_LEN, stride=N_RES)
        r4_rows = pl.ds(r // 4, RES_LEN, stride=4)
        parts = (
            (acc1[nat_rows, :], mx1[nat_rows, :], sm1[nat_rows, :]),
            (acc4[r % 4, r4_rows, :], mx4[r % 4, r4_rows, :], sm4[r % 4, r4_rows, :]),
            (acc16[r], mx16[r], sm16[r]),
        )
        m_all = jnp.maximum(jnp.maximum(parts[0][1], parts[1][1]), parts[2][1])
        num = jnp.zeros((RES_LEN, LANES), F32)
        den = jnp.zeros((RES_LEN, LANES), F32)
        for acc, m_b, l_b in parts:
            w = jnp.exp(m_b - m_all)
            num = num + w * acc
            den = den + w * l_b
        o_ref[r * RES_LEN:(r + 1) * RES_LEN, :] = (num / den).astype(BF16)


def _dilated_attention(qkv_nat, qkv_4, qkv_16):
    batch = qkv_nat[0].shape[0]
    nat_spec = pl.BlockSpec((None, SEQ, LANES), lambda b, p: (b, 0, p))
    r4_spec = pl.BlockSpec((None, 4, SEQ // 4, LANES), lambda b, p: (b, 0, 0, p))
    r16_spec = pl.BlockSpec((None, N_RES, RES_LEN, LANES), lambda b, p: (b, 0, 0, p))
    blk = SEQ * LANES
    est = 2 * 9 * blk * 2 + 2 * blk * 2 + 9 * blk * 4
    stat = lambda shape: pltpu.VMEM(shape, F32)
    return pl.pallas_call(
        _dilated_kernel,
        grid=(batch, A_PAIRS),
        in_specs=[nat_spec] * 3 + [r4_spec] * 3 + [r16_spec] * 3,
        out_specs=pl.BlockSpec((None, SEQ, LANES), lambda b, p: (b, 0, p)),
        out_shape=jax.ShapeDtypeStruct((batch, SEQ, A_WIDTH), BF16),
        scratch_shapes=([stat((SEQ, LANES))] * 3 + [stat((4, SEQ // 4, LANES))] * 3
                        + [stat((N_RES, RES_LEN, LANES))] * 3),
        compiler_params=pltpu.CompilerParams(
            dimension_semantics=("arbitrary", "arbitrary"), vmem_limit_bytes=_vmem_limit(est)),
        name="dilated_attn",
    )(*qkv_nat, *qkv_4, *qkv_16)


def _latent_kernel(q_ref, k_ref, v_ref, o_ref):
    v = v_ref[...]
    outs = []
    for hd in range(2):
        cols = slice(hd * MLA_SLOT, (hd + 1) * MLA_SLOT)
        s = lax.dot_general(q_ref[:, cols], k_ref[:, cols], (((1,), (1,)), ((), ())),
                            preferred_element_type=F32)
        m = jnp.max(s, axis=-1, keepdims=True)
        p = jnp.exp(s - m)
        l = jnp.sum(p, axis=-1, keepdims=True)
        o = jnp.dot(p.astype(BF16), v, preferred_element_type=F32)
        outs.append(o / l)
    lane = lax.broadcasted_iota(jnp.int32, outs[0].shape, 1)
    o_ref[...] = jnp.where(lane < MLA_V_DIM, outs[0], outs[1]).astype(BF16)


def _latent_attention(qb, kb, vb):
    batch = qb.shape[0]
    pairs = MLA_HEADS // 2
    est = (2 * MLA_Q_TILE * 2 * MLA_SLOT * 2 + 2 * SEQ * 2 * MLA_SLOT * 2 + 2 * SEQ * LANES * 2
           + 2 * MLA_Q_TILE * LANES * 2 + 4 * MLA_Q_TILE * SEQ * 4)
    return pl.pallas_call(
        _latent_kernel,
        grid=(batch, pairs, SEQ // MLA_Q_TILE),
        in_specs=[
            pl.BlockSpec((None, MLA_Q_TILE, 2 * MLA_SLOT), lambda b, p, i: (b, i, p)),
            pl.BlockSpec((None, SEQ, 2 * MLA_SLOT), lambda b, p, i: (b, 0, p)),
            pl.BlockSpec((None, SEQ, LANES), lambda b, p, i: (b, 0, p)),
        ],
        out_specs=pl.BlockSpec((None, MLA_Q_TILE, LANES), lambda b, p, i: (b, i, p)),
        out_shape=jax.ShapeDtypeStruct((batch, SEQ, MLA_WIDTH), BF16),
        compiler_params=pltpu.CompilerParams(
            dimension_semantics=("arbitrary", "arbitrary", "arbitrary"),
            vmem_limit_bytes=_vmem_limit(est)),
        name="latent_attn",
    )(qb, kb, vb)


def _mem_kv_kernel(mem_ref, w_ref, o_ref):
    o_ref[...] = jnp.dot(mem_ref[...].astype(BF16), w_ref[...], preferred_element_type=F32).astype(BF16)


def _mem_kv(mem, w_mem):
    batch = mem.shape[0]
    est = 2 * N_MEM * D_MODEL * 4 + 2 * w_mem.size * 2 + 2 * N_MEM * 2 * MEM_WIDTH * 2 + N_MEM * D_MODEL * 8
    return pl.pallas_call(
        _mem_kv_kernel,
        grid=(batch,),
        in_specs=[pl.BlockSpec((None, N_MEM, D_MODEL), lambda b: (b, 0, 0)),
                  pl.BlockSpec(w_mem.shape, lambda b: (0, 0))],
        out_specs=pl.BlockSpec((None, N_MEM, 2 * MEM_WIDTH), lambda b: (b, 0, 0)),
        out_shape=jax.ShapeDtypeStruct((batch, N_MEM, 2 * MEM_WIDTH), BF16),
        compiler_params=pltpu.CompilerParams(
            dimension_semantics=("arbitrary",), vmem_limit_bytes=_vmem_limit(est)),
        name="mem_kv",
    )(mem, w_mem)


def _output_kernel(x_ref, ya_ref, yb_ref, gate_ref, mq_ref, mkv_ref, wout_ref,
                   goa_ref, gob_ref, gom_ref, gemb_ref, bemb_ref, gpost_ref, bpost_ref,
                   o_ref, y_scr):
    for hd in range(MEM_HEADS):
        cols = slice(hd * MEM_HEAD_DIM, (hd + 1) * MEM_HEAD_DIM)
        s = lax.dot_general(mq_ref[:, cols], mkv_ref[:, cols], (((1,), (1,)), ((), ())),
                            preferred_element_type=F32)
        m = jnp.max(s, axis=-1, keepdims=True)
        p = jnp.exp(s - m)
        l = jnp.sum(p, axis=-1, keepdims=True)
        o = jnp.dot(p.astype(BF16), mkv_ref[:, MEM_WIDTH + hd * MEM_HEAD_DIM:MEM_WIDTH + (hd + 1) * MEM_HEAD_DIM],
                    preferred_element_type=F32)
        y_scr[:, cols] = o / l

    off_b = A_WIDTH
    off_m = A_WIDTH + MLA_WIDTH
    ya = _rms_rows(ya_ref[...].astype(F32), goa_ref[...]) * gate_ref[:, :off_b].astype(F32)
    yb = _rms_rows(yb_ref[...].astype(F32), gob_ref[...]) * gate_ref[:, off_b:off_m].astype(F32)
    ym = _rms_rows(y_scr[...], gom_ref[...]) * gate_ref[:, off_m:].astype(F32)
    sub = (jnp.dot(ya.astype(BF16), wout_ref[:off_b, :], preferred_element_type=F32)
           + jnp.dot(yb.astype(BF16), wout_ref[off_b:off_m, :], preferred_element_type=F32)
           + jnp.dot(ym.astype(BF16), wout_ref[off_m:, :], preferred_element_type=F32))

    for j in range(RES_PER_TILE):
        cols = slice(j * D_MODEL, (j + 1) * D_MODEL)
        h = _layer_norm_rows(x_ref[:, cols], gemb_ref[...], bemb_ref[...])
        z = DEEPNORM_ALPHA * h + sub[j * RES_LEN:(j + 1) * RES_LEN, :]
        o_ref[:, cols] = _layer_norm_rows(z, gpost_ref[...], bpost_ref[...])


def _output_stage(x_view, ya, yb, gates, mq, mkv, w_out, g_out_a, g_out_b, g_out_m, g_emb, b_emb,
                  g_post, b_post):
    batch = x_view.shape[0]

    def rows_spec(width):
        return pl.BlockSpec((None, ROW_TILE, width), lambda b, q: (b, q, 0))

    def full_spec(arr):
        return pl.BlockSpec(arr.shape, lambda b, q: (0,) * arr.ndim)

    x_spec = pl.BlockSpec((None, RES_LEN, RES_PER_TILE * D_MODEL), lambda b, q: (b, 0, q))
    est = (4 * ROW_TILE * D_MODEL * 4 + 2 * ROW_TILE * (A_WIDTH + MLA_WIDTH + D_MIX + MEM_WIDTH) * 2
           + 2 * N_MEM * 2 * MEM_WIDTH * 2 + 2 * w_out.size * 2 + ROW_TILE * MEM_WIDTH * 4
           + 4 * ROW_TILE * D_MODEL * 4)
    return pl.pallas_call(
        _output_kernel,
        grid=(batch, N_ROW_TILES),
        in_specs=[
            x_spec, rows_spec(A_WIDTH), rows_spec(MLA_WIDTH), rows_spec(D_MIX), rows_spec(MEM_WIDTH),
            pl.BlockSpec((None, N_MEM, 2 * MEM_WIDTH), lambda b, q: (b, 0, 0)),
            full_spec(w_out), full_spec(g_out_a), full_spec(g_out_b), full_spec(g_out_m),
            full_spec(g_emb), full_spec(b_emb), full_spec(g_post), full_spec(b_post),
        ],
        out_specs=x_spec,
        out_shape=jax.ShapeDtypeStruct(x_view.shape, F32),
        scratch_shapes=[pltpu.VMEM((ROW_TILE, MEM_WIDTH), F32)],
        compiler_params=pltpu.CompilerParams(
            dimension_semantics=("arbitrary", "arbitrary"), vmem_limit_bytes=_vmem_limit(est)),
        name="output_stage",
    )(x_view, ya, yb, gates, mq, mkv, w_out, g_out_a, g_out_b, g_out_m, g_emb, b_emb, g_post, b_post)


def _rope_lane_table(rot_dim, first_lane, period):
    half = rot_dim // 2
    inv_freq = (np.float32(ROPE_THETA) ** (-(np.arange(0, rot_dim, 2, dtype=np.float32) / np.float32(rot_dim)))
                ).astype(np.float32)
    tab = np.zeros((8, LANES), np.float32)
    for lane in range(LANES):
        rel = (lane % period) - first_lane
        if 0 <= rel < rot_dim:
            tab[0, lane] = inv_freq[rel % half]
            tab[1, lane] = 1.0
            tab[2 if rel < half else 3, lane] = 1.0
    return jnp.asarray(tab)


def kernel(x, mem, positions, g_emb, b_emb, w_in, g_cq, g_ckv, w_uq, w_ukv, w_mem_kv, g_out_a, g_out_b,
           g_out_m, w_out, g_post, b_post):
    batch = x.shape[0]
    assert x.shape == (batch, SEQ, D_MODEL) and w_in.shape[0] == DEPTH == 1
    row = lambda v: v.reshape(1, -1).astype(F32)

    splits = [int(i) for i in np.cumsum(IN_SPLITS)[:-1]]
    a_q, a_k, a_v, a_g, c_q, c_kv, b_kr, b_g, m_q, m_g = jnp.split(w_in[0], splits, axis=1)
    w_qkv = jnp.concatenate([a_q, a_k, a_v], axis=1).astype(BF16)
    kr_slot = jnp.pad(b_kr, ((0, 0), (MLA_NOPE_DIM, MLA_SLOT - MLA_NOPE_DIM - MLA_ROPE_DIM)))
    w_b = jnp.concatenate([a_g, b_g, m_g, m_q, c_q, c_kv, kr_slot], axis=1).astype(BF16)
    qk_dim = MLA_NOPE_DIM + MLA_ROPE_DIM
    w_uq_p = jnp.pad(w_uq[0].reshape(MLA_Q_RANK, MLA_HEADS, qk_dim),
                     ((0, 0), (0, 0), (0, MLA_SLOT - qk_dim))).reshape(MLA_Q_RANK, _MLA_QK_W).astype(BF16)
    ukv = w_ukv[0].reshape(MLA_KV_RANK, MLA_HEADS, MLA_NOPE_DIM + MLA_V_DIM)
    w_uk_p = jnp.pad(ukv[:, :, :MLA_NOPE_DIM], ((0, 0), (0, 0), (0, MLA_SLOT - MLA_NOPE_DIM)))
    w_ukv_p = jnp.concatenate([w_uk_p.reshape(MLA_KV_RANK, _MLA_QK_W),
                               ukv[:, :, MLA_NOPE_DIM:].reshape(MLA_KV_RANK, MLA_WIDTH)], axis=1).astype(BF16)

    x_view = x.reshape(batch, RES_LEN, N_RES * D_MODEL)
    pos_view = (positions.astype(F32).reshape(batch, RES_LEN, N_RES).transpose(0, 2, 1)
                .reshape(batch, N_ROW_TILES, RES_PER_TILE, RES_LEN))
    tab_a = _rope_lane_table(A_ROT_DIM, 0, A_HEAD_DIM)
    tab_b = _rope_lane_table(MLA_ROPE_DIM, MLA_NOPE_DIM, MLA_SLOT)
    g_emb_r, b_emb_r = row(g_emb), row(b_emb)

    outs_a = _proj_a(x_view, pos_view, tab_a, g_emb_r, b_emb_r, w_qkv)
    qkv_nat = [t.reshape(batch, SEQ, A_WIDTH) for t in outs_a[0:3]]
    qkv_4 = [t.reshape(batch, 4, SEQ // 4, A_WIDTH) for t in outs_a[3:6]]
    qkv_16 = list(outs_a[6:9])
    y_a = _dilated_attention(qkv_nat, qkv_4, qkv_16)

    gates, mq, qb, kb, vb = _proj_b(x_view, pos_view, tab_b, g_emb_r, b_emb_r, w_b, row(g_cq[0]),
                                    row(g_ckv[0]), w_uq_p, w_ukv_p)
    y_b = _latent_attention(qb, kb, vb)

    mkv = _mem_kv(mem, w_mem_kv[0].astype(BF16))
    out = _output_stage(x_view, y_a, y_b, gates, mq, mkv, w_out[0].astype(BF16), row(g_out_a[0]),
                        row(g_out_b[0]), row(g_out_m[0]), g_emb_r, b_emb_r, row(g_post[0]), row(b_post[0]))
    return out.reshape(batch, SEQ, D_MODEL)
```

```python
import math

import jax
import jax.numpy as jnp
import numpy as np
from jax import lax
from jax.experimental import pallas as pl
from jax.experimental.pallas import tpu as pltpu

D_MODEL = 1024
SEQ = 2048
A_HEADS = 16
A_HEAD_DIM = 64
A_WIDTH = A_HEADS * A_HEAD_DIM
A_ROT_DIM = A_HEAD_DIM // 4
A_SIDE = 64
DILATIONS = (1, 4, 16)
MLA_HEADS = 8
MLA_Q_RANK = 256
MLA_KV_RANK = 128
MLA_NOPE_DIM = 64
MLA_ROPE_DIM = 32
MLA_V_DIM = 64
MLA_WIDTH = MLA_HEADS * MLA_V_DIM
N_MEM = 256
MEM_HEADS = 4
MEM_HEAD_DIM = 128
MEM_WIDTH = MEM_HEADS * MEM_HEAD_DIM
D_MIX = A_WIDTH + MLA_WIDTH + MEM_WIDTH
ROPE_THETA = 500000.0
NORM_EPS = 1e-5
NEG_INF = -1e30
DEPTH = 1
DEEPNORM_ALPHA = (2 * DEPTH) ** 0.25
IN_SPLITS = (A_WIDTH, A_WIDTH, A_WIDTH, A_WIDTH, MLA_Q_RANK, MLA_KV_RANK, MLA_ROPE_DIM, MLA_WIDTH,
             MEM_WIDTH, MEM_WIDTH)
LOG2_E = math.log2(math.e)

LANES = 128
VMEM_BYTES_V7X = 64 * 1024 * 1024

ROW_TILE = 512
ROW_CHUNK = 128
N_CHUNKS = ROW_TILE // ROW_CHUNK
N_ROW_TILES = SEQ // ROW_TILE
A_PAIRS = A_WIDTH // LANES
A_Q_TILE = 128
A_K_WIN = 2 * A_Q_TILE
MLA_SLOT = LANES
MLA_Q_TILE = 256

F32 = jnp.float32
BF16 = jnp.bfloat16


def _vmem_limit(nbytes):
    return int(min(VMEM_BYTES_V7X - (4 << 20), max(32 << 20, nbytes + (8 << 20))))


def _layer_norm_rows(x, g, b):
    mu = jnp.mean(x, axis=-1, keepdims=True)
    xc = x - mu
    var = jnp.mean(xc * xc, axis=-1, keepdims=True)
    return xc * lax.rsqrt(var + NORM_EPS) * g + b


def _rms_rows(x, g):
    ms = jnp.mean(x * x, axis=-1, keepdims=True)
    return x * lax.rsqrt(ms + NORM_EPS) * g


def _chunk(c):
    return slice(c * ROW_CHUNK, (c + 1) * ROW_CHUNK)


def _normed_tile_to_scratch(x_ref, g_ref, b_ref, h_scr):
    for c in range(N_CHUNKS):
        h_scr[_chunk(c), :] = _layer_norm_rows(x_ref[_chunk(c), :], g_ref[...], b_ref[...]).astype(BF16)


def _rope_tables(pos_ref, tab_ref, cos_scr, sin1_scr, sin2_scr):
    inv_freq = tab_ref[0:1, :]
    rot = tab_ref[1:2, :] > 0.5
    first = tab_ref[2:3, :] > 0.5
    second = tab_ref[3:4, :] > 0.5
    for c in range(N_CHUNKS):
        pos_col = jnp.broadcast_to(pos_ref[c:c + 1, :], (ROW_CHUNK, LANES)).T
        ang = pos_col * inv_freq
        cos = jnp.cos(ang)
        sin = jnp.sin(ang)
        cos_scr[_chunk(c), :] = jnp.where(rot, cos, 1.0)
        sin1_scr[_chunk(c), :] = jnp.where(first, -sin, 0.0)
        sin2_scr[_chunk(c), :] = jnp.where(second, sin, 0.0)


def _rope_block(x, rows, half, cos_scr, sin1_scr, sin2_scr):
    return (x * cos_scr[rows, :]
            + pltpu.roll(x, LANES - half, 1) * sin1_scr[rows, :]
            + pltpu.roll(x, half, 1) * sin2_scr[rows, :])


def _proj_a_kernel(x_ref, pos_ref, tab_ref, g_ref, b_ref, w_ref,
                   qn_ref, kn_ref, vn_ref, q4_ref, k4_ref, v4_ref, q16_ref, k16_ref, v16_ref,
                   h_scr, acc_scr, cos_scr, sin1_scr, sin2_scr):
    _normed_tile_to_scratch(x_ref, g_ref, b_ref, h_scr)
    _rope_tables(pos_ref, tab_ref, cos_scr, sin1_scr, sin2_scr)
    outs = ((qn_ref, q4_ref, q16_ref), (kn_ref, k4_ref, k16_ref), (vn_ref, v4_ref, v16_ref))
    q_scale = (A_HEAD_DIM ** -0.5) * LOG2_E
    for sec, (nat_ref, r4_ref, r16_ref) in enumerate(outs):
        res = jnp.dot(h_scr[...], w_ref[:, sec * A_WIDTH:(sec + 1) * A_WIDTH], preferred_element_type=F32)
        for cg in range(A_PAIRS):
            acc_scr[cg] = res[:, cg * LANES:(cg + 1) * LANES]
        for cg in range(A_PAIRS):
            cols = slice(cg * LANES, (cg + 1) * LANES)
            for c in range(N_CHUNKS):
                rows = _chunk(c)
                blk = acc_scr[cg, rows, :]
                if sec < 2:
                    blk = _rope_block(blk, rows, A_ROT_DIM // 2, cos_scr, sin1_scr, sin2_scr)
                    if sec == 0:
                        blk = blk * q_scale
                    acc_scr[cg, rows, :] = blk
                nat_ref[rows, cols] = blk.astype(BF16)
            for r in range(4):
                r4_ref[r, :, cols] = acc_scr[cg, pl.ds(r, ROW_TILE // 4, stride=4), :].astype(BF16)
            for r in range(16):
                r16_ref[r, :, cols] = acc_scr[cg, pl.ds(r, ROW_TILE // 16, stride=16), :].astype(BF16)


def _proj_a(x, pos_view, tab_a, g_emb, b_emb, w_qkv):
    batch = x.shape[0]
    nat = jax.ShapeDtypeStruct((batch, SEQ, A_WIDTH), BF16)
    r4 = jax.ShapeDtypeStruct((batch, 4, SEQ // 4, A_WIDTH), BF16)
    r16 = jax.ShapeDtypeStruct((batch, 16, SEQ // 16, A_WIDTH), BF16)
    nat_spec = pl.BlockSpec((None, ROW_TILE, A_WIDTH), lambda b, i: (b, i, 0))
    r4_spec = pl.BlockSpec((None, 4, ROW_TILE // 4, A_WIDTH), lambda b, i: (b, 0, i, 0))
    r16_spec = pl.BlockSpec((None, 16, ROW_TILE // 16, A_WIDTH), lambda b, i: (b, 0, i, 0))
    tile_out = ROW_TILE * A_WIDTH * 2
    est = (2 * ROW_TILE * D_MODEL * 4 + 2 * w_qkv.size * 2 + 2 * 9 * tile_out
           + ROW_TILE * D_MODEL * 2 + ROW_TILE * A_WIDTH * 4 + 3 * ROW_TILE * LANES * 4)
    return pl.pallas_call(
        _proj_a_kernel,
        grid=(batch, N_ROW_TILES),
        in_specs=[
            pl.BlockSpec((None, ROW_TILE, D_MODEL), lambda b, i: (b, i, 0)),
            pl.BlockSpec((None, None, N_CHUNKS, ROW_CHUNK), lambda b, i: (b, i, 0, 0)),
            pl.BlockSpec((8, LANES), lambda b, i: (0, 0)),
            pl.BlockSpec((1, D_MODEL), lambda b, i: (0, 0)),
            pl.BlockSpec((1, D_MODEL), lambda b, i: (0, 0)),
            pl.BlockSpec(w_qkv.shape, lambda b, i: (0, 0)),
        ],
        out_specs=[nat_spec] * 3 + [r4_spec] * 3 + [r16_spec] * 3,
        out_shape=[nat] * 3 + [r4] * 3 + [r16] * 3,
        scratch_shapes=[
            pltpu.VMEM((ROW_TILE, D_MODEL), BF16),
            pltpu.VMEM((A_PAIRS, ROW_TILE, LANES), F32),
            pltpu.VMEM((ROW_TILE, LANES), F32),
            pltpu.VMEM((ROW_TILE, LANES), F32),
            pltpu.VMEM((ROW_TILE, LANES), F32),
        ],
        compiler_params=pltpu.CompilerParams(
            dimension_semantics=("arbitrary", "arbitrary"), vmem_limit_bytes=_vmem_limit(est)),
        name="proj_a",
    )(x, pos_view, tab_a, g_emb, b_emb, w_qkv)


_GATE_W = D_MIX
_OFF_MQ = _GATE_W
_OFF_CQ = _OFF_MQ + MEM_WIDTH
_OFF_CKV = _OFF_CQ + MLA_Q_RANK
_OFF_KR = _OFF_CKV + MLA_KV_RANK
_MLA_QK_W = MLA_HEADS * MLA_SLOT


def _proj_b_kernel(x_ref, pos_ref, tab_ref, g_ref, b_ref, w_ref, gcq_ref, gckv_ref, wuq_ref, wukv_ref,
                   gate_ref, mq_ref, qb_ref, kb_ref, vb_ref,
                   h_scr, acc_scr, cos_scr, sin1_scr, sin2_scr):
    _normed_tile_to_scratch(x_ref, g_ref, b_ref, h_scr)
    _rope_tables(pos_ref, tab_ref, cos_scr, sin1_scr, sin2_scr)
    h = h_scr[...]

    for n in range(_GATE_W // A_WIDTH):
        acc_scr[...] = jnp.dot(h, w_ref[:, n * A_WIDTH:(n + 1) * A_WIDTH], preferred_element_type=F32)
        for c in range(N_CHUNKS):
            g = acc_scr[_chunk(c), :]
            gate_ref[_chunk(c), n * A_WIDTH:(n + 1) * A_WIDTH] = (g / (1.0 + jnp.exp(-g))).astype(BF16)

    mq = jnp.dot(h, w_ref[:, _OFF_MQ:_OFF_MQ + MEM_WIDTH], preferred_element_type=F32)
    mq_ref[...] = (mq * (MEM_HEAD_DIM ** -0.5)).astype(BF16)

    cq = jnp.dot(h, w_ref[:, _OFF_CQ:_OFF_CQ + MLA_Q_RANK], preferred_element_type=F32)
    cqn = _rms_rows(cq, gcq_ref[...]).astype(BF16)
    acc_scr[...] = jnp.dot(cqn, wuq_ref[...], preferred_element_type=F32)
    mla_scale = (MLA_NOPE_DIM + MLA_ROPE_DIM) ** -0.5
    for c in range(N_CHUNKS):
        rows = _chunk(c)
        for hd in range(MLA_HEADS):
            cols = slice(hd * MLA_SLOT, (hd + 1) * MLA_SLOT)
            blk = _rope_block(acc_scr[rows, cols], rows, MLA_ROPE_DIM // 2, cos_scr, sin1_scr, sin2_scr)
            qb_ref[rows, cols] = (blk * mla_scale).astype(BF16)

    ckv = jnp.dot(h, w_ref[:, _OFF_CKV:_OFF_CKV + MLA_KV_RANK], preferred_element_type=F32)
    ckvn = _rms_rows(ckv, gckv_ref[...]).astype(BF16)
    acc_scr[...] = jnp.dot(ckvn, wukv_ref[:, :_MLA_QK_W], preferred_element_type=F32)
    vb = jnp.dot(ckvn, wukv_ref[:, _MLA_QK_W:], preferred_element_type=F32)
    vb_ref[...] = vb.astype(BF16)
    kr = jnp.dot(h, w_ref[:, _OFF_KR:_OFF_KR + MLA_SLOT], preferred_element_type=F32)
    for c in range(N_CHUNKS):
        rows = _chunk(c)
        kr_c = kr[c * ROW_CHUNK:(c + 1) * ROW_CHUNK, :]
        for hd in range(MLA_HEADS):
            cols = slice(hd * MLA_SLOT, (hd + 1) * MLA_SLOT)
            blk = _rope_block(acc_scr[rows, cols] + kr_c, rows, MLA_ROPE_DIM // 2,
                              cos_scr, sin1_scr, sin2_scr)
            kb_ref[rows, cols] = blk.astype(BF16)


def _proj_b(x, pos_view, tab_b, g_emb, b_emb, w_b, g_cq, g_ckv, w_uq_p, w_ukv_p):
    batch = x.shape[0]

    def rows_spec(width):
        return pl.BlockSpec((None, ROW_TILE, width), lambda b, i: (b, i, 0))

    def full_spec(arr):
        return pl.BlockSpec(arr.shape, lambda b, i: (0,) * arr.ndim)

    widths = (_GATE_W, MEM_WIDTH, _MLA_QK_W, _MLA_QK_W, MLA_WIDTH)
    est = (2 * ROW_TILE * D_MODEL * 4 + 2 * (w_b.size + w_uq_p.size + w_ukv_p.size) * 2
           + 2 * ROW_TILE * sum(widths) * 2 + ROW_TILE * D_MODEL * 2 + ROW_TILE * A_WIDTH * 4
           + 3 * ROW_TILE * LANES * 4 + 2 * ROW_TILE * A_WIDTH * 4)
    return pl.pallas_call(
        _proj_b_kernel,
        grid=(batch, N_ROW_TILES),
        in_specs=[
            rows_spec(D_MODEL),
            pl.BlockSpec((None, None, N_CHUNKS, ROW_CHUNK), lambda b, i: (b, i, 0, 0)),
            pl.BlockSpec((8, LANES), lambda b, i: (0, 0)),
            full_spec(g_emb), full_spec(b_emb), full_spec(w_b), full_spec(g_cq), full_spec(g_ckv),
            full_spec(w_uq_p), full_spec(w_ukv_p),
        ],
        out_specs=[rows_spec(w) for w in widths],
        out_shape=[jax.ShapeDtypeStruct((batch, SEQ, w), BF16) for w in widths],
        scratch_shapes=[
            pltpu.VMEM((ROW_TILE, D_MODEL), BF16),
            pltpu.VMEM((ROW_TILE, A_WIDTH), F32),
            pltpu.VMEM((ROW_TILE, LANES), F32),
            pltpu.VMEM((ROW_TILE, LANES), F32),
            pltpu.VMEM((ROW_TILE, LANES), F32),
        ],
        compiler_params=pltpu.CompilerParams(
            dimension_semantics=("arbitrary", "arbitrary"), vmem_limit_bytes=_vmem_limit(est)),
        name="proj_b",
    )(x, pos_view, tab_b, g_emb, b_emb, w_b, g_cq, g_ckv, w_uq_p, w_ukv_p)


def _band_tile(q, k, v_ones, bias):
    n_q = q.shape[0]
    lane = lax.broadcasted_iota(jnp.int32, (n_q, LANES), 1)
    first_head = lane < A_HEAD_DIM
    zero = jnp.zeros_like(q)
    q2 = jnp.concatenate([jnp.where(first_head, q, zero), jnp.where(first_head, zero, q)], axis=0)
    s = lax.dot_general(q2, k, (((1,), (1,)), ((), ())), preferred_element_type=F32) + bias
    m = jnp.max(s, axis=-1, keepdims=True)
    p = jnp.exp2(s - m).astype(BF16)
    o = jnp.dot(p, v_ones, preferred_element_type=F32)
    acc = jnp.where(first_head, o[:n_q, :LANES], o[n_q:, :LANES])
    l_b = jnp.where(first_head, o[:n_q, LANES:], o[n_q:, LANES:])
    m_b = jnp.where(first_head, m[:n_q], m[n_q:])
    return acc, m_b, l_b


def _dilated_kernel(qn_ref, kn_ref, vn_ref, q4_ref, k4_ref, v4_ref, q16_ref, k16_ref, v16_ref,
                    o_ref, von_scr, vo4_scr, vo16_scr, bias_scr, bias16_scr, acc_scr, mx_scr, sm_scr):
    def window_bias(n_keys, first_key):
        row = lax.broadcasted_iota(jnp.int32, (2 * A_Q_TILE, n_keys), 0)
        col = lax.broadcasted_iota(jnp.int32, (2 * A_Q_TILE, n_keys), 1)
        q_idx = jnp.where(row >= A_Q_TILE, row - A_Q_TILE, row)
        off = col - (q_idx + first_key)
        return jnp.where(jnp.abs(off) <= A_SIDE, 0.0, NEG_INF).astype(F32)

    for variant in range(3):
        bias_scr[variant] = window_bias(A_K_WIN, variant * A_SIDE)
    bias16_scr[...] = window_bias(A_Q_TILE, 0)

    ones = jnp.ones((SEQ // 16, LANES), BF16)
    for c in range(16):
        rows = slice(c * (SEQ // 16), (c + 1) * (SEQ // 16))
        von_scr[rows, :LANES] = vn_ref[rows, :]
        von_scr[rows, LANES:] = ones
        vo4_scr[c // 4, (c % 4) * 128:(c % 4 + 1) * 128, :LANES] = v4_ref[c // 4, (c % 4) * 128:(c % 4 + 1) * 128, :]
        vo4_scr[c // 4, (c % 4) * 128:(c % 4 + 1) * 128, LANES:] = ones
        vo16_scr[c, :, :LANES] = v16_ref[c]
        vo16_scr[c, :, LANES:] = ones

    len4 = SEQ // 4
    tiles4 = len4 // A_Q_TILE
    n_tiles = SEQ // A_Q_TILE

    def window(tile, n_seq_tiles, seq_len):
        qs = pl.multiple_of(tile * A_Q_TILE, A_Q_TILE)
        ks = pl.multiple_of(jnp.clip(qs - A_SIDE, 0, seq_len - A_K_WIN), A_SIDE)
        variant = jnp.where(tile == 0, 0, jnp.where(tile == n_seq_tiles - 1, 2, 1))
        return qs, ks, variant

    def body(i, carry):
        qs, ks, var = window(i, n_tiles, SEQ)
        t1 = _band_tile(qn_ref[pl.ds(qs, A_Q_TILE), :], kn_ref[pl.ds(ks, A_K_WIN), :],
                        von_scr[pl.ds(ks, A_K_WIN), :], bias_scr[var])
        res = i // tiles4
        qs4, ks4, var4 = window(i % tiles4, tiles4, len4)
        t4 = _band_tile(q4_ref[res, pl.ds(qs4, A_Q_TILE), :], k4_ref[res, pl.ds(ks4, A_K_WIN), :],
                        vo4_scr[res, pl.ds(ks4, A_K_WIN), :], bias_scr[var4])
        t16 = _band_tile(q16_ref[i], k16_ref[i], vo16_scr[i], bias16_scr[...])
        dests = (pl.ds(qs, A_Q_TILE), pl.ds(res + 4 * qs4, A_Q_TILE, stride=4), pl.ds(i, A_Q_TILE, stride=16))
        for pat, (dst, (acc, m_b, l_b)) in enumerate(zip(dests, (t1, t4, t16))):
            acc_scr[pat, dst, :] = acc
            mx_scr[pat, dst, :] = m_b
            sm_scr[pat, dst, :] = l_b
        return carry

    lax.fori_loop(0, n_tiles, body, 0)

    def merge(c, carry):
        rows = pl.ds(pl.multiple_of(c * ROW_CHUNK, ROW_CHUNK), ROW_CHUNK)
        m_all = jnp.maximum(jnp.maximum(mx_scr[0, rows, :], mx_scr[1, rows, :]), mx_scr[2, rows, :])
        num = jnp.zeros((ROW_CHUNK, LANES), F32)
        den = jnp.zeros((ROW_CHUNK, LANES), F32)
        for pat in range(3):
            w = jnp.exp2(mx_scr[pat, rows, :] - m_all)
            num = num + w * acc_scr[pat, rows, :]
            den = den + w * sm_scr[pat, rows, :]
        o_ref[rows, :] = (num / den).astype(BF16)
        return carry

    lax.fori_loop(0, SEQ // ROW_CHUNK, merge, 0)


def _dilated_attention(qkv_nat, qkv_4, qkv_16):
    batch = qkv_nat[0].shape[0]
    nat_spec = pl.BlockSpec((None, SEQ, LANES), lambda b, p: (b, 0, p))
    r4_spec = pl.BlockSpec((None, 4, SEQ // 4, LANES), lambda b, p: (b, 0, 0, p))
    r16_spec = pl.BlockSpec((None, 16, SEQ // 16, LANES), lambda b, p: (b, 0, 0, p))
    blk = SEQ * LANES
    est = (2 * 9 * blk * 2 + 2 * blk * 2 + 3 * blk * 2 * 2 + 9 * blk * 4
           + 4 * 2 * A_Q_TILE * A_K_WIN * 4)
    return pl.pallas_call(
        _dilated_kernel,
        grid=(batch, A_PAIRS),
        in_specs=[nat_spec] * 3 + [r4_spec] * 3 + [r16_spec] * 3,
        out_specs=pl.BlockSpec((None, SEQ, LANES), lambda b, p: (b, 0, p)),
        out_shape=jax.ShapeDtypeStruct((batch, SEQ, A_WIDTH), BF16),
        scratch_shapes=[
            pltpu.VMEM((SEQ, 2 * LANES), BF16),
            pltpu.VMEM((4, SEQ // 4, 2 * LANES), BF16),
            pltpu.VMEM((16, SEQ // 16, 2 * LANES), BF16),
            pltpu.VMEM((3, 2 * A_Q_TILE, A_K_WIN), F32),
            pltpu.VMEM((2 * A_Q_TILE, A_Q_TILE), F32),
            pltpu.VMEM((3, SEQ, LANES), F32),
            pltpu.VMEM((3, SEQ, LANES), F32),
            pltpu.VMEM((3, SEQ, LANES), F32),
        ],
        compiler_params=pltpu.CompilerParams(
            dimension_semantics=("arbitrary", "arbitrary"), vmem_limit_bytes=_vmem_limit(est)),
        name="dilated_attn",
    )(*qkv_nat, *qkv_4, *qkv_16)


def _latent_kernel(q_ref, k_ref, v_ref, o_ref):
    v = v_ref[...]
    outs = []
    for hd in range(2):
        cols = slice(hd * MLA_SLOT, (hd + 1) * MLA_SLOT)
        s = lax.dot_general(q_ref[:, cols], k_ref[:, cols], (((1,), (1,)), ((), ())),
                            preferred_element_type=F32)
        m = jnp.max(s, axis=-1, keepdims=True)
        p = jnp.exp(s - m)
        l = jnp.sum(p, axis=-1, keepdims=True)
        o = jnp.dot(p.astype(BF16), v, preferred_element_type=F32)
        outs.append(o / l)
    lane = lax.broadcasted_iota(jnp.int32, outs[0].shape, 1)
    o_ref[...] = jnp.where(lane < MLA_V_DIM, outs[0], outs[1]).astype(BF16)


def _latent_attention(qb, kb, vb):
    batch = qb.shape[0]
    pairs = MLA_HEADS // 2
    est = (2 * MLA_Q_TILE * 2 * MLA_SLOT * 2 + 2 * SEQ * 2 * MLA_SLOT * 2 + 2 * SEQ * LANES * 2
           + 2 * MLA_Q_TILE * LANES * 2 + 4 * MLA_Q_TILE * SEQ * 4)
    return pl.pallas_call(
        _latent_kernel,
        grid=(batch, pairs, SEQ // MLA_Q_TILE),
        in_specs=[
            pl.BlockSpec((None, MLA_Q_TILE, 2 * MLA_SLOT), lambda b, p, i: (b, i, p)),
            pl.BlockSpec((None, SEQ, 2 * MLA_SLOT), lambda b, p, i: (b, 0, p)),
            pl.BlockSpec((None, SEQ, LANES), lambda b, p, i: (b, 0, p)),
        ],
        out_specs=pl.BlockSpec((None, MLA_Q_TILE, LANES), lambda b, p, i: (b, i, p)),
        out_shape=jax.ShapeDtypeStruct((batch, SEQ, MLA_WIDTH), BF16),
        compiler_params=pltpu.CompilerParams(
            dimension_semantics=("arbitrary", "arbitrary", "arbitrary"),
            vmem_limit_bytes=_vmem_limit(est)),
        name="latent_attn",
    )(qb, kb, vb)


def _mem_kv_kernel(mem_ref, w_ref, o_ref):
    o_ref[...] = jnp.dot(mem_ref[...].astype(BF16), w_ref[...], preferred_element_type=F32).astype(BF16)


def _mem_kv(mem, w_mem):
    batch = mem.shape[0]
    est = 2 * N_MEM * D_MODEL * 4 + 2 * w_mem.size * 2 + 2 * N_MEM * 2 * MEM_WIDTH * 2 + N_MEM * D_MODEL * 8
    return pl.pallas_call(
        _mem_kv_kernel,
        grid=(batch,),
        in_specs=[pl.BlockSpec((None, N_MEM, D_MODEL), lambda b: (b, 0, 0)),
                  pl.BlockSpec(w_mem.shape, lambda b: (0, 0))],
        out_specs=pl.BlockSpec((None, N_MEM, 2 * MEM_WIDTH), lambda b: (b, 0, 0)),
        out_shape=jax.ShapeDtypeStruct((batch, N_MEM, 2 * MEM_WIDTH), BF16),
        compiler_params=pltpu.CompilerParams(
            dimension_semantics=("arbitrary",), vmem_limit_bytes=_vmem_limit(est)),
        name="mem_kv",
    )(mem, w_mem)


def _output_kernel(x_ref, ya_ref, yb_ref, gate_ref, mq_ref, mkv_ref, wout_ref,
                   goa_ref, gob_ref, gom_ref, gemb_ref, bemb_ref, gpost_ref, bpost_ref,
                   o_ref, y_scr):
    for hd in range(MEM_HEADS):
        cols = slice(hd * MEM_HEAD_DIM, (hd + 1) * MEM_HEAD_DIM)
        s = lax.dot_general(mq_ref[:, cols], mkv_ref[:, cols], (((1,), (1,)), ((), ())),
                            preferred_element_type=F32)
        m = jnp.max(s, axis=-1, keepdims=True)
        p = jnp.exp(s - m)
        l = jnp.sum(p, axis=-1, keepdims=True)
        o = jnp.dot(p.astype(BF16), mkv_ref[:, MEM_WIDTH + hd * MEM_HEAD_DIM:MEM_WIDTH + (hd + 1) * MEM_HEAD_DIM],
                    preferred_element_type=F32)
        y_scr[:, cols] = o / l

    off_b = A_WIDTH
    off_m = A_WIDTH + MLA_WIDTH
    ya = _rms_rows(ya_ref[...].astype(F32), goa_ref[...]) * gate_ref[:, :off_b].astype(F32)
    yb = _rms_rows(yb_ref[...].astype(F32), gob_ref[...]) * gate_ref[:, off_b:off_m].astype(F32)
    ym = _rms_rows(y_scr[...], gom_ref[...]) * gate_ref[:, off_m:].astype(F32)
    sub = (jnp.dot(ya.astype(BF16), wout_ref[:off_b, :], preferred_element_type=F32)
           + jnp.dot(yb.astype(BF16), wout_ref[off_b:off_m, :], preferred_element_type=F32)
           + jnp.dot(ym.astype(BF16), wout_ref[off_m:, :], preferred_element_type=F32))

    for c in range(N_CHUNKS):
        h = _layer_norm_rows(x_ref[_chunk(c), :], gemb_ref[...], bemb_ref[...])
        z = DEEPNORM_ALPHA * h + sub[c * ROW_CHUNK:(c + 1) * ROW_CHUNK, :]
        o_ref[_chunk(c), :] = _layer_norm_rows(z, gpost_ref[...], bpost_ref[...])


def _output_stage(x, ya, yb, gates, mq, mkv, w_out, g_out_a, g_out_b, g_out_m, g_emb, b_emb,
                  g_post, b_post):
    batch = x.shape[0]

    def rows_spec(width):
        return pl.BlockSpec((None, ROW_TILE, width), lambda b, i: (b, i, 0))

    def full_spec(arr):
        return pl.BlockSpec(arr.shape, lambda b, i: (0,) * arr.ndim)

    est = (4 * ROW_TILE * D_MODEL * 4 + 2 * ROW_TILE * (A_WIDTH + MLA_WIDTH + D_MIX + MEM_WIDTH) * 2
           + 2 * N_MEM * 2 * MEM_WIDTH * 2 + 2 * w_out.size * 2 + ROW_TILE * MEM_WIDTH * 4
           + 4 * ROW_TILE * D_MODEL * 4)
    return pl.pallas_call(
        _output_kernel,
        grid=(batch, N_ROW_TILES),
        in_specs=[
            rows_spec(D_MODEL), rows_spec(A_WIDTH), rows_spec(MLA_WIDTH), rows_spec(D_MIX),
            rows_spec(MEM_WIDTH),
            pl.BlockSpec((None, N_MEM, 2 * MEM_WIDTH), lambda b, i: (b, 0, 0)),
            full_spec(w_out), full_spec(g_out_a), full_spec(g_out_b), full_spec(g_out_m),
            full_spec(g_emb), full_spec(b_emb), full_spec(g_post), full_spec(b_post),
        ],
        out_specs=rows_spec(D_MODEL),
        out_shape=jax.ShapeDtypeStruct(x.shape, F32),
        scratch_shapes=[pltpu.VMEM((ROW_TILE, MEM_WIDTH), F32)],
        compiler_params=pltpu.CompilerParams(
            dimension_semantics=("arbitrary", "arbitrary"), vmem_limit_bytes=_vmem_limit(est)),
        name="output_stage",
    )(x, ya, yb, gates, mq, mkv, w_out, g_out_a, g_out_b, g_out_m, g_emb, b_emb, g_post, b_post)


def _rope_lane_table(rot_dim, first_lane, period):
    half = rot_dim // 2
    inv_freq = (np.float32(ROPE_THETA) ** (-(np.arange(0, rot_dim, 2, dtype=np.float32) / np.float32(rot_dim)))
                ).astype(np.float32)
    tab = np.zeros((8, LANES), np.float32)
    for lane in range(LANES):
        rel = (lane % period) - first_lane
        if 0 <= rel < rot_dim:
            tab[0, lane] = inv_freq[rel % half]
            tab[1, lane] = 1.0
            tab[2 if rel < half else 3, lane] = 1.0
    return jnp.asarray(tab)


def kernel(x, mem, positions, g_emb, b_emb, w_in, g_cq, g_ckv, w_uq, w_ukv, w_mem_kv, g_out_a, g_out_b,
           g_out_m, w_out, g_post, b_post):
    batch = x.shape[0]
    assert x.shape == (batch, SEQ, D_MODEL) and w_in.shape[0] == DEPTH == 1
    row = lambda v: v.reshape(1, -1).astype(F32)

    splits = [int(i) for i in np.cumsum(IN_SPLITS)[:-1]]
    a_q, a_k, a_v, a_g, c_q, c_kv, b_kr, b_g, m_q, m_g = jnp.split(w_in[0], splits, axis=1)
    w_qkv = jnp.concatenate([a_q, a_k, a_v], axis=1).astype(BF16)
    kr_slot = jnp.pad(b_kr, ((0, 0), (MLA_NOPE_DIM, MLA_SLOT - MLA_NOPE_DIM - MLA_ROPE_DIM)))
    w_b = jnp.concatenate([a_g, b_g, m_g, m_q, c_q, c_kv, kr_slot], axis=1).astype(BF16)
    qk_dim = MLA_NOPE_DIM + MLA_ROPE_DIM
    w_uq_p = jnp.pad(w_uq[0].reshape(MLA_Q_RANK, MLA_HEADS, qk_dim),
                     ((0, 0), (0, 0), (0, MLA_SLOT - qk_dim))).reshape(MLA_Q_RANK, _MLA_QK_W).astype(BF16)
    ukv = w_ukv[0].reshape(MLA_KV_RANK, MLA_HEADS, MLA_NOPE_DIM + MLA_V_DIM)
    w_uk_p = jnp.pad(ukv[:, :, :MLA_NOPE_DIM], ((0, 0), (0, 0), (0, MLA_SLOT - MLA_NOPE_DIM)))
    w_ukv_p = jnp.concatenate([w_uk_p.reshape(MLA_KV_RANK, _MLA_QK_W),
                               ukv[:, :, MLA_NOPE_DIM:].reshape(MLA_KV_RANK, MLA_WIDTH)], axis=1).astype(BF16)

    pos_view = positions.astype(F32).reshape(batch, N_ROW_TILES, N_CHUNKS, ROW_CHUNK)
    tab_a = _rope_lane_table(A_ROT_DIM, 0, A_HEAD_DIM)
    tab_b = _rope_lane_table(MLA_ROPE_DIM, MLA_NOPE_DIM, MLA_SLOT)
    g_emb_r, b_emb_r = row(g_emb), row(b_emb)

    outs_a = _proj_a(x, pos_view, tab_a, g_emb_r, b_emb_r, w_qkv)
    y_a = _dilated_attention(outs_a[0:3], outs_a[3:6], outs_a[6:9])

    gates, mq, qb, kb, vb = _proj_b(x, pos_view, tab_b, g_emb_r, b_emb_r, w_b, row(g_cq[0]),
                                    row(g_ckv[0]), w_uq_p, w_ukv_p)
    y_b = _latent_attention(qb, kb, vb)

    mkv = _mem_kv(mem, w_mem_kv[0].astype(BF16))
    return _output_stage(x, y_a, y_b, gates, mq, mkv, w_out[0].astype(BF16), row(g_out_a[0]),
                         row(g_out_b[0]), row(g_out_m[0]), g_emb_r, b_emb_r, row(g_post[0]), row(b_post[0]))
```

```python
import math

import jax
import jax.numpy as jnp
import numpy as np
from jax import lax
from jax.experimental import pallas as pl
from jax.experimental.pallas import tpu as pltpu

D_MODEL = 1024
SEQ = 2048
A_HEADS = 16
A_HEAD_DIM = 64
A_WIDTH = A_HEADS * A_HEAD_DIM
A_ROT_DIM = A_HEAD_DIM // 4
A_SIDE = 64
DILATIONS = (1, 4, 16)
MLA_HEADS = 8
MLA_Q_RANK = 256
MLA_KV_RANK = 128
MLA_NOPE_DIM = 64
MLA_ROPE_DIM = 32
MLA_V_DIM = 64
MLA_WIDTH = MLA_HEADS * MLA_V_DIM
N_MEM = 256
MEM_HEADS = 4
MEM_HEAD_DIM = 128
MEM_WIDTH = MEM_HEADS * MEM_HEAD_DIM
D_MIX = A_WIDTH + MLA_WIDTH + MEM_WIDTH
ROPE_THETA = 500000.0
NORM_EPS = 1e-5
NEG_INF = -1e30
DEPTH = 1
DEEPNORM_ALPHA = (2 * DEPTH) ** 0.25
IN_SPLITS = (A_WIDTH, A_WIDTH, A_WIDTH, A_WIDTH, MLA_Q_RANK, MLA_KV_RANK, MLA_ROPE_DIM, MLA_WIDTH,
             MEM_WIDTH, MEM_WIDTH)
LOG2_E = math.log2(math.e)

LANES = 128
VMEM_BYTES_V7X = 64 * 1024 * 1024

ROW_TILE = 512
ROW_CHUNK = 128
N_CHUNKS = ROW_TILE // ROW_CHUNK
N_ROW_TILES = SEQ // ROW_TILE
A_PAIRS = A_WIDTH // LANES
A_Q_TILE = 128
A_K_WIN = 2 * A_Q_TILE
MLA_SLOT = LANES
MLA_Q_TILE = 256
MLA_PAIRS_PER_STEP = 4

F32 = jnp.float32
BF16 = jnp.bfloat16


def _vmem_limit(nbytes):
    return int(min(VMEM_BYTES_V7X - (4 << 20), max(32 << 20, nbytes + (8 << 20))))


def _layer_norm_rows(x, g, b):
    mu = jnp.mean(x, axis=-1, keepdims=True)
    xc = x - mu
    var = jnp.mean(xc * xc, axis=-1, keepdims=True)
    return xc * lax.rsqrt(var + NORM_EPS) * g + b


def _rms_rows(x, g):
    ms = jnp.mean(x * x, axis=-1, keepdims=True)
    return x * lax.rsqrt(ms + NORM_EPS) * g


def _chunk(c):
    return slice(c * ROW_CHUNK, (c + 1) * ROW_CHUNK)


def _normed_tile_to_scratch(x_ref, g_ref, b_ref, h_scr):
    for c in range(N_CHUNKS):
        h_scr[_chunk(c), :] = _layer_norm_rows(x_ref[_chunk(c), :], g_ref[...], b_ref[...]).astype(BF16)


def _rope_tables(pos_ref, tab_ref, cos_scr, sin1_scr, sin2_scr):
    inv_freq = tab_ref[0:1, :]
    rot = tab_ref[1:2, :] > 0.5
    first = tab_ref[2:3, :] > 0.5
    second = tab_ref[3:4, :] > 0.5
    for c in range(N_CHUNKS):
        pos_col = jnp.broadcast_to(pos_ref[c:c + 1, :], (ROW_CHUNK, LANES)).T
        ang = pos_col * inv_freq
        cos = jnp.cos(ang)
        sin = jnp.sin(ang)
        cos_scr[_chunk(c), :] = jnp.where(rot, cos, 1.0)
        sin1_scr[_chunk(c), :] = jnp.where(first, -sin, 0.0)
        sin2_scr[_chunk(c), :] = jnp.where(second, sin, 0.0)


def _rope_block(x, rows, half, cos_scr, sin1_scr, sin2_scr):
    return (x * cos_scr[rows, :]
            + pltpu.roll(x, LANES - half, 1) * sin1_scr[rows, :]
            + pltpu.roll(x, half, 1) * sin2_scr[rows, :])


def _proj_a_kernel(x_ref, pos_ref, tab_ref, g_ref, b_ref, w_ref,
                   qn_ref, kn_ref, vn_ref, q4_ref, k4_ref, v4_ref, q16_ref, k16_ref, v16_ref,
                   h_scr, acc_scr, cos_scr, sin1_scr, sin2_scr):
    _normed_tile_to_scratch(x_ref, g_ref, b_ref, h_scr)
    _rope_tables(pos_ref, tab_ref, cos_scr, sin1_scr, sin2_scr)
    outs = ((qn_ref, q4_ref, q16_ref), (kn_ref, k4_ref, k16_ref), (vn_ref, v4_ref, v16_ref))
    q_scale = (A_HEAD_DIM ** -0.5) * LOG2_E
    for sec, (nat_ref, r4_ref, r16_ref) in enumerate(outs):
        res = jnp.dot(h_scr[...], w_ref[:, sec * A_WIDTH:(sec + 1) * A_WIDTH], preferred_element_type=F32)
        for cg in range(A_PAIRS):
            acc_scr[cg] = res[:, cg * LANES:(cg + 1) * LANES]
        for cg in range(A_PAIRS):
            cols = slice(cg * LANES, (cg + 1) * LANES)
            for c in range(N_CHUNKS):
                rows = _chunk(c)
                blk = acc_scr[cg, rows, :]
                if sec < 2:
                    blk = _rope_block(blk, rows, A_ROT_DIM // 2, cos_scr, sin1_scr, sin2_scr)
                    if sec == 0:
                        blk = blk * q_scale
                    acc_scr[cg, rows, :] = blk
                nat_ref[rows, cols] = blk.astype(BF16)
            for r in range(4):
                r4_ref[r, :, cols] = acc_scr[cg, pl.ds(r, ROW_TILE // 4, stride=4), :].astype(BF16)
            for r in range(16):
                r16_ref[r, :, cols] = acc_scr[cg, pl.ds(r, ROW_TILE // 16, stride=16), :].astype(BF16)


def _proj_a(x, pos_view, tab_a, g_emb, b_emb, w_qkv):
    batch = x.shape[0]
    nat = jax.ShapeDtypeStruct((batch, SEQ, A_WIDTH), BF16)
    r4 = jax.ShapeDtypeStruct((batch, 4, SEQ // 4, A_WIDTH), BF16)
    r16 = jax.ShapeDtypeStruct((batch, 16, SEQ // 16, A_WIDTH), BF16)
    nat_spec = pl.BlockSpec((None, ROW_TILE, A_WIDTH), lambda b, i: (b, i, 0))
    r4_spec = pl.BlockSpec((None, 4, ROW_TILE // 4, A_WIDTH), lambda b, i: (b, 0, i, 0))
    r16_spec = pl.BlockSpec((None, 16, ROW_TILE // 16, A_WIDTH), lambda b, i: (b, 0, i, 0))
    tile_out = ROW_TILE * A_WIDTH * 2
    est = (2 * ROW_TILE * D_MODEL * 4 + 2 * w_qkv.size * 2 + 2 * 9 * tile_out
           + ROW_TILE * D_MODEL * 2 + ROW_TILE * A_WIDTH * 4 + 3 * ROW_TILE * LANES * 4)
    return pl.pallas_call(
        _proj_a_kernel,
        grid=(batch, N_ROW_TILES),
        in_specs=[
            pl.BlockSpec((None, ROW_TILE, D_MODEL), lambda b, i: (b, i, 0)),
            pl.BlockSpec((None, None, N_CHUNKS, ROW_CHUNK), lambda b, i: (b, i, 0, 0)),
            pl.BlockSpec((8, LANES), lambda b, i: (0, 0)),
            pl.BlockSpec((1, D_MODEL), lambda b, i: (0, 0)),
            pl.BlockSpec((1, D_MODEL), lambda b, i: (0, 0)),
            pl.BlockSpec(w_qkv.shape, lambda b, i: (0, 0)),
        ],
        out_specs=[nat_spec] * 3 + [r4_spec] * 3 + [r16_spec] * 3,
        out_shape=[nat] * 3 + [r4] * 3 + [r16] * 3,
        scratch_shapes=[
            pltpu.VMEM((ROW_TILE, D_MODEL), BF16),
            pltpu.VMEM((A_PAIRS, ROW_TILE, LANES), F32),
            pltpu.VMEM((ROW_TILE, LANES), F32),
            pltpu.VMEM((ROW_TILE, LANES), F32),
            pltpu.VMEM((ROW_TILE, LANES), F32),
        ],
        compiler_params=pltpu.CompilerParams(
            dimension_semantics=("arbitrary", "arbitrary"), vmem_limit_bytes=_vmem_limit(est)),
        name="proj_a",
    )(x, pos_view, tab_a, g_emb, b_emb, w_qkv)


_GATE_W = D_MIX
_OFF_MQ = _GATE_W
_OFF_CQ = _OFF_MQ + MEM_WIDTH
_OFF_CKV = _OFF_CQ + MLA_Q_RANK
_OFF_KR = _OFF_CKV + MLA_KV_RANK
_MLA_QK_W = MLA_HEADS * MLA_SLOT


def _proj_b_kernel(x_ref, pos_ref, tab_ref, g_ref, b_ref, w_ref, gcq_ref, gckv_ref, wuq_ref, wukv_ref,
                   gate_ref, mq_ref, qb_ref, kb_ref, vb_ref,
                   h_scr, acc_scr, cos_scr, sin1_scr, sin2_scr):
    _normed_tile_to_scratch(x_ref, g_ref, b_ref, h_scr)
    _rope_tables(pos_ref, tab_ref, cos_scr, sin1_scr, sin2_scr)
    h = h_scr[...]

    for n in range(_GATE_W // A_WIDTH):
        acc_scr[...] = jnp.dot(h, w_ref[:, n * A_WIDTH:(n + 1) * A_WIDTH], preferred_element_type=F32)
        for c in range(N_CHUNKS):
            g = acc_scr[_chunk(c), :]
            gate_ref[_chunk(c), n * A_WIDTH:(n + 1) * A_WIDTH] = (g / (1.0 + jnp.exp(-g))).astype(BF16)

    mq = jnp.dot(h, w_ref[:, _OFF_MQ:_OFF_MQ + MEM_WIDTH], preferred_element_type=F32)
    mq_ref[...] = (mq * (MEM_HEAD_DIM ** -0.5)).astype(BF16)

    cq = jnp.dot(h, w_ref[:, _OFF_CQ:_OFF_CQ + MLA_Q_RANK], preferred_element_type=F32)
    cqn = _rms_rows(cq, gcq_ref[...]).astype(BF16)
    acc_scr[...] = jnp.dot(cqn, wuq_ref[...], preferred_element_type=F32)
    mla_scale = ((MLA_NOPE_DIM + MLA_ROPE_DIM) ** -0.5) * LOG2_E
    for c in range(N_CHUNKS):
        rows = _chunk(c)
        for hd in range(MLA_HEADS):
            cols = slice(hd * MLA_SLOT, (hd + 1) * MLA_SLOT)
            blk = _rope_block(acc_scr[rows, cols], rows, MLA_ROPE_DIM // 2, cos_scr, sin1_scr, sin2_scr)
            qb_ref[rows, cols] = (blk * mla_scale).astype(BF16)

    ckv = jnp.dot(h, w_ref[:, _OFF_CKV:_OFF_CKV + MLA_KV_RANK], preferred_element_type=F32)
    ckvn = _rms_rows(ckv, gckv_ref[...]).astype(BF16)
    acc_scr[...] = jnp.dot(ckvn, wukv_ref[:, :_MLA_QK_W], preferred_element_type=F32)
    vb = jnp.dot(ckvn, wukv_ref[:, _MLA_QK_W:], preferred_element_type=F32)
    vb_ref[...] = vb.astype(BF16)
    kr = jnp.dot(h, w_ref[:, _OFF_KR:_OFF_KR + MLA_SLOT], preferred_element_type=F32)
    for c in range(N_CHUNKS):
        rows = _chunk(c)
        kr_c = kr[c * ROW_CHUNK:(c + 1) * ROW_CHUNK, :]
        for hd in range(MLA_HEADS):
            cols = slice(hd * MLA_SLOT, (hd + 1) * MLA_SLOT)
            blk = _rope_block(acc_scr[rows, cols] + kr_c, rows, MLA_ROPE_DIM // 2,
                              cos_scr, sin1_scr, sin2_scr)
            kb_ref[rows, cols] = blk.astype(BF16)


def _proj_b(x, pos_view, tab_b, g_emb, b_emb, w_b, g_cq, g_ckv, w_uq_p, w_ukv_p):
    batch = x.shape[0]

    def rows_spec(width):
        return pl.BlockSpec((None, ROW_TILE, width), lambda b, i: (b, i, 0))

    def full_spec(arr):
        return pl.BlockSpec(arr.shape, lambda b, i: (0,) * arr.ndim)

    widths = (_GATE_W, MEM_WIDTH, _MLA_QK_W, _MLA_QK_W, MLA_WIDTH)
    est = (2 * ROW_TILE * D_MODEL * 4 + 2 * (w_b.size + w_uq_p.size + w_ukv_p.size) * 2
           + 2 * ROW_TILE * sum(widths) * 2 + ROW_TILE * D_MODEL * 2 + ROW_TILE * A_WIDTH * 4
           + 3 * ROW_TILE * LANES * 4 + 2 * ROW_TILE * A_WIDTH * 4)
    return pl.pallas_call(
        _proj_b_kernel,
        grid=(batch, N_ROW_TILES),
        in_specs=[
            rows_spec(D_MODEL),
            pl.BlockSpec((None, None, N_CHUNKS, ROW_CHUNK), lambda b, i: (b, i, 0, 0)),
            pl.BlockSpec((8, LANES), lambda b, i: (0, 0)),
            full_spec(g_emb), full_spec(b_emb), full_spec(w_b), full_spec(g_cq), full_spec(g_ckv),
            full_spec(w_uq_p), full_spec(w_ukv_p),
        ],
        out_specs=[rows_spec(w) for w in widths],
        out_shape=[jax.ShapeDtypeStruct((batch, SEQ, w), BF16) for w in widths],
        scratch_shapes=[
            pltpu.VMEM((ROW_TILE, D_MODEL), BF16),
            pltpu.VMEM((ROW_TILE, A_WIDTH), F32),
            pltpu.VMEM((ROW_TILE, LANES), F32),
            pltpu.VMEM((ROW_TILE, LANES), F32),
            pltpu.VMEM((ROW_TILE, LANES), F32),
        ],
        compiler_params=pltpu.CompilerParams(
            dimension_semantics=("arbitrary", "arbitrary"), vmem_limit_bytes=_vmem_limit(est)),
        name="proj_b",
    )(x, pos_view, tab_b, g_emb, b_emb, w_b, g_cq, g_ckv, w_uq_p, w_ukv_p)


def _band_tile(q, k, v_ones, bias):
    n_q = q.shape[0]
    lane = lax.broadcasted_iota(jnp.int32, (n_q, LANES), 1)
    first_head = lane < A_HEAD_DIM
    zero = jnp.zeros_like(q)
    q2 = jnp.concatenate([jnp.where(first_head, q, zero), jnp.where(first_head, zero, q)], axis=0)
    s = lax.dot_general(q2, k, (((1,), (1,)), ((), ())), preferred_element_type=F32) + bias
    m = jnp.max(s, axis=-1, keepdims=True)
    p = jnp.exp2(s - m).astype(BF16)
    o = jnp.dot(p, v_ones, preferred_element_type=F32)
    acc = jnp.where(first_head, o[:n_q, :LANES], o[n_q:, :LANES])
    l_b = jnp.where(first_head, o[:n_q, LANES:], o[n_q:, LANES:])
    m_b = jnp.where(first_head, m[:n_q], m[n_q:])
    return acc, m_b, l_b


def _dilated_kernel(qn_ref, kn_ref, vn_ref, q4_ref, k4_ref, v4_ref, q16_ref, k16_ref, v16_ref,
                    o_ref, von_scr, vo4_scr, vo16_scr, bias_scr, bias16_scr, acc_scr, mx_scr, sm_scr):
    def window_bias(n_keys, first_key):
        row = lax.broadcasted_iota(jnp.int32, (2 * A_Q_TILE, n_keys), 0)
        col = lax.broadcasted_iota(jnp.int32, (2 * A_Q_TILE, n_keys), 1)
        q_idx = jnp.where(row >= A_Q_TILE, row - A_Q_TILE, row)
        off = col - (q_idx + first_key)
        return jnp.where(jnp.abs(off) <= A_SIDE, 0.0, NEG_INF).astype(F32)

    @pl.when((pl.program_id(0) == 0) & (pl.program_id(1) == 0))
    def _():
        for variant in range(3):
            bias_scr[variant] = window_bias(A_K_WIN, variant * A_SIDE)
        bias16_scr[...] = window_bias(A_Q_TILE, 0)
        von_scr[:, LANES:] = jnp.ones((SEQ, LANES), BF16)
        vo4_scr[:, :, LANES:] = jnp.ones((4, SEQ // 4, LANES), BF16)
        vo16_scr[:, :, LANES:] = jnp.ones((16, SEQ // 16, LANES), BF16)

    von_scr[:, :LANES] = vn_ref[...]
    vo4_scr[:, :, :LANES] = v4_ref[...]
    vo16_scr[:, :, :LANES] = v16_ref[...]

    len4 = SEQ // 4
    tiles4 = len4 // A_Q_TILE
    n_tiles = SEQ // A_Q_TILE

    def window(tile, n_seq_tiles, seq_len):
        qs = pl.multiple_of(tile * A_Q_TILE, A_Q_TILE)
        ks = pl.multiple_of(jnp.clip(qs - A_SIDE, 0, seq_len - A_K_WIN), A_SIDE)
        variant = jnp.where(tile == 0, 0, jnp.where(tile == n_seq_tiles - 1, 2, 1))
        return qs, ks, variant

    def body(i, carry):
        qs, ks, var = window(i, n_tiles, SEQ)
        t1 = _band_tile(qn_ref[pl.ds(qs, A_Q_TILE), :], kn_ref[pl.ds(ks, A_K_WIN), :],
                        von_scr[pl.ds(ks, A_K_WIN), :], bias_scr[var])
        res = i // tiles4
        qs4, ks4, var4 = window(i % tiles4, tiles4, len4)
        t4 = _band_tile(q4_ref[res, pl.ds(qs4, A_Q_TILE), :], k4_ref[res, pl.ds(ks4, A_K_WIN), :],
                        vo4_scr[res, pl.ds(ks4, A_K_WIN), :], bias_scr[var4])
        t16 = _band_tile(q16_ref[i], k16_ref[i], vo16_scr[i], bias16_scr[...])
        dests = (pl.ds(qs, A_Q_TILE), pl.ds(res + 4 * qs4, A_Q_TILE, stride=4), pl.ds(i, A_Q_TILE, stride=16))
        for pat, (dst, (acc, m_b, l_b)) in enumerate(zip(dests, (t1, t4, t16))):
            acc_scr[pat, dst, :] = acc
            mx_scr[pat, dst, :] = m_b
            sm_scr[pat, dst, :] = l_b
        return carry

    lax.fori_loop(0, n_tiles, body, 0, unroll=2)

    def merge(c, carry):
        rows = pl.ds(pl.multiple_of(c * ROW_CHUNK, ROW_CHUNK), ROW_CHUNK)
        m_all = jnp.maximum(jnp.maximum(mx_scr[0, rows, :], mx_scr[1, rows, :]), mx_scr[2, rows, :])
        num = jnp.zeros((ROW_CHUNK, LANES), F32)
        den = jnp.zeros((ROW_CHUNK, LANES), F32)
        for pat in range(3):
            w = jnp.exp2(mx_scr[pat, rows, :] - m_all)
            num = num + w * acc_scr[pat, rows, :]
            den = den + w * sm_scr[pat, rows, :]
        o_ref[rows, :] = (num / den).astype(BF16)
        return carry

    lax.fori_loop(0, SEQ // ROW_CHUNK, merge, 0)


def _dilated_attention(qkv_nat, qkv_4, qkv_16):
    batch = qkv_nat[0].shape[0]
    nat_spec = pl.BlockSpec((None, SEQ, LANES), lambda b, p: (b, 0, p))
    r4_spec = pl.BlockSpec((None, 4, SEQ // 4, LANES), lambda b, p: (b, 0, 0, p))
    r16_spec = pl.BlockSpec((None, 16, SEQ // 16, LANES), lambda b, p: (b, 0, 0, p))
    blk = SEQ * LANES
    est = (2 * 9 * blk * 2 + 2 * blk * 2 + 3 * blk * 2 * 2 + 9 * blk * 4
           + 4 * 2 * A_Q_TILE * A_K_WIN * 4)
    return pl.pallas_call(
        _dilated_kernel,
        grid=(batch, A_PAIRS),
        in_specs=[nat_spec] * 3 + [r4_spec] * 3 + [r16_spec] * 3,
        out_specs=pl.BlockSpec((None, SEQ, LANES), lambda b, p: (b, 0, p)),
        out_shape=jax.ShapeDtypeStruct((batch, SEQ, A_WIDTH), BF16),
        scratch_shapes=[
            pltpu.VMEM((SEQ, 2 * LANES), BF16),
            pltpu.VMEM((4, SEQ // 4, 2 * LANES), BF16),
            pltpu.VMEM((16, SEQ // 16, 2 * LANES), BF16),
            pltpu.VMEM((3, 2 * A_Q_TILE, A_K_WIN), F32),
            pltpu.VMEM((2 * A_Q_TILE, A_Q_TILE), F32),
            pltpu.VMEM((3, SEQ, LANES), F32),
            pltpu.VMEM((3, SEQ, LANES), F32),
            pltpu.VMEM((3, SEQ, LANES), F32),
        ],
        compiler_params=pltpu.CompilerParams(
            dimension_semantics=("arbitrary", "arbitrary"), vmem_limit_bytes=_vmem_limit(est)),
        name="dilated_attn",
    )(*qkv_nat, *qkv_4, *qkv_16)


def _latent_kernel(q_ref, k_ref, v_ref, o_ref):
    for pair in range(MLA_PAIRS_PER_STEP):
        v = v_ref[:, pair * LANES:(pair + 1) * LANES]
        outs = []
        for hd in range(2):
            first = (2 * pair + hd) * MLA_SLOT
            cols = slice(first, first + MLA_SLOT)
            s = lax.dot_general(q_ref[:, cols], k_ref[:, cols], (((1,), (1,)), ((), ())),
                                preferred_element_type=F32)
            m = jnp.max(s, axis=-1, keepdims=True)
            p = jnp.exp2(s - m)
            l = jnp.sum(p, axis=-1, keepdims=True)
            o = jnp.dot(p.astype(BF16), v, preferred_element_type=F32)
            outs.append(o / l)
        lane = lax.broadcasted_iota(jnp.int32, outs[0].shape, 1)
        o_ref[:, pair * LANES:(pair + 1) * LANES] = jnp.where(lane < MLA_V_DIM, outs[0], outs[1]).astype(BF16)


def _latent_attention(qb, kb, vb):
    batch = qb.shape[0]
    pairs = MLA_HEADS // 2 // MLA_PAIRS_PER_STEP
    qk_w = MLA_PAIRS_PER_STEP * 2 * MLA_SLOT
    v_w = MLA_PAIRS_PER_STEP * LANES
    est = (2 * MLA_Q_TILE * qk_w * 2 + 2 * SEQ * qk_w * 2 + 2 * SEQ * v_w * 2
           + 2 * MLA_Q_TILE * v_w * 2 + 4 * MLA_PAIRS_PER_STEP * MLA_Q_TILE * SEQ * 4)
    return pl.pallas_call(
        _latent_kernel,
        grid=(batch, pairs, SEQ // MLA_Q_TILE),
        in_specs=[
            pl.BlockSpec((None, MLA_Q_TILE, qk_w), lambda b, p, i: (b, i, p)),
            pl.BlockSpec((None, SEQ, qk_w), lambda b, p, i: (b, 0, p)),
            pl.BlockSpec((None, SEQ, v_w), lambda b, p, i: (b, 0, p)),
        ],
        out_specs=pl.BlockSpec((None, MLA_Q_TILE, v_w), lambda b, p, i: (b, i, p)),
        out_shape=jax.ShapeDtypeStruct((batch, SEQ, MLA_WIDTH), BF16),
        compiler_params=pltpu.CompilerParams(
            dimension_semantics=("arbitrary", "arbitrary", "arbitrary"),
            vmem_limit_bytes=_vmem_limit(est)),
        name="latent_attn",
    )(qb, kb, vb)


def _mem_kv_kernel(mem_ref, w_ref, o_ref):
    o_ref[...] = jnp.dot(mem_ref[...].astype(BF16), w_ref[...], preferred_element_type=F32).astype(BF16)


def _mem_kv(mem, w_mem):
    batch = mem.shape[0]
    est = 2 * N_MEM * D_MODEL * 4 + 2 * w_mem.size * 2 + 2 * N_MEM * 2 * MEM_WIDTH * 2 + N_MEM * D_MODEL * 8
    return pl.pallas_call(
        _mem_kv_kernel,
        grid=(batch,),
        in_specs=[pl.BlockSpec((None, N_MEM, D_MODEL), lambda b: (b, 0, 0)),
                  pl.BlockSpec(w_mem.shape, lambda b: (0, 0))],
        out_specs=pl.BlockSpec((None, N_MEM, 2 * MEM_WIDTH), lambda b: (b, 0, 0)),
        out_shape=jax.ShapeDtypeStruct((batch, N_MEM, 2 * MEM_WIDTH), BF16),
        compiler_params=pltpu.CompilerParams(
            dimension_semantics=("arbitrary",), vmem_limit_bytes=_vmem_limit(est)),
        name="mem_kv",
    )(mem, w_mem)


def _output_kernel(x_ref, ya_ref, yb_ref, gate_ref, mq_ref, mkv_ref, wout_ref,
                   goa_ref, gob_ref, gom_ref, gemb_ref, bemb_ref, gpost_ref, bpost_ref,
                   o_ref, y_scr):
    for hd in range(MEM_HEADS):
        cols = slice(hd * MEM_HEAD_DIM, (hd + 1) * MEM_HEAD_DIM)
        s = lax.dot_general(mq_ref[:, cols], mkv_ref[:, cols], (((1,), (1,)), ((), ())),
                            preferred_element_type=F32)
        m = jnp.max(s, axis=-1, keepdims=True)
        p = jnp.exp(s - m)
        l = jnp.sum(p, axis=-1, keepdims=True)
        o = jnp.dot(p.astype(BF16), mkv_ref[:, MEM_WIDTH + hd * MEM_HEAD_DIM:MEM_WIDTH + (hd + 1) * MEM_HEAD_DIM],
                    preferred_element_type=F32)
        y_scr[:, cols] = o / l

    off_b = A_WIDTH
    off_m = A_WIDTH + MLA_WIDTH
    ya = _rms_rows(ya_ref[...].astype(F32), goa_ref[...]) * gate_ref[:, :off_b].astype(F32)
    yb = _rms_rows(yb_ref[...].astype(F32), gob_ref[...]) * gate_ref[:, off_b:off_m].astype(F32)
    ym = _rms_rows(y_scr[...], gom_ref[...]) * gate_ref[:, off_m:].astype(F32)
    sub = (jnp.dot(ya.astype(BF16), wout_ref[:off_b, :], preferred_element_type=F32)
           + jnp.dot(yb.astype(BF16), wout_ref[off_b:off_m, :], preferred_element_type=F32)
           + jnp.dot(ym.astype(BF16), wout_ref[off_m:, :], preferred_element_type=F32))

    for c in range(N_CHUNKS):
        h = _layer_norm_rows(x_ref[_chunk(c), :], gemb_ref[...], bemb_ref[...])
        z = DEEPNORM_ALPHA * h + sub[c * ROW_CHUNK:(c + 1) * ROW_CHUNK, :]
        o_ref[_chunk(c), :] = _layer_norm_rows(z, gpost_ref[...], bpost_ref[...])


def _output_stage(x, ya, yb, gates, mq, mkv, w_out, g_out_a, g_out_b, g_out_m, g_emb, b_emb,
                  g_post, b_post):
    batch = x.shape[0]

    def rows_spec(width):
        return pl.BlockSpec((None, ROW_TILE, width), lambda b, i: (b, i, 0))

    def full_spec(arr):
        return pl.BlockSpec(arr.shape, lambda b, i: (0,) * arr.ndim)

    est = (4 * ROW_TILE * D_MODEL * 4 + 2 * ROW_TILE * (A_WIDTH + MLA_WIDTH + D_MIX + MEM_WIDTH) * 2
           + 2 * N_MEM * 2 * MEM_WIDTH * 2 + 2 * w_out.size * 2 + ROW_TILE * MEM_WIDTH * 4
           + 4 * ROW_TILE * D_MODEL * 4)
    return pl.pallas_call(
        _output_kernel,
        grid=(batch, N_ROW_TILES),
        in_specs=[
            rows_spec(D_MODEL), rows_spec(A_WIDTH), rows_spec(MLA_WIDTH), rows_spec(D_MIX),
            rows_spec(MEM_WIDTH),
            pl.BlockSpec((None, N_MEM, 2 * MEM_WIDTH), lambda b, i: (b, 0, 0)),
            full_spec(w_out), full_spec(g_out_a), full_spec(g_out_b), full_spec(g_out_m),
            full_spec(g_emb), full_spec(b_emb), full_spec(g_post), full_spec(b_post),
        ],
        out_specs=rows_spec(D_MODEL),
        out_shape=jax.ShapeDtypeStruct(x.shape, F32),
        scratch_shapes=[pltpu.VMEM((ROW_TILE, MEM_WIDTH), F32)],
        compiler_params=pltpu.CompilerParams(
            dimension_semantics=("arbitrary", "arbitrary"), vmem_limit_bytes=_vmem_limit(est)),
        name="output_stage",
    )(x, ya, yb, gates, mq, mkv, w_out, g_out_a, g_out_b, g_out_m, g_emb, b_emb, g_post, b_post)


def _rope_lane_table(rot_dim, first_lane, period):
    half = rot_dim // 2
    inv_freq = (np.float32(ROPE_THETA) ** (-(np.arange(0, rot_dim, 2, dtype=np.float32) / np.float32(rot_dim)))
                ).astype(np.float32)
    tab = np.zeros((8, LANES), np.float32)
    for lane in range(LANES):
        rel = (lane % period) - first_lane
        if 0 <= rel < rot_dim:
            tab[0, lane] = inv_freq[rel % half]
            tab[1, lane] = 1.0
            tab[2 if rel < half else 3, lane] = 1.0
    return jnp.asarray(tab)


def kernel(x, mem, positions, g_emb, b_emb, w_in, g_cq, g_ckv, w_uq, w_ukv, w_mem_kv, g_out_a, g_out_b,
           g_out_m, w_out, g_post, b_post):
    batch = x.shape[0]
    assert x.shape == (batch, SEQ, D_MODEL) and w_in.shape[0] == DEPTH == 1
    row = lambda v: v.reshape(1, -1).astype(F32)

    splits = [int(i) for i in np.cumsum(IN_SPLITS)[:-1]]
    a_q, a_k, a_v, a_g, c_q, c_kv, b_kr, b_g, m_q, m_g = jnp.split(w_in[0], splits, axis=1)
    w_qkv = jnp.concatenate([a_q, a_k, a_v], axis=1).astype(BF16)
    kr_slot = jnp.pad(b_kr, ((0, 0), (MLA_NOPE_DIM, MLA_SLOT - MLA_NOPE_DIM - MLA_ROPE_DIM)))
    w_b = jnp.concatenate([a_g, b_g, m_g, m_q, c_q, c_kv, kr_slot], axis=1).astype(BF16)
    qk_dim = MLA_NOPE_DIM + MLA_ROPE_DIM
    w_uq_p = jnp.pad(w_uq[0].reshape(MLA_Q_RANK, MLA_HEADS, qk_dim),
                     ((0, 0), (0, 0), (0, MLA_SLOT - qk_dim))).reshape(MLA_Q_RANK, _MLA_QK_W).astype(BF16)
    ukv = w_ukv[0].reshape(MLA_KV_RANK, MLA_HEADS, MLA_NOPE_DIM + MLA_V_DIM)
    w_uk_p = jnp.pad(ukv[:, :, :MLA_NOPE_DIM], ((0, 0), (0, 0), (0, MLA_SLOT - MLA_NOPE_DIM)))
    w_ukv_p = jnp.concatenate([w_uk_p.reshape(MLA_KV_RANK, _MLA_QK_W),
                               ukv[:, :, MLA_NOPE_DIM:].reshape(MLA_KV_RANK, MLA_WIDTH)], axis=1).astype(BF16)

    pos_view = positions.astype(F32).reshape(batch, N_ROW_TILES, N_CHUNKS, ROW_CHUNK)
    tab_a = _rope_lane_table(A_ROT_DIM, 0, A_HEAD_DIM)
    tab_b = _rope_lane_table(MLA_ROPE_DIM, MLA_NOPE_DIM, MLA_SLOT)
    g_emb_r, b_emb_r = row(g_emb), row(b_emb)

    outs_a = _proj_a(x, pos_view, tab_a, g_emb_r, b_emb_r, w_qkv)
    y_a = _dilated_attention(outs_a[0:3], outs_a[3:6], outs_a[6:9])

    gates, mq, qb, kb, vb = _proj_b(x, pos_view, tab_b, g_emb_r, b_emb_r, w_b, row(g_cq[0]),
                                    row(g_ckv[0]), w_uq_p, w_ukv_p)
    y_b = _latent_attention(qb, kb, vb)

    mkv = _mem_kv(mem, w_mem_kv[0].astype(BF16))
    return _output_stage(x, y_a, y_b, gates, mq, mkv, w_out[0].astype(BF16), row(g_out_a[0]),
                         row(g_out_b[0]), row(g_out_m[0]), g_emb_r, b_emb_r, row(g_post[0]), row(b_post[0]))
```

```python
import math

import jax
import jax.numpy as jnp
import numpy as np
from jax import lax
from jax.experimental import pallas as pl
from jax.experimental.pallas import tpu as pltpu

D_MODEL = 1024
SEQ = 2048
A_HEADS = 16
A_HEAD_DIM = 64
A_WIDTH = A_HEADS * A_HEAD_DIM
A_ROT_DIM = A_HEAD_DIM // 4
A_SIDE = 64
DILATIONS = (1, 4, 16)
MLA_HEADS = 8
MLA_Q_RANK = 256
MLA_KV_RANK = 128
MLA_NOPE_DIM = 64
MLA_ROPE_DIM = 32
MLA_V_DIM = 64
MLA_WIDTH = MLA_HEADS * MLA_V_DIM
N_MEM = 256
MEM_HEADS = 4
MEM_HEAD_DIM = 128
MEM_WIDTH = MEM_HEADS * MEM_HEAD_DIM
D_MIX = A_WIDTH + MLA_WIDTH + MEM_WIDTH
ROPE_THETA = 500000.0
NORM_EPS = 1e-5
NEG_INF = -1e30
DEPTH = 1
DEEPNORM_ALPHA = (2 * DEPTH) ** 0.25
IN_SPLITS = (A_WIDTH, A_WIDTH, A_WIDTH, A_WIDTH, MLA_Q_RANK, MLA_KV_RANK, MLA_ROPE_DIM, MLA_WIDTH,
             MEM_WIDTH, MEM_WIDTH)
LOG2_E = math.log2(math.e)

LANES = 128
VMEM_BYTES_V7X = 64 * 1024 * 1024

ROW_TILE = 512
ROW_CHUNK = 128
N_CHUNKS = ROW_TILE // ROW_CHUNK
N_HALVES = 2
ROW_HALF = ROW_TILE // N_HALVES
CHUNKS_PER_HALF = N_CHUNKS // N_HALVES
N_ROW_TILES = SEQ // ROW_TILE
A_PAIRS = A_WIDTH // LANES
A_Q_TILE = 128
A_K_WIN = 2 * A_Q_TILE
MLA_SLOT = LANES
MLA_Q_TILE = 256
MLA_PAIRS_PER_STEP = 4

F32 = jnp.float32
BF16 = jnp.bfloat16


def _vmem_limit(nbytes):
    return int(min(VMEM_BYTES_V7X - (4 << 20), max(32 << 20, nbytes + (8 << 20))))


def _layer_norm_rows(x, g, b):
    mu = jnp.mean(x, axis=-1, keepdims=True)
    xc = x - mu
    var = jnp.mean(xc * xc, axis=-1, keepdims=True)
    return xc * lax.rsqrt(var + NORM_EPS) * g + b


def _rms_rows(x, g):
    ms = jnp.mean(x * x, axis=-1, keepdims=True)
    return x * lax.rsqrt(ms + NORM_EPS) * g


def _chunk(c):
    return slice(c * ROW_CHUNK, (c + 1) * ROW_CHUNK)


def _normed_chunks_to_scratch(x_ref, g_ref, b_ref, h_scr, chunks):
    for c in chunks:
        h_scr[_chunk(c), :] = _layer_norm_rows(x_ref[_chunk(c), :], g_ref[...], b_ref[...]).astype(BF16)


def _rope_tables(pos_ref, freq_ref, mask_ref, spread, chunks, cos_scr, sin1_scr, sin2_scr):
    n_freq = freq_ref.shape[0]
    rot = mask_ref[1:2, :] > 0.5
    first = mask_ref[2:3, :] > 0.5
    second = mask_ref[3:4, :] > 0.5
    pad = jnp.zeros((ROW_CHUNK - n_freq, LANES), F32)
    for c in chunks:
        ang = freq_ref[...] * pos_ref[c:c + 1, :]
        cos = spread(jnp.concatenate([jnp.cos(ang), pad], axis=0).T)
        sin = spread(jnp.concatenate([jnp.sin(ang), pad], axis=0).T)
        cos_scr[_chunk(c), :] = jnp.where(rot, cos, 1.0)
        sin1_scr[_chunk(c), :] = jnp.where(first, -sin, 0.0)
        sin2_scr[_chunk(c), :] = jnp.where(second, sin, 0.0)


def _spread_a(t):
    t = t + pltpu.roll(t, A_ROT_DIM // 2, 1)
    return t + pltpu.roll(t, A_HEAD_DIM, 1)


def _spread_b(t):
    t = pltpu.roll(t, MLA_NOPE_DIM, 1)
    return t + pltpu.roll(t, MLA_ROPE_DIM // 2, 1)


def _rope_block(x, rows, half, cos_scr, sin1_scr, sin2_scr):
    return (x * cos_scr[rows, :]
            + pltpu.roll(x, LANES - half, 1) * sin1_scr[rows, :]
            + pltpu.roll(x, half, 1) * sin2_scr[rows, :])


def _proj_a_kernel(x_ref, pos_ref, freq_ref, mask_ref, g_ref, b_ref, w_ref,
                   qn_ref, kn_ref, vn_ref, q4_ref, k4_ref, v4_ref, q16_ref, k16_ref, v16_ref,
                   h_scr, acc_scr, st4_scr, cos_scr, sin1_scr, sin2_scr):
    outs = ((qn_ref, q4_ref, q16_ref), (kn_ref, k4_ref, k16_ref), (vn_ref, v4_ref, v16_ref))
    q_scale = (A_HEAD_DIM ** -0.5) * LOG2_E
    for hf in range(N_HALVES):
        chunks = range(hf * CHUNKS_PER_HALF, (hf + 1) * CHUNKS_PER_HALF)
        half_rows = slice(hf * ROW_HALF, (hf + 1) * ROW_HALF)
        _normed_chunks_to_scratch(x_ref, g_ref, b_ref, h_scr, chunks)
        _rope_tables(pos_ref, freq_ref, mask_ref, _spread_a, chunks, cos_scr, sin1_scr, sin2_scr)
        for sec, (nat_ref, r4_ref, r16_ref) in enumerate(outs):
            res = jnp.dot(h_scr[half_rows, :], w_ref[:, sec * A_WIDTH:(sec + 1) * A_WIDTH],
                          preferred_element_type=F32)
            acc = acc_scr.at[hf * len(outs) + sec]
            st4 = st4_scr.at[hf * len(outs) + sec]
            for cg in range(A_PAIRS):
                cols = slice(cg * LANES, (cg + 1) * LANES)
                for lc in range(CHUNKS_PER_HALF):
                    rows = _chunk(hf * CHUNKS_PER_HALF + lc)
                    blk = res[lc * ROW_CHUNK:(lc + 1) * ROW_CHUNK, cols]
                    if sec < 2:
                        blk = _rope_block(blk, rows, A_ROT_DIM // 2, cos_scr, sin1_scr, sin2_scr)
                        if sec == 0:
                            blk = blk * q_scale
                    acc[cg, lc * ROW_CHUNK:(lc + 1) * ROW_CHUNK, :] = blk
                    nat_ref[rows, cols] = blk.astype(BF16)
                n4, n16 = ROW_HALF // 4, ROW_HALF // 16
                for r in range(4):
                    rows4 = acc[cg, pl.ds(r, n4, stride=4), :]
                    st4[cg, r] = rows4
                    r4_ref[r, hf * n4:(hf + 1) * n4, cols] = rows4.astype(BF16)
                for r in range(4):
                    for v in range(4):
                        r16_ref[r + 4 * v, hf * n16:(hf + 1) * n16, cols] = (
                            st4[cg, r, pl.ds(v, n16, stride=4), :].astype(BF16))


def _proj_a(x, pos_view, freq_a, mask_a, g_emb, b_emb, w_qkv):
    batch = x.shape[0]
    nat = jax.ShapeDtypeStruct((batch, SEQ, A_WIDTH), BF16)
    r4 = jax.ShapeDtypeStruct((batch, 4, SEQ // 4, A_WIDTH), BF16)
    r16 = jax.ShapeDtypeStruct((batch, 16, SEQ // 16, A_WIDTH), BF16)
    nat_spec = pl.BlockSpec((None, ROW_TILE, A_WIDTH), lambda b, i: (b, i, 0))
    r4_spec = pl.BlockSpec((None, 4, ROW_TILE // 4, A_WIDTH), lambda b, i: (b, 0, i, 0))
    r16_spec = pl.BlockSpec((None, 16, ROW_TILE // 16, A_WIDTH), lambda b, i: (b, 0, i, 0))
    tile_out = ROW_TILE * A_WIDTH * 2
    est = (2 * ROW_TILE * D_MODEL * 4 + 2 * w_qkv.size * 2 + 2 * 9 * tile_out
           + ROW_TILE * D_MODEL * 2 + 2 * 3 * ROW_TILE * A_WIDTH * 4 + 3 * ROW_TILE * LANES * 4
           + 2 * ROW_HALF * A_WIDTH * 4)
    return pl.pallas_call(
        _proj_a_kernel,
        grid=(batch, N_ROW_TILES),
        in_specs=[
            pl.BlockSpec((None, ROW_TILE, D_MODEL), lambda b, i: (b, i, 0)),
            pl.BlockSpec((None, None, N_CHUNKS, ROW_CHUNK), lambda b, i: (b, i, 0, 0)),
            pl.BlockSpec(freq_a.shape, lambda b, i: (0, 0)),
            pl.BlockSpec((8, LANES), lambda b, i: (0, 0)),
            pl.BlockSpec((1, D_MODEL), lambda b, i: (0, 0)),
            pl.BlockSpec((1, D_MODEL), lambda b, i: (0, 0)),
            pl.BlockSpec(w_qkv.shape, lambda b, i: (0, 0)),
        ],
        out_specs=[nat_spec] * 3 + [r4_spec] * 3 + [r16_spec] * 3,
        out_shape=[nat] * 3 + [r4] * 3 + [r16] * 3,
        scratch_shapes=[
            pltpu.VMEM((ROW_TILE, D_MODEL), BF16),
            pltpu.VMEM((N_HALVES * 3, A_PAIRS, ROW_HALF, LANES), F32),
            pltpu.VMEM((N_HALVES * 3, A_PAIRS, 4, ROW_HALF // 4, LANES), F32),
            pltpu.VMEM((ROW_TILE, LANES), F32),
            pltpu.VMEM((ROW_TILE, LANES), F32),
            pltpu.VMEM((ROW_TILE, LANES), F32),
        ],
        compiler_params=pltpu.CompilerParams(
            dimension_semantics=("arbitrary", "arbitrary"), vmem_limit_bytes=_vmem_limit(est)),
        name="proj_a",
    )(x, pos_view, freq_a, mask_a, g_emb, b_emb, w_qkv)


_GATE_W = D_MIX
_OFF_MQ = _GATE_W
_OFF_CQ = _OFF_MQ + MEM_WIDTH
_OFF_CKV = _OFF_CQ + MLA_Q_RANK
_OFF_KR = _OFF_CKV + MLA_KV_RANK
_MLA_QK_W = MLA_HEADS * MLA_SLOT


def _proj_b_kernel(x_ref, pos_ref, freq_ref, mask_ref, g_ref, b_ref, w_ref, gcq_ref, gckv_ref, wuq_ref,
                   wukv_ref, gate_ref, mq_ref, qb_ref, kb_ref, vb_ref,
                   h_scr, cos_scr, sin1_scr, sin2_scr):
    mla_scale = ((MLA_NOPE_DIM + MLA_ROPE_DIM) ** -0.5) * LOG2_E
    for hf in range(N_HALVES):
        chunks = range(hf * CHUNKS_PER_HALF, (hf + 1) * CHUNKS_PER_HALF)
        half_rows = slice(hf * ROW_HALF, (hf + 1) * ROW_HALF)
        local = lambda lc: slice(lc * ROW_CHUNK, (lc + 1) * ROW_CHUNK)
        _normed_chunks_to_scratch(x_ref, g_ref, b_ref, h_scr, chunks)
        _rope_tables(pos_ref, freq_ref, mask_ref, _spread_b, chunks, cos_scr, sin1_scr, sin2_scr)
        h = h_scr[half_rows, :]

        for n in range(_GATE_W // A_WIDTH):
            g = jnp.dot(h, w_ref[:, n * A_WIDTH:(n + 1) * A_WIDTH], preferred_element_type=F32)
            gate_ref[half_rows, n * A_WIDTH:(n + 1) * A_WIDTH] = (g / (1.0 + jnp.exp(-g))).astype(BF16)

        mq = jnp.dot(h, w_ref[:, _OFF_MQ:_OFF_MQ + MEM_WIDTH], preferred_element_type=F32)
        mq_ref[half_rows, :] = (mq * (MEM_HEAD_DIM ** -0.5)).astype(BF16)

        cq = jnp.dot(h, w_ref[:, _OFF_CQ:_OFF_CQ + MLA_Q_RANK], preferred_element_type=F32)
        cqn = _rms_rows(cq, gcq_ref[...]).astype(BF16)
        q_all = jnp.dot(cqn, wuq_ref[...], preferred_element_type=F32)
        for lc in range(CHUNKS_PER_HALF):
            rows = _chunk(hf * CHUNKS_PER_HALF + lc)
            for hd in range(MLA_HEADS):
                cols = slice(hd * MLA_SLOT, (hd + 1) * MLA_SLOT)
                blk = _rope_block(q_all[local(lc), cols], rows, MLA_ROPE_DIM // 2, cos_scr, sin1_scr, sin2_scr)
                qb_ref[rows, cols] = (blk * mla_scale).astype(BF16)

        ckv = jnp.dot(h, w_ref[:, _OFF_CKV:_OFF_CKV + MLA_KV_RANK], preferred_element_type=F32)
        ckvn = _rms_rows(ckv, gckv_ref[...]).astype(BF16)
        k_all = jnp.dot(ckvn, wukv_ref[:, :_MLA_QK_W], preferred_element_type=F32)
        vb = jnp.dot(ckvn, wukv_ref[:, _MLA_QK_W:], preferred_element_type=F32)
        vb_ref[half_rows, :] = vb.astype(BF16)
        kr = jnp.dot(h, w_ref[:, _OFF_KR:_OFF_KR + MLA_SLOT], preferred_element_type=F32)
        for lc in range(CHUNKS_PER_HALF):
            rows = _chunk(hf * CHUNKS_PER_HALF + lc)
            kr_c = kr[local(lc), :]
            for hd in range(MLA_HEADS):
                cols = slice(hd * MLA_SLOT, (hd + 1) * MLA_SLOT)
                blk = _rope_block(k_all[local(lc), cols] + kr_c, rows, MLA_ROPE_DIM // 2,
                                  cos_scr, sin1_scr, sin2_scr)
                kb_ref[rows, cols] = blk.astype(BF16)


def _proj_b(x, pos_view, freq_b, mask_b, g_emb, b_emb, w_b, g_cq, g_ckv, w_uq_p, w_ukv_p):
    batch = x.shape[0]

    def rows_spec(width):
        return pl.BlockSpec((None, ROW_TILE, width), lambda b, i: (b, i, 0))

    def full_spec(arr):
        return pl.BlockSpec(arr.shape, lambda b, i: (0,) * arr.ndim)

    widths = (_GATE_W, MEM_WIDTH, _MLA_QK_W, _MLA_QK_W, MLA_WIDTH)
    est = (2 * ROW_TILE * D_MODEL * 4 + 2 * (w_b.size + w_uq_p.size + w_ukv_p.size) * 2
           + 2 * ROW_TILE * sum(widths) * 2 + ROW_TILE * D_MODEL * 2 + ROW_TILE * A_WIDTH * 4
           + 3 * ROW_TILE * LANES * 4 + 2 * ROW_TILE * A_WIDTH * 4)
    return pl.pallas_call(
        _proj_b_kernel,
        grid=(batch, N_ROW_TILES),
        in_specs=[
            rows_spec(D_MODEL),
            pl.BlockSpec((None, None, N_CHUNKS, ROW_CHUNK), lambda b, i: (b, i, 0, 0)),
            full_spec(freq_b), full_spec(mask_b),
            full_spec(g_emb), full_spec(b_emb), full_spec(w_b), full_spec(g_cq), full_spec(g_ckv),
            full_spec(w_uq_p), full_spec(w_ukv_p),
        ],
        out_specs=[rows_spec(w) for w in widths],
        out_shape=[jax.ShapeDtypeStruct((batch, SEQ, w), BF16) for w in widths],
        scratch_shapes=[
            pltpu.VMEM((ROW_TILE, D_MODEL), BF16),
            pltpu.VMEM((ROW_TILE, LANES), F32),
            pltpu.VMEM((ROW_TILE, LANES), F32),
            pltpu.VMEM((ROW_TILE, LANES), F32),
        ],
        compiler_params=pltpu.CompilerParams(
            dimension_semantics=("arbitrary", "arbitrary"), vmem_limit_bytes=_vmem_limit(est)),
        name="proj_b",
    )(x, pos_view, freq_b, mask_b, g_emb, b_emb, w_b, g_cq, g_ckv, w_uq_p, w_ukv_p)


def _band_tile(q, k, v_ones, bias):
    n_q = q.shape[0]
    lane = lax.broadcasted_iota(jnp.int32, (n_q, LANES), 1)
    first_head = lane < A_HEAD_DIM
    zero = jnp.zeros_like(q)
    q2 = jnp.concatenate([jnp.where(first_head, q, zero), jnp.where(first_head, zero, q)], axis=0)
    s = lax.dot_general(q2, k, (((1,), (1,)), ((), ())), preferred_element_type=F32) + bias
    m = jnp.max(s, axis=-1, keepdims=True)
    p = jnp.exp2(s - m).astype(BF16)
    o = jnp.dot(p, v_ones, preferred_element_type=F32)
    acc = jnp.where(first_head, o[:n_q, :LANES], o[n_q:, :LANES])
    l_b = jnp.where(first_head, o[:n_q, LANES:], o[n_q:, LANES:])
    m_b = jnp.where(first_head, m[:n_q], m[n_q:])
    return acc, m_b, l_b


def _dilated_kernel(qn_ref, kn_ref, vn_ref, q4_ref, k4_ref, v4_ref, q16_ref, k16_ref, v16_ref,
                    o_ref, von_scr, vo4_scr, vo16_scr, bias_scr, bias16_scr, acc_scr, mx_scr, sm_scr):
    def window_bias(n_keys, first_key):
        row = lax.broadcasted_iota(jnp.int32, (2 * A_Q_TILE, n_keys), 0)
        col = lax.broadcasted_iota(jnp.int32, (2 * A_Q_TILE, n_keys), 1)
        q_idx = jnp.where(row >= A_Q_TILE, row - A_Q_TILE, row)
        off = col - (q_idx + first_key)
        return jnp.where(jnp.abs(off) <= A_SIDE, 0.0, NEG_INF).astype(F32)

    @pl.when((pl.program_id(0) == 0) & (pl.program_id(1) == 0))
    def _():
        for variant in range(3):
            bias_scr[variant] = window_bias(A_K_WIN, variant * A_SIDE)
        bias16_scr[...] = window_bias(A_Q_TILE, 0)
        von_scr[:, LANES:] = jnp.ones((SEQ, LANES), BF16)
        vo4_scr[:, :, LANES:] = jnp.ones((4, SEQ // 4, LANES), BF16)
        vo16_scr[:, :, LANES:] = jnp.ones((16, SEQ // 16, LANES), BF16)

    von_scr[:, :LANES] = vn_ref[...]
    vo4_scr[:, :, :LANES] = v4_ref[...]
    vo16_scr[:, :, :LANES] = v16_ref[...]

    len4 = SEQ // 4
    tiles4 = len4 // A_Q_TILE
    n_tiles = SEQ // A_Q_TILE

    def window(tile, n_seq_tiles, seq_len):
        qs = pl.multiple_of(tile * A_Q_TILE, A_Q_TILE)
        ks = pl.multiple_of(jnp.clip(qs - A_SIDE, 0, seq_len - A_K_WIN), A_SIDE)
        variant = jnp.where(tile == 0, 0, jnp.where(tile == n_seq_tiles - 1, 2, 1))
        return qs, ks, variant

    def body(i, carry):
        qs, ks, var = window(i, n_tiles, SEQ)
        t1 = _band_tile(qn_ref[pl.ds(qs, A_Q_TILE), :], kn_ref[pl.ds(ks, A_K_WIN), :],
                        von_scr[pl.ds(ks, A_K_WIN), :], bias_scr[var])
        res = i // tiles4
        qs4, ks4, var4 = window(i % tiles4, tiles4, len4)
        t4 = _band_tile(q4_ref[res, pl.ds(qs4, A_Q_TILE), :], k4_ref[res, pl.ds(ks4, A_K_WIN), :],
                        vo4_scr[res, pl.ds(ks4, A_K_WIN), :], bias_scr[var4])
        t16 = _band_tile(q16_ref[i], k16_ref[i], vo16_scr[i], bias16_scr[...])
        dests = (pl.ds(qs, A_Q_TILE), pl.ds(res + 4 * qs4, A_Q_TILE, stride=4), pl.ds(i, A_Q_TILE, stride=16))
        for pat, (dst, (acc, m_b, l_b)) in enumerate(zip(dests, (t1, t4, t16))):
            acc_scr[pat, dst, :] = acc
            mx_scr[pat, dst, :] = m_b
            sm_scr[pat, dst, :] = l_b
        return carry

    lax.fori_loop(0, n_tiles, body, 0, unroll=2)

    def merge(c, carry):
        rows = pl.ds(pl.multiple_of(c * ROW_CHUNK, ROW_CHUNK), ROW_CHUNK)
        m_all = jnp.maximum(jnp.maximum(mx_scr[0, rows, :], mx_scr[1, rows, :]), mx_scr[2, rows, :])
        num = jnp.zeros((ROW_CHUNK, LANES), F32)
        den = jnp.zeros((ROW_CHUNK, LANES), F32)
        for pat in range(3):
            w = jnp.exp2(mx_scr[pat, rows, :] - m_all)
            num = num + w * acc_scr[pat, rows, :]
            den = den + w * sm_scr[pat, rows, :]
        o_ref[rows, :] = (num / den).astype(BF16)
        return carry

    lax.fori_loop(0, SEQ // ROW_CHUNK, merge, 0)


def _dilated_attention(qkv_nat, qkv_4, qkv_16):
    batch = qkv_nat[0].shape[0]
    nat_spec = pl.BlockSpec((None, SEQ, LANES), lambda b, p: (b, 0, p))
    r4_spec = pl.BlockSpec((None, 4, SEQ // 4, LANES), lambda b, p: (b, 0, 0, p))
    r16_spec = pl.BlockSpec((None, 16, SEQ // 16, LANES), lambda b, p: (b, 0, 0, p))
    blk = SEQ * LANES
    est = (2 * 9 * blk * 2 + 2 * blk * 2 + 3 * blk * 2 * 2 + 9 * blk * 4
           + 4 * 2 * A_Q_TILE * A_K_WIN * 4)
    return pl.pallas_call(
        _dilated_kernel,
        grid=(batch, A_PAIRS),
        in_specs=[nat_spec] * 3 + [r4_spec] * 3 + [r16_spec] * 3,
        out_specs=pl.BlockSpec((None, SEQ, LANES), lambda b, p: (b, 0, p)),
        out_shape=jax.ShapeDtypeStruct((batch, SEQ, A_WIDTH), BF16),
        scratch_shapes=[
            pltpu.VMEM((SEQ, 2 * LANES), BF16),
            pltpu.VMEM((4, SEQ // 4, 2 * LANES), BF16),
            pltpu.VMEM((16, SEQ // 16, 2 * LANES), BF16),
            pltpu.VMEM((3, 2 * A_Q_TILE, A_K_WIN), F32),
            pltpu.VMEM((2 * A_Q_TILE, A_Q_TILE), F32),
            pltpu.VMEM((3, SEQ, LANES), F32),
            pltpu.VMEM((3, SEQ, LANES), F32),
            pltpu.VMEM((3, SEQ, LANES), F32),
        ],
        compiler_params=pltpu.CompilerParams(
            dimension_semantics=("arbitrary", "arbitrary"), vmem_limit_bytes=_vmem_limit(est)),
        name="dilated_attn",
    )(*qkv_nat, *qkv_4, *qkv_16)


def _latent_kernel(q_ref, k_ref, v_ref, o_ref):
    for pair in range(MLA_PAIRS_PER_STEP):
        v = v_ref[:, pair * LANES:(pair + 1) * LANES]
        outs = []
        for hd in range(2):
            first = (2 * pair + hd) * MLA_SLOT
            cols = slice(first, first + MLA_SLOT)
            s = lax.dot_general(q_ref[:, cols], k_ref[:, cols], (((1,), (1,)), ((), ())),
                                preferred_element_type=F32)
            m = jnp.max(s, axis=-1, keepdims=True)
            p = jnp.exp2(s - m)
            l = jnp.sum(p, axis=-1, keepdims=True)
            o = jnp.dot(p.astype(BF16), v, preferred_element_type=F32)
            outs.append(o / l)
        lane = lax.broadcasted_iota(jnp.int32, outs[0].shape, 1)
        o_ref[:, pair * LANES:(pair + 1) * LANES] = jnp.where(lane < MLA_V_DIM, outs[0], outs[1]).astype(BF16)


def _latent_attention(qb, kb, vb):
    batch = qb.shape[0]
    pairs = MLA_HEADS // 2 // MLA_PAIRS_PER_STEP
    qk_w = MLA_PAIRS_PER_STEP * 2 * MLA_SLOT
    v_w = MLA_PAIRS_PER_STEP * LANES
    est = (2 * MLA_Q_TILE * qk_w * 2 + 2 * SEQ * qk_w * 2 + 2 * SEQ * v_w * 2
           + 2 * MLA_Q_TILE * v_w * 2 + 4 * MLA_PAIRS_PER_STEP * MLA_Q_TILE * SEQ * 4)
    return pl.pallas_call(
        _latent_kernel,
        grid=(batch, pairs, SEQ // MLA_Q_TILE),
        in_specs=[
            pl.BlockSpec((None, MLA_Q_TILE, qk_w), lambda b, p, i: (b, i, p)),
            pl.BlockSpec((None, SEQ, qk_w), lambda b, p, i: (b, 0, p)),
            pl.BlockSpec((None, SEQ, v_w), lambda b, p, i: (b, 0, p)),
        ],
        out_specs=pl.BlockSpec((None, MLA_Q_TILE, v_w), lambda b, p, i: (b, i, p)),
        out_shape=jax.ShapeDtypeStruct((batch, SEQ, MLA_WIDTH), BF16),
        compiler_params=pltpu.CompilerParams(
            dimension_semantics=("arbitrary", "arbitrary", "arbitrary"),
            vmem_limit_bytes=_vmem_limit(est)),
        name="latent_attn",
    )(qb, kb, vb)


def _mem_kv_kernel(mem_ref, w_ref, o_ref):
    o_ref[...] = jnp.dot(mem_ref[...].astype(BF16), w_ref[...], preferred_element_type=F32).astype(BF16)


def _mem_kv(mem, w_mem):
    batch = mem.shape[0]
    est = 2 * N_MEM * D_MODEL * 4 + 2 * w_mem.size * 2 + 2 * N_MEM * 2 * MEM_WIDTH * 2 + N_MEM * D_MODEL * 8
    return pl.pallas_call(
        _mem_kv_kernel,
        grid=(batch,),
        in_specs=[pl.BlockSpec((None, N_MEM, D_MODEL), lambda b: (b, 0, 0)),
                  pl.BlockSpec(w_mem.shape, lambda b: (0, 0))],
        out_specs=pl.BlockSpec((None, N_MEM, 2 * MEM_WIDTH), lambda b: (b, 0, 0)),
        out_shape=jax.ShapeDtypeStruct((batch, N_MEM, 2 * MEM_WIDTH), BF16),
        compiler_params=pltpu.CompilerParams(
            dimension_semantics=("arbitrary",), vmem_limit_bytes=_vmem_limit(est)),
        name="mem_kv",
    )(mem, w_mem)


def _output_kernel(x_ref, ya_ref, yb_ref, gate_ref, mq_ref, mkv_ref, wout_ref,
                   goa_ref, gob_ref, gom_ref, gemb_ref, bemb_ref, gpost_ref, bpost_ref,
                   o_ref, y_scr):
    for hd in range(MEM_HEADS):
        cols = slice(hd * MEM_HEAD_DIM, (hd + 1) * MEM_HEAD_DIM)
        s = lax.dot_general(mq_ref[:, cols], mkv_ref[:, cols], (((1,), (1,)), ((), ())),
                            preferred_element_type=F32)
        m = jnp.max(s, axis=-1, keepdims=True)
        p = jnp.exp(s - m)
        l = jnp.sum(p, axis=-1, keepdims=True)
        o = jnp.dot(p.astype(BF16), mkv_ref[:, MEM_WIDTH + hd * MEM_HEAD_DIM:MEM_WIDTH + (hd + 1) * MEM_HEAD_DIM],
                    preferred_element_type=F32)
        y_scr[:, cols] = o / l

    off_b = A_WIDTH
    off_m = A_WIDTH + MLA_WIDTH
    ya = _rms_rows(ya_ref[...].astype(F32), goa_ref[...]) * gate_ref[:, :off_b].astype(F32)
    yb = _rms_rows(yb_ref[...].astype(F32), gob_ref[...]) * gate_ref[:, off_b:off_m].astype(F32)
    ym = _rms_rows(y_scr[...], gom_ref[...]) * gate_ref[:, off_m:].astype(F32)
    sub = (jnp.dot(ya.astype(BF16), wout_ref[:off_b, :], preferred_element_type=F32)
           + jnp.dot(yb.astype(BF16), wout_ref[off_b:off_m, :], preferred_element_type=F32)
           + jnp.dot(ym.astype(BF16), wout_ref[off_m:, :], preferred_element_type=F32))

    for c in range(N_CHUNKS):
        h = _layer_norm_rows(x_ref[_chunk(c), :], gemb_ref[...], bemb_ref[...])
        z = DEEPNORM_ALPHA * h + sub[c * ROW_CHUNK:(c + 1) * ROW_CHUNK, :]
        o_ref[_chunk(c), :] = _layer_norm_rows(z, gpost_ref[...], bpost_ref[...])


def _output_stage(x, ya, yb, gates, mq, mkv, w_out, g_out_a, g_out_b, g_out_m, g_emb, b_emb,
                  g_post, b_post):
    batch = x.shape[0]

    def rows_spec(width):
        return pl.BlockSpec((None, ROW_TILE, width), lambda b, i: (b, i, 0))

    def full_spec(arr):
        return pl.BlockSpec(arr.shape, lambda b, i: (0,) * arr.ndim)

    est = (4 * ROW_TILE * D_MODEL * 4 + 2 * ROW_TILE * (A_WIDTH + MLA_WIDTH + D_MIX + MEM_WIDTH) * 2
           + 2 * N_MEM * 2 * MEM_WIDTH * 2 + 2 * w_out.size * 2 + ROW_TILE * MEM_WIDTH * 4
           + 4 * ROW_TILE * D_MODEL * 4)
    return pl.pallas_call(
        _output_kernel,
        grid=(batch, N_ROW_TILES),
        in_specs=[
            rows_spec(D_MODEL), rows_spec(A_WIDTH), rows_spec(MLA_WIDTH), rows_spec(D_MIX),
            rows_spec(MEM_WIDTH),
            pl.BlockSpec((None, N_MEM, 2 * MEM_WIDTH), lambda b, i: (b, 0, 0)),
            full_spec(w_out), full_spec(g_out_a), full_spec(g_out_b), full_spec(g_out_m),
            full_spec(g_emb), full_spec(b_emb), full_spec(g_post), full_spec(b_post),
        ],
        out_specs=rows_spec(D_MODEL),
        out_shape=jax.ShapeDtypeStruct(x.shape, F32),
        scratch_shapes=[pltpu.VMEM((ROW_TILE, MEM_WIDTH), F32)],
        compiler_params=pltpu.CompilerParams(
            dimension_semantics=("arbitrary", "arbitrary"), vmem_limit_bytes=_vmem_limit(est)),
        name="output_stage",
    )(x, ya, yb, gates, mq, mkv, w_out, g_out_a, g_out_b, g_out_m, g_emb, b_emb, g_post, b_post)


def _rope_lane_tables(rot_dim, first_lane, period):
    half = rot_dim // 2
    inv_freq = (np.float32(ROPE_THETA) ** (-(np.arange(0, rot_dim, 2, dtype=np.float32) / np.float32(rot_dim)))
                ).astype(np.float32)
    freq = np.repeat(inv_freq[:, None], LANES, axis=1)
    mask = np.zeros((8, LANES), np.float32)
    for lane in range(LANES):
        rel = (lane % period) - first_lane
        if 0 <= rel < rot_dim:
            mask[1, lane] = 1.0
            mask[2 if rel < half else 3, lane] = 1.0
    return jnp.asarray(freq), jnp.asarray(mask)


def kernel(x, mem, positions, g_emb, b_emb, w_in, g_cq, g_ckv, w_uq, w_ukv, w_mem_kv, g_out_a, g_out_b,
           g_out_m, w_out, g_post, b_post):
    batch = x.shape[0]
    assert x.shape == (batch, SEQ, D_MODEL) and w_in.shape[0] == DEPTH == 1
    row = lambda v: v.reshape(1, -1).astype(F32)

    splits = [int(i) for i in np.cumsum(IN_SPLITS)[:-1]]
    a_q, a_k, a_v, a_g, c_q, c_kv, b_kr, b_g, m_q, m_g = jnp.split(w_in[0], splits, axis=1)
    w_qkv = jnp.concatenate([a_q, a_k, a_v], axis=1).astype(BF16)
    kr_slot = jnp.pad(b_kr, ((0, 0), (MLA_NOPE_DIM, MLA_SLOT - MLA_NOPE_DIM - MLA_ROPE_DIM)))
    w_b = jnp.concatenate([a_g, b_g, m_g, m_q, c_q, c_kv, kr_slot], axis=1).astype(BF16)
    qk_dim = MLA_NOPE_DIM + MLA_ROPE_DIM
    w_uq_p = jnp.pad(w_uq[0].reshape(MLA_Q_RANK, MLA_HEADS, qk_dim),
                     ((0, 0), (0, 0), (0, MLA_SLOT - qk_dim))).reshape(MLA_Q_RANK, _MLA_QK_W).astype(BF16)
    ukv = w_ukv[0].reshape(MLA_KV_RANK, MLA_HEADS, MLA_NOPE_DIM + MLA_V_DIM)
    w_uk_p = jnp.pad(ukv[:, :, :MLA_NOPE_DIM], ((0, 0), (0, 0), (0, MLA_SLOT - MLA_NOPE_DIM)))
    w_ukv_p = jnp.concatenate([w_uk_p.reshape(MLA_KV_RANK, _MLA_QK_W),
                               ukv[:, :, MLA_NOPE_DIM:].reshape(MLA_KV_RANK, MLA_WIDTH)], axis=1).astype(BF16)

    pos_view = positions.astype(F32).reshape(batch, N_ROW_TILES, N_CHUNKS, ROW_CHUNK)
    freq_a, mask_a = _rope_lane_tables(A_ROT_DIM, 0, A_HEAD_DIM)
    freq_b, mask_b = _rope_lane_tables(MLA_ROPE_DIM, MLA_NOPE_DIM, MLA_SLOT)
    g_emb_r, b_emb_r = row(g_emb), row(b_emb)

    outs_a = _proj_a(x, pos_view, freq_a, mask_a, g_emb_r, b_emb_r, w_qkv)
    y_a = _dilated_attention(outs_a[0:3], outs_a[3:6], outs_a[6:9])

    gates, mq, qb, kb, vb = _proj_b(x, pos_view, freq_b, mask_b, g_emb_r, b_emb_r, w_b, row(g_cq[0]),
                                    row(g_ckv[0]), w_uq_p, w_ukv_p)
    y_b = _latent_attention(qb, kb, vb)

    mkv = _mem_kv(mem, w_mem_kv[0].astype(BF16))
    return _output_stage(x, y_a, y_b, gates, mq, mkv, w_out[0].astype(BF16), row(g_out_a[0]),
                         row(g_out_b[0]), row(g_out_m[0]), g_emb_r, b_emb_r, row(g_post[0]), row(b_post[0]))
```

```python
import math

import jax
import jax.numpy as jnp
import numpy as np
from jax import lax
from jax.experimental import pallas as pl
from jax.experimental.pallas import tpu as pltpu

D_MODEL = 1024
SEQ = 2048
A_HEADS = 16
A_HEAD_DIM = 64
A_WIDTH = A_HEADS * A_HEAD_DIM
A_ROT_DIM = A_HEAD_DIM // 4
A_SIDE = 64
DILATIONS = (1, 4, 16)
MLA_HEADS = 8
MLA_Q_RANK = 256
MLA_KV_RANK = 128
MLA_NOPE_DIM = 64
MLA_ROPE_DIM = 32
MLA_V_DIM = 64
MLA_WIDTH = MLA_HEADS * MLA_V_DIM
N_MEM = 256
MEM_HEADS = 4
MEM_HEAD_DIM = 128
MEM_WIDTH = MEM_HEADS * MEM_HEAD_DIM
D_MIX = A_WIDTH + MLA_WIDTH + MEM_WIDTH
ROPE_THETA = 500000.0
NORM_EPS = 1e-5
NEG_INF = -1e30
DEPTH = 1
DEEPNORM_ALPHA = (2 * DEPTH) ** 0.25
IN_SPLITS = (A_WIDTH, A_WIDTH, A_WIDTH, A_WIDTH, MLA_Q_RANK, MLA_KV_RANK, MLA_ROPE_DIM, MLA_WIDTH,
             MEM_WIDTH, MEM_WIDTH)
LOG2_E = math.log2(math.e)

LANES = 128
VMEM_BYTES_V7X = 64 * 1024 * 1024

ROW_TILE = 512
ROW_CHUNK = 128
N_CHUNKS = ROW_TILE // ROW_CHUNK
N_HALVES = 2
ROW_HALF = ROW_TILE // N_HALVES
CHUNKS_PER_HALF = N_CHUNKS // N_HALVES
N_ROW_TILES = SEQ // ROW_TILE
A_PAIRS = A_WIDTH // LANES
A_Q_TILE = 128
A_K_WIN = 2 * A_Q_TILE
MLA_SLOT = LANES
MLA_Q_TILE = 256
MLA_PAIRS_PER_STEP = 4

F32 = jnp.float32
BF16 = jnp.bfloat16


def _vmem_limit(nbytes):
    return int(min(VMEM_BYTES_V7X - (4 << 20), max(32 << 20, nbytes + (8 << 20))))


def _layer_norm_rows(x, g, b):
    mu = jnp.mean(x, axis=-1, keepdims=True)
    xc = x - mu
    var = jnp.mean(xc * xc, axis=-1, keepdims=True)
    return xc * lax.rsqrt(var + NORM_EPS) * g + b


def _rms_rows(x, g):
    ms = jnp.mean(x * x, axis=-1, keepdims=True)
    return x * lax.rsqrt(ms + NORM_EPS) * g


def _chunk(c):
    return slice(c * ROW_CHUNK, (c + 1) * ROW_CHUNK)


def _normed_chunks_to_scratch(x_ref, g_ref, b_ref, h_scr, chunks):
    for c in chunks:
        h_scr[_chunk(c), :] = _layer_norm_rows(x_ref[_chunk(c), :], g_ref[...], b_ref[...]).astype(BF16)


def _rope_tables(pos_ref, freq_ref, mask_ref, spread, chunks, cos_scr, sin1_scr, sin2_scr):
    n_freq = freq_ref.shape[0]
    rot = mask_ref[1:2, :] > 0.5
    first = mask_ref[2:3, :] > 0.5
    second = mask_ref[3:4, :] > 0.5
    pad = jnp.zeros((ROW_CHUNK - n_freq, LANES), F32)
    for c in chunks:
        ang = freq_ref[...] * pos_ref[c:c + 1, :]
        cos = spread(jnp.concatenate([jnp.cos(ang), pad], axis=0).T)
        sin = spread(jnp.concatenate([jnp.sin(ang), pad], axis=0).T)
        cos_scr[_chunk(c), :] = jnp.where(rot, cos, 1.0)
        sin1_scr[_chunk(c), :] = jnp.where(first, -sin, 0.0)
        sin2_scr[_chunk(c), :] = jnp.where(second, sin, 0.0)


def _spread_a(t):
    t = t + pltpu.roll(t, A_ROT_DIM // 2, 1)
    return t + pltpu.roll(t, A_HEAD_DIM, 1)


def _spread_b(t):
    t = pltpu.roll(t, MLA_NOPE_DIM, 1)
    return t + pltpu.roll(t, MLA_ROPE_DIM // 2, 1)


def _rope_block(x, rows, half, cos_scr, sin1_scr, sin2_scr):
    return (x * cos_scr[rows, :]
            + pltpu.roll(x, LANES - half, 1) * sin1_scr[rows, :]
            + pltpu.roll(x, half, 1) * sin2_scr[rows, :])


def _proj_a_kernel(x_ref, pos_ref, freq_ref, mask_ref, g_ref, b_ref, w_ref,
                   qn_ref, kn_ref, vn_ref, q4_ref, k4_ref, v4_ref, q16_ref, k16_ref, v16_ref,
                   h_scr, acc_scr, st4_scr, cos_scr, sin1_scr, sin2_scr):
    outs = ((qn_ref, q4_ref, q16_ref), (kn_ref, k4_ref, k16_ref), (vn_ref, v4_ref, v16_ref))
    q_scale = (A_HEAD_DIM ** -0.5) * LOG2_E
    for hf in range(N_HALVES):
        chunks = range(hf * CHUNKS_PER_HALF, (hf + 1) * CHUNKS_PER_HALF)
        half_rows = slice(hf * ROW_HALF, (hf + 1) * ROW_HALF)
        _normed_chunks_to_scratch(x_ref, g_ref, b_ref, h_scr, chunks)
        _rope_tables(pos_ref, freq_ref, mask_ref, _spread_a, chunks, cos_scr, sin1_scr, sin2_scr)
        for sec, (nat_ref, r4_ref, r16_ref) in enumerate(outs):
            res = jnp.dot(h_scr[half_rows, :], w_ref[:, sec * A_WIDTH:(sec + 1) * A_WIDTH],
                          preferred_element_type=F32)
            acc = acc_scr.at[hf * len(outs) + sec]
            st4 = st4_scr.at[hf * len(outs) + sec]
            for cg in range(A_PAIRS):
                cols = slice(cg * LANES, (cg + 1) * LANES)
                for lc in range(CHUNKS_PER_HALF):
                    rows = _chunk(hf * CHUNKS_PER_HALF + lc)
                    blk = res[lc * ROW_CHUNK:(lc + 1) * ROW_CHUNK, cols]
                    if sec < 2:
                        blk = _rope_block(blk, rows, A_ROT_DIM // 2, cos_scr, sin1_scr, sin2_scr)
                        if sec == 0:
                            blk = blk * q_scale
                    acc[cg, lc * ROW_CHUNK:(lc + 1) * ROW_CHUNK, :] = blk
                    nat_ref[rows, cols] = blk.astype(BF16)
                n4, n16 = ROW_HALF // 4, ROW_HALF // 16
                for r in range(4):
                    rows4 = acc[cg, pl.ds(r, n4, stride=4), :]
                    st4[cg, r] = rows4
                    r4_ref[r, hf * n4:(hf + 1) * n4, cols] = rows4.astype(BF16)
                for r in range(4):
                    for v in range(4):
                        r16_ref[r + 4 * v, hf * n16:(hf + 1) * n16, cols] = (
                            st4[cg, r, pl.ds(v, n16, stride=4), :].astype(BF16))


def _proj_a(x, pos_view, freq_a, mask_a, g_emb, b_emb, w_qkv):
    batch = x.shape[0]
    nat = jax.ShapeDtypeStruct((batch, SEQ, A_WIDTH), BF16)
    r4 = jax.ShapeDtypeStruct((batch, 4, SEQ // 4, A_WIDTH), BF16)
    r16 = jax.ShapeDtypeStruct((batch, 16, SEQ // 16, A_WIDTH), BF16)
    nat_spec = pl.BlockSpec((None, ROW_TILE, A_WIDTH), lambda b, i: (b, i, 0))
    r4_spec = pl.BlockSpec((None, 4, ROW_TILE // 4, A_WIDTH), lambda b, i: (b, 0, i, 0))
    r16_spec = pl.BlockSpec((None, 16, ROW_TILE // 16, A_WIDTH), lambda b, i: (b, 0, i, 0))
    tile_out = ROW_TILE * A_WIDTH * 2
    est = (2 * ROW_TILE * D_MODEL * 4 + 2 * w_qkv.size * 2 + 2 * 9 * tile_out
           + ROW_TILE * D_MODEL * 2 + 2 * 3 * ROW_TILE * A_WIDTH * 4 + 3 * ROW_TILE * LANES * 4
           + 2 * ROW_HALF * A_WIDTH * 4)
    return pl.pallas_call(
        _proj_a_kernel,
        grid=(batch, N_ROW_TILES),
        in_specs=[
            pl.BlockSpec((None, ROW_TILE, D_MODEL), lambda b, i: (b, i, 0)),
            pl.BlockSpec((None, None, N_CHUNKS, ROW_CHUNK), lambda b, i: (b, i, 0, 0)),
            pl.BlockSpec(freq_a.shape, lambda b, i: (0, 0)),
            pl.BlockSpec((8, LANES), lambda b, i: (0, 0)),
            pl.BlockSpec((1, D_MODEL), lambda b, i: (0, 0)),
            pl.BlockSpec((1, D_MODEL), lambda b, i: (0, 0)),
            pl.BlockSpec(w_qkv.shape, lambda b, i: (0, 0)),
        ],
        out_specs=[nat_spec] * 3 + [r4_spec] * 3 + [r16_spec] * 3,
        out_shape=[nat] * 3 + [r4] * 3 + [r16] * 3,
        scratch_shapes=[
            pltpu.VMEM((ROW_TILE, D_MODEL), BF16),
            pltpu.VMEM((N_HALVES * 3, A_PAIRS, ROW_HALF, LANES), F32),
            pltpu.VMEM((N_HALVES * 3, A_PAIRS, 4, ROW_HALF // 4, LANES), F32),
            pltpu.VMEM((ROW_TILE, LANES), F32),
            pltpu.VMEM((ROW_TILE, LANES), F32),
            pltpu.VMEM((ROW_TILE, LANES), F32),
        ],
        compiler_params=pltpu.CompilerParams(
            dimension_semantics=("arbitrary", "arbitrary"), vmem_limit_bytes=_vmem_limit(est)),
        name="proj_a",
    )(x, pos_view, freq_a, mask_a, g_emb, b_emb, w_qkv)


_GATE_W = D_MIX
_OFF_MQ = _GATE_W
_OFF_CQ = _OFF_MQ + MEM_WIDTH
_OFF_CKV = _OFF_CQ + MLA_Q_RANK
_OFF_KR = _OFF_CKV + MLA_KV_RANK
_MLA_QK_W = MLA_HEADS * MLA_SLOT


def _proj_b_kernel(x_ref, pos_ref, freq_ref, mask_ref, g_ref, b_ref, w_ref, gcq_ref, gckv_ref, wuq_ref,
                   wukv_ref, gate_ref, mq_ref, qb_ref, kb_ref, vb_ref,
                   h_scr, cos_scr, sin1_scr, sin2_scr):
    mla_scale = ((MLA_NOPE_DIM + MLA_ROPE_DIM) ** -0.5) * LOG2_E
    for hf in range(N_HALVES):
        chunks = range(hf * CHUNKS_PER_HALF, (hf + 1) * CHUNKS_PER_HALF)
        half_rows = slice(hf * ROW_HALF, (hf + 1) * ROW_HALF)
        local = lambda lc: slice(lc * ROW_CHUNK, (lc + 1) * ROW_CHUNK)
        _normed_chunks_to_scratch(x_ref, g_ref, b_ref, h_scr, chunks)
        _rope_tables(pos_ref, freq_ref, mask_ref, _spread_b, chunks, cos_scr, sin1_scr, sin2_scr)
        h = h_scr[half_rows, :]

        for n in range(_GATE_W // A_WIDTH):
            g = jnp.dot(h, w_ref[:, n * A_WIDTH:(n + 1) * A_WIDTH], preferred_element_type=F32)
            gate_ref[half_rows, n * A_WIDTH:(n + 1) * A_WIDTH] = (g / (1.0 + jnp.exp(-g))).astype(BF16)

        mq = jnp.dot(h, w_ref[:, _OFF_MQ:_OFF_MQ + MEM_WIDTH], preferred_element_type=F32)
        mq_ref[half_rows, :] = (mq * (MEM_HEAD_DIM ** -0.5)).astype(BF16)

        cq = jnp.dot(h, w_ref[:, _OFF_CQ:_OFF_CQ + MLA_Q_RANK], preferred_element_type=F32)
        cqn = _rms_rows(cq, gcq_ref[...]).astype(BF16)
        q_all = jnp.dot(cqn, wuq_ref[...], preferred_element_type=F32)
        for lc in range(CHUNKS_PER_HALF):
            rows = _chunk(hf * CHUNKS_PER_HALF + lc)
            for hd in range(MLA_HEADS):
                cols = slice(hd * MLA_SLOT, (hd + 1) * MLA_SLOT)
                blk = _rope_block(q_all[local(lc), cols], rows, MLA_ROPE_DIM // 2, cos_scr, sin1_scr, sin2_scr)
                qb_ref[rows, cols] = (blk * mla_scale).astype(BF16)

        ckv = jnp.dot(h, w_ref[:, _OFF_CKV:_OFF_CKV + MLA_KV_RANK], preferred_element_type=F32)
        ckvn = _rms_rows(ckv, gckv_ref[...]).astype(BF16)
        k_all = jnp.dot(ckvn, wukv_ref[:, :_MLA_QK_W], preferred_element_type=F32)
        vb = jnp.dot(ckvn, wukv_ref[:, _MLA_QK_W:], preferred_element_type=F32)
        vb_ref[half_rows, :] = vb.astype(BF16)
        kr = jnp.dot(h, w_ref[:, _OFF_KR:_OFF_KR + MLA_SLOT], preferred_element_type=F32)
        for lc in range(CHUNKS_PER_HALF):
            rows = _chunk(hf * CHUNKS_PER_HALF + lc)
            kr_c = kr[local(lc), :]
            for hd in range(MLA_HEADS):
                cols = slice(hd * MLA_SLOT, (hd + 1) * MLA_SLOT)
                blk = _rope_block(k_all[local(lc), cols] + kr_c, rows, MLA_ROPE_DIM // 2,
                                  cos_scr, sin1_scr, sin2_scr)
                kb_ref[rows, cols] = blk.astype(BF16)


def _proj_b(x, pos_view, freq_b, mask_b, g_emb, b_emb, w_b, g_cq, g_ckv, w_uq_p, w_ukv_p):
    batch = x.shape[0]

    def rows_spec(width):
        return pl.BlockSpec((None, ROW_TILE, width), lambda b, i: (b, i, 0))

    def full_spec(arr):
        return pl.BlockSpec(arr.shape, lambda b, i: (0,) * arr.ndim)

    widths = (_GATE_W, MEM_WIDTH, _MLA_QK_W, _MLA_QK_W, MLA_WIDTH)
    est = (2 * ROW_TILE * D_MODEL * 4 + 2 * (w_b.size + w_uq_p.size + w_ukv_p.size) * 2
           + 2 * ROW_TILE * sum(widths) * 2 + ROW_TILE * D_MODEL * 2 + ROW_TILE * A_WIDTH * 4
           + 3 * ROW_TILE * LANES * 4 + 2 * ROW_TILE * A_WIDTH * 4)
    return pl.pallas_call(
        _proj_b_kernel,
        grid=(batch, N_ROW_TILES),
        in_specs=[
            rows_spec(D_MODEL),
            pl.BlockSpec((None, None, N_CHUNKS, ROW_CHUNK), lambda b, i: (b, i, 0, 0)),
            full_spec(freq_b), full_spec(mask_b),
            full_spec(g_emb), full_spec(b_emb), full_spec(w_b), full_spec(g_cq), full_spec(g_ckv),
            full_spec(w_uq_p), full_spec(w_ukv_p),
        ],
        out_specs=[rows_spec(w) for w in widths],
        out_shape=[jax.ShapeDtypeStruct((batch, SEQ, w), BF16) for w in widths],
        scratch_shapes=[
            pltpu.VMEM((ROW_TILE, D_MODEL), BF16),
            pltpu.VMEM((ROW_TILE, LANES), F32),
            pltpu.VMEM((ROW_TILE, LANES), F32),
            pltpu.VMEM((ROW_TILE, LANES), F32),
        ],
        compiler_params=pltpu.CompilerParams(
            dimension_semantics=("arbitrary", "arbitrary"), vmem_limit_bytes=_vmem_limit(est)),
        name="proj_b",
    )(x, pos_view, freq_b, mask_b, g_emb, b_emb, w_b, g_cq, g_ckv, w_uq_p, w_ukv_p)


def _first_head_lanes(n_rows):
    return lax.broadcasted_iota(jnp.int32, (n_rows, LANES), 1) < A_HEAD_DIM


def _band_scores(q, k, bias):
    n_q = q.shape[0]
    first_head = _first_head_lanes(n_q)
    zero = jnp.zeros_like(q)
    q2 = jnp.concatenate([jnp.where(first_head, q, zero), jnp.where(first_head, zero, q)], axis=0)
    s = lax.dot_general(q2, k, (((1,), (1,)), ((), ())), preferred_element_type=F32) + bias
    m = jnp.max(s, axis=-1, keepdims=True)
    p = jnp.exp2((s - m).astype(BF16))
    return p, jnp.where(first_head, m[:n_q], m[n_q:])


def _band_values(p, v_ones):
    n_q = p.shape[0] // 2
    first_head = _first_head_lanes(n_q)
    o = jnp.dot(p, v_ones, preferred_element_type=F32)
    acc = jnp.where(first_head, o[:n_q, :LANES], o[n_q:, :LANES])
    l_b = jnp.where(first_head, o[:n_q, LANES:], o[n_q:, LANES:])
    return acc, l_b


def _dilated_kernel(qn_ref, kn_ref, vn_ref, q4_ref, k4_ref, v4_ref, q16_ref, k16_ref, v16_ref,
                    o_ref, von_scr, vo4_scr, vo16_scr, bias_scr, bias16_scr, acc_scr, mx_scr, sm_scr):
    def window_bias(n_keys, first_key):
        row = lax.broadcasted_iota(jnp.int32, (2 * A_Q_TILE, n_keys), 0)
        col = lax.broadcasted_iota(jnp.int32, (2 * A_Q_TILE, n_keys), 1)
        q_idx = jnp.where(row >= A_Q_TILE, row - A_Q_TILE, row)
        off = col - (q_idx + first_key)
        return jnp.where(jnp.abs(off) <= A_SIDE, 0.0, NEG_INF).astype(F32)

    @pl.when((pl.program_id(0) == 0) & (pl.program_id(1) == 0))
    def _():
        for variant in range(3):
            bias_scr[variant] = window_bias(A_K_WIN, variant * A_SIDE)
        bias16_scr[...] = window_bias(A_Q_TILE, 0)
        von_scr[:, LANES:] = jnp.ones((SEQ, LANES), BF16)
        vo4_scr[:, :, LANES:] = jnp.ones((4, SEQ // 4, LANES), BF16)
        vo16_scr[:, :, LANES:] = jnp.ones((16, SEQ // 16, LANES), BF16)

    von_scr[:, :LANES] = vn_ref[...]
    vo4_scr[:, :, :LANES] = v4_ref[...]
    vo16_scr[:, :, :LANES] = v16_ref[...]

    len4 = SEQ // 4
    tiles4 = len4 // A_Q_TILE
    n_tiles = SEQ // A_Q_TILE

    def window(tile, n_seq_tiles, seq_len):
        qs = tile * A_Q_TILE
        ks = min(max(qs - A_SIDE, 0), seq_len - A_K_WIN)
        variant = 0 if tile == 0 else (2 if tile == n_seq_tiles - 1 else 1)
        return qs, ks, variant

    def score_half(i):
        qs, ks, var = window(i, n_tiles, SEQ)
        res = i // tiles4
        qs4, ks4, var4 = window(i % tiles4, tiles4, len4)
        parts = (
            _band_scores(qn_ref[pl.ds(qs, A_Q_TILE), :], kn_ref[pl.ds(ks, A_K_WIN), :], bias_scr[var]),
            _band_scores(q4_ref[res, pl.ds(qs4, A_Q_TILE), :], k4_ref[res, pl.ds(ks4, A_K_WIN), :], bias_scr[var4]),
            _band_scores(q16_ref[i], k16_ref[i], bias16_scr[...]),
        )
        dests = (pl.ds(qs, A_Q_TILE), pl.ds(res + 4 * qs4, A_Q_TILE, stride=4), pl.ds(i, A_Q_TILE, stride=16))
        v_wins = (von_scr.at[pl.ds(ks, A_K_WIN), :], vo4_scr.at[res, pl.ds(ks4, A_K_WIN), :], vo16_scr.at[i])
        for pat, (dst, (_, m_b)) in enumerate(zip(dests, parts)):
            mx_scr[pat, dst, :] = m_b
        return [(p, dst, v_win) for (p, _), dst, v_win in zip(parts, dests, v_wins)]

    def value_half(pending):
        for pat, (p, dst, v_win) in enumerate(pending):
            acc, l_b = _band_values(p, v_win[...])
            acc_scr[pat, dst, :] = acc
            sm_scr[pat, dst, :] = l_b

    pending = score_half(0)
    for i in range(1, n_tiles):
        ahead = score_half(i)
        value_half(pending)
        pending = ahead
    value_half(pending)

    def merge(c, carry):
        rows = pl.ds(pl.multiple_of(c * ROW_CHUNK, ROW_CHUNK), ROW_CHUNK)
        m_all = jnp.maximum(jnp.maximum(mx_scr[0, rows, :], mx_scr[1, rows, :]), mx_scr[2, rows, :])
        num = jnp.zeros((ROW_CHUNK, LANES), F32)
        den = jnp.zeros((ROW_CHUNK, LANES), F32)
        for pat in range(3):
            w = jnp.exp2(mx_scr[pat, rows, :] - m_all)
            num = num + w * acc_scr[pat, rows, :]
            den = den + w * sm_scr[pat, rows, :]
        o_ref[rows, :] = (num / den).astype(BF16)
        return carry

    lax.fori_loop(0, SEQ // ROW_CHUNK, merge, 0)


def _dilated_attention(qkv_nat, qkv_4, qkv_16):
    batch = qkv_nat[0].shape[0]
    nat_spec = pl.BlockSpec((None, SEQ, LANES), lambda b, p: (b, 0, p))
    r4_spec = pl.BlockSpec((None, 4, SEQ // 4, LANES), lambda b, p: (b, 0, 0, p))
    r16_spec = pl.BlockSpec((None, 16, SEQ // 16, LANES), lambda b, p: (b, 0, 0, p))
    blk = SEQ * LANES
    est = (2 * 9 * blk * 2 + 2 * blk * 2 + 3 * blk * 2 * 2 + 9 * blk * 4
           + 4 * 2 * A_Q_TILE * A_K_WIN * 4)
    return pl.pallas_call(
        _dilated_kernel,
        grid=(batch, A_PAIRS),
        in_specs=[nat_spec] * 3 + [r4_spec] * 3 + [r16_spec] * 3,
        out_specs=pl.BlockSpec((None, SEQ, LANES), lambda b, p: (b, 0, p)),
        out_shape=jax.ShapeDtypeStruct((batch, SEQ, A_WIDTH), BF16),
        scratch_shapes=[
            pltpu.VMEM((SEQ, 2 * LANES), BF16),
            pltpu.VMEM((4, SEQ // 4, 2 * LANES), BF16),
            pltpu.VMEM((16, SEQ // 16, 2 * LANES), BF16),
            pltpu.VMEM((3, 2 * A_Q_TILE, A_K_WIN), F32),
            pltpu.VMEM((2 * A_Q_TILE, A_Q_TILE), F32),
            pltpu.VMEM((3, SEQ, LANES), F32),
            pltpu.VMEM((3, SEQ, LANES), F32),
            pltpu.VMEM((3, SEQ, LANES), F32),
        ],
        compiler_params=pltpu.CompilerParams(
            dimension_semantics=("arbitrary", "arbitrary"), vmem_limit_bytes=_vmem_limit(est)),
        name="dilated_attn",
    )(*qkv_nat, *qkv_4, *qkv_16)


def _latent_kernel(q_ref, k_ref, v_ref, o_ref, vo_scr):
    @pl.when((pl.program_id(0) == 0) & (pl.program_id(1) == 0) & (pl.program_id(2) == 0))
    def _():
        vo_scr[:, :, LANES:] = jnp.ones((MLA_PAIRS_PER_STEP, SEQ, LANES), BF16)

    @pl.when(pl.program_id(2) == 0)
    def _():
        for pair in range(MLA_PAIRS_PER_STEP):
            vo_scr[pair, :, :LANES] = v_ref[:, pair * LANES:(pair + 1) * LANES]

    def score_half(head):
        cols = slice(head * MLA_SLOT, (head + 1) * MLA_SLOT)
        s = lax.dot_general(q_ref[:, cols], k_ref[:, cols], (((1,), (1,)), ((), ())),
                            preferred_element_type=F32)
        m = jnp.max(s, axis=-1, keepdims=True)
        return jnp.exp2((s - m).astype(BF16))

    def value_half(head, p):
        o = jnp.dot(p, vo_scr[head // 2], preferred_element_type=F32)
        return o[:, :LANES] / o[:, LANES:]

    n_heads = 2 * MLA_PAIRS_PER_STEP
    lane = lax.broadcasted_iota(jnp.int32, (MLA_Q_TILE, LANES), 1)
    outs = []
    p_next = score_half(0)
    for head in range(n_heads):
        p = p_next
        if head + 1 < n_heads:
            p_next = score_half(head + 1)
        outs.append(value_half(head, p))
        if head % 2 == 1:
            pair = head // 2
            o_ref[:, pair * LANES:(pair + 1) * LANES] = (
                jnp.where(lane < MLA_V_DIM, outs[head - 1], outs[head]).astype(BF16))


def _latent_attention(qb, kb, vb):
    batch = qb.shape[0]
    pairs = MLA_HEADS // 2 // MLA_PAIRS_PER_STEP
    qk_w = MLA_PAIRS_PER_STEP * 2 * MLA_SLOT
    v_w = MLA_PAIRS_PER_STEP * LANES
    est = (2 * MLA_Q_TILE * qk_w * 2 + 2 * SEQ * qk_w * 2 + 2 * SEQ * v_w * 2 + SEQ * 2 * v_w * 2
           + 2 * MLA_Q_TILE * v_w * 2 + 4 * MLA_PAIRS_PER_STEP * MLA_Q_TILE * SEQ * 4)
    return pl.pallas_call(
        _latent_kernel,
        grid=(batch, pairs, SEQ // MLA_Q_TILE),
        in_specs=[
            pl.BlockSpec((None, MLA_Q_TILE, qk_w), lambda b, p, i: (b, i, p)),
            pl.BlockSpec((None, SEQ, qk_w), lambda b, p, i: (b, 0, p)),
            pl.BlockSpec((None, SEQ, v_w), lambda b, p, i: (b, 0, p)),
        ],
        out_specs=pl.BlockSpec((None, MLA_Q_TILE, v_w), lambda b, p, i: (b, i, p)),
        out_shape=jax.ShapeDtypeStruct((batch, SEQ, MLA_WIDTH), BF16),
        scratch_shapes=[pltpu.VMEM((MLA_PAIRS_PER_STEP, SEQ, 2 * LANES), BF16)],
        compiler_params=pltpu.CompilerParams(
            dimension_semantics=("arbitrary", "arbitrary", "arbitrary"),
            vmem_limit_bytes=_vmem_limit(est)),
        name="latent_attn",
    )(qb, kb, vb)


def _mem_kv_kernel(mem_ref, w_ref, o_ref):
    o_ref[...] = jnp.dot(mem_ref[...].astype(BF16), w_ref[...], preferred_element_type=F32).astype(BF16)


def _mem_kv(mem, w_mem):
    batch = mem.shape[0]
    est = 2 * N_MEM * D_MODEL * 4 + 2 * w_mem.size * 2 + 2 * N_MEM * 2 * MEM_WIDTH * 2 + N_MEM * D_MODEL * 8
    return pl.pallas_call(
        _mem_kv_kernel,
        grid=(batch,),
        in_specs=[pl.BlockSpec((None, N_MEM, D_MODEL), lambda b: (b, 0, 0)),
                  pl.BlockSpec(w_mem.shape, lambda b: (0, 0))],
        out_specs=pl.BlockSpec((None, N_MEM, 2 * MEM_WIDTH), lambda b: (b, 0, 0)),
        out_shape=jax.ShapeDtypeStruct((batch, N_MEM, 2 * MEM_WIDTH), BF16),
        compiler_params=pltpu.CompilerParams(
            dimension_semantics=("arbitrary",), vmem_limit_bytes=_vmem_limit(est)),
        name="mem_kv",
    )(mem, w_mem)


def _output_kernel(x_ref, ya_ref, yb_ref, gate_ref, mq_ref, mkv_ref, wout_ref,
                   goa_ref, gob_ref, gom_ref, gemb_ref, bemb_ref, gpost_ref, bpost_ref,
                   o_ref, y_scr):
    for hd in range(MEM_HEADS):
        cols = slice(hd * MEM_HEAD_DIM, (hd + 1) * MEM_HEAD_DIM)
        s = lax.dot_general(mq_ref[:, cols], mkv_ref[:, cols], (((1,), (1,)), ((), ())),
                            preferred_element_type=F32)
        m = jnp.max(s, axis=-1, keepdims=True)
        p = jnp.exp(s - m)
        l = jnp.sum(p, axis=-1, keepdims=True)
        o = jnp.dot(p.astype(BF16), mkv_ref[:, MEM_WIDTH + hd * MEM_HEAD_DIM:MEM_WIDTH + (hd + 1) * MEM_HEAD_DIM],
                    preferred_element_type=F32)
        y_scr[:, cols] = o / l

    off_b = A_WIDTH
    off_m = A_WIDTH + MLA_WIDTH
    ya = _rms_rows(ya_ref[...].astype(F32), goa_ref[...]) * gate_ref[:, :off_b].astype(F32)
    yb = _rms_rows(yb_ref[...].astype(F32), gob_ref[...]) * gate_ref[:, off_b:off_m].astype(F32)
    ym = _rms_rows(y_scr[...], gom_ref[...]) * gate_ref[:, off_m:].astype(F32)
    sub = (jnp.dot(ya.astype(BF16), wout_ref[:off_b, :], preferred_element_type=F32)
           + jnp.dot(yb.astype(BF16), wout_ref[off_b:off_m, :], preferred_element_type=F32)
           + jnp.dot(ym.astype(BF16), wout_ref[off_m:, :], preferred_element_type=F32))

    for c in range(N_CHUNKS):
        h = _layer_norm_rows(x_ref[_chunk(c), :], gemb_ref[...], bemb_ref[...])
        z = DEEPNORM_ALPHA * h + sub[c * ROW_CHUNK:(c + 1) * ROW_CHUNK, :]
        o_ref[_chunk(c), :] = _layer_norm_rows(z, gpost_ref[...], bpost_ref[...])


def _output_stage(x, ya, yb, gates, mq, mkv, w_out, g_out_a, g_out_b, g_out_m, g_emb, b_emb,
                  g_post, b_post):
    batch = x.shape[0]

    def rows_spec(width):
        return pl.BlockSpec((None, ROW_TILE, width), lambda b, i: (b, i, 0))

    def full_spec(arr):
        return pl.BlockSpec(arr.shape, lambda b, i: (0,) * arr.ndim)

    est = (4 * ROW_TILE * D_MODEL * 4 + 2 * ROW_TILE * (A_WIDTH + MLA_WIDTH + D_MIX + MEM_WIDTH) * 2
           + 2 * N_MEM * 2 * MEM_WIDTH * 2 + 2 * w_out.size * 2 + ROW_TILE * MEM_WIDTH * 4
           + 4 * ROW_TILE * D_MODEL * 4)
    return pl.pallas_call(
        _output_kernel,
        grid=(batch, N_ROW_TILES),
        in_specs=[
            rows_spec(D_MODEL), rows_spec(A_WIDTH), rows_spec(MLA_WIDTH), rows_spec(D_MIX),
            rows_spec(MEM_WIDTH),
            pl.BlockSpec((None, N_MEM, 2 * MEM_WIDTH), lambda b, i: (b, 0, 0)),
            full_spec(w_out), full_spec(g_out_a), full_spec(g_out_b), full_spec(g_out_m),
            full_spec(g_emb), full_spec(b_emb), full_spec(g_post), full_spec(b_post),
        ],
        out_specs=rows_spec(D_MODEL),
        out_shape=jax.ShapeDtypeStruct(x.shape, F32),
        scratch_shapes=[pltpu.VMEM((ROW_TILE, MEM_WIDTH), F32)],
        compiler_params=pltpu.CompilerParams(
            dimension_semantics=("arbitrary", "arbitrary"), vmem_limit_bytes=_vmem_limit(est)),
        name="output_stage",
    )(x, ya, yb, gates, mq, mkv, w_out, g_out_a, g_out_b, g_out_m, g_emb, b_emb, g_post, b_post)


def _rope_lane_tables(rot_dim, first_lane, period):
    half = rot_dim // 2
    inv_freq = (np.float32(ROPE_THETA) ** (-(np.arange(0, rot_dim, 2, dtype=np.float32) / np.float32(rot_dim)))
                ).astype(np.float32)
    freq = np.repeat(inv_freq[:, None], LANES, axis=1)
    mask = np.zeros((8, LANES), np.float32)
    for lane in range(LANES):
        rel = (lane % period) - first_lane
        if 0 <= rel < rot_dim:
            mask[1, lane] = 1.0
            mask[2 if rel < half else 3, lane] = 1.0
    return jnp.asarray(freq), jnp.asarray(mask)


def kernel(x, mem, positions, g_emb, b_emb, w_in, g_cq, g_ckv, w_uq, w_ukv, w_mem_kv, g_out_a, g_out_b,
           g_out_m, w_out, g_post, b_post):
    batch = x.shape[0]
    assert x.shape == (batch, SEQ, D_MODEL) and w_in.shape[0] == DEPTH == 1
    row = lambda v: v.reshape(1, -1).astype(F32)

    splits = [int(i) for i in np.cumsum(IN_SPLITS)[:-1]]
    a_q, a_k, a_v, a_g, c_q, c_kv, b_kr, b_g, m_q, m_g = jnp.split(w_in[0], splits, axis=1)
    w_qkv = jnp.concatenate([a_q, a_k, a_v], axis=1).astype(BF16)
    kr_slot = jnp.pad(b_kr, ((0, 0), (MLA_NOPE_DIM, MLA_SLOT - MLA_NOPE_DIM - MLA_ROPE_DIM)))
    w_b = jnp.concatenate([a_g, b_g, m_g, m_q, c_q, c_kv, kr_slot], axis=1).astype(BF16)
    qk_dim = MLA_NOPE_DIM + MLA_ROPE_DIM
    w_uq_p = jnp.pad(w_uq[0].reshape(MLA_Q_RANK, MLA_HEADS, qk_dim),
                     ((0, 0), (0, 0), (0, MLA_SLOT - qk_dim))).reshape(MLA_Q_RANK, _MLA_QK_W).astype(BF16)
    ukv = w_ukv[0].reshape(MLA_KV_RANK, MLA_HEADS, MLA_NOPE_DIM + MLA_V_DIM)
    w_uk_p = jnp.pad(ukv[:, :, :MLA_NOPE_DIM], ((0, 0), (0, 0), (0, MLA_SLOT - MLA_NOPE_DIM)))
    w_ukv_p = jnp.concatenate([w_uk_p.reshape(MLA_KV_RANK, _MLA_QK_W),
                               ukv[:, :, MLA_NOPE_DIM:].reshape(MLA_KV_RANK, MLA_WIDTH)], axis=1).astype(BF16)

    pos_view = positions.astype(F32).reshape(batch, N_ROW_TILES, N_CHUNKS, ROW_CHUNK)
    freq_a, mask_a = _rope_lane_tables(A_ROT_DIM, 0, A_HEAD_DIM)
    freq_b, mask_b = _rope_lane_tables(MLA_ROPE_DIM, MLA_NOPE_DIM, MLA_SLOT)
    g_emb_r, b_emb_r = row(g_emb), row(b_emb)

    outs_a = _proj_a(x, pos_view, freq_a, mask_a, g_emb_r, b_emb_r, w_qkv)
    y_a = _dilated_attention(outs_a[0:3], outs_a[3:6], outs_a[6:9])

    gates, mq, qb, kb, vb = _proj_b(x, pos_view, freq_b, mask_b, g_emb_r, b_emb_r, w_b, row(g_cq[0]),
                                    row(g_ckv[0]), w_uq_p, w_ukv_p)
    y_b = _latent_attention(qb, kb, vb)

    mkv = _mem_kv(mem, w_mem_kv[0].astype(BF16))
    return _output_stage(x, y_a, y_b, gates, mq, mkv, w_out[0].astype(BF16), row(g_out_a[0]),
                         row(g_out_b[0]), row(g_out_m[0]), g_emb_r, b_emb_r, row(g_post[0]), row(b_post[0]))
```

```python
import math

import jax
import jax.numpy as jnp
import numpy as np
from jax import lax
from jax.experimental import pallas as pl
from jax.experimental.pallas import tpu as pltpu

D_MODEL = 1024
SEQ = 2048
A_HEADS = 16
A_HEAD_DIM = 64
A_WIDTH = A_HEADS * A_HEAD_DIM
A_ROT_DIM = A_HEAD_DIM // 4
A_SIDE = 64
DILATIONS = (1, 4, 16)
MLA_HEADS = 8
MLA_Q_RANK = 256
MLA_KV_RANK = 128
MLA_NOPE_DIM = 64
MLA_ROPE_DIM = 32
MLA_V_DIM = 64
MLA_WIDTH = MLA_HEADS * MLA_V_DIM
N_MEM = 256
MEM_HEADS = 4
MEM_HEAD_DIM = 128
MEM_WIDTH = MEM_HEADS * MEM_HEAD_DIM
D_MIX = A_WIDTH + MLA_WIDTH + MEM_WIDTH
ROPE_THETA = 500000.0
NORM_EPS = 1e-5
NEG_INF = -1e30
DEPTH = 1
DEEPNORM_ALPHA = (2 * DEPTH) ** 0.25
IN_SPLITS = (A_WIDTH, A_WIDTH, A_WIDTH, A_WIDTH, MLA_Q_RANK, MLA_KV_RANK, MLA_ROPE_DIM, MLA_WIDTH,
             MEM_WIDTH, MEM_WIDTH)
LOG2_E = math.log2(math.e)

LANES = 128
VMEM_BYTES_V7X = 64 * 1024 * 1024

ROW_TILE = 512
ROW_CHUNK = 128
N_CHUNKS = ROW_TILE // ROW_CHUNK
N_HALVES = 2
ROW_HALF = ROW_TILE // N_HALVES
CHUNKS_PER_HALF = N_CHUNKS // N_HALVES
N_ROW_TILES = SEQ // ROW_TILE
A_PAIRS = A_WIDTH // LANES
A_Q_TILE = 128
A_K_WIN = 2 * A_Q_TILE
MLA_SLOT = LANES
MLA_Q_TILE = 256
MLA_PAIRS_PER_STEP = 4

F32 = jnp.float32
BF16 = jnp.bfloat16


def _vmem_limit(nbytes):
    return int(min(VMEM_BYTES_V7X - (4 << 20), max(32 << 20, nbytes + (8 << 20))))


def _layer_norm_rows(x, g, b):
    mu = jnp.mean(x, axis=-1, keepdims=True)
    xc = x - mu
    var = jnp.mean(xc * xc, axis=-1, keepdims=True)
    return xc * lax.rsqrt(var + NORM_EPS) * g + b


def _rms_rows(x, g):
    ms = jnp.mean(x * x, axis=-1, keepdims=True)
    return x * lax.rsqrt(ms + NORM_EPS) * g


def _chunk(c):
    return slice(c * ROW_CHUNK, (c + 1) * ROW_CHUNK)


def _normed_chunks_to_scratch(x_ref, g_ref, b_ref, h_scr, chunks):
    for c in chunks:
        h_scr[_chunk(c), :] = _layer_norm_rows(x_ref[_chunk(c), :], g_ref[...], b_ref[...]).astype(BF16)


def _rope_tables(pos_ref, freq_ref, mask_ref, spread, chunks, cos_scr, sin1_scr, sin2_scr):
    n_freq = freq_ref.shape[0]
    rot = mask_ref[1:2, :] > 0.5
    first = mask_ref[2:3, :] > 0.5
    second = mask_ref[3:4, :] > 0.5
    pad = jnp.zeros((ROW_CHUNK - n_freq, LANES), F32)
    for c in chunks:
        ang = freq_ref[...] * pos_ref[c:c + 1, :]
        cos = spread(jnp.concatenate([jnp.cos(ang), pad], axis=0).T)
        sin = spread(jnp.concatenate([jnp.sin(ang), pad], axis=0).T)
        cos_scr[_chunk(c), :] = jnp.where(rot, cos, 1.0)
        sin1_scr[_chunk(c), :] = jnp.where(first, -sin, 0.0)
        sin2_scr[_chunk(c), :] = jnp.where(second, sin, 0.0)


def _spread_a(t):
    t = t + pltpu.roll(t, A_ROT_DIM // 2, 1)
    return t + pltpu.roll(t, A_HEAD_DIM, 1)


def _spread_b(t):
    t = pltpu.roll(t, MLA_NOPE_DIM, 1)
    return t + pltpu.roll(t, MLA_ROPE_DIM // 2, 1)


def _rope_block(x, rows, half, cos_scr, sin1_scr, sin2_scr):
    return (x * cos_scr[rows, :]
            + pltpu.roll(x, LANES - half, 1) * sin1_scr[rows, :]
            + pltpu.roll(x, half, 1) * sin2_scr[rows, :])


def _proj_a_kernel(x_ref, pos_ref, freq_ref, mask_ref, g_ref, b_ref, w_ref,
                   qn_ref, kn_ref, vn_ref, q4_ref, k4_ref, v4_ref, q16_ref, k16_ref, v16_ref,
                   h_scr, acc_scr, st4_scr, cos_scr, sin1_scr, sin2_scr):
    outs = ((qn_ref, q4_ref, q16_ref), (kn_ref, k4_ref, k16_ref), (vn_ref, v4_ref, v16_ref))
    q_scale = (A_HEAD_DIM ** -0.5) * LOG2_E

    def prepare(hf):
        chunks = range(hf * CHUNKS_PER_HALF, (hf + 1) * CHUNKS_PER_HALF)
        _normed_chunks_to_scratch(x_ref, g_ref, b_ref, h_scr, chunks)
        _rope_tables(pos_ref, freq_ref, mask_ref, _spread_a, chunks, cos_scr, sin1_scr, sin2_scr)

    def matmul(hf, sec):
        if sec == 1 and hf + 1 < N_HALVES:
            prepare(hf + 1)
        return jnp.dot(h_scr[hf * ROW_HALF:(hf + 1) * ROW_HALF, :],
                       w_ref[:, sec * A_WIDTH:(sec + 1) * A_WIDTH], preferred_element_type=F32)

    def epilogue(hf, sec, res):
        nat_ref, r4_ref, r16_ref = outs[sec]
        acc = acc_scr.at[hf * len(outs) + sec]
        st4 = st4_scr.at[hf * len(outs) + sec]
        for cg in range(A_PAIRS):
            cols = slice(cg * LANES, (cg + 1) * LANES)
            for lc in range(CHUNKS_PER_HALF):
                rows = _chunk(hf * CHUNKS_PER_HALF + lc)
                blk = res[lc * ROW_CHUNK:(lc + 1) * ROW_CHUNK, cols]
                if sec < 2:
                    blk = _rope_block(blk, rows, A_ROT_DIM // 2, cos_scr, sin1_scr, sin2_scr)
                    if sec == 0:
                        blk = blk * q_scale
                acc[cg, lc * ROW_CHUNK:(lc + 1) * ROW_CHUNK, :] = blk
                nat_ref[rows, cols] = blk.astype(BF16)
            n4, n16 = ROW_HALF // 4, ROW_HALF // 16
            for r in range(4):
                rows4 = acc[cg, pl.ds(r, n4, stride=4), :]
                st4[cg, r] = rows4
                r4_ref[r, hf * n4:(hf + 1) * n4, cols] = rows4.astype(BF16)
            for r in range(4):
                for v in range(4):
                    r16_ref[r + 4 * v, hf * n16:(hf + 1) * n16, cols] = (
                        st4[cg, r, pl.ds(v, n16, stride=4), :].astype(BF16))

    prepare(0)
    items = [(hf, sec) for hf in range(N_HALVES) for sec in range(len(outs))]
    pending = None
    for hf, sec in items:
        res = matmul(hf, sec)
        if pending is not None:
            epilogue(*pending)
        pending = (hf, sec, res)
    epilogue(*pending)


def _proj_a(x, pos_view, freq_a, mask_a, g_emb, b_emb, w_qkv):
    batch = x.shape[0]
    nat = jax.ShapeDtypeStruct((batch, SEQ, A_WIDTH), BF16)
    r4 = jax.ShapeDtypeStruct((batch, 4, SEQ // 4, A_WIDTH), BF16)
    r16 = jax.ShapeDtypeStruct((batch, 16, SEQ // 16, A_WIDTH), BF16)
    nat_spec = pl.BlockSpec((None, ROW_TILE, A_WIDTH), lambda b, i: (b, i, 0))
    r4_spec = pl.BlockSpec((None, 4, ROW_TILE // 4, A_WIDTH), lambda b, i: (b, 0, i, 0))
    r16_spec = pl.BlockSpec((None, 16, ROW_TILE // 16, A_WIDTH), lambda b, i: (b, 0, i, 0))
    tile_out = ROW_TILE * A_WIDTH * 2
    est = (2 * ROW_TILE * D_MODEL * 4 + 2 * w_qkv.size * 2 + 2 * 9 * tile_out
           + ROW_TILE * D_MODEL * 2 + 2 * 3 * ROW_TILE * A_WIDTH * 4 + 3 * ROW_TILE * LANES * 4
           + 2 * ROW_HALF * A_WIDTH * 4)
    return pl.pallas_call(
        _proj_a_kernel,
        grid=(batch, N_ROW_TILES),
        in_specs=[
            pl.BlockSpec((None, ROW_TILE, D_MODEL), lambda b, i: (b, i, 0)),
            pl.BlockSpec((None, None, N_CHUNKS, ROW_CHUNK), lambda b, i: (b, i, 0, 0)),
            pl.BlockSpec(freq_a.shape, lambda b, i: (0, 0)),
            pl.BlockSpec((8, LANES), lambda b, i: (0, 0)),
            pl.BlockSpec((1, D_MODEL), lambda b, i: (0, 0)),
            pl.BlockSpec((1, D_MODEL), lambda b, i: (0, 0)),
            pl.BlockSpec(w_qkv.shape, lambda b, i: (0, 0)),
        ],
        out_specs=[nat_spec] * 3 + [r4_spec] * 3 + [r16_spec] * 3,
        out_shape=[nat] * 3 + [r4] * 3 + [r16] * 3,
        scratch_shapes=[
            pltpu.VMEM((ROW_TILE, D_MODEL), BF16),
            pltpu.VMEM((N_HALVES * 3, A_PAIRS, ROW_HALF, LANES), F32),
            pltpu.VMEM((N_HALVES * 3, A_PAIRS, 4, ROW_HALF // 4, LANES), F32),
            pltpu.VMEM((ROW_TILE, LANES), F32),
            pltpu.VMEM((ROW_TILE, LANES), F32),
            pltpu.VMEM((ROW_TILE, LANES), F32),
        ],
        compiler_params=pltpu.CompilerParams(
            dimension_semantics=("arbitrary", "arbitrary"), vmem_limit_bytes=_vmem_limit(est)),
        name="proj_a",
    )(x, pos_view, freq_a, mask_a, g_emb, b_emb, w_qkv)


_GATE_W = D_MIX
_OFF_MQ = _GATE_W
_OFF_CQ = _OFF_MQ + MEM_WIDTH
_OFF_CKV = _OFF_CQ + MLA_Q_RANK
_OFF_KR = _OFF_CKV + MLA_KV_RANK
_MLA_QK_W = MLA_HEADS * MLA_SLOT


def _proj_b_kernel(x_ref, pos_ref, freq_ref, mask_ref, g_ref, b_ref, w_ref, gcq_ref, gckv_ref, wuq_ref,
                   wukv_ref, gate_ref, mq_ref, qb_ref, kb_ref, vb_ref,
                   h_scr, cos_scr, sin1_scr, sin2_scr):
    mla_scale = ((MLA_NOPE_DIM + MLA_ROPE_DIM) ** -0.5) * LOG2_E
    local = lambda lc: slice(lc * ROW_CHUNK, (lc + 1) * ROW_CHUNK)
    half_rows = lambda hf: slice(hf * ROW_HALF, (hf + 1) * ROW_HALF)

    def prepare(hf):
        chunks = range(hf * CHUNKS_PER_HALF, (hf + 1) * CHUNKS_PER_HALF)
        _normed_chunks_to_scratch(x_ref, g_ref, b_ref, h_scr, chunks)
        _rope_tables(pos_ref, freq_ref, mask_ref, _spread_b, chunks, cos_scr, sin1_scr, sin2_scr)

    def h_dot(hf, first, width):
        return jnp.dot(h_scr[half_rows(hf), :], w_ref[:, first:first + width], preferred_element_type=F32)

    def rope_slots(hf, slots, out_ref, scale=None, extra=None):
        for lc in range(CHUNKS_PER_HALF):
            rows = _chunk(hf * CHUNKS_PER_HALF + lc)
            for hd in range(MLA_HEADS):
                cols = slice(hd * MLA_SLOT, (hd + 1) * MLA_SLOT)
                blk = slots[local(lc), cols]
                if extra is not None:
                    blk = blk + extra[local(lc), :]
                blk = _rope_block(blk, rows, MLA_ROPE_DIM // 2, cos_scr, sin1_scr, sin2_scr)
                out_ref[rows, cols] = (blk if scale is None else blk * scale).astype(BF16)

    def stages(hf):
        state = {}

        def down():
            lat = h_dot(hf, _OFF_CQ, MLA_Q_RANK + MLA_KV_RANK + MLA_SLOT)
            state["cqn"] = _rms_rows(lat[:, :MLA_Q_RANK], gcq_ref[...]).astype(BF16)
            state["ckvn"] = _rms_rows(lat[:, MLA_Q_RANK:MLA_Q_RANK + MLA_KV_RANK], gckv_ref[...]).astype(BF16)
            state["kr"] = lat[:, MLA_Q_RANK + MLA_KV_RANK:]

        def queries():
            return jnp.dot(state["cqn"], wuq_ref[...], preferred_element_type=F32)

        def keys_values():
            k_all = jnp.dot(state["ckvn"], wukv_ref[:, :_MLA_QK_W], preferred_element_type=F32)
            return k_all, jnp.dot(state["ckvn"], wukv_ref[:, _MLA_QK_W:], preferred_element_type=F32)

        def store_keys_values(res):
            k_all, vb = res
            vb_ref[half_rows(hf), :] = vb.astype(BF16)
            rope_slots(hf, k_all, kb_ref, extra=state["kr"])

        def gate_block(n):
            def produce():
                if n == 0 and hf + 1 < N_HALVES:
                    prepare(hf + 1)
                return h_dot(hf, n * A_WIDTH, A_WIDTH)

            def consume(g):
                gate_ref[half_rows(hf), n * A_WIDTH:(n + 1) * A_WIDTH] = (g / (1.0 + jnp.exp(-g))).astype(BF16)

            return produce, consume

        def store_mq(mq):
            mq_ref[half_rows(hf), :] = (mq * (MEM_HEAD_DIM ** -0.5)).astype(BF16)

        return [
            (down, lambda _: None),
            (queries, lambda q_all: rope_slots(hf, q_all, qb_ref, scale=mla_scale)),
            (keys_values, store_keys_values),
            *[gate_block(n) for n in range(_GATE_W // A_WIDTH)],
            (lambda: h_dot(hf, _OFF_MQ, MEM_WIDTH), store_mq),
        ]

    prepare(0)
    pending = None
    for produce, consume in [st for hf in range(N_HALVES) for st in stages(hf)]:
        res = produce()
        if pending is not None:
            pending[0](pending[1])
        pending = (consume, res)
    pending[0](pending[1])


def _proj_b(x, pos_view, freq_b, mask_b, g_emb, b_emb, w_b, g_cq, g_ckv, w_uq_p, w_ukv_p):
    batch = x.shape[0]

    def rows_spec(width):
        return pl.BlockSpec((None, ROW_TILE, width), lambda b, i: (b, i, 0))

    def full_spec(arr):
        return pl.BlockSpec(arr.shape, lambda b, i: (0,) * arr.ndim)

    widths = (_GATE_W, MEM_WIDTH, _MLA_QK_W, _MLA_QK_W, MLA_WIDTH)
    est = (2 * ROW_TILE * D_MODEL * 4 + 2 * (w_b.size + w_uq_p.size + w_ukv_p.size) * 2
           + 2 * ROW_TILE * sum(widths) * 2 + ROW_TILE * D_MODEL * 2 + ROW_TILE * A_WIDTH * 4
           + 3 * ROW_TILE * LANES * 4 + 2 * ROW_TILE * A_WIDTH * 4)
    return pl.pallas_call(
        _proj_b_kernel,
        grid=(batch, N_ROW_TILES),
        in_specs=[
            rows_spec(D_MODEL),
            pl.BlockSpec((None, None, N_CHUNKS, ROW_CHUNK), lambda b, i: (b, i, 0, 0)),
            full_spec(freq_b), full_spec(mask_b),
            full_spec(g_emb), full_spec(b_emb), full_spec(w_b), full_spec(g_cq), full_spec(g_ckv),
            full_spec(w_uq_p), full_spec(w_ukv_p),
        ],
        out_specs=[rows_spec(w) for w in widths],
        out_shape=[jax.ShapeDtypeStruct((batch, SEQ, w), BF16) for w in widths],
        scratch_shapes=[
            pltpu.VMEM((ROW_TILE, D_MODEL), BF16),
            pltpu.VMEM((ROW_TILE, LANES), F32),
            pltpu.VMEM((ROW_TILE, LANES), F32),
            pltpu.VMEM((ROW_TILE, LANES), F32),
        ],
        compiler_params=pltpu.CompilerParams(
            dimension_semantics=("arbitrary", "arbitrary"), vmem_limit_bytes=_vmem_limit(est)),
        name="proj_b",
    )(x, pos_view, freq_b, mask_b, g_emb, b_emb, w_b, g_cq, g_ckv, w_uq_p, w_ukv_p)


def _first_head_lanes(n_rows):
    return lax.broadcasted_iota(jnp.int32, (n_rows, LANES), 1) < A_HEAD_DIM


def _band_scores(q, k, bias):
    n_q = q.shape[0]
    first_head = _first_head_lanes(n_q)
    zero = jnp.zeros_like(q)
    q2 = jnp.concatenate([jnp.where(first_head, q, zero), jnp.where(first_head, zero, q)], axis=0)
    s = lax.dot_general(q2, k, (((1,), (1,)), ((), ())), preferred_element_type=F32) + bias
    m = jnp.max(s, axis=-1, keepdims=True)
    p = jnp.exp2((s - m).astype(BF16))
    return p, jnp.where(first_head, m[:n_q], m[n_q:])


def _band_values(p, v_ones):
    n_q = p.shape[0] // 2
    first_head = _first_head_lanes(n_q)
    o = jnp.dot(p, v_ones, preferred_element_type=F32)
    acc = jnp.where(first_head, o[:n_q, :LANES], o[n_q:, :LANES])
    l_b = jnp.where(first_head, o[:n_q, LANES:], o[n_q:, LANES:])
    return acc, l_b


def _dilated_kernel(qn_ref, kn_ref, vn_ref, q4_ref, k4_ref, v4_ref, q16_ref, k16_ref, v16_ref,
                    o_ref, von_scr, vo4_scr, vo16_scr, bias_scr, bias16_scr, acc_scr, mx_scr, sm_scr):
    def window_bias(n_keys, first_key):
        row = lax.broadcasted_iota(jnp.int32, (2 * A_Q_TILE, n_keys), 0)
        col = lax.broadcasted_iota(jnp.int32, (2 * A_Q_TILE, n_keys), 1)
        q_idx = jnp.where(row >= A_Q_TILE, row - A_Q_TILE, row)
        off = col - (q_idx + first_key)
        return jnp.where(jnp.abs(off) <= A_SIDE, 0.0, NEG_INF).astype(F32)

    @pl.when((pl.program_id(0) == 0) & (pl.program_id(1) == 0))
    def _():
        for variant in range(3):
            bias_scr[variant] = window_bias(A_K_WIN, variant * A_SIDE)
        bias16_scr[...] = window_bias(A_Q_TILE, 0)
        von_scr[:, LANES:] = jnp.ones((SEQ, LANES), BF16)
        vo4_scr[:, :, LANES:] = jnp.ones((4, SEQ // 4, LANES), BF16)
        vo16_scr[:, :, LANES:] = jnp.ones((16, SEQ // 16, LANES), BF16)

    von_scr[:, :LANES] = vn_ref[...]
    vo4_scr[:, :, :LANES] = v4_ref[...]
    vo16_scr[:, :, :LANES] = v16_ref[...]

    len4 = SEQ // 4
    tiles4 = len4 // A_Q_TILE
    n_tiles = SEQ // A_Q_TILE

    def window(tile, n_seq_tiles, seq_len):
        qs = tile * A_Q_TILE
        ks = min(max(qs - A_SIDE, 0), seq_len - A_K_WIN)
        variant = 0 if tile == 0 else (2 if tile == n_seq_tiles - 1 else 1)
        return qs, ks, variant

    def score_half(i):
        qs, ks, var = window(i, n_tiles, SEQ)
        res = i // tiles4
        qs4, ks4, var4 = window(i % tiles4, tiles4, len4)
        parts = (
            _band_scores(qn_ref[pl.ds(qs, A_Q_TILE), :], kn_ref[pl.ds(ks, A_K_WIN), :], bias_scr[var]),
            _band_scores(q4_ref[res, pl.ds(qs4, A_Q_TILE), :], k4_ref[res, pl.ds(ks4, A_K_WIN), :], bias_scr[var4]),
            _band_scores(q16_ref[i], k16_ref[i], bias16_scr[...]),
        )
        dests = (pl.ds(qs, A_Q_TILE), pl.ds(res + 4 * qs4, A_Q_TILE, stride=4), pl.ds(i, A_Q_TILE, stride=16))
        v_wins = (von_scr.at[pl.ds(ks, A_K_WIN), :], vo4_scr.at[res, pl.ds(ks4, A_K_WIN), :], vo16_scr.at[i])
        for pat, (dst, (_, m_b)) in enumerate(zip(dests, parts)):
            mx_scr[pat, dst, :] = m_b
        return [(p, dst, v_win) for (p, _), dst, v_win in zip(parts, dests, v_wins)]

    def value_half(pending):
        for pat, (p, dst, v_win) in enumerate(pending):
            acc, l_b = _band_values(p, v_win[...])
            acc_scr[pat, dst, :] = acc
            sm_scr[pat, dst, :] = l_b

    pending = score_half(0)
    for i in range(1, n_tiles):
        ahead = score_half(i)
        value_half(pending)
        pending = ahead
    value_half(pending)

    def merge(c, carry):
        rows = pl.ds(pl.multiple_of(c * ROW_CHUNK, ROW_CHUNK), ROW_CHUNK)
        m_all = jnp.maximum(jnp.maximum(mx_scr[0, rows, :], mx_scr[1, rows, :]), mx_scr[2, rows, :])
        num = jnp.zeros((ROW_CHUNK, LANES), F32)
        den = jnp.zeros((ROW_CHUNK, LANES), F32)
        for pat in range(3):
            w = jnp.exp2(mx_scr[pat, rows, :] - m_all)
            num = num + w * acc_scr[pat, rows, :]
            den = den + w * sm_scr[pat, rows, :]
        o_ref[rows, :] = (num / den).astype(BF16)
        return carry

    lax.fori_loop(0, SEQ // ROW_CHUNK, merge, 0, unroll=4)


def _dilated_attention(qkv_nat, qkv_4, qkv_16):
    batch = qkv_nat[0].shape[0]
    nat_spec = pl.BlockSpec((None, SEQ, LANES), lambda b, p: (b, 0, p))
    r4_spec = pl.BlockSpec((None, 4, SEQ // 4, LANES), lambda b, p: (b, 0, 0, p))
    r16_spec = pl.BlockSpec((None, 16, SEQ // 16, LANES), lambda b, p: (b, 0, 0, p))
    blk = SEQ * LANES
    est = (2 * 9 * blk * 2 + 2 * blk * 2 + 3 * blk * 2 * 2 + 9 * blk * 4
           + 4 * 2 * A_Q_TILE * A_K_WIN * 4)
    return pl.pallas_call(
        _dilated_kernel,
        grid=(batch, A_PAIRS),
        in_specs=[nat_spec] * 3 + [r4_spec] * 3 + [r16_spec] * 3,
        out_specs=pl.BlockSpec((None, SEQ, LANES), lambda b, p: (b, 0, p)),
        out_shape=jax.ShapeDtypeStruct((batch, SEQ, A_WIDTH), BF16),
        scratch_shapes=[
            pltpu.VMEM((SEQ, 2 * LANES), BF16),
            pltpu.VMEM((4, SEQ // 4, 2 * LANES), BF16),
            pltpu.VMEM((16, SEQ // 16, 2 * LANES), BF16),
            pltpu.VMEM((3, 2 * A_Q_TILE, A_K_WIN), F32),
            pltpu.VMEM((2 * A_Q_TILE, A_Q_TILE), F32),
            pltpu.VMEM((3, SEQ, LANES), F32),
            pltpu.VMEM((3, SEQ, LANES), F32),
            pltpu.VMEM((3, SEQ, LANES), F32),
        ],
        compiler_params=pltpu.CompilerParams(
            dimension_semantics=("arbitrary", "arbitrary"), vmem_limit_bytes=_vmem_limit(est)),
        name="dilated_attn",
    )(*qkv_nat, *qkv_4, *qkv_16)


def _latent_kernel(q_ref, k_ref, v_ref, o_ref, vo_scr):
    @pl.when((pl.program_id(0) == 0) & (pl.program_id(1) == 0) & (pl.program_id(2) == 0))
    def _():
        vo_scr[:, :, LANES:] = jnp.ones((MLA_PAIRS_PER_STEP, SEQ, LANES), BF16)

    @pl.when(pl.program_id(2) == 0)
    def _():
        for pair in range(MLA_PAIRS_PER_STEP):
            vo_scr[pair, :, :LANES] = v_ref[:, pair * LANES:(pair + 1) * LANES]

    def score_half(head):
        cols = slice(head * MLA_SLOT, (head + 1) * MLA_SLOT)
        s = lax.dot_general(q_ref[:, cols], k_ref[:, cols], (((1,), (1,)), ((), ())),
                            preferred_element_type=F32)
        m = jnp.max(s, axis=-1, keepdims=True)
        return jnp.exp2((s - m).astype(BF16))

    def value_half(head, p):
        o = jnp.dot(p, vo_scr[head // 2], preferred_element_type=F32)
        return o[:, :LANES] / o[:, LANES:]

    n_heads = 2 * MLA_PAIRS_PER_STEP
    lane = lax.broadcasted_iota(jnp.int32, (MLA_Q_TILE, LANES), 1)
    outs = []
    p_next = score_half(0)
    for head in range(n_heads):
        p = p_next
        if head + 1 < n_heads:
            p_next = score_half(head + 1)
        outs.append(value_half(head, p))
        if head % 2 == 1:
            pair = head // 2
            o_ref[:, pair * LANES:(pair + 1) * LANES] = (
                jnp.where(lane < MLA_V_DIM, outs[head - 1], outs[head]).astype(BF16))


def _latent_attention(qb, kb, vb):
    batch = qb.shape[0]
    pairs = MLA_HEADS // 2 // MLA_PAIRS_PER_STEP
    qk_w = MLA_PAIRS_PER_STEP * 2 * MLA_SLOT
    v_w = MLA_PAIRS_PER_STEP * LANES
    est = (2 * MLA_Q_TILE * qk_w * 2 + 2 * SEQ * qk_w * 2 + 2 * SEQ * v_w * 2 + SEQ * 2 * v_w * 2
           + 2 * MLA_Q_TILE * v_w * 2 + 4 * MLA_PAIRS_PER_STEP * MLA_Q_TILE * SEQ * 4)
    return pl.pallas_call(
        _latent_kernel,
        grid=(batch, pairs, SEQ // MLA_Q_TILE),
        in_specs=[
            pl.BlockSpec((None, MLA_Q_TILE, qk_w), lambda b, p, i: (b, i, p)),
            pl.BlockSpec((None, SEQ, qk_w), lambda b, p, i: (b, 0, p)),
            pl.BlockSpec((None, SEQ, v_w), lambda b, p, i: (b, 0, p)),
        ],
        out_specs=pl.BlockSpec((None, MLA_Q_TILE, v_w), lambda b, p, i: (b, i, p)),
        out_shape=jax.ShapeDtypeStruct((batch, SEQ, MLA_WIDTH), BF16),
        scratch_shapes=[pltpu.VMEM((MLA_PAIRS_PER_STEP, SEQ, 2 * LANES), BF16)],
        compiler_params=pltpu.CompilerParams(
            dimension_semantics=("arbitrary", "arbitrary", "arbitrary"),
            vmem_limit_bytes=_vmem_limit(est)),
        name="latent_attn",
    )(qb, kb, vb)


def _mem_kv_kernel(mem_ref, w_ref, o_ref):
    o_ref[...] = jnp.dot(mem_ref[...].astype(BF16), w_ref[...], preferred_element_type=F32).astype(BF16)


def _mem_kv(mem, w_mem):
    batch = mem.shape[0]
    est = 2 * N_MEM * D_MODEL * 4 + 2 * w_mem.size * 2 + 2 * N_MEM * 2 * MEM_WIDTH * 2 + N_MEM * D_MODEL * 8
    return pl.pallas_call(
        _mem_kv_kernel,
        grid=(batch,),
        in_specs=[pl.BlockSpec((None, N_MEM, D_MODEL), lambda b: (b, 0, 0)),
                  pl.BlockSpec(w_mem.shape, lambda b: (0, 0))],
        out_specs=pl.BlockSpec((None, N_MEM, 2 * MEM_WIDTH), lambda b: (b, 0, 0)),
        out_shape=jax.ShapeDtypeStruct((batch, N_MEM, 2 * MEM_WIDTH), BF16),
        compiler_params=pltpu.CompilerParams(
            dimension_semantics=("arbitrary",), vmem_limit_bytes=_vmem_limit(est)),
        name="mem_kv",
    )(mem, w_mem)


def _output_kernel(x_ref, ya_ref, yb_ref, gate_ref, mq_ref, mkv_ref, wout_ref,
                   goa_ref, gob_ref, gom_ref, gemb_ref, bemb_ref, gpost_ref, bpost_ref,
                   o_ref, y_scr):
    for hd in range(MEM_HEADS):
        cols = slice(hd * MEM_HEAD_DIM, (hd + 1) * MEM_HEAD_DIM)
        s = lax.dot_general(mq_ref[:, cols], mkv_ref[:, cols], (((1,), (1,)), ((), ())),
                            preferred_element_type=F32)
        m = jnp.max(s, axis=-1, keepdims=True)
        p = jnp.exp(s - m)
        l = jnp.sum(p, axis=-1, keepdims=True)
        o = jnp.dot(p.astype(BF16), mkv_ref[:, MEM_WIDTH + hd * MEM_HEAD_DIM:MEM_WIDTH + (hd + 1) * MEM_HEAD_DIM],
                    preferred_element_type=F32)
        y_scr[:, cols] = o / l

    off_b = A_WIDTH
    off_m = A_WIDTH + MLA_WIDTH
    ya = _rms_rows(ya_ref[...].astype(F32), goa_ref[...]) * gate_ref[:, :off_b].astype(F32)
    yb = _rms_rows(yb_ref[...].astype(F32), gob_ref[...]) * gate_ref[:, off_b:off_m].astype(F32)
    ym = _rms_rows(y_scr[...], gom_ref[...]) * gate_ref[:, off_m:].astype(F32)
    sub = (jnp.dot(ya.astype(BF16), wout_ref[:off_b, :], preferred_element_type=F32)
           + jnp.dot(yb.astype(BF16), wout_ref[off_b:off_m, :], preferred_element_type=F32)
           + jnp.dot(ym.astype(BF16), wout_ref[off_m:, :], preferred_element_type=F32))

    for c in range(N_CHUNKS):
        h = _layer_norm_rows(x_ref[_chunk(c), :], gemb_ref[...], bemb_ref[...])
        z = DEEPNORM_ALPHA * h + sub[c * ROW_CHUNK:(c + 1) * ROW_CHUNK, :]
        o_ref[_chunk(c), :] = _layer_norm_rows(z, gpost_ref[...], bpost_ref[...])


def _output_stage(x, ya, yb, gates, mq, mkv, w_out, g_out_a, g_out_b, g_out_m, g_emb, b_emb,
                  g_post, b_post):
    batch = x.shape[0]

    def rows_spec(width):
        return pl.BlockSpec((None, ROW_TILE, width), lambda b, i: (b, i, 0))

    def full_spec(arr):
        return pl.BlockSpec(arr.shape, lambda b, i: (0,) * arr.ndim)

    est = (4 * ROW_TILE * D_MODEL * 4 + 2 * ROW_TILE * (A_WIDTH + MLA_WIDTH + D_MIX + MEM_WIDTH) * 2
           + 2 * N_MEM * 2 * MEM_WIDTH * 2 + 2 * w_out.size * 2 + ROW_TILE * MEM_WIDTH * 4
           + 4 * ROW_TILE * D_MODEL * 4)
    return pl.pallas_call(
        _output_kernel,
        grid=(batch, N_ROW_TILES),
        in_specs=[
            rows_spec(D_MODEL), rows_spec(A_WIDTH), rows_spec(MLA_WIDTH), rows_spec(D_MIX),
            rows_spec(MEM_WIDTH),
            pl.BlockSpec((None, N_MEM, 2 * MEM_WIDTH), lambda b, i: (b, 0, 0)),
            full_spec(w_out), full_spec(g_out_a), full_spec(g_out_b), full_spec(g_out_m),
            full_spec(g_emb), full_spec(b_emb), full_spec(g_post), full_spec(b_post),
        ],
        out_specs=rows_spec(D_MODEL),
        out_shape=jax.ShapeDtypeStruct(x.shape, F32),
        scratch_shapes=[pltpu.VMEM((ROW_TILE, MEM_WIDTH), F32)],
        compiler_params=pltpu.CompilerParams(
            dimension_semantics=("arbitrary", "arbitrary"), vmem_limit_bytes=_vmem_limit(est)),
        name="output_stage",
    )(x, ya, yb, gates, mq, mkv, w_out, g_out_a, g_out_b, g_out_m, g_emb, b_emb, g_post, b_post)


def _rope_lane_tables(rot_dim, first_lane, period):
    half = rot_dim // 2
    inv_freq = (np.float32(ROPE_THETA) ** (-(np.arange(0, rot_dim, 2, dtype=np.float32) / np.float32(rot_dim)))
                ).astype(np.float32)
    freq = np.repeat(inv_freq[:, None], LANES, axis=1)
    mask = np.zeros((8, LANES), np.float32)
    for lane in range(LANES):
        rel = (lane % period) - first_lane
        if 0 <= rel < rot_dim:
            mask[1, lane] = 1.0
            mask[2 if rel < half else 3, lane] = 1.0
    return jnp.asarray(freq), jnp.asarray(mask)


def kernel(x, mem, positions, g_emb, b_emb, w_in, g_cq, g_ckv, w_uq, w_ukv, w_mem_kv, g_out_a, g_out_b,
           g_out_m, w_out, g_post, b_post):
    batch = x.shape[0]
    assert x.shape == (batch, SEQ, D_MODEL) and w_in.shape[0] == DEPTH == 1
    row = lambda v: v.reshape(1, -1).astype(F32)

    splits = [int(i) for i in np.cumsum(IN_SPLITS)[:-1]]
    a_q, a_k, a_v, a_g, c_q, c_kv, b_kr, b_g, m_q, m_g = jnp.split(w_in[0], splits, axis=1)
    w_qkv = jnp.concatenate([a_q, a_k, a_v], axis=1).astype(BF16)
    kr_slot = jnp.pad(b_kr, ((0, 0), (MLA_NOPE_DIM, MLA_SLOT - MLA_NOPE_DIM - MLA_ROPE_DIM)))
    w_b = jnp.concatenate([a_g, b_g, m_g, m_q, c_q, c_kv, kr_slot], axis=1).astype(BF16)
    qk_dim = MLA_NOPE_DIM + MLA_ROPE_DIM
    w_uq_p = jnp.pad(w_uq[0].reshape(MLA_Q_RANK, MLA_HEADS, qk_dim),
                     ((0, 0), (0, 0), (0, MLA_SLOT - qk_dim))).reshape(MLA_Q_RANK, _MLA_QK_W).astype(BF16)
    ukv = w_ukv[0].reshape(MLA_KV_RANK, MLA_HEADS, MLA_NOPE_DIM + MLA_V_DIM)
    w_uk_p = jnp.pad(ukv[:, :, :MLA_NOPE_DIM], ((0, 0), (0, 0), (0, MLA_SLOT - MLA_NOPE_DIM)))
    w_ukv_p = jnp.concatenate([w_uk_p.reshape(MLA_KV_RANK, _MLA_QK_W),
                               ukv[:, :, MLA_NOPE_DIM:].reshape(MLA_KV_RANK, MLA_WIDTH)], axis=1).astype(BF16)

    pos_view = positions.astype(F32).reshape(batch, N_ROW_TILES, N_CHUNKS, ROW_CHUNK)
    freq_a, mask_a = _rope_lane_tables(A_ROT_DIM, 0, A_HEAD_DIM)
    freq_b, mask_b = _rope_lane_tables(MLA_ROPE_DIM, MLA_NOPE_DIM, MLA_SLOT)
    g_emb_r, b_emb_r = row(g_emb), row(b_emb)

    outs_a = _proj_a(x, pos_view, freq_a, mask_a, g_emb_r, b_emb_r, w_qkv)
    y_a = _dilated_attention(outs_a[0:3], outs_a[3:6], outs_a[6:9])

    gates, mq, qb, kb, vb = _proj_b(x, pos_view, freq_b, mask_b, g_emb_r, b_emb_r, w_b, row(g_cq[0]),
                                    row(g_ckv[0]), w_uq_p, w_ukv_p)
    y_b = _latent_attention(qb, kb, vb)

    mkv = _mem_kv(mem, w_mem_kv[0].astype(BF16))
    return _output_stage(x, y_a, y_b, gates, mq, mkv, w_out[0].astype(BF16), row(g_out_a[0]),
                         row(g_out_b[0]), row(g_out_m[0]), g_emb_r, b_emb_r, row(g_post[0]), row(b_post[0]))
```

```python
import math

import jax
import jax.numpy as jnp
import numpy as np
from jax import lax
from jax.experimental import pallas as pl
from jax.experimental.pallas import tpu as pltpu

D_MODEL = 1024
SEQ = 2048
A_HEADS = 16
A_HEAD_DIM = 64
A_WIDTH = A_HEADS * A_HEAD_DIM
A_ROT_DIM = A_HEAD_DIM // 4
A_SIDE = 64
DILATIONS = (1, 4, 16)
MLA_HEADS = 8
MLA_Q_RANK = 256
MLA_KV_RANK = 128
MLA_NOPE_DIM = 64
MLA_ROPE_DIM = 32
MLA_V_DIM = 64
MLA_WIDTH = MLA_HEADS * MLA_V_DIM
N_MEM = 256
MEM_HEADS = 4
MEM_HEAD_DIM = 128
MEM_WIDTH = MEM_HEADS * MEM_HEAD_DIM
D_MIX = A_WIDTH + MLA_WIDTH + MEM_WIDTH
ROPE_THETA = 500000.0
NORM_EPS = 1e-5
NEG_INF = -1e30
DEPTH = 1
DEEPNORM_ALPHA = (2 * DEPTH) ** 0.25
IN_SPLITS = (A_WIDTH, A_WIDTH, A_WIDTH, A_WIDTH, MLA_Q_RANK, MLA_KV_RANK, MLA_ROPE_DIM, MLA_WIDTH,
             MEM_WIDTH, MEM_WIDTH)
LOG2_E = math.log2(math.e)

LANES = 128
VMEM_BYTES_V7X = 64 * 1024 * 1024

ROW_TILE = 512
ROW_CHUNK = 128
N_CHUNKS = ROW_TILE // ROW_CHUNK
N_HALVES = 2
ROW_HALF = ROW_TILE // N_HALVES
CHUNKS_PER_HALF = N_CHUNKS // N_HALVES
N_ROW_TILES = SEQ // ROW_TILE
A_PAIRS = A_WIDTH // LANES
A_Q_TILE = 128
A_K_WIN = 2 * A_Q_TILE
MLA_SLOT = LANES
MLA_Q_TILE = 256
MLA_PAIRS_PER_STEP = 4

F32 = jnp.float32
BF16 = jnp.bfloat16


def _vmem_limit(nbytes):
    return int(min(VMEM_BYTES_V7X - (4 << 20), max(32 << 20, nbytes + (8 << 20))))


def _layer_norm_rows(x, g, b):
    mu = jnp.mean(x, axis=-1, keepdims=True)
    xc = x - mu
    var = jnp.mean(xc * xc, axis=-1, keepdims=True)
    return xc * lax.rsqrt(var + NORM_EPS) * g + b


def _rms_rows(x, g):
    ms = jnp.mean(x * x, axis=-1, keepdims=True)
    return x * lax.rsqrt(ms + NORM_EPS) * g


def _chunk(c):
    return slice(c * ROW_CHUNK, (c + 1) * ROW_CHUNK)


def _normed_chunks_to_scratch(x_ref, g_ref, b_ref, h_scr, chunks):
    for c in chunks:
        h_scr[_chunk(c), :] = _layer_norm_rows(x_ref[_chunk(c), :], g_ref[...], b_ref[...]).astype(BF16)


def _rope_tables(pos_ref, freq_ref, mask_ref, spread, chunks, cos_scr, sin1_scr, sin2_scr):
    n_freq = freq_ref.shape[0]
    rot = mask_ref[1:2, :] > 0.5
    first = mask_ref[2:3, :] > 0.5
    second = mask_ref[3:4, :] > 0.5
    pad = jnp.zeros((ROW_CHUNK - n_freq, LANES), F32)
    for c in chunks:
        ang = freq_ref[...] * pos_ref[c:c + 1, :]
        cos = spread(jnp.concatenate([jnp.cos(ang), pad], axis=0).T)
        sin = spread(jnp.concatenate([jnp.sin(ang), pad], axis=0).T)
        cos_scr[_chunk(c), :] = jnp.where(rot, cos, 1.0)
        sin1_scr[_chunk(c), :] = jnp.where(first, -sin, 0.0)
        sin2_scr[_chunk(c), :] = jnp.where(second, sin, 0.0)


def _spread_a(t):
    t = t + pltpu.roll(t, A_ROT_DIM // 2, 1)
    return t + pltpu.roll(t, A_HEAD_DIM, 1)


def _spread_b(t):
    t = pltpu.roll(t, MLA_NOPE_DIM, 1)
    return t + pltpu.roll(t, MLA_ROPE_DIM // 2, 1)


def _rope_block(x, rows, half, cos_scr, sin1_scr, sin2_scr):
    return (x * cos_scr[rows, :]
            + pltpu.roll(x, LANES - half, 1) * sin1_scr[rows, :]
            + pltpu.roll(x, half, 1) * sin2_scr[rows, :])


def _proj_a_kernel(x_ref, pos_ref, freq_ref, mask_ref, g_ref, b_ref, w_ref,
                   qn_ref, kn_ref, vn_ref, q4_ref, k4_ref, v4_ref, q16_ref, k16_ref, v16_ref,
                   h_scr, acc_scr, st4_scr, cos_scr, sin1_scr, sin2_scr):
    outs = ((qn_ref, q4_ref, q16_ref), (kn_ref, k4_ref, k16_ref), (vn_ref, v4_ref, v16_ref))
    q_scale = (A_HEAD_DIM ** -0.5) * LOG2_E

    def prepare(hf):
        chunks = range(hf * CHUNKS_PER_HALF, (hf + 1) * CHUNKS_PER_HALF)
        _normed_chunks_to_scratch(x_ref, g_ref, b_ref, h_scr, chunks)
        _rope_tables(pos_ref, freq_ref, mask_ref, _spread_a, chunks, cos_scr, sin1_scr, sin2_scr)

    def matmul(hf, sec):
        if sec == 1 and hf + 1 < N_HALVES:
            prepare(hf + 1)
        return jnp.dot(h_scr[hf * ROW_HALF:(hf + 1) * ROW_HALF, :],
                       w_ref[:, sec * A_WIDTH:(sec + 1) * A_WIDTH], preferred_element_type=F32)

    def epilogue(hf, sec, res):
        nat_ref, r4_ref, r16_ref = outs[sec]
        acc = acc_scr.at[hf * len(outs) + sec]
        st4 = st4_scr.at[hf * len(outs) + sec]
        for cg in range(A_PAIRS):
            cols = slice(cg * LANES, (cg + 1) * LANES)
            for lc in range(CHUNKS_PER_HALF):
                rows = _chunk(hf * CHUNKS_PER_HALF + lc)
                blk = res[lc * ROW_CHUNK:(lc + 1) * ROW_CHUNK, cols]
                if sec < 2:
                    blk = _rope_block(blk, rows, A_ROT_DIM // 2, cos_scr, sin1_scr, sin2_scr)
                    if sec == 0:
                        blk = blk * q_scale
                acc[cg, lc * ROW_CHUNK:(lc + 1) * ROW_CHUNK, :] = blk
                nat_ref[cg, rows, :] = blk.astype(BF16)
            n4, n16 = ROW_HALF // 4, ROW_HALF // 16
            for r in range(4):
                rows4 = acc[cg, pl.ds(r, n4, stride=4), :]
                st4[cg, r] = rows4
                r4_ref[cg, r, hf * n4:(hf + 1) * n4, :] = rows4.astype(BF16)
            for r in range(4):
                for v in range(4):
                    r16_ref[cg, r + 4 * v, hf * n16:(hf + 1) * n16, :] = (
                        st4[cg, r, pl.ds(v, n16, stride=4), :].astype(BF16))

    prepare(0)
    items = [(hf, sec) for hf in range(N_HALVES) for sec in range(len(outs))]
    pending = None
    for hf, sec in items:
        res = matmul(hf, sec)
        if pending is not None:
            epilogue(*pending)
        pending = (hf, sec, res)
    epilogue(*pending)


def _proj_a(x, pos_view, freq_a, mask_a, g_emb, b_emb, w_qkv):
    batch = x.shape[0]
    nat = jax.ShapeDtypeStruct((batch, A_PAIRS, SEQ, LANES), BF16)
    r4 = jax.ShapeDtypeStruct((batch, A_PAIRS, 4, SEQ // 4, LANES), BF16)
    r16 = jax.ShapeDtypeStruct((batch, A_PAIRS, 16, SEQ // 16, LANES), BF16)
    nat_spec = pl.BlockSpec((None, A_PAIRS, ROW_TILE, LANES), lambda b, i: (b, 0, i, 0))
    r4_spec = pl.BlockSpec((None, A_PAIRS, 4, ROW_TILE // 4, LANES), lambda b, i: (b, 0, 0, i, 0))
    r16_spec = pl.BlockSpec((None, A_PAIRS, 16, ROW_TILE // 16, LANES), lambda b, i: (b, 0, 0, i, 0))
    tile_out = ROW_TILE * A_WIDTH * 2
    est = (2 * ROW_TILE * D_MODEL * 4 + 2 * w_qkv.size * 2 + 2 * 9 * tile_out
           + ROW_TILE * D_MODEL * 2 + 2 * 3 * ROW_TILE * A_WIDTH * 4 + 3 * ROW_TILE * LANES * 4
           + 2 * ROW_HALF * A_WIDTH * 4)
    return pl.pallas_call(
        _proj_a_kernel,
        grid=(batch, N_ROW_TILES),
        in_specs=[
            pl.BlockSpec((None, ROW_TILE, D_MODEL), lambda b, i: (b, i, 0)),
            pl.BlockSpec((None, None, N_CHUNKS, ROW_CHUNK), lambda b, i: (b, i, 0, 0)),
            pl.BlockSpec(freq_a.shape, lambda b, i: (0, 0)),
            pl.BlockSpec((8, LANES), lambda b, i: (0, 0)),
            pl.BlockSpec((1, D_MODEL), lambda b, i: (0, 0)),
            pl.BlockSpec((1, D_MODEL), lambda b, i: (0, 0)),
            pl.BlockSpec(w_qkv.shape, lambda b, i: (0, 0)),
        ],
        out_specs=[nat_spec] * 3 + [r4_spec] * 3 + [r16_spec] * 3,
        out_shape=[nat] * 3 + [r4] * 3 + [r16] * 3,
        scratch_shapes=[
            pltpu.VMEM((ROW_TILE, D_MODEL), BF16),
            pltpu.VMEM((N_HALVES * 3, A_PAIRS, ROW_HALF, LANES), F32),
            pltpu.VMEM((N_HALVES * 3, A_PAIRS, 4, ROW_HALF // 4, LANES), F32),
            pltpu.VMEM((ROW_TILE, LANES), F32),
            pltpu.VMEM((ROW_TILE, LANES), F32),
            pltpu.VMEM((ROW_TILE, LANES), F32),
        ],
        compiler_params=pltpu.CompilerParams(
            dimension_semantics=("arbitrary", "arbitrary"), vmem_limit_bytes=_vmem_limit(est)),
        name="proj_a",
    )(x, pos_view, freq_a, mask_a, g_emb, b_emb, w_qkv)


_GATE_W = D_MIX
_OFF_MQ = _GATE_W
_OFF_CQ = _OFF_MQ + MEM_WIDTH
_OFF_CKV = _OFF_CQ + MLA_Q_RANK
_OFF_KR = _OFF_CKV + MLA_KV_RANK
_MLA_QK_W = MLA_HEADS * MLA_SLOT


def _proj_b_kernel(x_ref, pos_ref, freq_ref, mask_ref, g_ref, b_ref, w_ref, gcq_ref, gckv_ref, wuq_ref,
                   wukv_ref, gate_ref, mq_ref, qb_ref, kb_ref, vb_ref,
                   h_scr, cos_scr, sin1_scr, sin2_scr):
    mla_scale = ((MLA_NOPE_DIM + MLA_ROPE_DIM) ** -0.5) * LOG2_E
    local = lambda lc: slice(lc * ROW_CHUNK, (lc + 1) * ROW_CHUNK)
    half_rows = lambda hf: slice(hf * ROW_HALF, (hf + 1) * ROW_HALF)

    def prepare(hf):
        chunks = range(hf * CHUNKS_PER_HALF, (hf + 1) * CHUNKS_PER_HALF)
        _normed_chunks_to_scratch(x_ref, g_ref, b_ref, h_scr, chunks)
        _rope_tables(pos_ref, freq_ref, mask_ref, _spread_b, chunks, cos_scr, sin1_scr, sin2_scr)

    def h_dot(hf, first, width):
        return jnp.dot(h_scr[half_rows(hf), :], w_ref[:, first:first + width], preferred_element_type=F32)

    def rope_slots(hf, slots, out_ref, scale=None, extra=None):
        for lc in range(CHUNKS_PER_HALF):
            rows = _chunk(hf * CHUNKS_PER_HALF + lc)
            for hd in range(MLA_HEADS):
                cols = slice(hd * MLA_SLOT, (hd + 1) * MLA_SLOT)
                blk = slots[local(lc), cols]
                if extra is not None:
                    blk = blk + extra[local(lc), :]
                blk = _rope_block(blk, rows, MLA_ROPE_DIM // 2, cos_scr, sin1_scr, sin2_scr)
                out_ref[rows, cols] = (blk if scale is None else blk * scale).astype(BF16)

    def stages(hf):
        state = {}

        def down():
            lat = h_dot(hf, _OFF_CQ, MLA_Q_RANK + MLA_KV_RANK + MLA_SLOT)
            state["cqn"] = _rms_rows(lat[:, :MLA_Q_RANK], gcq_ref[...]).astype(BF16)
            state["ckvn"] = _rms_rows(lat[:, MLA_Q_RANK:MLA_Q_RANK + MLA_KV_RANK], gckv_ref[...]).astype(BF16)
            state["kr"] = lat[:, MLA_Q_RANK + MLA_KV_RANK:]

        def queries():
            return jnp.dot(state["cqn"], wuq_ref[...], preferred_element_type=F32)

        def keys_values():
            k_all = jnp.dot(state["ckvn"], wukv_ref[:, :_MLA_QK_W], preferred_element_type=F32)
            return k_all, jnp.dot(state["ckvn"], wukv_ref[:, _MLA_QK_W:], preferred_element_type=F32)

        def store_keys_values(res):
            k_all, vb = res
            vb_ref[half_rows(hf), :] = vb.astype(BF16)
            rope_slots(hf, k_all, kb_ref, extra=state["kr"])

        def gate_block(n):
            def produce():
                if n == 0 and hf + 1 < N_HALVES:
                    prepare(hf + 1)
                return h_dot(hf, n * A_WIDTH, A_WIDTH)

            def consume(g):
                gate_ref[half_rows(hf), n * A_WIDTH:(n + 1) * A_WIDTH] = (g / (1.0 + jnp.exp(-g))).astype(BF16)

            return produce, consume

        def store_mq(mq):
            mq_ref[half_rows(hf), :] = (mq * (MEM_HEAD_DIM ** -0.5)).astype(BF16)

        return [
            (down, lambda _: None),
            (queries, lambda q_all: rope_slots(hf, q_all, qb_ref, scale=mla_scale)),
            (keys_values, store_keys_values),
            *[gate_block(n) for n in range(_GATE_W // A_WIDTH)],
            (lambda: h_dot(hf, _OFF_MQ, MEM_WIDTH), store_mq),
        ]

    prepare(0)
    pending = None
    for produce, consume in [st for hf in range(N_HALVES) for st in stages(hf)]:
        res = produce()
        if pending is not None:
            pending[0](pending[1])
        pending = (consume, res)
    pending[0](pending[1])


def _proj_b(x, pos_view, freq_b, mask_b, g_emb, b_emb, w_b, g_cq, g_ckv, w_uq_p, w_ukv_p):
    batch = x.shape[0]

    def rows_spec(width):
        return pl.BlockSpec((None, ROW_TILE, width), lambda b, i: (b, i, 0))

    def full_spec(arr):
        return pl.BlockSpec(arr.shape, lambda b, i: (0,) * arr.ndim)

    widths = (_GATE_W, MEM_WIDTH, _MLA_QK_W, _MLA_QK_W, MLA_WIDTH)
    est = (2 * ROW_TILE * D_MODEL * 4 + 2 * (w_b.size + w_uq_p.size + w_ukv_p.size) * 2
           + 2 * ROW_TILE * sum(widths) * 2 + ROW_TILE * D_MODEL * 2 + ROW_TILE * A_WIDTH * 4
           + 3 * ROW_TILE * LANES * 4 + 2 * ROW_TILE * A_WIDTH * 4)
    return pl.pallas_call(
        _proj_b_kernel,
        grid=(batch, N_ROW_TILES),
        in_specs=[
            rows_spec(D_MODEL),
            pl.BlockSpec((None, None, N_CHUNKS, ROW_CHUNK), lambda b, i: (b, i, 0, 0)),
            full_spec(freq_b), full_spec(mask_b),
            full_spec(g_emb), full_spec(b_emb), full_spec(w_b), full_spec(g_cq), full_spec(g_ckv),
            full_spec(w_uq_p), full_spec(w_ukv_p),
        ],
        out_specs=[rows_spec(w) for w in widths],
        out_shape=[jax.ShapeDtypeStruct((batch, SEQ, w), BF16) for w in widths],
        scratch_shapes=[
            pltpu.VMEM((ROW_TILE, D_MODEL), BF16),
            pltpu.VMEM((ROW_TILE, LANES), F32),
            pltpu.VMEM((ROW_TILE, LANES), F32),
            pltpu.VMEM((ROW_TILE, LANES), F32),
        ],
        compiler_params=pltpu.CompilerParams(
            dimension_semantics=("arbitrary", "arbitrary"), vmem_limit_bytes=_vmem_limit(est)),
        name="proj_b",
    )(x, pos_view, freq_b, mask_b, g_emb, b_emb, w_b, g_cq, g_ckv, w_uq_p, w_ukv_p)


def _first_head_lanes(n_rows):
    return lax.broadcasted_iota(jnp.int32, (n_rows, LANES), 1) < A_HEAD_DIM


def _band_scores(q, k, bias):
    n_q = q.shape[0]
    first_head = _first_head_lanes(n_q)
    zero = jnp.zeros_like(q)
    q2 = jnp.concatenate([jnp.where(first_head, q, zero), jnp.where(first_head, zero, q)], axis=0)
    s = lax.dot_general(q2, k, (((1,), (1,)), ((), ())), preferred_element_type=F32) + bias
    m = jnp.max(s, axis=-1, keepdims=True)
    p = jnp.exp2((s - m).astype(BF16))
    return p, jnp.where(first_head, m[:n_q], m[n_q:])


def _band_values(p, v_ones):
    n_q = p.shape[0] // 2
    first_head = _first_head_lanes(n_q)
    o = jnp.dot(p, v_ones, preferred_element_type=F32)
    acc = jnp.where(first_head, o[:n_q, :LANES], o[n_q:, :LANES])
    l_b = jnp.where(first_head, o[:n_q, LANES:], o[n_q:, LANES:])
    return acc, l_b


def _dilated_kernel(qn_ref, kn_ref, vn_ref, q4_ref, k4_ref, v4_ref, q16_ref, k16_ref, v16_ref,
                    o_ref, von_scr, vo4_scr, vo16_scr, bias_scr, bias16_scr, acc_scr, mx_scr, sm_scr):
    def window_bias(n_keys, first_key):
        row = lax.broadcasted_iota(jnp.int32, (2 * A_Q_TILE, n_keys), 0)
        col = lax.broadcasted_iota(jnp.int32, (2 * A_Q_TILE, n_keys), 1)
        q_idx = jnp.where(row >= A_Q_TILE, row - A_Q_TILE, row)
        off = col - (q_idx + first_key)
        return jnp.where(jnp.abs(off) <= A_SIDE, 0.0, NEG_INF).astype(F32)

    @pl.when((pl.program_id(0) == 0) & (pl.program_id(1) == 0))
    def _():
        for variant in range(3):
            bias_scr[variant] = window_bias(A_K_WIN, variant * A_SIDE)
        bias16_scr[...] = window_bias(A_Q_TILE, 0)
        von_scr[:, LANES:] = jnp.ones((SEQ, LANES), BF16)
        vo4_scr[:, :, LANES:] = jnp.ones((4, SEQ // 4, LANES), BF16)
        vo16_scr[:, :, LANES:] = jnp.ones((16, SEQ // 16, LANES), BF16)

    von_scr[:, :LANES] = vn_ref[...]
    vo4_scr[:, :, :LANES] = v4_ref[...]
    vo16_scr[:, :, :LANES] = v16_ref[...]

    len4 = SEQ // 4
    tiles4 = len4 // A_Q_TILE
    n_tiles = SEQ // A_Q_TILE

    def window(tile, n_seq_tiles, seq_len):
        qs = tile * A_Q_TILE
        ks = min(max(qs - A_SIDE, 0), seq_len - A_K_WIN)
        variant = 0 if tile == 0 else (2 if tile == n_seq_tiles - 1 else 1)
        return qs, ks, variant

    def score_half(i):
        qs, ks, var = window(i, n_tiles, SEQ)
        res = i // tiles4
        qs4, ks4, var4 = window(i % tiles4, tiles4, len4)
        parts = (
            _band_scores(qn_ref[pl.ds(qs, A_Q_TILE), :], kn_ref[pl.ds(ks, A_K_WIN), :], bias_scr[var]),
            _band_scores(q4_ref[res, pl.ds(qs4, A_Q_TILE), :], k4_ref[res, pl.ds(ks4, A_K_WIN), :], bias_scr[var4]),
            _band_scores(q16_ref[i], k16_ref[i], bias16_scr[...]),
        )
        dests = (pl.ds(qs, A_Q_TILE), pl.ds(res + 4 * qs4, A_Q_TILE, stride=4), pl.ds(i, A_Q_TILE, stride=16))
        v_wins = (von_scr.at[pl.ds(ks, A_K_WIN), :], vo4_scr.at[res, pl.ds(ks4, A_K_WIN), :], vo16_scr.at[i])
        for pat, (dst, (_, m_b)) in enumerate(zip(dests, parts)):
            mx_scr[pat, dst, :] = m_b
        return [(p, dst, v_win) for (p, _), dst, v_win in zip(parts, dests, v_wins)]

    def value_half(pending):
        for pat, (p, dst, v_win) in enumerate(pending):
            acc, l_b = _band_values(p, v_win[...])
            acc_scr[pat, dst, :] = acc
            sm_scr[pat, dst, :] = l_b

    pending = score_half(0)
    for i in range(1, n_tiles):
        ahead = score_half(i)
        value_half(pending)
        pending = ahead
    value_half(pending)

    def merge(c, carry):
        rows = pl.ds(pl.multiple_of(c * ROW_CHUNK, ROW_CHUNK), ROW_CHUNK)
        m_all = jnp.maximum(jnp.maximum(mx_scr[0, rows, :], mx_scr[1, rows, :]), mx_scr[2, rows, :])
        num = jnp.zeros((ROW_CHUNK, LANES), F32)
        den = jnp.zeros((ROW_CHUNK, LANES), F32)
        for pat in range(3):
            w = jnp.exp2(mx_scr[pat, rows, :] - m_all)
            num = num + w * acc_scr[pat, rows, :]
            den = den + w * sm_scr[pat, rows, :]
        o_ref[rows, :] = (num / den).astype(BF16)
        return carry

    lax.fori_loop(0, SEQ // ROW_CHUNK, merge, 0, unroll=4)


def _dilated_attention(qkv_nat, qkv_4, qkv_16):
    batch = qkv_nat[0].shape[0]
    nat_spec = pl.BlockSpec((None, None, SEQ, LANES), lambda b, p: (b, p, 0, 0))
    r4_spec = pl.BlockSpec((None, None, 4, SEQ // 4, LANES), lambda b, p: (b, p, 0, 0, 0))
    r16_spec = pl.BlockSpec((None, None, 16, SEQ // 16, LANES), lambda b, p: (b, p, 0, 0, 0))
    blk = SEQ * LANES
    est = (2 * 9 * blk * 2 + 2 * blk * 2 + 3 * blk * 2 * 2 + 9 * blk * 4
           + 4 * 2 * A_Q_TILE * A_K_WIN * 4)
    return pl.pallas_call(
        _dilated_kernel,
        grid=(batch, A_PAIRS),
        in_specs=[nat_spec] * 3 + [r4_spec] * 3 + [r16_spec] * 3,
        out_specs=pl.BlockSpec((None, None, SEQ, LANES), lambda b, p: (b, p, 0, 0)),
        out_shape=jax.ShapeDtypeStruct((batch, A_PAIRS, SEQ, LANES), BF16),
        scratch_shapes=[
            pltpu.VMEM((SEQ, 2 * LANES), BF16),
            pltpu.VMEM((4, SEQ // 4, 2 * LANES), BF16),
            pltpu.VMEM((16, SEQ // 16, 2 * LANES), BF16),
            pltpu.VMEM((3, 2 * A_Q_TILE, A_K_WIN), F32),
            pltpu.VMEM((2 * A_Q_TILE, A_Q_TILE), F32),
            pltpu.VMEM((3, SEQ, LANES), F32),
            pltpu.VMEM((3, SEQ, LANES), F32),
            pltpu.VMEM((3, SEQ, LANES), F32),
        ],
        compiler_params=pltpu.CompilerParams(
            dimension_semantics=("arbitrary", "arbitrary"), vmem_limit_bytes=_vmem_limit(est)),
        name="dilated_attn",
    )(*qkv_nat, *qkv_4, *qkv_16)


def _latent_kernel(q_ref, k_ref, v_ref, o_ref, vo_scr):
    @pl.when((pl.program_id(0) == 0) & (pl.program_id(1) == 0) & (pl.program_id(2) == 0))
    def _():
        vo_scr[:, :, LANES:] = jnp.ones((MLA_PAIRS_PER_STEP, SEQ, LANES), BF16)

    @pl.when(pl.program_id(2) == 0)
    def _():
        for pair in range(MLA_PAIRS_PER_STEP):
            vo_scr[pair, :, :LANES] = v_ref[:, pair * LANES:(pair + 1) * LANES]

    def score_half(head):
        cols = slice(head * MLA_SLOT, (head + 1) * MLA_SLOT)
        s = lax.dot_general(q_ref[:, cols], k_ref[:, cols], (((1,), (1,)), ((), ())),
                            preferred_element_type=F32)
        m = jnp.max(s, axis=-1, keepdims=True)
        return jnp.exp2((s - m).astype(BF16))

    def value_half(head, p):
        o = jnp.dot(p, vo_scr[head // 2], preferred_element_type=F32)
        return o[:, :LANES] / o[:, LANES:]

    n_heads = 2 * MLA_PAIRS_PER_STEP
    lane = lax.broadcasted_iota(jnp.int32, (MLA_Q_TILE, LANES), 1)
    outs = []
    p_next = score_half(0)
    for head in range(n_heads):
        p = p_next
        if head + 1 < n_heads:
            p_next = score_half(head + 1)
        outs.append(value_half(head, p))
        if head % 2 == 1:
            pair = head // 2
            o_ref[:, pair * LANES:(pair + 1) * LANES] = (
                jnp.where(lane < MLA_V_DIM, outs[head - 1], outs[head]).astype(BF16))


def _latent_attention(qb, kb, vb):
    batch = qb.shape[0]
    pairs = MLA_HEADS // 2 // MLA_PAIRS_PER_STEP
    qk_w = MLA_PAIRS_PER_STEP * 2 * MLA_SLOT
    v_w = MLA_PAIRS_PER_STEP * LANES
    est = (2 * MLA_Q_TILE * qk_w * 2 + 2 * SEQ * qk_w * 2 + 2 * SEQ * v_w * 2 + SEQ * 2 * v_w * 2
           + 2 * MLA_Q_TILE * v_w * 2 + 4 * MLA_PAIRS_PER_STEP * MLA_Q_TILE * SEQ * 4)
    return pl.pallas_call(
        _latent_kernel,
        grid=(batch, pairs, SEQ // MLA_Q_TILE),
        in_specs=[
            pl.BlockSpec((None, MLA_Q_TILE, qk_w), lambda b, p, i: (b, i, p)),
            pl.BlockSpec((None, SEQ, qk_w), lambda b, p, i: (b, 0, p)),
            pl.BlockSpec((None, SEQ, v_w), lambda b, p, i: (b, 0, p)),
        ],
        out_specs=pl.BlockSpec((None, MLA_Q_TILE, v_w), lambda b, p, i: (b, i, p)),
        out_shape=jax.ShapeDtypeStruct((batch, SEQ, MLA_WIDTH), BF16),
        scratch_shapes=[pltpu.VMEM((MLA_PAIRS_PER_STEP, SEQ, 2 * LANES), BF16)],
        compiler_params=pltpu.CompilerParams(
            dimension_semantics=("arbitrary", "arbitrary", "arbitrary"),
            vmem_limit_bytes=_vmem_limit(est)),
        name="latent_attn",
    )(qb, kb, vb)


def _mem_kv_kernel(mem_ref, w_ref, o_ref):
    o_ref[...] = jnp.dot(mem_ref[...].astype(BF16), w_ref[...], preferred_element_type=F32).astype(BF16)


def _mem_kv(mem, w_mem):
    batch = mem.shape[0]
    est = 2 * N_MEM * D_MODEL * 4 + 2 * w_mem.size * 2 + 2 * N_MEM * 2 * MEM_WIDTH * 2 + N_MEM * D_MODEL * 8
    return pl.pallas_call(
        _mem_kv_kernel,
        grid=(batch,),
        in_specs=[pl.BlockSpec((None, N_MEM, D_MODEL), lambda b: (b, 0, 0)),
                  pl.BlockSpec(w_mem.shape, lambda b: (0, 0))],
        out_specs=pl.BlockSpec((None, N_MEM, 2 * MEM_WIDTH), lambda b: (b, 0, 0)),
        out_shape=jax.ShapeDtypeStruct((batch, N_MEM, 2 * MEM_WIDTH), BF16),
        compiler_params=pltpu.CompilerParams(
            dimension_semantics=("arbitrary",), vmem_limit_bytes=_vmem_limit(est)),
        name="mem_kv",
    )(mem, w_mem)


def _output_kernel(x_ref, ya_ref, yb_ref, gate_ref, mq_ref, mkv_ref, wout_ref,
                   goa_ref, gob_ref, gom_ref, gemb_ref, bemb_ref, gpost_ref, bpost_ref,
                   o_ref, y_scr):
    for hd in range(MEM_HEADS):
        cols = slice(hd * MEM_HEAD_DIM, (hd + 1) * MEM_HEAD_DIM)
        s = lax.dot_general(mq_ref[:, cols], mkv_ref[:, cols], (((1,), (1,)), ((), ())),
                            preferred_element_type=F32)
        m = jnp.max(s, axis=-1, keepdims=True)
        p = jnp.exp(s - m)
        l = jnp.sum(p, axis=-1, keepdims=True)
        o = jnp.dot(p.astype(BF16), mkv_ref[:, MEM_WIDTH + hd * MEM_HEAD_DIM:MEM_WIDTH + (hd + 1) * MEM_HEAD_DIM],
                    preferred_element_type=F32)
        y_scr[:, cols] = o / l

    off_b = A_WIDTH
    off_m = A_WIDTH + MLA_WIDTH
    ya_all = jnp.concatenate([ya_ref[cg] for cg in range(A_PAIRS)], axis=1)
    ya = _rms_rows(ya_all.astype(F32), goa_ref[...]) * gate_ref[:, :off_b].astype(F32)
    yb = _rms_rows(yb_ref[...].astype(F32), gob_ref[...]) * gate_ref[:, off_b:off_m].astype(F32)
    ym = _rms_rows(y_scr[...], gom_ref[...]) * gate_ref[:, off_m:].astype(F32)
    sub = (jnp.dot(ya.astype(BF16), wout_ref[:off_b, :], preferred_element_type=F32)
           + jnp.dot(yb.astype(BF16), wout_ref[off_b:off_m, :], preferred_element_type=F32)
           + jnp.dot(ym.astype(BF16), wout_ref[off_m:, :], preferred_element_type=F32))

    for c in range(N_CHUNKS):
        h = _layer_norm_rows(x_ref[_chunk(c), :], gemb_ref[...], bemb_ref[...])
        z = DEEPNORM_ALPHA * h + sub[c * ROW_CHUNK:(c + 1) * ROW_CHUNK, :]
        o_ref[_chunk(c), :] = _layer_norm_rows(z, gpost_ref[...], bpost_ref[...])


def _output_stage(x, ya, yb, gates, mq, mkv, w_out, g_out_a, g_out_b, g_out_m, g_emb, b_emb,
                  g_post, b_post):
    batch = x.shape[0]

    def rows_spec(width):
        return pl.BlockSpec((None, ROW_TILE, width), lambda b, i: (b, i, 0))

    def full_spec(arr):
        return pl.BlockSpec(arr.shape, lambda b, i: (0,) * arr.ndim)

    est = (4 * ROW_TILE * D_MODEL * 4 + 2 * ROW_TILE * (A_WIDTH + MLA_WIDTH + D_MIX + MEM_WIDTH) * 2
           + 2 * N_MEM * 2 * MEM_WIDTH * 2 + 2 * w_out.size * 2 + ROW_TILE * MEM_WIDTH * 4
           + 4 * ROW_TILE * D_MODEL * 4)
    return pl.pallas_call(
        _output_kernel,
        grid=(batch, N_ROW_TILES),
        in_specs=[
            rows_spec(D_MODEL),
            pl.BlockSpec((None, A_PAIRS, ROW_TILE, LANES), lambda b, i: (b, 0, i, 0)),
            rows_spec(MLA_WIDTH), rows_spec(D_MIX), rows_spec(MEM_WIDTH),
            pl.BlockSpec((None, N_MEM, 2 * MEM_WIDTH), lambda b, i: (b, 0, 0)),
            full_spec(w_out), full_spec(g_out_a), full_spec(g_out_b), full_spec(g_out_m),
            full_spec(g_emb), full_spec(b_emb), full_spec(g_post), full_spec(b_post),
        ],
        out_specs=rows_spec(D_MODEL),
        out_shape=jax.ShapeDtypeStruct(x.shape, F32),
        scratch_shapes=[pltpu.VMEM((ROW_TILE, MEM_WIDTH), F32)],
        compiler_params=pltpu.CompilerParams(
            dimension_semantics=("arbitrary", "arbitrary"), vmem_limit_bytes=_vmem_limit(est)),
        name="output_stage",
    )(x, ya, yb, gates, mq, mkv, w_out, g_out_a, g_out_b, g_out_m, g_emb, b_emb, g_post, b_post)


def _rope_lane_tables(rot_dim, first_lane, period):
    half = rot_dim // 2
    inv_freq = (np.float32(ROPE_THETA) ** (-(np.arange(0, rot_dim, 2, dtype=np.float32) / np.float32(rot_dim)))
                ).astype(np.float32)
    freq = np.repeat(inv_freq[:, None], LANES, axis=1)
    mask = np.zeros((8, LANES), np.float32)
    for lane in range(LANES):
        rel = (lane % period) - first_lane
        if 0 <= rel < rot_dim:
            mask[1, lane] = 1.0
            mask[2 if rel < half else 3, lane] = 1.0
    return jnp.asarray(freq), jnp.asarray(mask)


def kernel(x, mem, positions, g_emb, b_emb, w_in, g_cq, g_ckv, w_uq, w_ukv, w_mem_kv, g_out_a, g_out_b,
           g_out_m, w_out, g_post, b_post):
    batch = x.shape[0]
    assert x.shape == (batch, SEQ, D_MODEL) and w_in.shape[0] == DEPTH == 1
    row = lambda v: v.reshape(1, -1).astype(F32)

    splits = [int(i) for i in np.cumsum(IN_SPLITS)[:-1]]
    a_q, a_k, a_v, a_g, c_q, c_kv, b_kr, b_g, m_q, m_g = jnp.split(w_in[0], splits, axis=1)
    w_qkv = jnp.concatenate([a_q, a_k, a_v], axis=1).astype(BF16)
    kr_slot = jnp.pad(b_kr, ((0, 0), (MLA_NOPE_DIM, MLA_SLOT - MLA_NOPE_DIM - MLA_ROPE_DIM)))
    w_b = jnp.concatenate([a_g, b_g, m_g, m_q, c_q, c_kv, kr_slot], axis=1).astype(BF16)
    qk_dim = MLA_NOPE_DIM + MLA_ROPE_DIM
    w_uq_p = jnp.pad(w_uq[0].reshape(MLA_Q_RANK, MLA_HEADS, qk_dim),
                     ((0, 0), (0, 0), (0, MLA_SLOT - qk_dim))).reshape(MLA_Q_RANK, _MLA_QK_W).astype(BF16)
    ukv = w_ukv[0].reshape(MLA_KV_RANK, MLA_HEADS, MLA_NOPE_DIM + MLA_V_DIM)
    w_uk_p = jnp.pad(ukv[:, :, :MLA_NOPE_DIM], ((0, 0), (0, 0), (0, MLA_SLOT - MLA_NOPE_DIM)))
    w_ukv_p = jnp.concatenate([w_uk_p.reshape(MLA_KV_RANK, _MLA_QK_W),
                               ukv[:, :, MLA_NOPE_DIM:].reshape(MLA_KV_RANK, MLA_WIDTH)], axis=1).astype(BF16)

    pos_view = positions.astype(F32).reshape(batch, N_ROW_TILES, N_CHUNKS, ROW_CHUNK)
    freq_a, mask_a = _rope_lane_tables(A_ROT_DIM, 0, A_HEAD_DIM)
    freq_b, mask_b = _rope_lane_tables(MLA_ROPE_DIM, MLA_NOPE_DIM, MLA_SLOT)
    g_emb_r, b_emb_r = row(g_emb), row(b_emb)

    outs_a = _proj_a(x, pos_view, freq_a, mask_a, g_emb_r, b_emb_r, w_qkv)
    y_a = _dilated_attention(outs_a[0:3], outs_a[3:6], outs_a[6:9])

    gates, mq, qb, kb, vb = _proj_b(x, pos_view, freq_b, mask_b, g_emb_r, b_emb_r, w_b, row(g_cq[0]),
                                    row(g_ckv[0]), w_uq_p, w_ukv_p)
    y_b = _latent_attention(qb, kb, vb)

    mkv = _mem_kv(mem, w_mem_kv[0].astype(BF16))
    return _output_stage(x, y_a, y_b, gates, mq, mkv, w_out[0].astype(BF16), row(g_out_a[0]),
                         row(g_out_b[0]), row(g_out_m[0]), g_emb_r, b_emb_r, row(g_post[0]), row(b_post[0]))
```

```python
import math

import jax
import jax.numpy as jnp
import numpy as np
from jax import lax
from jax.experimental import pallas as pl
from jax.experimental.pallas import tpu as pltpu

D_MODEL = 1024
SEQ = 2048
A_HEADS = 16
A_HEAD_DIM = 64
A_WIDTH = A_HEADS * A_HEAD_DIM
A_ROT_DIM = A_HEAD_DIM // 4
A_SIDE = 64
DILATIONS = (1, 4, 16)
MLA_HEADS = 8
MLA_Q_RANK = 256
MLA_KV_RANK = 128
MLA_NOPE_DIM = 64
MLA_ROPE_DIM = 32
MLA_V_DIM = 64
MLA_WIDTH = MLA_HEADS * MLA_V_DIM
N_MEM = 256
MEM_HEADS = 4
MEM_HEAD_DIM = 128
MEM_WIDTH = MEM_HEADS * MEM_HEAD_DIM
D_MIX = A_WIDTH + MLA_WIDTH + MEM_WIDTH
ROPE_THETA = 500000.0
NORM_EPS = 1e-5
NEG_INF = -1e30
DEPTH = 1
DEEPNORM_ALPHA = (2 * DEPTH) ** 0.25
IN_SPLITS = (A_WIDTH, A_WIDTH, A_WIDTH, A_WIDTH, MLA_Q_RANK, MLA_KV_RANK, MLA_ROPE_DIM, MLA_WIDTH,
             MEM_WIDTH, MEM_WIDTH)
LOG2_E = math.log2(math.e)

LANES = 128
VMEM_BYTES_V7X = 64 * 1024 * 1024

ROW_TILE = 512
ROW_CHUNK = 128
N_CHUNKS = ROW_TILE // ROW_CHUNK
N_HALVES = 1
ROW_HALF = ROW_TILE // N_HALVES
CHUNKS_PER_HALF = N_CHUNKS // N_HALVES
N_ROW_TILES = SEQ // ROW_TILE
A_PAIRS = A_WIDTH // LANES
A_Q_TILE = 128
A_K_WIN = 2 * A_Q_TILE
MLA_SLOT = LANES
MLA_Q_TILE = 256
MLA_PAIRS_PER_STEP = 4

F32 = jnp.float32
BF16 = jnp.bfloat16


def _vmem_limit(nbytes):
    return int(min(VMEM_BYTES_V7X - (4 << 20), max(32 << 20, nbytes + (8 << 20))))


def _layer_norm_rows(x, g, b):
    mu = jnp.mean(x, axis=-1, keepdims=True)
    xc = x - mu
    var = jnp.mean(xc * xc, axis=-1, keepdims=True)
    return xc * lax.rsqrt(var + NORM_EPS) * g + b


def _rms_rows(x, g):
    ms = jnp.mean(x * x, axis=-1, keepdims=True)
    return x * lax.rsqrt(ms + NORM_EPS) * g


def _chunk(c):
    return slice(c * ROW_CHUNK, (c + 1) * ROW_CHUNK)


def _normed_chunks_to_scratch(x_ref, g_ref, b_ref, h_scr, chunks):
    for c in chunks:
        h_scr[_chunk(c), :] = _layer_norm_rows(x_ref[_chunk(c), :], g_ref[...], b_ref[...]).astype(BF16)


ROPE_FIRST_LANES = (0, 16)
ROPE_PARTNER_SHIFT = LANES // 2


def _rope_tables(pos_ref, freq_ref, spread, chunks, cos_scr, sin_scr):
    n_freq = freq_ref.shape[0]
    lane = lax.broadcasted_iota(jnp.int32, (ROW_CHUNK, LANES), 1)
    first = (lane >= ROPE_FIRST_LANES[0]) & (lane < ROPE_FIRST_LANES[1])
    second = (lane >= ROPE_FIRST_LANES[0] + ROPE_PARTNER_SHIFT) & (lane < ROPE_FIRST_LANES[1] + ROPE_PARTNER_SHIFT)
    pad = jnp.zeros((ROW_CHUNK - n_freq, LANES), F32)
    for c in chunks:
        ang = freq_ref[...] * pos_ref[c:c + 1, :]
        cos = spread(jnp.concatenate([jnp.cos(ang), pad], axis=0).T)
        sin = spread(jnp.concatenate([jnp.sin(ang), pad], axis=0).T)
        cos_scr[_chunk(c), :] = jnp.where(first | second, cos, 1.0)
        sin_scr[_chunk(c), :] = jnp.where(first, -sin, jnp.where(second, sin, 0.0))


def _spread_a(t):
    t = t + pltpu.roll(t, A_ROT_DIM // 2, 1)
    return t + pltpu.roll(t, ROPE_PARTNER_SHIFT, 1)


def _spread_b(t):
    return t + pltpu.roll(t, ROPE_PARTNER_SHIFT, 1)


def _rope_block(x, rows, cos_scr, sin_scr):
    return x * cos_scr[rows, :] + pltpu.roll(x, ROPE_PARTNER_SHIFT, 1) * sin_scr[rows, :]


def _proj_a_kernel(x_ref, pos_ref, freq_ref, g_ref, b_ref, w_ref,
                   qn_ref, kn_ref, vn_ref, q4_ref, k4_ref, v4_ref, q16_ref, k16_ref, v16_ref,
                   h_scr, acc_scr, st4_scr, cos_scr, sin_scr):
    outs = ((qn_ref, q4_ref, q16_ref), (kn_ref, k4_ref, k16_ref), (vn_ref, v4_ref, v16_ref))
    q_scale = (A_HEAD_DIM ** -0.5) * LOG2_E

    def prepare(hf):
        chunks = range(hf * CHUNKS_PER_HALF, (hf + 1) * CHUNKS_PER_HALF)
        _normed_chunks_to_scratch(x_ref, g_ref, b_ref, h_scr, chunks)
        _rope_tables(pos_ref, freq_ref, _spread_a, chunks, cos_scr, sin_scr)

    def matmul(hf, sec):
        if sec == 1 and hf + 1 < N_HALVES:
            prepare(hf + 1)
        return jnp.dot(h_scr[hf * ROW_HALF:(hf + 1) * ROW_HALF, :],
                       w_ref[:, sec * A_WIDTH:(sec + 1) * A_WIDTH], preferred_element_type=F32)

    def epilogue(hf, sec, res):
        nat_ref, r4_ref, r16_ref = outs[sec]
        acc = acc_scr.at[hf * len(outs) + sec]
        st4 = st4_scr.at[hf * len(outs) + sec]
        for cg in range(A_PAIRS):
            cols = slice(cg * LANES, (cg + 1) * LANES)
            for lc in range(CHUNKS_PER_HALF):
                rows = _chunk(hf * CHUNKS_PER_HALF + lc)
                blk = res[lc * ROW_CHUNK:(lc + 1) * ROW_CHUNK, cols]
                if sec < 2:
                    blk = _rope_block(blk, rows, cos_scr, sin_scr)
                    if sec == 0:
                        blk = blk * q_scale
                acc[cg, lc * ROW_CHUNK:(lc + 1) * ROW_CHUNK, :] = blk
                nat_ref[cg, rows, :] = blk.astype(BF16)
            n4, n16 = ROW_HALF // 4, ROW_HALF // 16
            for r in range(4):
                rows4 = acc[cg, pl.ds(r, n4, stride=4), :]
                st4[cg, r] = rows4
                r4_ref[cg, r, hf * n4:(hf + 1) * n4, :] = rows4.astype(BF16)
            for r in range(4):
                for v in range(4):
                    r16_ref[cg, r + 4 * v, hf * n16:(hf + 1) * n16, :] = (
                        st4[cg, r, pl.ds(v, n16, stride=4), :].astype(BF16))

    prepare(0)
    items = [(hf, sec) for hf in range(N_HALVES) for sec in range(len(outs))]
    pending = None
    for hf, sec in items:
        res = matmul(hf, sec)
        if pending is not None:
            epilogue(*pending)
        pending = (hf, sec, res)
    epilogue(*pending)


def _proj_a(x, pos_view, freq_a, g_emb, b_emb, w_qkv):
    batch = x.shape[0]
    nat = jax.ShapeDtypeStruct((batch, A_PAIRS, SEQ, LANES), BF16)
    r4 = jax.ShapeDtypeStruct((batch, A_PAIRS, 4, SEQ // 4, LANES), BF16)
    r16 = jax.ShapeDtypeStruct((batch, A_PAIRS, 16, SEQ // 16, LANES), BF16)
    nat_spec = pl.BlockSpec((None, A_PAIRS, ROW_TILE, LANES), lambda b, i: (b, 0, i, 0))
    r4_spec = pl.BlockSpec((None, A_PAIRS, 4, ROW_TILE // 4, LANES), lambda b, i: (b, 0, 0, i, 0))
    r16_spec = pl.BlockSpec((None, A_PAIRS, 16, ROW_TILE // 16, LANES), lambda b, i: (b, 0, 0, i, 0))
    tile_out = ROW_TILE * A_WIDTH * 2
    est = (2 * ROW_TILE * D_MODEL * 4 + 2 * w_qkv.size * 2 + 2 * 9 * tile_out
           + ROW_TILE * D_MODEL * 2 + 2 * 3 * ROW_TILE * A_WIDTH * 4 + 3 * ROW_TILE * LANES * 4
           + 2 * ROW_HALF * A_WIDTH * 4)
    return pl.pallas_call(
        _proj_a_kernel,
        grid=(batch, N_ROW_TILES),
        in_specs=[
            pl.BlockSpec((None, ROW_TILE, D_MODEL), lambda b, i: (b, i, 0)),
            pl.BlockSpec((None, None, N_CHUNKS, ROW_CHUNK), lambda b, i: (b, i, 0, 0)),
            pl.BlockSpec(freq_a.shape, lambda b, i: (0, 0)),
            pl.BlockSpec((1, D_MODEL), lambda b, i: (0, 0)),
            pl.BlockSpec((1, D_MODEL), lambda b, i: (0, 0)),
            pl.BlockSpec(w_qkv.shape, lambda b, i: (0, 0)),
        ],
        out_specs=[nat_spec] * 3 + [r4_spec] * 3 + [r16_spec] * 3,
        out_shape=[nat] * 3 + [r4] * 3 + [r16] * 3,
        scratch_shapes=[
            pltpu.VMEM((ROW_TILE, D_MODEL), BF16),
            pltpu.VMEM((N_HALVES * 3, A_PAIRS, ROW_HALF, LANES), F32),
            pltpu.VMEM((N_HALVES * 3, A_PAIRS, 4, ROW_HALF // 4, LANES), F32),
            pltpu.VMEM((ROW_TILE, LANES), F32),
            pltpu.VMEM((ROW_TILE, LANES), F32),
        ],
        compiler_params=pltpu.CompilerParams(
            dimension_semantics=("arbitrary", "arbitrary"), vmem_limit_bytes=_vmem_limit(est)),
        name="proj_a",
    )(x, pos_view, freq_a, g_emb, b_emb, w_qkv)


_GATE_W = D_MIX
_OFF_MQ = _GATE_W
_OFF_CQ = _OFF_MQ + MEM_WIDTH
_OFF_CKV = _OFF_CQ + MLA_Q_RANK
_OFF_KR = _OFF_CKV + MLA_KV_RANK
_MLA_QK_W = MLA_HEADS * MLA_SLOT


def _proj_b_kernel(x_ref, pos_ref, freq_ref, g_ref, b_ref, w_ref, gcq_ref, gckv_ref, wuq_ref,
                   wukv_ref, gate_ref, mq_ref, qb_ref, kb_ref, vb_ref,
                   h_scr, cos_scr, sin_scr):
    mla_scale = ((MLA_NOPE_DIM + MLA_ROPE_DIM) ** -0.5) * LOG2_E
    local = lambda lc: slice(lc * ROW_CHUNK, (lc + 1) * ROW_CHUNK)
    half_rows = lambda hf: slice(hf * ROW_HALF, (hf + 1) * ROW_HALF)

    def prepare(hf):
        chunks = range(hf * CHUNKS_PER_HALF, (hf + 1) * CHUNKS_PER_HALF)
        _normed_chunks_to_scratch(x_ref, g_ref, b_ref, h_scr, chunks)
        _rope_tables(pos_ref, freq_ref, _spread_b, chunks, cos_scr, sin_scr)

    def h_dot(hf, first, width):
        return jnp.dot(h_scr[half_rows(hf), :], w_ref[:, first:first + width], preferred_element_type=F32)

    def rope_slots(hf, slots, out_ref, scale=None, extra=None):
        for lc in range(CHUNKS_PER_HALF):
            rows = _chunk(hf * CHUNKS_PER_HALF + lc)
            for hd in range(MLA_HEADS):
                cols = slice(hd * MLA_SLOT, (hd + 1) * MLA_SLOT)
                blk = slots[local(lc), cols]
                if extra is not None:
                    blk = blk + extra[local(lc), :]
                blk = _rope_block(blk, rows, cos_scr, sin_scr)
                out_ref[rows, cols] = (blk if scale is None else blk * scale).astype(BF16)

    def stages(hf):
        state = {}

        def down():
            lat = h_dot(hf, _OFF_CQ, MLA_Q_RANK + MLA_KV_RANK + MLA_SLOT)
            state["cqn"] = _rms_rows(lat[:, :MLA_Q_RANK], gcq_ref[...]).astype(BF16)
            state["ckvn"] = _rms_rows(lat[:, MLA_Q_RANK:MLA_Q_RANK + MLA_KV_RANK], gckv_ref[...]).astype(BF16)
            state["kr"] = lat[:, MLA_Q_RANK + MLA_KV_RANK:]

        def queries():
            return jnp.dot(state["cqn"], wuq_ref[...], preferred_element_type=F32)

        def keys_values():
            k_all = jnp.dot(state["ckvn"], wukv_ref[:, :_MLA_QK_W], preferred_element_type=F32)
            return k_all, jnp.dot(state["ckvn"], wukv_ref[:, _MLA_QK_W:], preferred_element_type=F32)

        def store_keys_values(res):
            k_all, vb = res
            vb_ref[half_rows(hf), :] = vb.astype(BF16)
            rope_slots(hf, k_all, kb_ref, extra=state["kr"])

        def gate_block(n):
            def produce():
                if n == 0 and hf + 1 < N_HALVES:
                    prepare(hf + 1)
                return h_dot(hf, n * A_WIDTH, A_WIDTH)

            def consume(g):
                gate_ref[half_rows(hf), n * A_WIDTH:(n + 1) * A_WIDTH] = (g / (1.0 + jnp.exp(-g))).astype(BF16)

            return produce, consume

        def store_mq(mq):
            mq_ref[half_rows(hf), :] = (mq * (MEM_HEAD_DIM ** -0.5)).astype(BF16)

        return [
            (down, lambda _: None),
            (queries, lambda q_all: rope_slots(hf, q_all, qb_ref, scale=mla_scale)),
            (keys_values, store_keys_values),
            *[gate_block(n) for n in range(_GATE_W // A_WIDTH)],
            (lambda: h_dot(hf, _OFF_MQ, MEM_WIDTH), store_mq),
        ]

    prepare(0)
    pending = None
    for produce, consume in [st for hf in range(N_HALVES) for st in stages(hf)]:
        res = produce()
        if pending is not None:
            pending[0](pending[1])
        pending = (consume, res)
    pending[0](pending[1])


def _proj_b(x, pos_view, freq_b, g_emb, b_emb, w_b, g_cq, g_ckv, w_uq_p, w_ukv_p):
    batch = x.shape[0]

    def rows_spec(width):
        return pl.BlockSpec((None, ROW_TILE, width), lambda b, i: (b, i, 0))

    def full_spec(arr):
        return pl.BlockSpec(arr.shape, lambda b, i: (0,) * arr.ndim)

    widths = (_GATE_W, MEM_WIDTH, _MLA_QK_W, _MLA_QK_W, MLA_WIDTH)
    est = (2 * ROW_TILE * D_MODEL * 4 + 2 * (w_b.size + w_uq_p.size + w_ukv_p.size) * 2
           + 2 * ROW_TILE * sum(widths) * 2 + ROW_TILE * D_MODEL * 2 + ROW_TILE * A_WIDTH * 4
           + 3 * ROW_TILE * LANES * 4 + 2 * ROW_TILE * A_WIDTH * 4)
    return pl.pallas_call(
        _proj_b_kernel,
        grid=(batch, N_ROW_TILES),
        in_specs=[
            rows_spec(D_MODEL),
            pl.BlockSpec((None, None, N_CHUNKS, ROW_CHUNK), lambda b, i: (b, i, 0, 0)),
            full_spec(freq_b),
            full_spec(g_emb), full_spec(b_emb), full_spec(w_b), full_spec(g_cq), full_spec(g_ckv),
            full_spec(w_uq_p), full_spec(w_ukv_p),
        ],
        out_specs=[rows_spec(w) for w in widths],
        out_shape=[jax.ShapeDtypeStruct((batch, SEQ, w), BF16) for w in widths],
        scratch_shapes=[
            pltpu.VMEM((ROW_TILE, D_MODEL), BF16),
            pltpu.VMEM((ROW_TILE, LANES), F32),
            pltpu.VMEM((ROW_TILE, LANES), F32),
        ],
        compiler_params=pltpu.CompilerParams(
            dimension_semantics=("arbitrary", "arbitrary"), vmem_limit_bytes=_vmem_limit(est)),
        name="proj_b",
    )(x, pos_view, freq_b, g_emb, b_emb, w_b, g_cq, g_ckv, w_uq_p, w_ukv_p)


def _first_head_lanes(n_rows):
    return lax.broadcasted_iota(jnp.int32, (n_rows, LANES), 1) < A_HEAD_DIM


def _a_qk_lane_order():
    half = A_ROT_DIM // 2
    rest = A_HEAD_DIM - A_ROT_DIM
    order = ([(0, d) for d in range(half)] + [(1, d) for d in range(half)]
             + [(0, A_ROT_DIM + d) for d in range(rest)]
             + [(0, half + d) for d in range(half)] + [(1, half + d) for d in range(half)]
             + [(1, A_ROT_DIM + d) for d in range(rest)])
    assert len(order) == LANES and len(set(order)) == LANES
    return order


def _first_head_qk_lanes(n_rows):
    lane = lax.broadcasted_iota(jnp.int32, (n_rows, LANES), 1)
    half = A_ROT_DIM // 2
    return (lane < half) | ((lane >= A_ROT_DIM) & (lane < A_ROT_DIM + A_HEAD_DIM - half))


def _band_scores(q, k, bias):
    n_q = q.shape[0]
    first_head = _first_head_lanes(n_q)
    first_head_qk = _first_head_qk_lanes(n_q)
    zero = jnp.zeros_like(q)
    q2 = jnp.concatenate([jnp.where(first_head_qk, q, zero), jnp.where(first_head_qk, zero, q)], axis=0)
    s = lax.dot_general(q2, k, (((1,), (1,)), ((), ())), preferred_element_type=F32) + bias
    m = jnp.max(s, axis=-1, keepdims=True)
    p = jnp.exp2((s - m).astype(BF16))
    return p, jnp.where(first_head, m[:n_q], m[n_q:])


def _band_values(p, v_ones):
    n_q = p.shape[0] // 2
    first_head = _first_head_lanes(n_q)
    o = jnp.dot(p, v_ones, preferred_element_type=F32)
    acc = jnp.where(first_head, o[:n_q, :LANES], o[n_q:, :LANES])
    l_b = jnp.where(first_head, o[:n_q, LANES:], o[n_q:, LANES:])
    return acc, l_b


def _dilated_kernel(qn_ref, kn_ref, vn_ref, q4_ref, k4_ref, v4_ref, q16_ref, k16_ref, v16_ref,
                    o_ref, von_scr, vo4_scr, vo16_scr, bias_scr, bias16_scr, acc_scr, mx_scr, sm_scr):
    def window_bias(n_keys, first_key):
        row = lax.broadcasted_iota(jnp.int32, (2 * A_Q_TILE, n_keys), 0)
        col = lax.broadcasted_iota(jnp.int32, (2 * A_Q_TILE, n_keys), 1)
        q_idx = jnp.where(row >= A_Q_TILE, row - A_Q_TILE, row)
        off = col - (q_idx + first_key)
        return jnp.where(jnp.abs(off) <= A_SIDE, 0.0, NEG_INF).astype(F32)

    @pl.when((pl.program_id(0) == 0) & (pl.program_id(1) == 0))
    def _():
        for variant in range(3):
            bias_scr[variant] = window_bias(A_K_WIN, variant * A_SIDE)
        bias16_scr[...] = window_bias(A_Q_TILE, 0)
        von_scr[:, LANES:] = jnp.ones((SEQ, LANES), BF16)
        vo4_scr[:, :, LANES:] = jnp.ones((4, SEQ // 4, LANES), BF16)
        vo16_scr[:, :, LANES:] = jnp.ones((16, SEQ // 16, LANES), BF16)

    von_scr[:, :LANES] = vn_ref[...]
    vo4_scr[:, :, :LANES] = v4_ref[...]
    vo16_scr[:, :, :LANES] = v16_ref[...]

    len4 = SEQ // 4
    tiles4 = len4 // A_Q_TILE
    n_tiles = SEQ // A_Q_TILE

    def window(tile, n_seq_tiles, seq_len):
        qs = tile * A_Q_TILE
        ks = min(max(qs - A_SIDE, 0), seq_len - A_K_WIN)
        variant = 0 if tile == 0 else (2 if tile == n_seq_tiles - 1 else 1)
        return qs, ks, variant

    def score_half(i):
        qs, ks, var = window(i, n_tiles, SEQ)
        res = i // tiles4
        qs4, ks4, var4 = window(i % tiles4, tiles4, len4)
        parts = (
            _band_scores(qn_ref[pl.ds(qs, A_Q_TILE), :], kn_ref[pl.ds(ks, A_K_WIN), :], bias_scr[var]),
            _band_scores(q4_ref[res, pl.ds(qs4, A_Q_TILE), :], k4_ref[res, pl.ds(ks4, A_K_WIN), :], bias_scr[var4]),
            _band_scores(q16_ref[i], k16_ref[i], bias16_scr[...]),
        )
        dests = (pl.ds(qs, A_Q_TILE), pl.ds(res + 4 * qs4, A_Q_TILE, stride=4), pl.ds(i, A_Q_TILE, stride=16))
        v_wins = (von_scr.at[pl.ds(ks, A_K_WIN), :], vo4_scr.at[res, pl.ds(ks4, A_K_WIN), :], vo16_scr.at[i])
        for pat, (dst, (_, m_b)) in enumerate(zip(dests, parts)):
            mx_scr[pat, dst, :] = m_b
        return [(p, dst, v_win) for (p, _), dst, v_win in zip(parts, dests, v_wins)]

    def value_half(pending):
        for pat, (p, dst, v_win) in enumerate(pending):
            acc, l_b = _band_values(p, v_win[...])
            acc_scr[pat, dst, :] = acc
            sm_scr[pat, dst, :] = l_b

    pending = score_half(0)
    for i in range(1, n_tiles):
        ahead = score_half(i)
        value_half(pending)
        pending = ahead
    value_half(pending)

    def merge(c, carry):
        rows = pl.ds(pl.multiple_of(c * ROW_CHUNK, ROW_CHUNK), ROW_CHUNK)
        m_all = jnp.maximum(jnp.maximum(mx_scr[0, rows, :], mx_scr[1, rows, :]), mx_scr[2, rows, :])
        num = jnp.zeros((ROW_CHUNK, LANES), F32)
        den = jnp.zeros((ROW_CHUNK, LANES), F32)
        for pat in range(3):
            w = jnp.exp2(mx_scr[pat, rows, :] - m_all)
            num = num + w * acc_scr[pat, rows, :]
            den = den + w * sm_scr[pat, rows, :]
        o_ref[rows, :] = (num / den).astype(BF16)
        return carry

    lax.fori_loop(0, SEQ // ROW_CHUNK, merge, 0, unroll=4)


def _dilated_attention(qkv_nat, qkv_4, qkv_16):
    batch = qkv_nat[0].shape[0]
    nat_spec = pl.BlockSpec((None, None, SEQ, LANES), lambda b, p: (b, p, 0, 0))
    r4_spec = pl.BlockSpec((None, None, 4, SEQ // 4, LANES), lambda b, p: (b, p, 0, 0, 0))
    r16_spec = pl.BlockSpec((None, None, 16, SEQ // 16, LANES), lambda b, p: (b, p, 0, 0, 0))
    blk = SEQ * LANES
    est = (2 * 9 * blk * 2 + 2 * blk * 2 + 3 * blk * 2 * 2 + 9 * blk * 4
           + 4 * 2 * A_Q_TILE * A_K_WIN * 4)
    return pl.pallas_call(
        _dilated_kernel,
        grid=(batch, A_PAIRS),
        in_specs=[nat_spec] * 3 + [r4_spec] * 3 + [r16_spec] * 3,
        out_specs=pl.BlockSpec((None, None, SEQ, LANES), lambda b, p: (b, p, 0, 0)),
        out_shape=jax.ShapeDtypeStruct((batch, A_PAIRS, SEQ, LANES), BF16),
        scratch_shapes=[
            pltpu.VMEM((SEQ, 2 * LANES), BF16),
            pltpu.VMEM((4, SEQ // 4, 2 * LANES), BF16),
            pltpu.VMEM((16, SEQ // 16, 2 * LANES), BF16),
            pltpu.VMEM((3, 2 * A_Q_TILE, A_K_WIN), F32),
            pltpu.VMEM((2 * A_Q_TILE, A_Q_TILE), F32),
            pltpu.VMEM((3, SEQ, LANES), F32),
            pltpu.VMEM((3, SEQ, LANES), F32),
            pltpu.VMEM((3, SEQ, LANES), F32),
        ],
        compiler_params=pltpu.CompilerParams(
            dimension_semantics=("arbitrary", "arbitrary"), vmem_limit_bytes=_vmem_limit(est)),
        name="dilated_attn",
    )(*qkv_nat, *qkv_4, *qkv_16)


def _latent_kernel(q_ref, k_ref, v_ref, o_ref, vo_scr):
    @pl.when((pl.program_id(0) == 0) & (pl.program_id(1) == 0) & (pl.program_id(2) == 0))
    def _():
        vo_scr[:, :, LANES:] = jnp.ones((MLA_PAIRS_PER_STEP, SEQ, LANES), BF16)

    @pl.when(pl.program_id(2) == 0)
    def _():
        for pair in range(MLA_PAIRS_PER_STEP):
            vo_scr[pair, :, :LANES] = v_ref[:, pair * LANES:(pair + 1) * LANES]

    def score_half(head):
        cols = slice(head * MLA_SLOT, (head + 1) * MLA_SLOT)
        s = lax.dot_general(q_ref[:, cols], k_ref[:, cols], (((1,), (1,)), ((), ())),
                            preferred_element_type=F32)
        m = jnp.max(s, axis=-1, keepdims=True)
        return jnp.exp2((s - m).astype(BF16))

    def value_half(head, p):
        o = jnp.dot(p, vo_scr[head // 2], preferred_element_type=F32)
        return o[:, :LANES] / o[:, LANES:]

    n_heads = 2 * MLA_PAIRS_PER_STEP
    lane = lax.broadcasted_iota(jnp.int32, (MLA_Q_TILE, LANES), 1)
    outs = []
    p_next = score_half(0)
    for head in range(n_heads):
        p = p_next
        if head + 1 < n_heads:
            p_next = score_half(head + 1)
        outs.append(value_half(head, p))
        if head % 2 == 1:
            pair = head // 2
            o_ref[:, pair * LANES:(pair + 1) * LANES] = (
                jnp.where(lane < MLA_V_DIM, outs[head - 1], outs[head]).astype(BF16))


def _latent_attention(qb, kb, vb):
    batch = qb.shape[0]
    pairs = MLA_HEADS // 2 // MLA_PAIRS_PER_STEP
    qk_w = MLA_PAIRS_PER_STEP * 2 * MLA_SLOT
    v_w = MLA_PAIRS_PER_STEP * LANES
    est = (2 * MLA_Q_TILE * qk_w * 2 + 2 * SEQ * qk_w * 2 + 2 * SEQ * v_w * 2 + SEQ * 2 * v_w * 2
           + 2 * MLA_Q_TILE * v_w * 2 + 4 * MLA_PAIRS_PER_STEP * MLA_Q_TILE * SEQ * 4)
    return pl.pallas_call(
        _latent_kernel,
        grid=(batch, pairs, SEQ // MLA_Q_TILE),
        in_specs=[
            pl.BlockSpec((None, MLA_Q_TILE, qk_w), lambda b, p, i: (b, i, p)),
            pl.BlockSpec((None, SEQ, qk_w), lambda b, p, i: (b, 0, p)),
            pl.BlockSpec((None, SEQ, v_w), lambda b, p, i: (b, 0, p)),
        ],
        out_specs=pl.BlockSpec((None, MLA_Q_TILE, v_w), lambda b, p, i: (b, i, p)),
        out_shape=jax.ShapeDtypeStruct((batch, SEQ, MLA_WIDTH), BF16),
        scratch_shapes=[pltpu.VMEM((MLA_PAIRS_PER_STEP, SEQ, 2 * LANES), BF16)],
        compiler_params=pltpu.CompilerParams(
            dimension_semantics=("arbitrary", "arbitrary", "arbitrary"),
            vmem_limit_bytes=_vmem_limit(est)),
        name="latent_attn",
    )(qb, kb, vb)


def _mem_kv_kernel(mem_ref, w_ref, o_ref):
    o_ref[...] = jnp.dot(mem_ref[...].astype(BF16), w_ref[...], preferred_element_type=F32).astype(BF16)


def _mem_kv(mem, w_mem):
    batch = mem.shape[0]
    est = 2 * N_MEM * D_MODEL * 4 + 2 * w_mem.size * 2 + 2 * N_MEM * 2 * MEM_WIDTH * 2 + N_MEM * D_MODEL * 8
    return pl.pallas_call(
        _mem_kv_kernel,
        grid=(batch,),
        in_specs=[pl.BlockSpec((None, N_MEM, D_MODEL), lambda b: (b, 0, 0)),
                  pl.BlockSpec(w_mem.shape, lambda b: (0, 0))],
        out_specs=pl.BlockSpec((None, N_MEM, 2 * MEM_WIDTH), lambda b: (b, 0, 0)),
        out_shape=jax.ShapeDtypeStruct((batch, N_MEM, 2 * MEM_WIDTH), BF16),
        compiler_params=pltpu.CompilerParams(
            dimension_semantics=("arbitrary",), vmem_limit_bytes=_vmem_limit(est)),
        name="mem_kv",
    )(mem, w_mem)


def _output_kernel(x_ref, ya_ref, yb_ref, gate_ref, mq_ref, mkv_ref, wout_ref,
                   goa_ref, gob_ref, gom_ref, gemb_ref, bemb_ref, gpost_ref, bpost_ref,
                   o_ref, y_scr):
    for hd in range(MEM_HEADS):
        cols = slice(hd * MEM_HEAD_DIM, (hd + 1) * MEM_HEAD_DIM)
        s = lax.dot_general(mq_ref[:, cols], mkv_ref[:, cols], (((1,), (1,)), ((), ())),
                            preferred_element_type=F32)
        m = jnp.max(s, axis=-1, keepdims=True)
        p = jnp.exp(s - m)
        l = jnp.sum(p, axis=-1, keepdims=True)
        o = jnp.dot(p.astype(BF16), mkv_ref[:, MEM_WIDTH + hd * MEM_HEAD_DIM:MEM_WIDTH + (hd + 1) * MEM_HEAD_DIM],
                    preferred_element_type=F32)
        y_scr[:, cols] = o / l

    off_b = A_WIDTH
    off_m = A_WIDTH + MLA_WIDTH
    ya_all = jnp.concatenate([ya_ref[cg] for cg in range(A_PAIRS)], axis=1)
    ya = _rms_rows(ya_all.astype(F32), goa_ref[...]) * gate_ref[:, :off_b].astype(F32)
    yb = _rms_rows(yb_ref[...].astype(F32), gob_ref[...]) * gate_ref[:, off_b:off_m].astype(F32)
    ym = _rms_rows(y_scr[...], gom_ref[...]) * gate_ref[:, off_m:].astype(F32)
    sub = (jnp.dot(ya.astype(BF16), wout_ref[:off_b, :], preferred_element_type=F32)
           + jnp.dot(yb.astype(BF16), wout_ref[off_b:off_m, :], preferred_element_type=F32)
           + jnp.dot(ym.astype(BF16), wout_ref[off_m:, :], preferred_element_type=F32))

    for c in range(N_CHUNKS):
        h = _layer_norm_rows(x_ref[_chunk(c), :], gemb_ref[...], bemb_ref[...])
        z = DEEPNORM_ALPHA * h + sub[c * ROW_CHUNK:(c + 1) * ROW_CHUNK, :]
        o_ref[_chunk(c), :] = _layer_norm_rows(z, gpost_ref[...], bpost_ref[...])


def _output_stage(x, ya, yb, gates, mq, mkv, w_out, g_out_a, g_out_b, g_out_m, g_emb, b_emb,
                  g_post, b_post):
    batch = x.shape[0]

    def rows_spec(width):
        return pl.BlockSpec((None, ROW_TILE, width), lambda b, i: (b, i, 0))

    def full_spec(arr):
        return pl.BlockSpec(arr.shape, lambda b, i: (0,) * arr.ndim)

    est = (4 * ROW_TILE * D_MODEL * 4 + 2 * ROW_TILE * (A_WIDTH + MLA_WIDTH + D_MIX + MEM_WIDTH) * 2
           + 2 * N_MEM * 2 * MEM_WIDTH * 2 + 2 * w_out.size * 2 + ROW_TILE * MEM_WIDTH * 4
           + 4 * ROW_TILE * D_MODEL * 4)
    return pl.pallas_call(
        _output_kernel,
        grid=(batch, N_ROW_TILES),
        in_specs=[
            rows_spec(D_MODEL),
            pl.BlockSpec((None, A_PAIRS, ROW_TILE, LANES), lambda b, i: (b, 0, i, 0)),
            rows_spec(MLA_WIDTH), rows_spec(D_MIX), rows_spec(MEM_WIDTH),
            pl.BlockSpec((None, N_MEM, 2 * MEM_WIDTH), lambda b, i: (b, 0, 0)),
            full_spec(w_out), full_spec(g_out_a), full_spec(g_out_b), full_spec(g_out_m),
            full_spec(g_emb), full_spec(b_emb), full_spec(g_post), full_spec(b_post),
        ],
        out_specs=rows_spec(D_MODEL),
        out_shape=jax.ShapeDtypeStruct(x.shape, F32),
        scratch_shapes=[pltpu.VMEM((ROW_TILE, MEM_WIDTH), F32)],
        compiler_params=pltpu.CompilerParams(
            dimension_semantics=("arbitrary", "arbitrary"), vmem_limit_bytes=_vmem_limit(est)),
        name="output_stage",
    )(x, ya, yb, gates, mq, mkv, w_out, g_out_a, g_out_b, g_out_m, g_emb, b_emb, g_post, b_post)


def _rope_freq_rows(rot_dim):
    inv_freq = (np.float32(ROPE_THETA) ** (-(np.arange(0, rot_dim, 2, dtype=np.float32) / np.float32(rot_dim)))
                ).astype(np.float32)
    return jnp.asarray(np.repeat(inv_freq[:, None], LANES, axis=1))


def _a_qk_columns():
    return np.asarray([(2 * pair + hh) * A_HEAD_DIM + d
                       for pair in range(A_PAIRS) for hh, d in _a_qk_lane_order()])


def _mla_slot_lanes():
    half = MLA_ROPE_DIM // 2
    assert ROPE_FIRST_LANES == (0, half)
    lanes = [None] * MLA_SLOT
    for i in range(half):
        lanes[i] = ("rope", i)
        lanes[ROPE_PARTNER_SHIFT + i] = ("rope", half + i)
    free = [lane for lane in range(MLA_SLOT) if lanes[lane] is None]
    for d in range(MLA_NOPE_DIM):
        lanes[free[d]] = ("nope", d)
    return lanes


def _slot_gather(cols, index_of):
    n = cols.shape[-1]
    idx = np.asarray([n if index_of(lane) is None else index_of(lane) for lane in _mla_slot_lanes()])
    return jnp.pad(cols, [(0, 0)] * (cols.ndim - 1) + [(0, 1)])[..., idx]


def kernel(x, mem, positions, g_emb, b_emb, w_in, g_cq, g_ckv, w_uq, w_ukv, w_mem_kv, g_out_a, g_out_b,
           g_out_m, w_out, g_post, b_post):
    batch = x.shape[0]
    assert x.shape == (batch, SEQ, D_MODEL) and w_in.shape[0] == DEPTH == 1
    row = lambda v: v.reshape(1, -1).astype(F32)

    splits = [int(i) for i in np.cumsum(IN_SPLITS)[:-1]]
    a_q, a_k, a_v, a_g, c_q, c_kv, b_kr, b_g, m_q, m_g = jnp.split(w_in[0], splits, axis=1)
    qk_cols = _a_qk_columns()
    w_qkv = jnp.concatenate([a_q[:, qk_cols], a_k[:, qk_cols], a_v], axis=1).astype(BF16)
    rope_only = lambda lane: lane[1] if lane is not None and lane[0] == "rope" else None
    nope_only = lambda lane: lane[1] if lane is not None and lane[0] == "nope" else None
    nope_then_rope = lambda lane: None if lane is None else (lane[1] + (MLA_NOPE_DIM if lane[0] == "rope" else 0))
    kr_slot = _slot_gather(b_kr, rope_only)
    w_b = jnp.concatenate([a_g, b_g, m_g, m_q, c_q, c_kv, kr_slot], axis=1).astype(BF16)
    qk_dim = MLA_NOPE_DIM + MLA_ROPE_DIM
    w_uq_p = _slot_gather(w_uq[0].reshape(MLA_Q_RANK, MLA_HEADS, qk_dim), nope_then_rope
                          ).reshape(MLA_Q_RANK, _MLA_QK_W).astype(BF16)
    ukv = w_ukv[0].reshape(MLA_KV_RANK, MLA_HEADS, MLA_NOPE_DIM + MLA_V_DIM)
    w_uk_p = _slot_gather(ukv[:, :, :MLA_NOPE_DIM], nope_only)
    w_ukv_p = jnp.concatenate([w_uk_p.reshape(MLA_KV_RANK, _MLA_QK_W),
                               ukv[:, :, MLA_NOPE_DIM:].reshape(MLA_KV_RANK, MLA_WIDTH)], axis=1).astype(BF16)

    pos_view = positions.astype(F32).reshape(batch, N_ROW_TILES, N_CHUNKS, ROW_CHUNK)
    freq_a = _rope_freq_rows(A_ROT_DIM)
    freq_b = _rope_freq_rows(MLA_ROPE_DIM)
    g_emb_r, b_emb_r = row(g_emb), row(b_emb)

    outs_a = _proj_a(x, pos_view, freq_a, g_emb_r, b_emb_r, w_qkv)
    y_a = _dilated_attention(outs_a[0:3], outs_a[3:6], outs_a[6:9])

    gates, mq, qb, kb, vb = _proj_b(x, pos_view, freq_b, g_emb_r, b_emb_r, w_b, row(g_cq[0]),
                                    row(g_ckv[0]), w_uq_p, w_ukv_p)
    y_b = _latent_attention(qb, kb, vb)

    mkv = _mem_kv(mem, w_mem_kv[0].astype(BF16))
    return _output_stage(x, y_a, y_b, gates, mq, mkv, w_out[0].astype(BF16), row(g_out_a[0]),
                         row(g_out_b[0]), row(g_out_m[0]), g_emb_r, b_emb_r, row(g_post[0]), row(b_post[0]))
```

```python
import math

import jax
import jax.numpy as jnp
import numpy as np
from jax import lax
from jax.experimental import pallas as pl
from jax.experimental.pallas import tpu as pltpu

D_MODEL = 1024
SEQ = 2048
A_HEADS = 16
A_HEAD_DIM = 64
A_WIDTH = A_HEADS * A_HEAD_DIM
A_ROT_DIM = A_HEAD_DIM // 4
A_SIDE = 64
DILATIONS = (1, 4, 16)
MLA_HEADS = 8
MLA_Q_RANK = 256
MLA_KV_RANK = 128
MLA_NOPE_DIM = 64
MLA_ROPE_DIM = 32
MLA_V_DIM = 64
MLA_WIDTH = MLA_HEADS * MLA_V_DIM
N_MEM = 256
MEM_HEADS = 4
MEM_HEAD_DIM = 128
MEM_WIDTH = MEM_HEADS * MEM_HEAD_DIM
D_MIX = A_WIDTH + MLA_WIDTH + MEM_WIDTH
ROPE_THETA = 500000.0
NORM_EPS = 1e-5
NEG_INF = -1e30
DEPTH = 1
DEEPNORM_ALPHA = (2 * DEPTH) ** 0.25
IN_SPLITS = (A_WIDTH, A_WIDTH, A_WIDTH, A_WIDTH, MLA_Q_RANK, MLA_KV_RANK, MLA_ROPE_DIM, MLA_WIDTH,
             MEM_WIDTH, MEM_WIDTH)
LOG2_E = math.log2(math.e)

LANES = 128
VMEM_BYTES_V7X = 64 * 1024 * 1024

ROW_TILE = 512
ROW_CHUNK = 128
N_CHUNKS = ROW_TILE // ROW_CHUNK
N_HALVES = 1
ROW_HALF = ROW_TILE // N_HALVES
CHUNKS_PER_HALF = N_CHUNKS // N_HALVES
N_ROW_TILES = SEQ // ROW_TILE
A_PAIRS = A_WIDTH // LANES
A_Q_TILE = 128
A_K_WIN = 2 * A_Q_TILE
C16_PITCH = SEQ // 16 + 8
MLA_SLOT = LANES
MLA_Q_TILE = 256
MLA_PAIRS_PER_STEP = 4

F32 = jnp.float32
BF16 = jnp.bfloat16


def _vmem_limit(nbytes):
    return int(min(VMEM_BYTES_V7X - (4 << 20), max(32 << 20, nbytes + (8 << 20))))


def _layer_norm_rows(x, g, b):
    mu = jnp.mean(x, axis=-1, keepdims=True)
    xc = x - mu
    var = jnp.mean(xc * xc, axis=-1, keepdims=True)
    return xc * lax.rsqrt(var + NORM_EPS) * g + b


def _rms_rows(x, g):
    ms = jnp.mean(x * x, axis=-1, keepdims=True)
    return x * lax.rsqrt(ms + NORM_EPS) * g


def _chunk(c):
    return slice(c * ROW_CHUNK, (c + 1) * ROW_CHUNK)


def _normed_chunks_to_scratch(x_ref, g_ref, b_ref, h_scr, chunks):
    for c in chunks:
        h_scr[_chunk(c), :] = _layer_norm_rows(x_ref[_chunk(c), :], g_ref[...], b_ref[...]).astype(BF16)


ROPE_FIRST_LANES = (0, 16)
ROPE_PARTNER_SHIFT = LANES // 2


def _rope_tables(pos_ref, freq_ref, spread, chunks, cos_scr, sin_scr):
    n_freq = freq_ref.shape[0]
    lane = lax.broadcasted_iota(jnp.int32, (ROW_CHUNK, LANES), 1)
    first = (lane >= ROPE_FIRST_LANES[0]) & (lane < ROPE_FIRST_LANES[1])
    second = (lane >= ROPE_FIRST_LANES[0] + ROPE_PARTNER_SHIFT) & (lane < ROPE_FIRST_LANES[1] + ROPE_PARTNER_SHIFT)
    pad = jnp.zeros((ROW_CHUNK - n_freq, LANES), F32)
    for c in chunks:
        ang = freq_ref[...] * pos_ref[c:c + 1, :]
        cos = spread(jnp.concatenate([jnp.cos(ang), pad], axis=0).T)
        sin = spread(jnp.concatenate([jnp.sin(ang), pad], axis=0).T)
        cos_scr[_chunk(c), :] = jnp.where(first | second, cos, 1.0)
        sin_scr[_chunk(c), :] = jnp.where(first, -sin, jnp.where(second, sin, 0.0))


def _spread_a(t):
    t = t + pltpu.roll(t, A_ROT_DIM // 2, 1)
    return t + pltpu.roll(t, ROPE_PARTNER_SHIFT, 1)


def _spread_b(t):
    return t + pltpu.roll(t, ROPE_PARTNER_SHIFT, 1)


def _rope_block(x, rows, cos_scr, sin_scr):
    return x * cos_scr[rows, :] + pltpu.roll(x, ROPE_PARTNER_SHIFT, 1) * sin_scr[rows, :]


def _proj_a_kernel(x_ref, pos_ref, freq_ref, g_ref, b_ref, w_ref,
                   qn_ref, kn_ref, vn_ref, q4_ref, k4_ref, v4_ref, q16_ref, k16_ref, v16_ref,
                   h_scr, acc_scr, st4_scr, cos_scr, sin_scr):
    outs = ((qn_ref, q4_ref, q16_ref), (kn_ref, k4_ref, k16_ref), (vn_ref, v4_ref, v16_ref))
    q_scale = (A_HEAD_DIM ** -0.5) * LOG2_E

    def prepare(hf):
        chunks = range(hf * CHUNKS_PER_HALF, (hf + 1) * CHUNKS_PER_HALF)
        _normed_chunks_to_scratch(x_ref, g_ref, b_ref, h_scr, chunks)
        _rope_tables(pos_ref, freq_ref, _spread_a, chunks, cos_scr, sin_scr)

    def matmul(hf, sec):
        if sec == 1 and hf + 1 < N_HALVES:
            prepare(hf + 1)
        return jnp.dot(h_scr[hf * ROW_HALF:(hf + 1) * ROW_HALF, :],
                       w_ref[:, sec * A_WIDTH:(sec + 1) * A_WIDTH], preferred_element_type=F32)

    def epilogue(hf, sec, res):
        nat_ref, r4_ref, r16_ref = outs[sec]
        acc = acc_scr.at[hf * len(outs) + sec]
        st4 = st4_scr.at[hf * len(outs) + sec]
        for cg in range(A_PAIRS):
            cols = slice(cg * LANES, (cg + 1) * LANES)
            for lc in range(CHUNKS_PER_HALF):
                rows = _chunk(hf * CHUNKS_PER_HALF + lc)
                blk = res[lc * ROW_CHUNK:(lc + 1) * ROW_CHUNK, cols]
                if sec < 2:
                    blk = _rope_block(blk, rows, cos_scr, sin_scr)
                    if sec == 0:
                        blk = blk * q_scale
                acc[cg, lc * ROW_CHUNK:(lc + 1) * ROW_CHUNK, :] = blk
                nat_ref[cg, rows, :] = blk.astype(BF16)
            n4, n16 = ROW_HALF // 4, ROW_HALF // 16
            for r in range(4):
                rows4 = acc[cg, pl.ds(r, n4, stride=4), :]
                st4[cg, r] = rows4
                r4_ref[cg, r, hf * n4:(hf + 1) * n4, :] = rows4.astype(BF16)
            for r in range(4):
                for v in range(4):
                    r16_ref[cg, r + 4 * v, hf * n16:(hf + 1) * n16, :] = (
                        st4[cg, r, pl.ds(v, n16, stride=4), :].astype(BF16))

    prepare(0)
    items = [(hf, sec) for hf in range(N_HALVES) for sec in range(len(outs))]
    pending = None
    for hf, sec in items:
        res = matmul(hf, sec)
        if pending is not None:
            epilogue(*pending)
        pending = (hf, sec, res)
    epilogue(*pending)


def _proj_a(x, pos_view, freq_a, g_emb, b_emb, w_qkv):
    batch = x.shape[0]
    nat = jax.ShapeDtypeStruct((batch, A_PAIRS, SEQ, LANES), BF16)
    r4 = jax.ShapeDtypeStruct((batch, A_PAIRS, 4, SEQ // 4, LANES), BF16)
    r16 = jax.ShapeDtypeStruct((batch, A_PAIRS, 16, SEQ // 16, LANES), BF16)
    nat_spec = pl.BlockSpec((None, A_PAIRS, ROW_TILE, LANES), lambda b, i: (b, 0, i, 0))
    r4_spec = pl.BlockSpec((None, A_PAIRS, 4, ROW_TILE // 4, LANES), lambda b, i: (b, 0, 0, i, 0))
    r16_spec = pl.BlockSpec((None, A_PAIRS, 16, ROW_TILE // 16, LANES), lambda b, i: (b, 0, 0, i, 0))
    tile_out = ROW_TILE * A_WIDTH * 2
    est = (2 * ROW_TILE * D_MODEL * 4 + 2 * w_qkv.size * 2 + 2 * 9 * tile_out
           + ROW_TILE * D_MODEL * 2 + 2 * 3 * ROW_TILE * A_WIDTH * 4 + 3 * ROW_TILE * LANES * 4
           + 2 * ROW_HALF * A_WIDTH * 4)
    return pl.pallas_call(
        _proj_a_kernel,
        grid=(batch, N_ROW_TILES),
        in_specs=[
            pl.BlockSpec((None, ROW_TILE, D_MODEL), lambda b, i: (b, i, 0)),
            pl.BlockSpec((None, None, N_CHUNKS, ROW_CHUNK), lambda b, i: (b, i, 0, 0)),
            pl.BlockSpec(freq_a.shape, lambda b, i: (0, 0)),
            pl.BlockSpec((1, D_MODEL), lambda b, i: (0, 0)),
            pl.BlockSpec((1, D_MODEL), lambda b, i: (0, 0)),
            pl.BlockSpec(w_qkv.shape, lambda b, i: (0, 0)),
        ],
        out_specs=[nat_spec] * 3 + [r4_spec] * 3 + [r16_spec] * 3,
        out_shape=[nat] * 3 + [r4] * 3 + [r16] * 3,
        scratch_shapes=[
            pltpu.VMEM((ROW_TILE, D_MODEL), BF16),
            pltpu.VMEM((N_HALVES * 3, A_PAIRS, ROW_HALF, LANES), F32),
            pltpu.VMEM((N_HALVES * 3, A_PAIRS, 4, ROW_HALF // 4, LANES), F32),
            pltpu.VMEM((ROW_TILE, LANES), F32),
            pltpu.VMEM((ROW_TILE, LANES), F32),
        ],
        compiler_params=pltpu.CompilerParams(
            dimension_semantics=("arbitrary", "arbitrary"), vmem_limit_bytes=_vmem_limit(est)),
        name="proj_a",
    )(x, pos_view, freq_a, g_emb, b_emb, w_qkv)


_GATE_W = D_MIX
_OFF_MQ = _GATE_W
_OFF_CQ = _OFF_MQ + MEM_WIDTH
_OFF_CKV = _OFF_CQ + MLA_Q_RANK
_OFF_KR = _OFF_CKV + MLA_KV_RANK
_MLA_QK_W = MLA_HEADS * MLA_SLOT


def _proj_b_kernel(x_ref, pos_ref, freq_ref, g_ref, b_ref, w_ref, gcq_ref, gckv_ref, wuq_ref,
                   wukv_ref, gate_ref, mq_ref, qb_ref, kb_ref, vb_ref,
                   h_scr, cos_scr, sin_scr):
    mla_scale = ((MLA_NOPE_DIM + MLA_ROPE_DIM) ** -0.5) * LOG2_E
    local = lambda lc: slice(lc * ROW_CHUNK, (lc + 1) * ROW_CHUNK)
    half_rows = lambda hf: slice(hf * ROW_HALF, (hf + 1) * ROW_HALF)

    def prepare(hf):
        chunks = range(hf * CHUNKS_PER_HALF, (hf + 1) * CHUNKS_PER_HALF)
        _normed_chunks_to_scratch(x_ref, g_ref, b_ref, h_scr, chunks)
        _rope_tables(pos_ref, freq_ref, _spread_b, chunks, cos_scr, sin_scr)

    def h_dot(hf, first, width):
        return jnp.dot(h_scr[half_rows(hf), :], w_ref[:, first:first + width], preferred_element_type=F32)

    def rope_slots(hf, slots, out_ref, scale=None, extra=None):
        for lc in range(CHUNKS_PER_HALF):
            rows = _chunk(hf * CHUNKS_PER_HALF + lc)
            for hd in range(MLA_HEADS):
                cols = slice(hd * MLA_SLOT, (hd + 1) * MLA_SLOT)
                blk = slots[local(lc), cols]
                if extra is not None:
                    blk = blk + extra[local(lc), :]
                blk = _rope_block(blk, rows, cos_scr, sin_scr)
                out_ref[rows, cols] = (blk if scale is None else blk * scale).astype(BF16)

    def stages(hf):
        state = {}

        def down():
            lat = h_dot(hf, _OFF_CQ, MLA_Q_RANK + MLA_KV_RANK + MLA_SLOT)
            state["cqn"] = _rms_rows(lat[:, :MLA_Q_RANK], gcq_ref[...]).astype(BF16)
            state["ckvn"] = _rms_rows(lat[:, MLA_Q_RANK:MLA_Q_RANK + MLA_KV_RANK], gckv_ref[...]).astype(BF16)
            state["kr"] = lat[:, MLA_Q_RANK + MLA_KV_RANK:]

        def queries():
            return jnp.dot(state["cqn"], wuq_ref[...], preferred_element_type=F32)

        def keys_values():
            k_all = jnp.dot(state["ckvn"], wukv_ref[:, :_MLA_QK_W], preferred_element_type=F32)
            return k_all, jnp.dot(state["ckvn"], wukv_ref[:, _MLA_QK_W:], preferred_element_type=F32)

        def store_keys_values(res):
            k_all, vb = res
            vb_ref[half_rows(hf), :] = vb.astype(BF16)
            rope_slots(hf, k_all, kb_ref, extra=state["kr"])

        def gate_block(n):
            def produce():
                if n == 0 and hf + 1 < N_HALVES:
                    prepare(hf + 1)
                return h_dot(hf, n * A_WIDTH, A_WIDTH)

            def consume(g):
                gate_ref[half_rows(hf), n * A_WIDTH:(n + 1) * A_WIDTH] = (g / (1.0 + jnp.exp(-g))).astype(BF16)

            return produce, consume

        def store_mq(mq):
            mq_ref[half_rows(hf), :] = (mq * (MEM_HEAD_DIM ** -0.5)).astype(BF16)

        return [
            (down, lambda _: None),
            (queries, lambda q_all: rope_slots(hf, q_all, qb_ref, scale=mla_scale)),
            (keys_values, store_keys_values),
            *[gate_block(n) for n in range(_GATE_W // A_WIDTH)],
            (lambda: h_dot(hf, _OFF_MQ, MEM_WIDTH), store_mq),
        ]

    prepare(0)
    pending = None
    for produce, consume in [st for hf in range(N_HALVES) for st in stages(hf)]:
        res = produce()
        if pending is not None:
            pending[0](pending[1])
        pending = (consume, res)
    pending[0](pending[1])


def _proj_b(x, pos_view, freq_b, g_emb, b_emb, w_b, g_cq, g_ckv, w_uq_p, w_ukv_p):
    batch = x.shape[0]

    def rows_spec(width):
        return pl.BlockSpec((None, ROW_TILE, width), lambda b, i: (b, i, 0))

    def full_spec(arr):
        return pl.BlockSpec(arr.shape, lambda b, i: (0,) * arr.ndim)

    widths = (_GATE_W, MEM_WIDTH, _MLA_QK_W, _MLA_QK_W, MLA_WIDTH)
    est = (2 * ROW_TILE * D_MODEL * 4 + 2 * (w_b.size + w_uq_p.size + w_ukv_p.size) * 2
           + 2 * ROW_TILE * sum(widths) * 2 + ROW_TILE * D_MODEL * 2 + ROW_TILE * A_WIDTH * 4
           + 3 * ROW_TILE * LANES * 4 + 2 * ROW_TILE * A_WIDTH * 4)
    return pl.pallas_call(
        _proj_b_kernel,
        grid=(batch, N_ROW_TILES),
        in_specs=[
            rows_spec(D_MODEL),
            pl.BlockSpec((None, None, N_CHUNKS, ROW_CHUNK), lambda b, i: (b, i, 0, 0)),
            full_spec(freq_b),
            full_spec(g_emb), full_spec(b_emb), full_spec(w_b), full_spec(g_cq), full_spec(g_ckv),
            full_spec(w_uq_p), full_spec(w_ukv_p),
        ],
        out_specs=[rows_spec(w) for w in widths],
        out_shape=[jax.ShapeDtypeStruct((batch, SEQ, w), BF16) for w in widths],
        scratch_shapes=[
            pltpu.VMEM((ROW_TILE, D_MODEL), BF16),
            pltpu.VMEM((ROW_TILE, LANES), F32),
            pltpu.VMEM((ROW_TILE, LANES), F32),
        ],
        compiler_params=pltpu.CompilerParams(
            dimension_semantics=("arbitrary", "arbitrary"), vmem_limit_bytes=_vmem_limit(est)),
        name="proj_b",
    )(x, pos_view, freq_b, g_emb, b_emb, w_b, g_cq, g_ckv, w_uq_p, w_ukv_p)


def _first_head_lanes(n_rows):
    return lax.broadcasted_iota(jnp.int32, (n_rows, LANES), 1) < A_HEAD_DIM


def _a_qk_lane_order():
    half = A_ROT_DIM // 2
    rest = A_HEAD_DIM - A_ROT_DIM
    order = ([(0, d) for d in range(half)] + [(1, d) for d in range(half)]
             + [(0, A_ROT_DIM + d) for d in range(rest)]
             + [(0, half + d) for d in range(half)] + [(1, half + d) for d in range(half)]
             + [(1, A_ROT_DIM + d) for d in range(rest)])
    assert len(order) == LANES and len(set(order)) == LANES
    return order


def _first_head_qk_lanes(n_rows):
    lane = lax.broadcasted_iota(jnp.int32, (n_rows, LANES), 1)
    half = A_ROT_DIM // 2
    return (lane < half) | ((lane >= A_ROT_DIM) & (lane < A_ROT_DIM + A_HEAD_DIM - half))


def _band_scores(q, k, bias):
    n_q = q.shape[0]
    first_head = _first_head_lanes(n_q)
    first_head_qk = _first_head_qk_lanes(n_q)
    zero = jnp.zeros_like(q)
    q2 = jnp.concatenate([jnp.where(first_head_qk, q, zero), jnp.where(first_head_qk, zero, q)], axis=0)
    s = lax.dot_general(q2, k, (((1,), (1,)), ((), ())), preferred_element_type=F32) + bias
    m = jnp.max(s, axis=-1, keepdims=True)
    p = jnp.exp2((s - m).astype(BF16))
    return p, jnp.where(first_head, m[:n_q], m[n_q:])


def _band_values(p, v_ones):
    n_q = p.shape[0] // 2
    first_head = _first_head_lanes(n_q)
    o = jnp.dot(p, v_ones, preferred_element_type=F32)
    acc = jnp.where(first_head, o[:n_q, :LANES], o[n_q:, :LANES])
    l_b = jnp.where(first_head, o[:n_q, LANES:], o[n_q:, LANES:])
    return acc, l_b


def _dilated_kernel(qn_ref, kn_ref, vn_ref, q4_ref, k4_ref, v4_ref, q16_ref, k16_ref, v16_ref,
                    o_ref, von_scr, vo4_scr, vo16_scr, bias_scr, bias16_scr, acc_scr, mx_scr, sm_scr, c16_scr):
    def window_bias(n_keys, first_key):
        row = lax.broadcasted_iota(jnp.int32, (2 * A_Q_TILE, n_keys), 0)
        col = lax.broadcasted_iota(jnp.int32, (2 * A_Q_TILE, n_keys), 1)
        q_idx = jnp.where(row >= A_Q_TILE, row - A_Q_TILE, row)
        off = col - (q_idx + first_key)
        return jnp.where(jnp.abs(off) <= A_SIDE, 0.0, NEG_INF).astype(F32)

    @pl.when((pl.program_id(0) == 0) & (pl.program_id(1) == 0))
    def _():
        for variant in range(3):
            bias_scr[variant] = window_bias(A_K_WIN, variant * A_SIDE)
        bias16_scr[...] = window_bias(A_Q_TILE, 0)
        von_scr[:, LANES:] = jnp.ones((SEQ, LANES), BF16)
        vo4_scr[:, :, LANES:] = jnp.ones((4, SEQ // 4, LANES), BF16)
        vo16_scr[:, :, LANES:] = jnp.ones((16, SEQ // 16, LANES), BF16)

    von_scr[:, :LANES] = vn_ref[...]
    vo4_scr[:, :, :LANES] = v4_ref[...]
    vo16_scr[:, :, :LANES] = v16_ref[...]

    len4 = SEQ // 4
    tiles4 = len4 // A_Q_TILE
    n_tiles = SEQ // A_Q_TILE

    def window(tile, n_seq_tiles, seq_len):
        qs = tile * A_Q_TILE
        ks = min(max(qs - A_SIDE, 0), seq_len - A_K_WIN)
        variant = 0 if tile == 0 else (2 if tile == n_seq_tiles - 1 else 1)
        return qs, ks, variant

    acc16, mx16, sm16 = (c16_scr.at[k] for k in range(3))

    def score_half(i):
        qs, ks, var = window(i, n_tiles, SEQ)
        res = i // tiles4
        qs4, ks4, var4 = window(i % tiles4, tiles4, len4)
        parts = (
            _band_scores(qn_ref[pl.ds(qs, A_Q_TILE), :], kn_ref[pl.ds(ks, A_K_WIN), :], bias_scr[var]),
            _band_scores(q4_ref[res, pl.ds(qs4, A_Q_TILE), :], k4_ref[res, pl.ds(ks4, A_K_WIN), :], bias_scr[var4]),
            _band_scores(q16_ref[i], k16_ref[i], bias16_scr[...]),
        )
        class16 = pl.ds(i * C16_PITCH, A_Q_TILE)
        dests = ((acc_scr.at[0], mx_scr.at[0], sm_scr.at[0], pl.ds(qs, A_Q_TILE)),
                 (acc_scr.at[1], mx_scr.at[1], sm_scr.at[1], pl.ds(res + 4 * qs4, A_Q_TILE, stride=4)),
                 (acc16, mx16, sm16, class16))
        v_wins = (von_scr.at[pl.ds(ks, A_K_WIN), :], vo4_scr.at[res, pl.ds(ks4, A_K_WIN), :], vo16_scr.at[i])
        for (_, mx_at, _, dst), (_, m_b) in zip(dests, parts):
            mx_at[dst, :] = m_b
        return [(p, dest, v_win) for (p, _), dest, v_win in zip(parts, dests, v_wins)]

    def value_half(pending):
        for p, (acc_at, _, sm_at, dst), v_win in pending:
            acc, l_b = _band_values(p, v_win[...])
            acc_at[dst, :] = acc
            sm_at[dst, :] = l_b

    pending = score_half(0)
    for i in range(1, n_tiles):
        ahead = score_half(i)
        value_half(pending)
        pending = ahead
    value_half(pending)

    def class16_rows(scr, c):
        groups = []
        for g in range(ROW_CHUNK // 8):
            first_class = 8 * (g % 2)
            t = (ROW_CHUNK // 16) * c + g // 2
            groups.append(scr[pl.ds(first_class * C16_PITCH + t, 8, stride=C16_PITCH), :])
        return jnp.concatenate(groups, axis=0)

    def merge(c, carry):
        rows = pl.ds(pl.multiple_of(c * ROW_CHUNK, ROW_CHUNK), ROW_CHUNK)
        parts = [(acc_scr[pat, rows, :], mx_scr[pat, rows, :], sm_scr[pat, rows, :]) for pat in range(2)]
        parts.append((class16_rows(acc16, c), class16_rows(mx16, c), class16_rows(sm16, c)))
        m_all = jnp.maximum(jnp.maximum(parts[0][1], parts[1][1]), parts[2][1])
        num = jnp.zeros((ROW_CHUNK, LANES), F32)
        den = jnp.zeros((ROW_CHUNK, LANES), F32)
        for acc, m_b, l_b in parts:
            w = jnp.exp2(m_b - m_all)
            num = num + w * acc
            den = den + w * l_b
        o_ref[rows, :] = (num / den).astype(BF16)
        return carry

    lax.fori_loop(0, SEQ // ROW_CHUNK, merge, 0, unroll=4)


def _dilated_attention(qkv_nat, qkv_4, qkv_16):
    batch = qkv_nat[0].shape[0]
    nat_spec = pl.BlockSpec((None, None, SEQ, LANES), lambda b, p: (b, p, 0, 0))
    r4_spec = pl.BlockSpec((None, None, 4, SEQ // 4, LANES), lambda b, p: (b, p, 0, 0, 0))
    r16_spec = pl.BlockSpec((None, None, 16, SEQ // 16, LANES), lambda b, p: (b, p, 0, 0, 0))
    blk = SEQ * LANES
    est = (2 * 9 * blk * 2 + 2 * blk * 2 + 3 * blk * 2 * 2 + 9 * blk * 4
           + 4 * 2 * A_Q_TILE * A_K_WIN * 4)
    return pl.pallas_call(
        _dilated_kernel,
        grid=(batch, A_PAIRS),
        in_specs=[nat_spec] * 3 + [r4_spec] * 3 + [r16_spec] * 3,
        out_specs=pl.BlockSpec((None, None, SEQ, LANES), lambda b, p: (b, p, 0, 0)),
        out_shape=jax.ShapeDtypeStruct((batch, A_PAIRS, SEQ, LANES), BF16),
        scratch_shapes=[
            pltpu.VMEM((SEQ, 2 * LANES), BF16),
            pltpu.VMEM((4, SEQ // 4, 2 * LANES), BF16),
            pltpu.VMEM((16, SEQ // 16, 2 * LANES), BF16),
            pltpu.VMEM((3, 2 * A_Q_TILE, A_K_WIN), F32),
            pltpu.VMEM((2 * A_Q_TILE, A_Q_TILE), F32),
            pltpu.VMEM((2, SEQ, LANES), F32),
            pltpu.VMEM((2, SEQ, LANES), F32),
            pltpu.VMEM((2, SEQ, LANES), F32),
            pltpu.VMEM((3, 16 * C16_PITCH, LANES), F32),
        ],
        compiler_params=pltpu.CompilerParams(
            dimension_semantics=("arbitrary", "arbitrary"), vmem_limit_bytes=_vmem_limit(est)),
        name="dilated_attn",
    )(*qkv_nat, *qkv_4, *qkv_16)


def _latent_kernel(q_ref, k_ref, v_ref, o_ref, vo_scr):
    @pl.when((pl.program_id(0) == 0) & (pl.program_id(1) == 0) & (pl.program_id(2) == 0))
    def _():
        vo_scr[:, :, LANES:] = jnp.ones((MLA_PAIRS_PER_STEP, SEQ, LANES), BF16)

    @pl.when(pl.program_id(2) == 0)
    def _():
        for pair in range(MLA_PAIRS_PER_STEP):
            vo_scr[pair, :, :LANES] = v_ref[:, pair * LANES:(pair + 1) * LANES]

    def score_half(head):
        cols = slice(head * MLA_SLOT, (head + 1) * MLA_SLOT)
        s = lax.dot_general(q_ref[:, cols], k_ref[:, cols], (((1,), (1,)), ((), ())),
                            preferred_element_type=F32)
        m = jnp.max(s, axis=-1, keepdims=True)
        return jnp.exp2((s - m).astype(BF16))

    def value_half(head, p):
        o = jnp.dot(p, vo_scr[head // 2], preferred_element_type=F32)
        return o[:, :LANES] / o[:, LANES:]

    n_heads = 2 * MLA_PAIRS_PER_STEP
    lane = lax.broadcasted_iota(jnp.int32, (MLA_Q_TILE, LANES), 1)
    outs = []
    p_next = score_half(0)
    for head in range(n_heads):
        p = p_next
        if head + 1 < n_heads:
            p_next = score_half(head + 1)
        outs.append(value_half(head, p))
        if head % 2 == 1:
            pair = head // 2
            o_ref[:, pair * LANES:(pair + 1) * LANES] = (
                jnp.where(lane < MLA_V_DIM, outs[head - 1], outs[head]).astype(BF16))


def _latent_attention(qb, kb, vb):
    batch = qb.shape[0]
    pairs = MLA_HEADS // 2 // MLA_PAIRS_PER_STEP
    qk_w = MLA_PAIRS_PER_STEP * 2 * MLA_SLOT
    v_w = MLA_PAIRS_PER_STEP * LANES
    est = (2 * MLA_Q_TILE * qk_w * 2 + 2 * SEQ * qk_w * 2 + 2 * SEQ * v_w * 2 + SEQ * 2 * v_w * 2
           + 2 * MLA_Q_TILE * v_w * 2 + 4 * MLA_PAIRS_PER_STEP * MLA_Q_TILE * SEQ * 4)
    return pl.pallas_call(
        _latent_kernel,
        grid=(batch, pairs, SEQ // MLA_Q_TILE),
        in_specs=[
            pl.BlockSpec((None, MLA_Q_TILE, qk_w), lambda b, p, i: (b, i, p)),
            pl.BlockSpec((None, SEQ, qk_w), lambda b, p, i: (b, 0, p)),
            pl.BlockSpec((None, SEQ, v_w), lambda b, p, i: (b, 0, p)),
        ],
        out_specs=pl.BlockSpec((None, MLA_Q_TILE, v_w), lambda b, p, i: (b, i, p)),
        out_shape=jax.ShapeDtypeStruct((batch, SEQ, MLA_WIDTH), BF16),
        scratch_shapes=[pltpu.VMEM((MLA_PAIRS_PER_STEP, SEQ, 2 * LANES), BF16)],
        compiler_params=pltpu.CompilerParams(
            dimension_semantics=("arbitrary", "arbitrary", "arbitrary"),
            vmem_limit_bytes=_vmem_limit(est)),
        name="latent_attn",
    )(qb, kb, vb)


def _mem_kv_kernel(mem_ref, w_ref, o_ref):
    o_ref[...] = jnp.dot(mem_ref[...].astype(BF16), w_ref[...], preferred_element_type=F32).astype(BF16)


def _mem_kv(mem, w_mem):
    batch = mem.shape[0]
    est = 2 * N_MEM * D_MODEL * 4 + 2 * w_mem.size * 2 + 2 * N_MEM * 2 * MEM_WIDTH * 2 + N_MEM * D_MODEL * 8
    return pl.pallas_call(
        _mem_kv_kernel,
        grid=(batch,),
        in_specs=[pl.BlockSpec((None, N_MEM, D_MODEL), lambda b: (b, 0, 0)),
                  pl.BlockSpec(w_mem.shape, lambda b: (0, 0))],
        out_specs=pl.BlockSpec((None, N_MEM, 2 * MEM_WIDTH), lambda b: (b, 0, 0)),
        out_shape=jax.ShapeDtypeStruct((batch, N_MEM, 2 * MEM_WIDTH), BF16),
        compiler_params=pltpu.CompilerParams(
            dimension_semantics=("arbitrary",), vmem_limit_bytes=_vmem_limit(est)),
        name="mem_kv",
    )(mem, w_mem)


def _output_kernel(x_ref, ya_ref, yb_ref, gate_ref, mq_ref, mkv_ref, wout_ref,
                   goa_ref, gob_ref, gom_ref, gemb_ref, bemb_ref, gpost_ref, bpost_ref,
                   o_ref, y_scr):
    for hd in range(MEM_HEADS):
        cols = slice(hd * MEM_HEAD_DIM, (hd + 1) * MEM_HEAD_DIM)
        s = lax.dot_general(mq_ref[:, cols], mkv_ref[:, cols], (((1,), (1,)), ((), ())),
                            preferred_element_type=F32)
        m = jnp.max(s, axis=-1, keepdims=True)
        p = jnp.exp(s - m)
        l = jnp.sum(p, axis=-1, keepdims=True)
        o = jnp.dot(p.astype(BF16), mkv_ref[:, MEM_WIDTH + hd * MEM_HEAD_DIM:MEM_WIDTH + (hd + 1) * MEM_HEAD_DIM],
                    preferred_element_type=F32)
        y_scr[:, cols] = o / l

    off_b = A_WIDTH
    off_m = A_WIDTH + MLA_WIDTH
    ya_all = jnp.concatenate([ya_ref[cg] for cg in range(A_PAIRS)], axis=1)
    ya = _rms_rows(ya_all.astype(F32), goa_ref[...]) * gate_ref[:, :off_b].astype(F32)
    yb = _rms_rows(yb_ref[...].astype(F32), gob_ref[...]) * gate_ref[:, off_b:off_m].astype(F32)
    ym = _rms_rows(y_scr[...], gom_ref[...]) * gate_ref[:, off_m:].astype(F32)
    sub = (jnp.dot(ya.astype(BF16), wout_ref[:off_b, :], preferred_element_type=F32)
           + jnp.dot(yb.astype(BF16), wout_ref[off_b:off_m, :], preferred_element_type=F32)
           + jnp.dot(ym.astype(BF16), wout_ref[off_m:, :], preferred_element_type=F32))

    for c in range(N_CHUNKS):
        h = _layer_norm_rows(x_ref[_chunk(c), :], gemb_ref[...], bemb_ref[...])
        z = DEEPNORM_ALPHA * h + sub[c * ROW_CHUNK:(c + 1) * ROW_CHUNK, :]
        o_ref[_chunk(c), :] = _layer_norm_rows(z, gpost_ref[...], bpost_ref[...])


def _output_stage(x, ya, yb, gates, mq, mkv, w_out, g_out_a, g_out_b, g_out_m, g_emb, b_emb,
                  g_post, b_post):
    batch = x.shape[0]

    def rows_spec(width):
        return pl.BlockSpec((None, ROW_TILE, width), lambda b, i: (b, i, 0))

    def full_spec(arr):
        return pl.BlockSpec(arr.shape, lambda b, i: (0,) * arr.ndim)

    est = (4 * ROW_TILE * D_MODEL * 4 + 2 * ROW_TILE * (A_WIDTH + MLA_WIDTH + D_MIX + MEM_WIDTH) * 2
           + 2 * N_MEM * 2 * MEM_WIDTH * 2 + 2 * w_out.size * 2 + ROW_TILE * MEM_WIDTH * 4
           + 4 * ROW_TILE * D_MODEL * 4)
    return pl.pallas_call(
        _output_kernel,
        grid=(batch, N_ROW_TILES),
        in_specs=[
            rows_spec(D_MODEL),
            pl.BlockSpec((None, A_PAIRS, ROW_TILE, LANES), lambda b, i: (b, 0, i, 0)),
            rows_spec(MLA_WIDTH), rows_spec(D_MIX), rows_spec(MEM_WIDTH),
            pl.BlockSpec((None, N_MEM, 2 * MEM_WIDTH), lambda b, i: (b, 0, 0)),
            full_spec(w_out), full_spec(g_out_a), full_spec(g_out_b), full_spec(g_out_m),
            full_spec(g_emb), full_spec(b_emb), full_spec(g_post), full_spec(b_post),
        ],
        out_specs=rows_spec(D_MODEL),
        out_shape=jax.ShapeDtypeStruct(x.shape, F32),
        scratch_shapes=[pltpu.VMEM((ROW_TILE, MEM_WIDTH), F32)],
        compiler_params=pltpu.CompilerParams(
            dimension_semantics=("arbitrary", "arbitrary"), vmem_limit_bytes=_vmem_limit(est)),
        name="output_stage",
    )(x, ya, yb, gates, mq, mkv, w_out, g_out_a, g_out_b, g_out_m, g_emb, b_emb, g_post, b_post)


def _rope_freq_rows(rot_dim):
    inv_freq = (np.float32(ROPE_THETA) ** (-(np.arange(0, rot_dim, 2, dtype=np.float32) / np.float32(rot_dim)))
                ).astype(np.float32)
    return jnp.asarray(np.repeat(inv_freq[:, None], LANES, axis=1))


def _a_qk_columns():
    return np.asarray([(2 * pair + hh) * A_HEAD_DIM + d
                       for pair in range(A_PAIRS) for hh, d in _a_qk_lane_order()])


def _mla_slot_lanes():
    half = MLA_ROPE_DIM // 2
    assert ROPE_FIRST_LANES == (0, half)
    lanes = [None] * MLA_SLOT
    for i in range(half):
        lanes[i] = ("rope", i)
        lanes[ROPE_PARTNER_SHIFT + i] = ("rope", half + i)
    free = [lane for lane in range(MLA_SLOT) if lanes[lane] is None]
    for d in range(MLA_NOPE_DIM):
        lanes[free[d]] = ("nope", d)
    return lanes


def _slot_gather(cols, index_of):
    n = cols.shape[-1]
    idx = np.asarray([n if index_of(lane) is None else index_of(lane) for lane in _mla_slot_lanes()])
    return jnp.pad(cols, [(0, 0)] * (cols.ndim - 1) + [(0, 1)])[..., idx]


def kernel(x, mem, positions, g_emb, b_emb, w_in, g_cq, g_ckv, w_uq, w_ukv, w_mem_kv, g_out_a, g_out_b,
           g_out_m, w_out, g_post, b_post):
    batch = x.shape[0]
    assert x.shape == (batch, SEQ, D_MODEL) and w_in.shape[0] == DEPTH == 1
    row = lambda v: v.reshape(1, -1).astype(F32)

    splits = [int(i) for i in np.cumsum(IN_SPLITS)[:-1]]
    a_q, a_k, a_v, a_g, c_q, c_kv, b_kr, b_g, m_q, m_g = jnp.split(w_in[0], splits, axis=1)
    qk_cols = _a_qk_columns()
    w_qkv = jnp.concatenate([a_q[:, qk_cols], a_k[:, qk_cols], a_v], axis=1).astype(BF16)
    rope_only = lambda lane: lane[1] if lane is not None and lane[0] == "rope" else None
    nope_only = lambda lane: lane[1] if lane is not None and lane[0] == "nope" else None
    nope_then_rope = lambda lane: None if lane is None else (lane[1] + (MLA_NOPE_DIM if lane[0] == "rope" else 0))
    kr_slot = _slot_gather(b_kr, rope_only)
    w_b = jnp.concatenate([a_g, b_g, m_g, m_q, c_q, c_kv, kr_slot], axis=1).astype(BF16)
    qk_dim = MLA_NOPE_DIM + MLA_ROPE_DIM
    w_uq_p = _slot_gather(w_uq[0].reshape(MLA_Q_RANK, MLA_HEADS, qk_dim), nope_then_rope
                          ).reshape(MLA_Q_RANK, _MLA_QK_W).astype(BF16)
    ukv = w_ukv[0].reshape(MLA_KV_RANK, MLA_HEADS, MLA_NOPE_DIM + MLA_V_DIM)
    w_uk_p = _slot_gather(ukv[:, :, :MLA_NOPE_DIM], nope_only)
    w_ukv_p = jnp.concatenate([w_uk_p.reshape(MLA_KV_RANK, _MLA_QK_W),
                               ukv[:, :, MLA_NOPE_DIM:].reshape(MLA_KV_RANK, MLA_WIDTH)], axis=1).astype(BF16)

    pos_view = positions.astype(F32).reshape(batch, N_ROW_TILES, N_CHUNKS, ROW_CHUNK)
    freq_a = _rope_freq_rows(A_ROT_DIM)
    freq_b = _rope_freq_rows(MLA_ROPE_DIM)
    g_emb_r, b_emb_r = row(g_emb), row(b_emb)

    outs_a = _proj_a(x, pos_view, freq_a, g_emb_r, b_emb_r, w_qkv)
    y_a = _dilated_attention(outs_a[0:3], outs_a[3:6], outs_a[6:9])

    gates, mq, qb, kb, vb = _proj_b(x, pos_view, freq_b, g_emb_r, b_emb_r, w_b, row(g_cq[0]),
                                    row(g_ckv[0]), w_uq_p, w_ukv_p)
    y_b = _latent_attention(qb, kb, vb)

    mkv = _mem_kv(mem, w_mem_kv[0].astype(BF16))
    return _output_stage(x, y_a, y_b, gates, mq, mkv, w_out[0].astype(BF16), row(g_out_a[0]),
                         row(g_out_b[0]), row(g_out_m[0]), g_emb_r, b_emb_r, row(g_post[0]), row(b_post[0]))
```

```python
import math

import jax
import jax.numpy as jnp
import numpy as np
from jax import lax
from jax.experimental import pallas as pl
from jax.experimental.pallas import tpu as pltpu

D_MODEL = 1024
SEQ = 2048
A_HEADS = 16
A_HEAD_DIM = 64
A_WIDTH = A_HEADS * A_HEAD_DIM
A_ROT_DIM = A_HEAD_DIM // 4
A_SIDE = 64
DILATIONS = (1, 4, 16)
MLA_HEADS = 8
MLA_Q_RANK = 256
MLA_KV_RANK = 128
MLA_NOPE_DIM = 64
MLA_ROPE_DIM = 32
MLA_V_DIM = 64
MLA_WIDTH = MLA_HEADS * MLA_V_DIM
N_MEM = 256
MEM_HEADS = 4
MEM_HEAD_DIM = 128
MEM_WIDTH = MEM_HEADS * MEM_HEAD_DIM
D_MIX = A_WIDTH + MLA_WIDTH + MEM_WIDTH
ROPE_THETA = 500000.0
NORM_EPS = 1e-5
NEG_INF = -1e30
DEPTH = 1
DEEPNORM_ALPHA = (2 * DEPTH) ** 0.25
IN_SPLITS = (A_WIDTH, A_WIDTH, A_WIDTH, A_WIDTH, MLA_Q_RANK, MLA_KV_RANK, MLA_ROPE_DIM, MLA_WIDTH,
             MEM_WIDTH, MEM_WIDTH)
LOG2_E = math.log2(math.e)

LANES = 128
VMEM_BYTES_V7X = 64 * 1024 * 1024

ROW_TILE = 512
ROW_CHUNK = 128
N_CHUNKS = ROW_TILE // ROW_CHUNK
N_HALVES = 1
ROW_HALF = ROW_TILE // N_HALVES
CHUNKS_PER_HALF = N_CHUNKS // N_HALVES
N_ROW_TILES = SEQ // ROW_TILE
A_PAIRS = A_WIDTH // LANES
A_Q_TILE = 128
A_K_WIN = 2 * A_Q_TILE
C16_PITCH = SEQ // 16 + 8
MLA_SLOT = LANES
MLA_Q_TILE = 256
MLA_Q_TILES_PER_STEP = 2
MLA_PAIRS_PER_STEP = 4

F32 = jnp.float32
BF16 = jnp.bfloat16


def _vmem_limit(nbytes):
    return int(min(VMEM_BYTES_V7X - (4 << 20), max(32 << 20, nbytes + (8 << 20))))


def _layer_norm_rows(x, g, b):
    mu = jnp.mean(x, axis=-1, keepdims=True)
    xc = x - mu
    var = jnp.mean(xc * xc, axis=-1, keepdims=True)
    return xc * lax.rsqrt(var + NORM_EPS) * g + b


def _rms_rows(x, g):
    ms = jnp.mean(x * x, axis=-1, keepdims=True)
    return x * lax.rsqrt(ms + NORM_EPS) * g


def _chunk(c):
    return slice(c * ROW_CHUNK, (c + 1) * ROW_CHUNK)


def _normed_chunks_to_scratch(x_ref, g_ref, b_ref, h_scr, chunks):
    for c in chunks:
        h_scr[_chunk(c), :] = _layer_norm_rows(x_ref[_chunk(c), :], g_ref[...], b_ref[...]).astype(BF16)


ROPE_FIRST_LANES = (0, 16)
ROPE_PARTNER_SHIFT = LANES // 2


def _rope_tables(pos_ref, freq_ref, spread, chunks, cos_scr, sin_scr):
    n_freq = freq_ref.shape[0]
    lane = lax.broadcasted_iota(jnp.int32, (ROW_CHUNK, LANES), 1)
    first = (lane >= ROPE_FIRST_LANES[0]) & (lane < ROPE_FIRST_LANES[1])
    second = (lane >= ROPE_FIRST_LANES[0] + ROPE_PARTNER_SHIFT) & (lane < ROPE_FIRST_LANES[1] + ROPE_PARTNER_SHIFT)
    pad = jnp.zeros((ROW_CHUNK - n_freq, LANES), F32)
    for c in chunks:
        ang = freq_ref[...] * pos_ref[c:c + 1, :]
        cos = spread(jnp.concatenate([jnp.cos(ang), pad], axis=0).T)
        sin = spread(jnp.concatenate([jnp.sin(ang), pad], axis=0).T)
        cos_scr[_chunk(c), :] = jnp.where(first | second, cos, 1.0)
        sin_scr[_chunk(c), :] = jnp.where(first, -sin, jnp.where(second, sin, 0.0))


def _spread_a(t):
    t = t + pltpu.roll(t, A_ROT_DIM // 2, 1)
    return t + pltpu.roll(t, ROPE_PARTNER_SHIFT, 1)


def _spread_b(t):
    return t + pltpu.roll(t, ROPE_PARTNER_SHIFT, 1)


def _rope_block(x, rows, cos_scr, sin_scr):
    return x * cos_scr[rows, :] + pltpu.roll(x, ROPE_PARTNER_SHIFT, 1) * sin_scr[rows, :]


def _proj_a_kernel(x_ref, pos_ref, freq_ref, g_ref, b_ref, w_ref,
                   qn_ref, kn_ref, vn_ref, q4_ref, k4_ref, v4_ref, q16_ref, k16_ref, v16_ref,
                   h_scr, acc_scr, st4_scr, cos_scr, sin_scr):
    outs = ((qn_ref, q4_ref, q16_ref), (kn_ref, k4_ref, k16_ref), (vn_ref, v4_ref, v16_ref))
    q_scale = (A_HEAD_DIM ** -0.5) * LOG2_E

    def prepare(hf):
        chunks = range(hf * CHUNKS_PER_HALF, (hf + 1) * CHUNKS_PER_HALF)
        _normed_chunks_to_scratch(x_ref, g_ref, b_ref, h_scr, chunks)
        _rope_tables(pos_ref, freq_ref, _spread_a, chunks, cos_scr, sin_scr)

    def matmul(hf, sec):
        if sec == 1 and hf + 1 < N_HALVES:
            prepare(hf + 1)
        return jnp.dot(h_scr[hf * ROW_HALF:(hf + 1) * ROW_HALF, :],
                       w_ref[:, sec * A_WIDTH:(sec + 1) * A_WIDTH], preferred_element_type=F32)

    def epilogue(hf, sec, res):
        nat_ref, r4_ref, r16_ref = outs[sec]
        acc = acc_scr.at[hf * len(outs) + sec]
        st4 = st4_scr.at[hf * len(outs) + sec]
        for cg in range(A_PAIRS):
            cols = slice(cg * LANES, (cg + 1) * LANES)
            for lc in range(CHUNKS_PER_HALF):
                rows = _chunk(hf * CHUNKS_PER_HALF + lc)
                blk = res[lc * ROW_CHUNK:(lc + 1) * ROW_CHUNK, cols]
                if sec < 2:
                    blk = _rope_block(blk, rows, cos_scr, sin_scr)
                    if sec == 0:
                        blk = blk * q_scale
                acc[cg, lc * ROW_CHUNK:(lc + 1) * ROW_CHUNK, :] = blk
                nat_ref[cg, rows, :] = blk.astype(BF16)
            n4, n16 = ROW_HALF // 4, ROW_HALF // 16
            for r in range(4):
                rows4 = acc[cg, pl.ds(r, n4, stride=4), :]
                st4[cg, r] = rows4
                r4_ref[cg, r, hf * n4:(hf + 1) * n4, :] = rows4.astype(BF16)
            for r in range(4):
                for v in range(4):
                    r16_ref[cg, r + 4 * v, hf * n16:(hf + 1) * n16, :] = (
                        st4[cg, r, pl.ds(v, n16, stride=4), :].astype(BF16))

    prepare(0)
    items = [(hf, sec) for hf in range(N_HALVES) for sec in range(len(outs))]
    pending = None
    for hf, sec in items:
        res = matmul(hf, sec)
        if pending is not None:
            epilogue(*pending)
        pending = (hf, sec, res)
    epilogue(*pending)


def _proj_a(x, pos_view, freq_a, g_emb, b_emb, w_qkv):
    batch = x.shape[0]
    nat = jax.ShapeDtypeStruct((batch, A_PAIRS, SEQ, LANES), BF16)
    r4 = jax.ShapeDtypeStruct((batch, A_PAIRS, 4, SEQ // 4, LANES), BF16)
    r16 = jax.ShapeDtypeStruct((batch, A_PAIRS, 16, SEQ // 16, LANES), BF16)
    nat_spec = pl.BlockSpec((None, A_PAIRS, ROW_TILE, LANES), lambda b, i: (b, 0, i, 0))
    r4_spec = pl.BlockSpec((None, A_PAIRS, 4, ROW_TILE // 4, LANES), lambda b, i: (b, 0, 0, i, 0))
    r16_spec = pl.BlockSpec((None, A_PAIRS, 16, ROW_TILE // 16, LANES), lambda b, i: (b, 0, 0, i, 0))
    tile_out = ROW_TILE * A_WIDTH * 2
    est = (2 * ROW_TILE * D_MODEL * 4 + 2 * w_qkv.size * 2 + 2 * 9 * tile_out
           + ROW_TILE * D_MODEL * 2 + 2 * 3 * ROW_TILE * A_WIDTH * 4 + 3 * ROW_TILE * LANES * 4
           + 2 * ROW_HALF * A_WIDTH * 4)
    return pl.pallas_call(
        _proj_a_kernel,
        grid=(batch, N_ROW_TILES),
        in_specs=[
            pl.BlockSpec((None, ROW_TILE, D_MODEL), lambda b, i: (b, i, 0)),
            pl.BlockSpec((None, None, N_CHUNKS, ROW_CHUNK), lambda b, i: (b, i, 0, 0)),
            pl.BlockSpec(freq_a.shape, lambda b, i: (0, 0)),
            pl.BlockSpec((1, D_MODEL), lambda b, i: (0, 0)),
            pl.BlockSpec((1, D_MODEL), lambda b, i: (0, 0)),
            pl.BlockSpec(w_qkv.shape, lambda b, i: (0, 0)),
        ],
        out_specs=[nat_spec] * 3 + [r4_spec] * 3 + [r16_spec] * 3,
        out_shape=[nat] * 3 + [r4] * 3 + [r16] * 3,
        scratch_shapes=[
            pltpu.VMEM((ROW_TILE, D_MODEL), BF16),
            pltpu.VMEM((N_HALVES * 3, A_PAIRS, ROW_HALF, LANES), F32),
            pltpu.VMEM((N_HALVES * 3, A_PAIRS, 4, ROW_HALF // 4, LANES), F32),
            pltpu.VMEM((ROW_TILE, LANES), F32),
            pltpu.VMEM((ROW_TILE, LANES), F32),
        ],
        compiler_params=pltpu.CompilerParams(
            dimension_semantics=("arbitrary", "arbitrary"), vmem_limit_bytes=_vmem_limit(est)),
        name="proj_a",
    )(x, pos_view, freq_a, g_emb, b_emb, w_qkv)


_GATE_W = D_MIX
_OFF_MQ = _GATE_W
_OFF_CQ = _OFF_MQ + MEM_WIDTH
_OFF_CKV = _OFF_CQ + MLA_Q_RANK
_OFF_KR = _OFF_CKV + MLA_KV_RANK
_MLA_QK_W = MLA_HEADS * MLA_SLOT


def _proj_b_kernel(x_ref, pos_ref, freq_ref, g_ref, b_ref, w_ref, gcq_ref, gckv_ref, wuq_ref,
                   wukv_ref, gate_ref, mq_ref, qb_ref, kb_ref, vb_ref,
                   h_scr, cos_scr, sin_scr):
    mla_scale = ((MLA_NOPE_DIM + MLA_ROPE_DIM) ** -0.5) * LOG2_E
    local = lambda lc: slice(lc * ROW_CHUNK, (lc + 1) * ROW_CHUNK)
    half_rows = lambda hf: slice(hf * ROW_HALF, (hf + 1) * ROW_HALF)

    def prepare(hf):
        chunks = range(hf * CHUNKS_PER_HALF, (hf + 1) * CHUNKS_PER_HALF)
        _normed_chunks_to_scratch(x_ref, g_ref, b_ref, h_scr, chunks)
        _rope_tables(pos_ref, freq_ref, _spread_b, chunks, cos_scr, sin_scr)

    def h_dot(hf, first, width):
        return jnp.dot(h_scr[half_rows(hf), :], w_ref[:, first:first + width], preferred_element_type=F32)

    def rope_slots(hf, slots, out_ref, scale=None, extra=None):
        for lc in range(CHUNKS_PER_HALF):
            rows = _chunk(hf * CHUNKS_PER_HALF + lc)
            for hd in range(MLA_HEADS):
                cols = slice(hd * MLA_SLOT, (hd + 1) * MLA_SLOT)
                blk = slots[local(lc), cols]
                if extra is not None:
                    blk = blk + extra[local(lc), :]
                blk = _rope_block(blk, rows, cos_scr, sin_scr)
                out_ref[rows, cols] = (blk if scale is None else blk * scale).astype(BF16)

    def stages(hf):
        state = {}

        def down():
            lat = h_dot(hf, _OFF_CQ, MLA_Q_RANK + MLA_KV_RANK + MLA_SLOT)
            state["cqn"] = _rms_rows(lat[:, :MLA_Q_RANK], gcq_ref[...]).astype(BF16)
            state["ckvn"] = _rms_rows(lat[:, MLA_Q_RANK:MLA_Q_RANK + MLA_KV_RANK], gckv_ref[...]).astype(BF16)
            state["kr"] = lat[:, MLA_Q_RANK + MLA_KV_RANK:]

        def queries():
            return jnp.dot(state["cqn"], wuq_ref[...], preferred_element_type=F32)

        def keys_values():
            k_all = jnp.dot(state["ckvn"], wukv_ref[:, :_MLA_QK_W], preferred_element_type=F32)
            return k_all, jnp.dot(state["ckvn"], wukv_ref[:, _MLA_QK_W:], preferred_element_type=F32)

        def store_keys_values(res):
            k_all, vb = res
            vb_ref[half_rows(hf), :] = vb.astype(BF16)
            rope_slots(hf, k_all, kb_ref, extra=state["kr"])

        def gate_block(n):
            def produce():
                if n == 0 and hf + 1 < N_HALVES:
                    prepare(hf + 1)
                return h_dot(hf, n * A_WIDTH, A_WIDTH)

            def consume(g):
                gate_ref[half_rows(hf), n * A_WIDTH:(n + 1) * A_WIDTH] = (g / (1.0 + jnp.exp(-g))).astype(BF16)

            return produce, consume

        def store_mq(mq):
            mq_ref[half_rows(hf), :] = (mq * (MEM_HEAD_DIM ** -0.5)).astype(BF16)

        return [
            (down, lambda _: None),
            (queries, lambda q_all: rope_slots(hf, q_all, qb_ref, scale=mla_scale)),
            (keys_values, store_keys_values),
            *[gate_block(n) for n in range(_GATE_W // A_WIDTH)],
            (lambda: h_dot(hf, _OFF_MQ, MEM_WIDTH), store_mq),
        ]

    prepare(0)
    pending = None
    for produce, consume in [st for hf in range(N_HALVES) for st in stages(hf)]:
        res = produce()
        if pending is not None:
            pending[0](pending[1])
        pending = (consume, res)
    pending[0](pending[1])


def _proj_b(x, pos_view, freq_b, g_emb, b_emb, w_b, g_cq, g_ckv, w_uq_p, w_ukv_p):
    batch = x.shape[0]

    def rows_spec(width):
        return pl.BlockSpec((None, ROW_TILE, width), lambda b, i: (b, i, 0))

    def full_spec(arr):
        return pl.BlockSpec(arr.shape, lambda b, i: (0,) * arr.ndim)

    widths = (_GATE_W, MEM_WIDTH, _MLA_QK_W, _MLA_QK_W, MLA_WIDTH)
    est = (2 * ROW_TILE * D_MODEL * 4 + 2 * (w_b.size + w_uq_p.size + w_ukv_p.size) * 2
           + 2 * ROW_TILE * sum(widths) * 2 + ROW_TILE * D_MODEL * 2 + ROW_TILE * A_WIDTH * 4
           + 3 * ROW_TILE * LANES * 4 + 2 * ROW_TILE * A_WIDTH * 4)
    return pl.pallas_call(
        _proj_b_kernel,
        grid=(batch, N_ROW_TILES),
        in_specs=[
            rows_spec(D_MODEL),
            pl.BlockSpec((None, None, N_CHUNKS, ROW_CHUNK), lambda b, i: (b, i, 0, 0)),
            full_spec(freq_b),
            full_spec(g_emb), full_spec(b_emb), full_spec(w_b), full_spec(g_cq), full_spec(g_ckv),
            full_spec(w_uq_p), full_spec(w_ukv_p),
        ],
        out_specs=[rows_spec(w) for w in widths],
        out_shape=[jax.ShapeDtypeStruct((batch, SEQ, w), BF16) for w in widths],
        scratch_shapes=[
            pltpu.VMEM((ROW_TILE, D_MODEL), BF16),
            pltpu.VMEM((ROW_TILE, LANES), F32),
            pltpu.VMEM((ROW_TILE, LANES), F32),
        ],
        compiler_params=pltpu.CompilerParams(
            dimension_semantics=("arbitrary", "arbitrary"), vmem_limit_bytes=_vmem_limit(est)),
        name="proj_b",
    )(x, pos_view, freq_b, g_emb, b_emb, w_b, g_cq, g_ckv, w_uq_p, w_ukv_p)


def _first_head_lanes(n_rows):
    return lax.broadcasted_iota(jnp.int32, (n_rows, LANES), 1) < A_HEAD_DIM


def _a_qk_lane_order():
    half = A_ROT_DIM // 2
    rest = A_HEAD_DIM - A_ROT_DIM
    order = ([(0, d) for d in range(half)] + [(1, d) for d in range(half)]
             + [(0, A_ROT_DIM + d) for d in range(rest)]
             + [(0, half + d) for d in range(half)] + [(1, half + d) for d in range(half)]
             + [(1, A_ROT_DIM + d) for d in range(rest)])
    assert len(order) == LANES and len(set(order)) == LANES
    return order


def _first_head_qk_lanes(n_rows):
    lane = lax.broadcasted_iota(jnp.int32, (n_rows, LANES), 1)
    half = A_ROT_DIM // 2
    return (lane < half) | ((lane >= A_ROT_DIM) & (lane < A_ROT_DIM + A_HEAD_DIM - half))


def _band_scores(q, k, bias):
    n_q = q.shape[0]
    first_head = _first_head_lanes(n_q)
    first_head_qk = _first_head_qk_lanes(n_q)
    zero = jnp.zeros_like(q)
    q2 = jnp.concatenate([jnp.where(first_head_qk, q, zero), jnp.where(first_head_qk, zero, q)], axis=0)
    s = lax.dot_general(q2, k, (((1,), (1,)), ((), ())), preferred_element_type=F32) + bias
    m = jnp.max(s, axis=-1, keepdims=True)
    p = jnp.exp2((s - m).astype(BF16))
    return p, jnp.where(first_head, m[:n_q], m[n_q:])


def _band_values(p, v_ones):
    n_q = p.shape[0] // 2
    first_head = _first_head_lanes(n_q)
    o = jnp.dot(p, v_ones, preferred_element_type=F32)
    acc = jnp.where(first_head, o[:n_q, :LANES], o[n_q:, :LANES])
    l_b = jnp.where(first_head, o[:n_q, LANES:], o[n_q:, LANES:])
    return acc, l_b


def _dilated_kernel(qn_ref, kn_ref, vn_ref, q4_ref, k4_ref, v4_ref, q16_ref, k16_ref, v16_ref,
                    o_ref, von_scr, vo4_scr, vo16_scr, bias_scr, bias16_scr, acc_scr, mx_scr, sm_scr, c16_scr):
    def window_bias(n_keys, first_key):
        row = lax.broadcasted_iota(jnp.int32, (2 * A_Q_TILE, n_keys), 0)
        col = lax.broadcasted_iota(jnp.int32, (2 * A_Q_TILE, n_keys), 1)
        q_idx = jnp.where(row >= A_Q_TILE, row - A_Q_TILE, row)
        off = col - (q_idx + first_key)
        return jnp.where(jnp.abs(off) <= A_SIDE, 0.0, NEG_INF).astype(F32)

    @pl.when((pl.program_id(0) == 0) & (pl.program_id(1) == 0))
    def _():
        for variant in range(3):
            bias_scr[variant] = window_bias(A_K_WIN, variant * A_SIDE)
        bias16_scr[...] = window_bias(A_Q_TILE, 0)
        von_scr[:, LANES:] = jnp.ones((SEQ, LANES), BF16)
        vo4_scr[:, :, LANES:] = jnp.ones((4, SEQ // 4, LANES), BF16)
        vo16_scr[:, :, LANES:] = jnp.ones((16, SEQ // 16, LANES), BF16)

    von_scr[:, :LANES] = vn_ref[...]
    vo4_scr[:, :, :LANES] = v4_ref[...]
    vo16_scr[:, :, :LANES] = v16_ref[...]

    len4 = SEQ // 4
    tiles4 = len4 // A_Q_TILE
    n_tiles = SEQ // A_Q_TILE

    def window(tile, n_seq_tiles, seq_len):
        qs = tile * A_Q_TILE
        ks = min(max(qs - A_SIDE, 0), seq_len - A_K_WIN)
        variant = 0 if tile == 0 else (2 if tile == n_seq_tiles - 1 else 1)
        return qs, ks, variant

    acc16, mx16, sm16 = (c16_scr.at[k] for k in range(3))

    def score_half(i):
        qs, ks, var = window(i, n_tiles, SEQ)
        res = i // tiles4
        qs4, ks4, var4 = window(i % tiles4, tiles4, len4)
        parts = (
            _band_scores(qn_ref[pl.ds(qs, A_Q_TILE), :], kn_ref[pl.ds(ks, A_K_WIN), :], bias_scr[var]),
            _band_scores(q4_ref[res, pl.ds(qs4, A_Q_TILE), :], k4_ref[res, pl.ds(ks4, A_K_WIN), :], bias_scr[var4]),
            _band_scores(q16_ref[i], k16_ref[i], bias16_scr[...]),
        )
        class16 = pl.ds(i * C16_PITCH, A_Q_TILE)
        dests = ((acc_scr.at[0], mx_scr.at[0], sm_scr.at[0], pl.ds(qs, A_Q_TILE)),
                 (acc_scr.at[1], mx_scr.at[1], sm_scr.at[1], pl.ds(res + 4 * qs4, A_Q_TILE, stride=4)),
                 (acc16, mx16, sm16, class16))
        v_wins = (von_scr.at[pl.ds(ks, A_K_WIN), :], vo4_scr.at[res, pl.ds(ks4, A_K_WIN), :], vo16_scr.at[i])
        for (_, mx_at, _, dst), (_, m_b) in zip(dests, parts):
            mx_at[dst, :] = m_b
        return [(p, dest, v_win) for (p, _), dest, v_win in zip(parts, dests, v_wins)]

    def value_half(pending):
        for p, (acc_at, _, sm_at, dst), v_win in pending:
            acc, l_b = _band_values(p, v_win[...])
            acc_at[dst, :] = acc
            sm_at[dst, :] = l_b

    pending = score_half(0)
    for i in range(1, n_tiles):
        ahead = score_half(i)
        value_half(pending)
        pending = ahead
    value_half(pending)

    def class16_rows(scr, c):
        groups = []
        for g in range(ROW_CHUNK // 8):
            first_class = 8 * (g % 2)
            t = (ROW_CHUNK // 16) * c + g // 2
            groups.append(scr[pl.ds(first_class * C16_PITCH + t, 8, stride=C16_PITCH), :])
        return jnp.concatenate(groups, axis=0)

    def merge(c, carry):
        rows = pl.ds(pl.multiple_of(c * ROW_CHUNK, ROW_CHUNK), ROW_CHUNK)
        parts = [(acc_scr[pat, rows, :], mx_scr[pat, rows, :], sm_scr[pat, rows, :]) for pat in range(2)]
        parts.append((class16_rows(acc16, c), class16_rows(mx16, c), class16_rows(sm16, c)))
        m_all = jnp.maximum(jnp.maximum(parts[0][1], parts[1][1]), parts[2][1])
        num = jnp.zeros((ROW_CHUNK, LANES), F32)
        den = jnp.zeros((ROW_CHUNK, LANES), F32)
        for acc, m_b, l_b in parts:
            w = jnp.exp2(m_b - m_all)
            num = num + w * acc
            den = den + w * l_b
        o_ref[rows, :] = (num / den).astype(BF16)
        return carry

    lax.fori_loop(0, SEQ // ROW_CHUNK, merge, 0, unroll=4)


def _dilated_attention(qkv_nat, qkv_4, qkv_16):
    batch = qkv_nat[0].shape[0]
    nat_spec = pl.BlockSpec((None, None, SEQ, LANES), lambda b, p: (b, p, 0, 0))
    r4_spec = pl.BlockSpec((None, None, 4, SEQ // 4, LANES), lambda b, p: (b, p, 0, 0, 0))
    r16_spec = pl.BlockSpec((None, None, 16, SEQ // 16, LANES), lambda b, p: (b, p, 0, 0, 0))
    blk = SEQ * LANES
    est = (2 * 9 * blk * 2 + 2 * blk * 2 + 3 * blk * 2 * 2 + 9 * blk * 4
           + 4 * 2 * A_Q_TILE * A_K_WIN * 4)
    return pl.pallas_call(
        _dilated_kernel,
        grid=(batch, A_PAIRS),
        in_specs=[nat_spec] * 3 + [r4_spec] * 3 + [r16_spec] * 3,
        out_specs=pl.BlockSpec((None, None, SEQ, LANES), lambda b, p: (b, p, 0, 0)),
        out_shape=jax.ShapeDtypeStruct((batch, A_PAIRS, SEQ, LANES), BF16),
        scratch_shapes=[
            pltpu.VMEM((SEQ, 2 * LANES), BF16),
            pltpu.VMEM((4, SEQ // 4, 2 * LANES), BF16),
            pltpu.VMEM((16, SEQ // 16, 2 * LANES), BF16),
            pltpu.VMEM((3, 2 * A_Q_TILE, A_K_WIN), F32),
            pltpu.VMEM((2 * A_Q_TILE, A_Q_TILE), F32),
            pltpu.VMEM((2, SEQ, LANES), F32),
            pltpu.VMEM((2, SEQ, LANES), F32),
            pltpu.VMEM((2, SEQ, LANES), F32),
            pltpu.VMEM((3, 16 * C16_PITCH, LANES), F32),
        ],
        compiler_params=pltpu.CompilerParams(
            dimension_semantics=("arbitrary", "arbitrary"), vmem_limit_bytes=_vmem_limit(est)),
        name="dilated_attn",
    )(*qkv_nat, *qkv_4, *qkv_16)


def _latent_kernel(q_ref, k_ref, v_ref, o_ref, vo_scr):
    @pl.when((pl.program_id(0) == 0) & (pl.program_id(1) == 0) & (pl.program_id(2) == 0))
    def _():
        vo_scr[:, :, LANES:] = jnp.ones((MLA_PAIRS_PER_STEP, SEQ, LANES), BF16)

    @pl.when(pl.program_id(2) == 0)
    def _():
        for pair in range(MLA_PAIRS_PER_STEP):
            vo_scr[pair, :, :LANES] = v_ref[:, pair * LANES:(pair + 1) * LANES]

    def score_half(rows, head):
        cols = slice(head * MLA_SLOT, (head + 1) * MLA_SLOT)
        s = lax.dot_general(q_ref[rows, cols], k_ref[:, cols], (((1,), (1,)), ((), ())),
                            preferred_element_type=F32)
        m = jnp.max(s, axis=-1, keepdims=True)
        return jnp.exp2((s - m).astype(BF16))

    def value_half(head, p):
        o = jnp.dot(p, vo_scr[head // 2], preferred_element_type=F32)
        return o[:, :LANES] / o[:, LANES:]

    n_heads = 2 * MLA_PAIRS_PER_STEP
    chains = [(slice(t * MLA_Q_TILE, (t + 1) * MLA_Q_TILE), head)
              for t in range(MLA_Q_TILES_PER_STEP) for head in range(n_heads)]
    lane = lax.broadcasted_iota(jnp.int32, (MLA_Q_TILE, LANES), 1)
    p_next = score_half(*chains[0])
    for n, (rows, head) in enumerate(chains):
        p = p_next
        if n + 1 < len(chains):
            p_next = score_half(*chains[n + 1])
        out = value_half(head, p)
        if head % 2 == 0:
            first_of_pair = out
        else:
            pair = head // 2
            o_ref[rows, pair * LANES:(pair + 1) * LANES] = (
                jnp.where(lane < MLA_V_DIM, first_of_pair, out).astype(BF16))


def _latent_attention(qb, kb, vb):
    batch = qb.shape[0]
    pairs = MLA_HEADS // 2 // MLA_PAIRS_PER_STEP
    qk_w = MLA_PAIRS_PER_STEP * 2 * MLA_SLOT
    v_w = MLA_PAIRS_PER_STEP * LANES
    step_rows = MLA_Q_TILES_PER_STEP * MLA_Q_TILE
    est = (2 * step_rows * qk_w * 2 + 2 * SEQ * qk_w * 2 + 2 * SEQ * v_w * 2 + SEQ * 2 * v_w * 2
           + 2 * step_rows * v_w * 2 + 4 * MLA_PAIRS_PER_STEP * MLA_Q_TILE * SEQ * 4)
    return pl.pallas_call(
        _latent_kernel,
        grid=(batch, pairs, SEQ // step_rows),
        in_specs=[
            pl.BlockSpec((None, step_rows, qk_w), lambda b, p, i: (b, i, p)),
            pl.BlockSpec((None, SEQ, qk_w), lambda b, p, i: (b, 0, p)),
            pl.BlockSpec((None, SEQ, v_w), lambda b, p, i: (b, 0, p)),
        ],
        out_specs=pl.BlockSpec((None, step_rows, v_w), lambda b, p, i: (b, i, p)),
        out_shape=jax.ShapeDtypeStruct((batch, SEQ, MLA_WIDTH), BF16),
        scratch_shapes=[pltpu.VMEM((MLA_PAIRS_PER_STEP, SEQ, 2 * LANES), BF16)],
        compiler_params=pltpu.CompilerParams(
            dimension_semantics=("arbitrary", "arbitrary", "arbitrary"),
            vmem_limit_bytes=_vmem_limit(est)),
        name="latent_attn",
    )(qb, kb, vb)


def _mem_kv_kernel(mem_ref, w_ref, o_ref):
    o_ref[...] = jnp.dot(mem_ref[...].astype(BF16), w_ref[...], preferred_element_type=F32).astype(BF16)


def _mem_kv(mem, w_mem):
    batch = mem.shape[0]
    est = 2 * N_MEM * D_MODEL * 4 + 2 * w_mem.size * 2 + 2 * N_MEM * 2 * MEM_WIDTH * 2 + N_MEM * D_MODEL * 8
    return pl.pallas_call(
        _mem_kv_kernel,
        grid=(batch,),
        in_specs=[pl.BlockSpec((None, N_MEM, D_MODEL), lambda b: (b, 0, 0)),
                  pl.BlockSpec(w_mem.shape, lambda b: (0, 0))],
        out_specs=pl.BlockSpec((None, N_MEM, 2 * MEM_WIDTH), lambda b: (b, 0, 0)),
        out_shape=jax.ShapeDtypeStruct((batch, N_MEM, 2 * MEM_WIDTH), BF16),
        compiler_params=pltpu.CompilerParams(
            dimension_semantics=("arbitrary",), vmem_limit_bytes=_vmem_limit(est)),
        name="mem_kv",
    )(mem, w_mem)


def _output_kernel(x_ref, ya_ref, yb_ref, gate_ref, mq_ref, mkv_ref, wout_ref,
                   goa_ref, gob_ref, gom_ref, gemb_ref, bemb_ref, gpost_ref, bpost_ref,
                   o_ref, y_scr):
    for hd in range(MEM_HEADS):
        cols = slice(hd * MEM_HEAD_DIM, (hd + 1) * MEM_HEAD_DIM)
        s = lax.dot_general(mq_ref[:, cols], mkv_ref[:, cols], (((1,), (1,)), ((), ())),
                            preferred_element_type=F32)
        m = jnp.max(s, axis=-1, keepdims=True)
        p = jnp.exp(s - m)
        l = jnp.sum(p, axis=-1, keepdims=True)
        o = jnp.dot(p.astype(BF16), mkv_ref[:, MEM_WIDTH + hd * MEM_HEAD_DIM:MEM_WIDTH + (hd + 1) * MEM_HEAD_DIM],
                    preferred_element_type=F32)
        y_scr[:, cols] = o / l

    off_b = A_WIDTH
    off_m = A_WIDTH + MLA_WIDTH
    ya_all = jnp.concatenate([ya_ref[cg] for cg in range(A_PAIRS)], axis=1)
    ya = _rms_rows(ya_all.astype(F32), goa_ref[...]) * gate_ref[:, :off_b].astype(F32)
    yb = _rms_rows(yb_ref[...].astype(F32), gob_ref[...]) * gate_ref[:, off_b:off_m].astype(F32)
    ym = _rms_rows(y_scr[...], gom_ref[...]) * gate_ref[:, off_m:].astype(F32)
    sub = (jnp.dot(ya.astype(BF16), wout_ref[:off_b, :], preferred_element_type=F32)
           + jnp.dot(yb.astype(BF16), wout_ref[off_b:off_m, :], preferred_element_type=F32)
           + jnp.dot(ym.astype(BF16), wout_ref[off_m:, :], preferred_element_type=F32))

    for c in range(N_CHUNKS):
        h = _layer_norm_rows(x_ref[_chunk(c), :], gemb_ref[...], bemb_ref[...])
        z = DEEPNORM_ALPHA * h + sub[c * ROW_CHUNK:(c + 1) * ROW_CHUNK, :]
        o_ref[_chunk(c), :] = _layer_norm_rows(z, gpost_ref[...], bpost_ref[...])


def _output_stage(x, ya, yb, gates, mq, mkv, w_out, g_out_a, g_out_b, g_out_m, g_emb, b_emb,
                  g_post, b_post):
    batch = x.shape[0]

    def rows_spec(width):
        return pl.BlockSpec((None, ROW_TILE, width), lambda b, i: (b, i, 0))

    def full_spec(arr):
        return pl.BlockSpec(arr.shape, lambda b, i: (0,) * arr.ndim)

    est = (4 * ROW_TILE * D_MODEL * 4 + 2 * ROW_TILE * (A_WIDTH + MLA_WIDTH + D_MIX + MEM_WIDTH) * 2
           + 2 * N_MEM * 2 * MEM_WIDTH * 2 + 2 * w_out.size * 2 + ROW_TILE * MEM_WIDTH * 4
           + 4 * ROW_TILE * D_MODEL * 4)
    return pl.pallas_call(
        _output_kernel,
        grid=(batch, N_ROW_TILES),
        in_specs=[
            rows_spec(D_MODEL),
            pl.BlockSpec((None, A_PAIRS, ROW_TILE, LANES), lambda b, i: (b, 0, i, 0)),
            rows_spec(MLA_WIDTH), rows_spec(D_MIX), rows_spec(MEM_WIDTH),
            pl.BlockSpec((None, N_MEM, 2 * MEM_WIDTH), lambda b, i: (b, 0, 0)),
            full_spec(w_out), full_spec(g_out_a), full_spec(g_out_b), full_spec(g_out_m),
            full_spec(g_emb), full_spec(b_emb), full_spec(g_post), full_spec(b_post),
        ],
        out_specs=rows_spec(D_MODEL),
        out_shape=jax.ShapeDtypeStruct(x.shape, F32),
        scratch_shapes=[pltpu.VMEM((ROW_TILE, MEM_WIDTH), F32)],
        compiler_params=pltpu.CompilerParams(
            dimension_semantics=("arbitrary", "arbitrary"), vmem_limit_bytes=_vmem_limit(est)),
        name="output_stage",
    )(x, ya, yb, gates, mq, mkv, w_out, g_out_a, g_out_b, g_out_m, g_emb, b_emb, g_post, b_post)


def _rope_freq_rows(rot_dim):
    inv_freq = (np.float32(ROPE_THETA) ** (-(np.arange(0, rot_dim, 2, dtype=np.float32) / np.float32(rot_dim)))
                ).astype(np.float32)
    return jnp.asarray(np.repeat(inv_freq[:, None], LANES, axis=1))


def _a_qk_columns():
    return np.asarray([(2 * pair + hh) * A_HEAD_DIM + d
                       for pair in range(A_PAIRS) for hh, d in _a_qk_lane_order()])


def _mla_slot_lanes():
    half = MLA_ROPE_DIM // 2
    assert ROPE_FIRST_LANES == (0, half)
    lanes = [None] * MLA_SLOT
    for i in range(half):
        lanes[i] = ("rope", i)
        lanes[ROPE_PARTNER_SHIFT + i] = ("rope", half + i)
    free = [lane for lane in range(MLA_SLOT) if lanes[lane] is None]
    for d in range(MLA_NOPE_DIM):
        lanes[free[d]] = ("nope", d)
    return lanes


def _slot_gather(cols, index_of):
    n = cols.shape[-1]
    idx = np.asarray([n if index_of(lane) is None else index_of(lane) for lane in _mla_slot_lanes()])
    return jnp.pad(cols, [(0, 0)] * (cols.ndim - 1) + [(0, 1)])[..., idx]


def kernel(x, mem, positions, g_emb, b_emb, w_in, g_cq, g_ckv, w_uq, w_ukv, w_mem_kv, g_out_a, g_out_b,
           g_out_m, w_out, g_post, b_post):
    batch = x.shape[0]
    assert x.shape == (batch, SEQ, D_MODEL) and w_in.shape[0] == DEPTH == 1
    row = lambda v: v.reshape(1, -1).astype(F32)

    splits = [int(i) for i in np.cumsum(IN_SPLITS)[:-1]]
    a_q, a_k, a_v, a_g, c_q, c_kv, b_kr, b_g, m_q, m_g = jnp.split(w_in[0].astype(BF16), splits, axis=1)
    qk_cols = _a_qk_columns()
    w_qkv = jnp.concatenate([a_q[:, qk_cols], a_k[:, qk_cols], a_v], axis=1).astype(BF16)
    rope_only = lambda lane: lane[1] if lane is not None and lane[0] == "rope" else None
    nope_only = lambda lane: lane[1] if lane is not None and lane[0] == "nope" else None
    nope_then_rope = lambda lane: None if lane is None else (lane[1] + (MLA_NOPE_DIM if lane[0] == "rope" else 0))
    kr_slot = _slot_gather(b_kr, rope_only)
    w_b = jnp.concatenate([a_g, b_g, m_g, m_q, c_q, c_kv, kr_slot], axis=1).astype(BF16)
    qk_dim = MLA_NOPE_DIM + MLA_ROPE_DIM
    w_uq_p = _slot_gather(w_uq[0].reshape(MLA_Q_RANK, MLA_HEADS, qk_dim), nope_then_rope
                          ).reshape(MLA_Q_RANK, _MLA_QK_W).astype(BF16)
    ukv = w_ukv[0].reshape(MLA_KV_RANK, MLA_HEADS, MLA_NOPE_DIM + MLA_V_DIM)
    w_uk_p = _slot_gather(ukv[:, :, :MLA_NOPE_DIM], nope_only)
    w_ukv_p = jnp.concatenate([w_uk_p.reshape(MLA_KV_RANK, _MLA_QK_W),
                               ukv[:, :, MLA_NOPE_DIM:].reshape(MLA_KV_RANK, MLA_WIDTH)], axis=1).astype(BF16)

    pos_view = positions.astype(F32).reshape(batch, N_ROW_TILES, N_CHUNKS, ROW_CHUNK)
    freq_a = _rope_freq_rows(A_ROT_DIM)
    freq_b = _rope_freq_rows(MLA_ROPE_DIM)
    g_emb_r, b_emb_r = row(g_emb), row(b_emb)

    outs_a = _proj_a(x, pos_view, freq_a, g_emb_r, b_emb_r, w_qkv)
    y_a = _dilated_attention(outs_a[0:3], outs_a[3:6], outs_a[6:9])

    gates, mq, qb, kb, vb = _proj_b(x, pos_view, freq_b, g_emb_r, b_emb_r, w_b, row(g_cq[0]),
                                    row(g_ckv[0]), w_uq_p, w_ukv_p)
    y_b = _latent_attention(qb, kb, vb)

    mkv = _mem_kv(mem, w_mem_kv[0].astype(BF16))
    return _output_stage(x, y_a, y_b, gates, mq, mkv, w_out[0].astype(BF16), row(g_out_a[0]),
                         row(g_out_b[0]), row(g_out_m[0]), g_emb_r, b_emb_r, row(g_post[0]), row(b_post[0]))
```

```python
import math

import jax
import jax.numpy as jnp
import numpy as np
from jax import lax
from jax.experimental import pallas as pl
from jax.experimental.pallas import tpu as pltpu

D_MODEL = 1024
SEQ = 2048
A_HEADS = 16
A_HEAD_DIM = 64
A_WIDTH = A_HEADS * A_HEAD_DIM
A_ROT_DIM = A_HEAD_DIM // 4
A_SIDE = 64
DILATIONS = (1, 4, 16)
MLA_HEADS = 8
MLA_Q_RANK = 256
MLA_KV_RANK = 128
MLA_NOPE_DIM = 64
MLA_ROPE_DIM = 32
MLA_V_DIM = 64
MLA_WIDTH = MLA_HEADS * MLA_V_DIM
N_MEM = 256
MEM_HEADS = 4
MEM_HEAD_DIM = 128
MEM_WIDTH = MEM_HEADS * MEM_HEAD_DIM
D_MIX = A_WIDTH + MLA_WIDTH + MEM_WIDTH
ROPE_THETA = 500000.0
NORM_EPS = 1e-5
NEG_INF = -1e30
DEPTH = 1
DEEPNORM_ALPHA = (2 * DEPTH) ** 0.25
IN_SPLITS = (A_WIDTH, A_WIDTH, A_WIDTH, A_WIDTH, MLA_Q_RANK, MLA_KV_RANK, MLA_ROPE_DIM, MLA_WIDTH,
             MEM_WIDTH, MEM_WIDTH)
LOG2_E = math.log2(math.e)

LANES = 128
VMEM_BYTES_V7X = 64 * 1024 * 1024

ROW_TILE = 512
ROW_CHUNK = 128
N_CHUNKS = ROW_TILE // ROW_CHUNK
N_HALVES = 1
ROW_HALF = ROW_TILE // N_HALVES
CHUNKS_PER_HALF = N_CHUNKS // N_HALVES
N_ROW_TILES = SEQ // ROW_TILE
A_PAIRS = A_WIDTH // LANES
A_Q_TILE = 128
A_K_WIN = 2 * A_Q_TILE
C16_PITCH = SEQ // 16 + 8
MLA_SLOT = LANES
MLA_Q_TILE = 256
MLA_Q_TILES_PER_STEP = 4
MLA_PAIRS_PER_STEP = 4

F32 = jnp.float32
BF16 = jnp.bfloat16


def _vmem_limit(nbytes):
    return int(min(VMEM_BYTES_V7X - (4 << 20), max(32 << 20, nbytes + (8 << 20))))


def _layer_norm_rows(x, g, b):
    mu = jnp.mean(x, axis=-1, keepdims=True)
    xc = x - mu
    var = jnp.mean(xc * xc, axis=-1, keepdims=True)
    return xc * lax.rsqrt(var + NORM_EPS) * g + b


def _rms_rows(x, g):
    ms = jnp.mean(x * x, axis=-1, keepdims=True)
    return x * lax.rsqrt(ms + NORM_EPS) * g


def _chunk(c):
    return slice(c * ROW_CHUNK, (c + 1) * ROW_CHUNK)


def _normed_chunks_to_scratch(x_ref, g_ref, b_ref, h_scr, chunks):
    for c in chunks:
        h_scr[_chunk(c), :] = _layer_norm_rows(x_ref[_chunk(c), :], g_ref[...], b_ref[...]).astype(BF16)


ROPE_FIRST_LANES = (0, 16)
ROPE_PARTNER_SHIFT = LANES // 2


def _rope_tables(pos_ref, freq_ref, spread, chunks, cos_scr, sin_scr):
    n_freq = freq_ref.shape[0]
    lane = lax.broadcasted_iota(jnp.int32, (ROW_CHUNK, LANES), 1)
    first = (lane >= ROPE_FIRST_LANES[0]) & (lane < ROPE_FIRST_LANES[1])
    second = (lane >= ROPE_FIRST_LANES[0] + ROPE_PARTNER_SHIFT) & (lane < ROPE_FIRST_LANES[1] + ROPE_PARTNER_SHIFT)
    pad = jnp.zeros((ROW_CHUNK - n_freq, LANES), F32)
    for c in chunks:
        ang = freq_ref[...] * pos_ref[c:c + 1, :]
        cos = spread(jnp.concatenate([jnp.cos(ang), pad], axis=0).T)
        sin = spread(jnp.concatenate([jnp.sin(ang), pad], axis=0).T)
        cos_scr[_chunk(c), :] = jnp.where(first | second, cos, 1.0)
        sin_scr[_chunk(c), :] = jnp.where(first, -sin, jnp.where(second, sin, 0.0))


def _spread_a(t):
    t = t + pltpu.roll(t, A_ROT_DIM // 2, 1)
    return t + pltpu.roll(t, ROPE_PARTNER_SHIFT, 1)


def _spread_b(t):
    return t + pltpu.roll(t, ROPE_PARTNER_SHIFT, 1)


def _rope_block(x, rows, cos_scr, sin_scr):
    return x * cos_scr[rows, :] + pltpu.roll(x, ROPE_PARTNER_SHIFT, 1) * sin_scr[rows, :]


def _proj_a_kernel(x_ref, pos_ref, freq_ref, g_ref, b_ref, w_ref,
                   qn_ref, kn_ref, vn_ref, q4_ref, k4_ref, v4_ref, q16_ref, k16_ref, v16_ref,
                   h_scr, acc_scr, st4_scr, cos_scr, sin_scr):
    outs = ((qn_ref, q4_ref, q16_ref), (kn_ref, k4_ref, k16_ref), (vn_ref, v4_ref, v16_ref))
    q_scale = (A_HEAD_DIM ** -0.5) * LOG2_E

    def prepare(hf):
        chunks = range(hf * CHUNKS_PER_HALF, (hf + 1) * CHUNKS_PER_HALF)
        _normed_chunks_to_scratch(x_ref, g_ref, b_ref, h_scr, chunks)
        _rope_tables(pos_ref, freq_ref, _spread_a, chunks, cos_scr, sin_scr)

    def matmul(hf, sec):
        if sec == 1 and hf + 1 < N_HALVES:
            prepare(hf + 1)
        return jnp.dot(h_scr[hf * ROW_HALF:(hf + 1) * ROW_HALF, :],
                       w_ref[:, sec * A_WIDTH:(sec + 1) * A_WIDTH], preferred_element_type=F32)

    def epilogue(hf, sec, res):
        nat_ref, r4_ref, r16_ref = outs[sec]
        acc = acc_scr.at[hf * len(outs) + sec]
        st4 = st4_scr.at[hf * len(outs) + sec]
        for cg in range(A_PAIRS):
            cols = slice(cg * LANES, (cg + 1) * LANES)
            for lc in range(CHUNKS_PER_HALF):
                rows = _chunk(hf * CHUNKS_PER_HALF + lc)
                blk = res[lc * ROW_CHUNK:(lc + 1) * ROW_CHUNK, cols]
                if sec < 2:
                    blk = _rope_block(blk, rows, cos_scr, sin_scr)
                    if sec == 0:
                        blk = blk * q_scale
                acc[cg, lc * ROW_CHUNK:(lc + 1) * ROW_CHUNK, :] = blk
                nat_ref[cg, rows, :] = blk.astype(BF16)
            n4, n16 = ROW_HALF // 4, ROW_HALF // 16
            for r in range(4):
                rows4 = acc[cg, pl.ds(r, n4, stride=4), :]
                st4[cg, r] = rows4
                r4_ref[cg, r, hf * n4:(hf + 1) * n4, :] = rows4.astype(BF16)
            for r in range(4):
                for v in range(4):
                    r16_ref[cg, r + 4 * v, hf * n16:(hf + 1) * n16, :] = (
                        st4[cg, r, pl.ds(v, n16, stride=4), :].astype(BF16))

    prepare(0)
    items = [(hf, sec) for hf in range(N_HALVES) for sec in range(len(outs))]
    pending = None
    for hf, sec in items:
        res = matmul(hf, sec)
        if pending is not None:
            epilogue(*pending)
        pending = (hf, sec, res)
    epilogue(*pending)


def _proj_a(x, pos_view, freq_a, g_emb, b_emb, w_qkv):
    batch = x.shape[0]
    nat = jax.ShapeDtypeStruct((batch, A_PAIRS, SEQ, LANES), BF16)
    r4 = jax.ShapeDtypeStruct((batch, A_PAIRS, 4, SEQ // 4, LANES), BF16)
    r16 = jax.ShapeDtypeStruct((batch, A_PAIRS, 16, SEQ // 16, LANES), BF16)
    nat_spec = pl.BlockSpec((None, A_PAIRS, ROW_TILE, LANES), lambda b, i: (b, 0, i, 0))
    r4_spec = pl.BlockSpec((None, A_PAIRS, 4, ROW_TILE // 4, LANES), lambda b, i: (b, 0, 0, i, 0))
    r16_spec = pl.BlockSpec((None, A_PAIRS, 16, ROW_TILE // 16, LANES), lambda b, i: (b, 0, 0, i, 0))
    tile_out = ROW_TILE * A_WIDTH * 2
    est = (2 * ROW_TILE * D_MODEL * 4 + 2 * w_qkv.size * 2 + 2 * 9 * tile_out
           + ROW_TILE * D_MODEL * 2 + 2 * 3 * ROW_TILE * A_WIDTH * 4 + 3 * ROW_TILE * LANES * 4
           + 2 * ROW_HALF * A_WIDTH * 4)
    return pl.pallas_call(
        _proj_a_kernel,
        grid=(batch, N_ROW_TILES),
        in_specs=[
            pl.BlockSpec((None, ROW_TILE, D_MODEL), lambda b, i: (b, i, 0)),
            pl.BlockSpec((None, None, N_CHUNKS, ROW_CHUNK), lambda b, i: (b, i, 0, 0)),
            pl.BlockSpec(freq_a.shape, lambda b, i: (0, 0)),
            pl.BlockSpec((1, D_MODEL), lambda b, i: (0, 0)),
            pl.BlockSpec((1, D_MODEL), lambda b, i: (0, 0)),
            pl.BlockSpec(w_qkv.shape, lambda b, i: (0, 0)),
        ],
        out_specs=[nat_spec] * 3 + [r4_spec] * 3 + [r16_spec] * 3,
        out_shape=[nat] * 3 + [r4] * 3 + [r16] * 3,
        scratch_shapes=[
            pltpu.VMEM((ROW_TILE, D_MODEL), BF16),
            pltpu.VMEM((N_HALVES * 3, A_PAIRS, ROW_HALF, LANES), F32),
            pltpu.VMEM((N_HALVES * 3, A_PAIRS, 4, ROW_HALF // 4, LANES), F32),
            pltpu.VMEM((ROW_TILE, LANES), F32),
            pltpu.VMEM((ROW_TILE, LANES), F32),
        ],
        compiler_params=pltpu.CompilerParams(
            dimension_semantics=("arbitrary", "arbitrary"), vmem_limit_bytes=_vmem_limit(est)),
        name="proj_a",
    )(x, pos_view, freq_a, g_emb, b_emb, w_qkv)


_GATE_W = D_MIX
_OFF_MQ = _GATE_W
_OFF_CQ = _OFF_MQ + MEM_WIDTH
_OFF_CKV = _OFF_CQ + MLA_Q_RANK
_OFF_KR = _OFF_CKV + MLA_KV_RANK
_MLA_QK_W = MLA_HEADS * MLA_SLOT


def _proj_b_kernel(x_ref, pos_ref, freq_ref, g_ref, b_ref, w_ref, gcq_ref, gckv_ref, wuq_ref,
                   wukv_ref, gate_ref, mq_ref, qb_ref, kb_ref, vb_ref,
                   h_scr, cos_scr, sin_scr):
    mla_scale = ((MLA_NOPE_DIM + MLA_ROPE_DIM) ** -0.5) * LOG2_E
    local = lambda lc: slice(lc * ROW_CHUNK, (lc + 1) * ROW_CHUNK)
    half_rows = lambda hf: slice(hf * ROW_HALF, (hf + 1) * ROW_HALF)

    def prepare(hf):
        chunks = range(hf * CHUNKS_PER_HALF, (hf + 1) * CHUNKS_PER_HALF)
        _normed_chunks_to_scratch(x_ref, g_ref, b_ref, h_scr, chunks)
        _rope_tables(pos_ref, freq_ref, _spread_b, chunks, cos_scr, sin_scr)

    def h_dot(hf, first, width):
        return jnp.dot(h_scr[half_rows(hf), :], w_ref[:, first:first + width], preferred_element_type=F32)

    def rope_slots(hf, slots, out_ref, scale=None, extra=None):
        for lc in range(CHUNKS_PER_HALF):
            rows = _chunk(hf * CHUNKS_PER_HALF + lc)
            for hd in range(MLA_HEADS):
                cols = slice(hd * MLA_SLOT, (hd + 1) * MLA_SLOT)
                blk = slots[local(lc), cols]
                if extra is not None:
                    blk = blk + extra[local(lc), :]
                blk = _rope_block(blk, rows, cos_scr, sin_scr)
                out_ref[rows, cols] = (blk if scale is None else blk * scale).astype(BF16)

    def stages(hf):
        state = {}

        def down():
            lat = h_dot(hf, _OFF_CQ, MLA_Q_RANK + MLA_KV_RANK + MLA_SLOT)
            state["cqn"] = _rms_rows(lat[:, :MLA_Q_RANK], gcq_ref[...]).astype(BF16)
            state["ckvn"] = _rms_rows(lat[:, MLA_Q_RANK:MLA_Q_RANK + MLA_KV_RANK], gckv_ref[...]).astype(BF16)
            state["kr"] = lat[:, MLA_Q_RANK + MLA_KV_RANK:]

        def queries():
            return jnp.dot(state["cqn"], wuq_ref[...], preferred_element_type=F32)

        def keys_values():
            k_all = jnp.dot(state["ckvn"], wukv_ref[:, :_MLA_QK_W], preferred_element_type=F32)
            return k_all, jnp.dot(state["ckvn"], wukv_ref[:, _MLA_QK_W:], preferred_element_type=F32)

        def store_keys_values(res):
            k_all, vb = res
            vb_ref[half_rows(hf), :] = vb.astype(BF16)
            rope_slots(hf, k_all, kb_ref, extra=state["kr"])

        def gate_block(n):
            def produce():
                if n == 0 and hf + 1 < N_HALVES:
                    prepare(hf + 1)
                return h_dot(hf, n * A_WIDTH, A_WIDTH)

            def consume(g):
                gate_ref[half_rows(hf), n * A_WIDTH:(n + 1) * A_WIDTH] = (g / (1.0 + jnp.exp(-g))).astype(BF16)

            return produce, consume

        def store_mq(mq):
            mq_ref[half_rows(hf), :] = (mq * (MEM_HEAD_DIM ** -0.5)).astype(BF16)

        return [
            (down, lambda _: None),
            (queries, lambda q_all: rope_slots(hf, q_all, qb_ref, scale=mla_scale)),
            (keys_values, store_keys_values),
            *[gate_block(n) for n in range(_GATE_W // A_WIDTH)],
            (lambda: h_dot(hf, _OFF_MQ, MEM_WIDTH), store_mq),
        ]

    prepare(0)
    pending = None
    for produce, consume in [st for hf in range(N_HALVES) for st in stages(hf)]:
        res = produce()
        if pending is not None:
            pending[0](pending[1])
        pending = (consume, res)
    pending[0](pending[1])


def _proj_b(x, pos_view, freq_b, g_emb, b_emb, w_b, g_cq, g_ckv, w_uq_p, w_ukv_p):
    batch = x.shape[0]

    def rows_spec(width):
        return pl.BlockSpec((None, ROW_TILE, width), lambda b, i: (b, i, 0))

    def full_spec(arr):
        return pl.BlockSpec(arr.shape, lambda b, i: (0,) * arr.ndim)

    widths = (_GATE_W, MEM_WIDTH, _MLA_QK_W, _MLA_QK_W, MLA_WIDTH)
    est = (2 * ROW_TILE * D_MODEL * 4 + 2 * (w_b.size + w_uq_p.size + w_ukv_p.size) * 2
           + 2 * ROW_TILE * sum(widths) * 2 + ROW_TILE * D_MODEL * 2 + ROW_TILE * A_WIDTH * 4
           + 3 * ROW_TILE * LANES * 4 + 2 * ROW_TILE * A_WIDTH * 4)
    return pl.pallas_call(
        _proj_b_kernel,
        grid=(batch, N_ROW_TILES),
        in_specs=[
            rows_spec(D_MODEL),
            pl.BlockSpec((None, None, N_CHUNKS, ROW_CHUNK), lambda b, i: (b, i, 0, 0)),
            full_spec(freq_b),
            full_spec(g_emb), full_spec(b_emb), full_spec(w_b), full_spec(g_cq), full_spec(g_ckv),
            full_spec(w_uq_p), full_spec(w_ukv_p),
        ],
        out_specs=[rows_spec(w) for w in widths],
        out_shape=[jax.ShapeDtypeStruct((batch, SEQ, w), BF16) for w in widths],
        scratch_shapes=[
            pltpu.VMEM((ROW_TILE, D_MODEL), BF16),
            pltpu.VMEM((ROW_TILE, LANES), F32),
            pltpu.VMEM((ROW_TILE, LANES), F32),
        ],
        compiler_params=pltpu.CompilerParams(
            dimension_semantics=("arbitrary", "arbitrary"), vmem_limit_bytes=_vmem_limit(est)),
        name="proj_b",
    )(x, pos_view, freq_b, g_emb, b_emb, w_b, g_cq, g_ckv, w_uq_p, w_ukv_p)


def _first_head_lanes(n_rows):
    return lax.broadcasted_iota(jnp.int32, (n_rows, LANES), 1) < A_HEAD_DIM


def _a_qk_lane_order():
    half = A_ROT_DIM // 2
    rest = A_HEAD_DIM - A_ROT_DIM
    order = ([(0, d) for d in range(half)] + [(1, d) for d in range(half)]
             + [(0, A_ROT_DIM + d) for d in range(rest)]
             + [(0, half + d) for d in range(half)] + [(1, half + d) for d in range(half)]
             + [(1, A_ROT_DIM + d) for d in range(rest)])
    assert len(order) == LANES and len(set(order)) == LANES
    return order


def _first_head_qk_lanes(n_rows):
    lane = lax.broadcasted_iota(jnp.int32, (n_rows, LANES), 1)
    half = A_ROT_DIM // 2
    return (lane < half) | ((lane >= A_ROT_DIM) & (lane < A_ROT_DIM + A_HEAD_DIM - half))


def _band_scores(q, k, bias):
    n_q = q.shape[0]
    first_head = _first_head_lanes(n_q)
    first_head_qk = _first_head_qk_lanes(n_q)
    zero = jnp.zeros_like(q)
    q2 = jnp.concatenate([jnp.where(first_head_qk, q, zero), jnp.where(first_head_qk, zero, q)], axis=0)
    s = lax.dot_general(q2, k, (((1,), (1,)), ((), ())), preferred_element_type=F32) + bias
    m = jnp.max(s, axis=-1, keepdims=True)
    p = jnp.exp2((s - m).astype(BF16))
    return p, jnp.where(first_head, m[:n_q], m[n_q:])


def _band_values(p, v_ones):
    n_q = p.shape[0] // 2
    first_head = _first_head_lanes(n_q)
    o = jnp.dot(p, v_ones, preferred_element_type=F32)
    acc = jnp.where(first_head, o[:n_q, :LANES], o[n_q:, :LANES])
    l_b = jnp.where(first_head, o[:n_q, LANES:], o[n_q:, LANES:])
    return acc, l_b


def _dilated_kernel(qn_ref, kn_ref, vn_ref, q4_ref, k4_ref, v4_ref, q16_ref, k16_ref, v16_ref,
                    o_ref, von_scr, vo4_scr, vo16_scr, bias_scr, bias16_scr, acc_scr, mx_scr, sm_scr, c16_scr):
    def window_bias(n_keys, first_key):
        row = lax.broadcasted_iota(jnp.int32, (2 * A_Q_TILE, n_keys), 0)
        col = lax.broadcasted_iota(jnp.int32, (2 * A_Q_TILE, n_keys), 1)
        q_idx = jnp.where(row >= A_Q_TILE, row - A_Q_TILE, row)
        off = col - (q_idx + first_key)
        return jnp.where(jnp.abs(off) <= A_SIDE, 0.0, NEG_INF).astype(F32)

    @pl.when((pl.program_id(0) == 0) & (pl.program_id(1) == 0))
    def _():
        for variant in range(3):
            bias_scr[variant] = window_bias(A_K_WIN, variant * A_SIDE)
        bias16_scr[...] = window_bias(A_Q_TILE, 0)
        von_scr[:, LANES:] = jnp.ones((SEQ, LANES), BF16)
        vo4_scr[:, :, LANES:] = jnp.ones((4, SEQ // 4, LANES), BF16)
        vo16_scr[:, :, LANES:] = jnp.ones((16, SEQ // 16, LANES), BF16)

    von_scr[:, :LANES] = vn_ref[...]
    vo4_scr[:, :, :LANES] = v4_ref[...]
    vo16_scr[:, :, :LANES] = v16_ref[...]

    len4 = SEQ // 4
    tiles4 = len4 // A_Q_TILE
    n_tiles = SEQ // A_Q_TILE

    def window(tile, n_seq_tiles, seq_len):
        qs = tile * A_Q_TILE
        ks = min(max(qs - A_SIDE, 0), seq_len - A_K_WIN)
        variant = 0 if tile == 0 else (2 if tile == n_seq_tiles - 1 else 1)
        return qs, ks, variant

    acc16, mx16, sm16 = (c16_scr.at[k] for k in range(3))

    def score_half(i):
        qs, ks, var = window(i, n_tiles, SEQ)
        res = i // tiles4
        qs4, ks4, var4 = window(i % tiles4, tiles4, len4)
        parts = (
            _band_scores(qn_ref[pl.ds(qs, A_Q_TILE), :], kn_ref[pl.ds(ks, A_K_WIN), :], bias_scr[var]),
            _band_scores(q4_ref[res, pl.ds(qs4, A_Q_TILE), :], k4_ref[res, pl.ds(ks4, A_K_WIN), :], bias_scr[var4]),
            _band_scores(q16_ref[i], k16_ref[i], bias16_scr[...]),
        )
        class16 = pl.ds(i * C16_PITCH, A_Q_TILE)
        dests = ((acc_scr.at[0], mx_scr.at[0], sm_scr.at[0], pl.ds(qs, A_Q_TILE)),
                 (acc_scr.at[1], mx_scr.at[1], sm_scr.at[1], pl.ds(res + 4 * qs4, A_Q_TILE, stride=4)),
                 (acc16, mx16, sm16, class16))
        v_wins = (von_scr.at[pl.ds(ks, A_K_WIN), :], vo4_scr.at[res, pl.ds(ks4, A_K_WIN), :], vo16_scr.at[i])
        for (_, mx_at, _, dst), (_, m_b) in zip(dests, parts):
            mx_at[dst, :] = m_b
        return [(p, dest, v_win) for (p, _), dest, v_win in zip(parts, dests, v_wins)]

    def value_half(pending):
        for p, (acc_at, _, sm_at, dst), v_win in pending:
            acc, l_b = _band_values(p, v_win[...])
            acc_at[dst, :] = acc
            sm_at[dst, :] = l_b

    pending = score_half(0)
    for i in range(1, n_tiles):
        ahead = score_half(i)
        value_half(pending)
        pending = ahead
    value_half(pending)

    def class16_rows(scr, c):
        groups = []
        for g in range(ROW_CHUNK // 8):
            first_class = 8 * (g % 2)
            t = (ROW_CHUNK // 16) * c + g // 2
            groups.append(scr[pl.ds(first_class * C16_PITCH + t, 8, stride=C16_PITCH), :])
        return jnp.concatenate(groups, axis=0)

    def merge(c, carry):
        rows = pl.ds(pl.multiple_of(c * ROW_CHUNK, ROW_CHUNK), ROW_CHUNK)
        parts = [(acc_scr[pat, rows, :], mx_scr[pat, rows, :], sm_scr[pat, rows, :]) for pat in range(2)]
        parts.append((class16_rows(acc16, c), class16_rows(mx16, c), class16_rows(sm16, c)))
        m_all = jnp.maximum(jnp.maximum(parts[0][1], parts[1][1]), parts[2][1])
        num = jnp.zeros((ROW_CHUNK, LANES), F32)
        den = jnp.zeros((ROW_CHUNK, LANES), F32)
        for acc, m_b, l_b in parts:
            w = jnp.exp2(m_b - m_all)
            num = num + w * acc
            den = den + w * l_b
        o_ref[rows, :] = (num / den).astype(BF16)
        return carry

    lax.fori_loop(0, SEQ // ROW_CHUNK, merge, 0, unroll=4)


def _dilated_attention(qkv_nat, qkv_4, qkv_16):
    batch = qkv_nat[0].shape[0]
    nat_spec = pl.BlockSpec((None, None, SEQ, LANES), lambda b, p: (b, p, 0, 0))
    r4_spec = pl.BlockSpec((None, None, 4, SEQ // 4, LANES), lambda b, p: (b, p, 0, 0, 0))
    r16_spec = pl.BlockSpec((None, None, 16, SEQ // 16, LANES), lambda b, p: (b, p, 0, 0, 0))
    blk = SEQ * LANES
    est = (2 * 9 * blk * 2 + 2 * blk * 2 + 3 * blk * 2 * 2 + 9 * blk * 4
           + 4 * 2 * A_Q_TILE * A_K_WIN * 4)
    return pl.pallas_call(
        _dilated_kernel,
        grid=(batch, A_PAIRS),
        in_specs=[nat_spec] * 3 + [r4_spec] * 3 + [r16_spec] * 3,
        out_specs=pl.BlockSpec((None, None, SEQ, LANES), lambda b, p: (b, p, 0, 0)),
        out_shape=jax.ShapeDtypeStruct((batch, A_PAIRS, SEQ, LANES), BF16),
        scratch_shapes=[
            pltpu.VMEM((SEQ, 2 * LANES), BF16),
            pltpu.VMEM((4, SEQ // 4, 2 * LANES), BF16),
            pltpu.VMEM((16, SEQ // 16, 2 * LANES), BF16),
            pltpu.VMEM((3, 2 * A_Q_TILE, A_K_WIN), F32),
            pltpu.VMEM((2 * A_Q_TILE, A_Q_TILE), F32),
            pltpu.VMEM((2, SEQ, LANES), F32),
            pltpu.VMEM((2, SEQ, LANES), F32),
            pltpu.VMEM((2, SEQ, LANES), F32),
            pltpu.VMEM((3, 16 * C16_PITCH, LANES), F32),
        ],
        compiler_params=pltpu.CompilerParams(
            dimension_semantics=("arbitrary", "arbitrary"), vmem_limit_bytes=_vmem_limit(est)),
        name="dilated_attn",
    )(*qkv_nat, *qkv_4, *qkv_16)


def _latent_kernel(q_ref, k_ref, v_ref, o_ref, vo_scr):
    @pl.when((pl.program_id(0) == 0) & (pl.program_id(1) == 0) & (pl.program_id(2) == 0))
    def _():
        vo_scr[:, :, LANES:] = jnp.ones((MLA_PAIRS_PER_STEP, SEQ, LANES), BF16)

    @pl.when(pl.program_id(2) == 0)
    def _():
        for pair in range(MLA_PAIRS_PER_STEP):
            vo_scr[pair, :, :LANES] = v_ref[:, pair * LANES:(pair + 1) * LANES]

    def score_half(rows, head):
        cols = slice(head * MLA_SLOT, (head + 1) * MLA_SLOT)
        s = lax.dot_general(q_ref[rows, cols], k_ref[:, cols], (((1,), (1,)), ((), ())),
                            preferred_element_type=F32)
        m = jnp.max(s, axis=-1, keepdims=True)
        return jnp.exp2((s - m).astype(BF16))

    def value_half(head, p):
        o = jnp.dot(p, vo_scr[head // 2], preferred_element_type=F32)
        return o[:, :LANES] / o[:, LANES:]

    n_heads = 2 * MLA_PAIRS_PER_STEP
    chains = [(slice(t * MLA_Q_TILE, (t + 1) * MLA_Q_TILE), head)
              for t in range(MLA_Q_TILES_PER_STEP) for head in range(n_heads)]
    lane = lax.broadcasted_iota(jnp.int32, (MLA_Q_TILE, LANES), 1)
    p_next = score_half(*chains[0])
    for n, (rows, head) in enumerate(chains):
        p = p_next
        if n + 1 < len(chains):
            p_next = score_half(*chains[n + 1])
        out = value_half(head, p)
        if head % 2 == 0:
            first_of_pair = out
        else:
            pair = head // 2
            o_ref[rows, pair * LANES:(pair + 1) * LANES] = (
                jnp.where(lane < MLA_V_DIM, first_of_pair, out).astype(BF16))


def _latent_attention(qb, kb, vb):
    batch = qb.shape[0]
    pairs = MLA_HEADS // 2 // MLA_PAIRS_PER_STEP
    qk_w = MLA_PAIRS_PER_STEP * 2 * MLA_SLOT
    v_w = MLA_PAIRS_PER_STEP * LANES
    step_rows = MLA_Q_TILES_PER_STEP * MLA_Q_TILE
    est = (2 * step_rows * qk_w * 2 + 2 * SEQ * qk_w * 2 + 2 * SEQ * v_w * 2 + SEQ * 2 * v_w * 2
           + 2 * step_rows * v_w * 2 + 4 * MLA_PAIRS_PER_STEP * MLA_Q_TILE * SEQ * 4)
    return pl.pallas_call(
        _latent_kernel,
        grid=(batch, pairs, SEQ // step_rows),
        in_specs=[
            pl.BlockSpec((None, step_rows, qk_w), lambda b, p, i: (b, i, p)),
            pl.BlockSpec((None, SEQ, qk_w), lambda b, p, i: (b, 0, p)),
            pl.BlockSpec((None, SEQ, v_w), lambda b, p, i: (b, 0, p)),
        ],
        out_specs=pl.BlockSpec((None, step_rows, v_w), lambda b, p, i: (b, i, p)),
        out_shape=jax.ShapeDtypeStruct((batch, SEQ, MLA_WIDTH), BF16),
        scratch_shapes=[pltpu.VMEM((MLA_PAIRS_PER_STEP, SEQ, 2 * LANES), BF16)],
        compiler_params=pltpu.CompilerParams(
            dimension_semantics=("arbitrary", "arbitrary", "arbitrary"),
            vmem_limit_bytes=_vmem_limit(est)),
        name="latent_attn",
    )(qb, kb, vb)


def _mem_kv_kernel(mem_ref, w_ref, o_ref):
    o_ref[...] = jnp.dot(mem_ref[...].astype(BF16), w_ref[...], preferred_element_type=F32).astype(BF16)


def _mem_kv(mem, w_mem):
    batch = mem.shape[0]
    est = 2 * N_MEM * D_MODEL * 4 + 2 * w_mem.size * 2 + 2 * N_MEM * 2 * MEM_WIDTH * 2 + N_MEM * D_MODEL * 8
    return pl.pallas_call(
        _mem_kv_kernel,
        grid=(batch,),
        in_specs=[pl.BlockSpec((None, N_MEM, D_MODEL), lambda b: (b, 0, 0)),
                  pl.BlockSpec(w_mem.shape, lambda b: (0, 0))],
        out_specs=pl.BlockSpec((None, N_MEM, 2 * MEM_WIDTH), lambda b: (b, 0, 0)),
        out_shape=jax.ShapeDtypeStruct((batch, N_MEM, 2 * MEM_WIDTH), BF16),
        compiler_params=pltpu.CompilerParams(
            dimension_semantics=("arbitrary",), vmem_limit_bytes=_vmem_limit(est)),
        name="mem_kv",
    )(mem, w_mem)


def _output_kernel(x_ref, ya_ref, yb_ref, gate_ref, mq_ref, mkv_ref, wout_ref,
                   goa_ref, gob_ref, gom_ref, gemb_ref, bemb_ref, gpost_ref, bpost_ref,
                   o_ref, y_scr):
    for hd in range(MEM_HEADS):
        cols = slice(hd * MEM_HEAD_DIM, (hd + 1) * MEM_HEAD_DIM)
        s = lax.dot_general(mq_ref[:, cols], mkv_ref[:, cols], (((1,), (1,)), ((), ())),
                            preferred_element_type=F32)
        m = jnp.max(s, axis=-1, keepdims=True)
        p = jnp.exp(s - m)
        l = jnp.sum(p, axis=-1, keepdims=True)
        o = jnp.dot(p.astype(BF16), mkv_ref[:, MEM_WIDTH + hd * MEM_HEAD_DIM:MEM_WIDTH + (hd + 1) * MEM_HEAD_DIM],
                    preferred_element_type=F32)
        y_scr[:, cols] = o / l

    off_b = A_WIDTH
    off_m = A_WIDTH + MLA_WIDTH
    ya_all = jnp.concatenate([ya_ref[cg] for cg in range(A_PAIRS)], axis=1)
    ya = _rms_rows(ya_all.astype(F32), goa_ref[...]) * gate_ref[:, :off_b].astype(F32)
    yb = _rms_rows(yb_ref[...].astype(F32), gob_ref[...]) * gate_ref[:, off_b:off_m].astype(F32)
    ym = _rms_rows(y_scr[...], gom_ref[...]) * gate_ref[:, off_m:].astype(F32)
    sub = (jnp.dot(ya.astype(BF16), wout_ref[:off_b, :], preferred_element_type=F32)
           + jnp.dot(yb.astype(BF16), wout_ref[off_b:off_m, :], preferred_element_type=F32)
           + jnp.dot(ym.astype(BF16), wout_ref[off_m:, :], preferred_element_type=F32))

    for c in range(N_CHUNKS):
        h = _layer_norm_rows(x_ref[_chunk(c), :], gemb_ref[...], bemb_ref[...])
        z = DEEPNORM_ALPHA * h + sub[c * ROW_CHUNK:(c + 1) * ROW_CHUNK, :]
        o_ref[_chunk(c), :] = _layer_norm_rows(z, gpost_ref[...], bpost_ref[...])


def _output_stage(x, ya, yb, gates, mq, mkv, w_out, g_out_a, g_out_b, g_out_m, g_emb, b_emb,
                  g_post, b_post):
    batch = x.shape[0]

    def rows_spec(width):
        return pl.BlockSpec((None, ROW_TILE, width), lambda b, i: (b, i, 0))

    def full_spec(arr):
        return pl.BlockSpec(arr.shape, lambda b, i: (0,) * arr.ndim)

    est = (4 * ROW_TILE * D_MODEL * 4 + 2 * ROW_TILE * (A_WIDTH + MLA_WIDTH + D_MIX + MEM_WIDTH) * 2
           + 2 * N_MEM * 2 * MEM_WIDTH * 2 + 2 * w_out.size * 2 + ROW_TILE * MEM_WIDTH * 4
           + 4 * ROW_TILE * D_MODEL * 4)
    return pl.pallas_call(
        _output_kernel,
        grid=(batch, N_ROW_TILES),
        in_specs=[
            rows_spec(D_MODEL),
            pl.BlockSpec((None, A_PAIRS, ROW_TILE, LANES), lambda b, i: (b, 0, i, 0)),
            rows_spec(MLA_WIDTH), rows_spec(D_MIX), rows_spec(MEM_WIDTH),
            pl.BlockSpec((None, N_MEM, 2 * MEM_WIDTH), lambda b, i: (b, 0, 0)),
            full_spec(w_out), full_spec(g_out_a), full_spec(g_out_b), full_spec(g_out_m),
            full_spec(g_emb), full_spec(b_emb), full_spec(g_post), full_spec(b_post),
        ],
        out_specs=rows_spec(D_MODEL),
        out_shape=jax.ShapeDtypeStruct(x.shape, F32),
        scratch_shapes=[pltpu.VMEM((ROW_TILE, MEM_WIDTH), F32)],
        compiler_params=pltpu.CompilerParams(
            dimension_semantics=("arbitrary", "arbitrary"), vmem_limit_bytes=_vmem_limit(est)),
        name="output_stage",
    )(x, ya, yb, gates, mq, mkv, w_out, g_out_a, g_out_b, g_out_m, g_emb, b_emb, g_post, b_post)


def _rope_freq_rows(rot_dim):
    inv_freq = (np.float32(ROPE_THETA) ** (-(np.arange(0, rot_dim, 2, dtype=np.float32) / np.float32(rot_dim)))
                ).astype(np.float32)
    return jnp.asarray(np.repeat(inv_freq[:, None], LANES, axis=1))


def _a_qk_columns():
    return np.asarray([(2 * pair + hh) * A_HEAD_DIM + d
                       for pair in range(A_PAIRS) for hh, d in _a_qk_lane_order()])


def _mla_slot_lanes():
    half = MLA_ROPE_DIM // 2
    assert ROPE_FIRST_LANES == (0, half)
    lanes = [None] * MLA_SLOT
    for i in range(half):
        lanes[i] = ("rope", i)
        lanes[ROPE_PARTNER_SHIFT + i] = ("rope", half + i)
    free = [lane for lane in range(MLA_SLOT) if lanes[lane] is None]
    for d in range(MLA_NOPE_DIM):
        lanes[free[d]] = ("nope", d)
    return lanes


def _slot_gather(cols, index_of):
    n = cols.shape[-1]
    idx = np.asarray([n if index_of(lane) is None else index_of(lane) for lane in _mla_slot_lanes()])
    return jnp.pad(cols, [(0, 0)] * (cols.ndim - 1) + [(0, 1)])[..., idx]


def kernel(x, mem, positions, g_emb, b_emb, w_in, g_cq, g_ckv, w_uq, w_ukv, w_mem_kv, g_out_a, g_out_b,
           g_out_m, w_out, g_post, b_post):
    batch = x.shape[0]
    assert x.shape == (batch, SEQ, D_MODEL) and w_in.shape[0] == DEPTH == 1
    row = lambda v: v.reshape(1, -1).astype(F32)

    splits = [int(i) for i in np.cumsum(IN_SPLITS)[:-1]]
    a_q, a_k, a_v, a_g, c_q, c_kv, b_kr, b_g, m_q, m_g = jnp.split(w_in[0].astype(BF16), splits, axis=1)
    qk_cols = _a_qk_columns()
    w_qkv = jnp.concatenate([a_q[:, qk_cols], a_k[:, qk_cols], a_v], axis=1).astype(BF16)
    rope_only = lambda lane: lane[1] if lane is not None and lane[0] == "rope" else None
    nope_only = lambda lane: lane[1] if lane is not None and lane[0] == "nope" else None
    nope_then_rope = lambda lane: None if lane is None else (lane[1] + (MLA_NOPE_DIM if lane[0] == "rope" else 0))
    kr_slot = _slot_gather(b_kr, rope_only)
    w_b = jnp.concatenate([a_g, b_g, m_g, m_q, c_q, c_kv, kr_slot], axis=1).astype(BF16)
    qk_dim = MLA_NOPE_DIM + MLA_ROPE_DIM
    w_uq_p = _slot_gather(w_uq[0].reshape(MLA_Q_RANK, MLA_HEADS, qk_dim), nope_then_rope
                          ).reshape(MLA_Q_RANK, _MLA_QK_W).astype(BF16)
    ukv = w_ukv[0].reshape(MLA_KV_RANK, MLA_HEADS, MLA_NOPE_DIM + MLA_V_DIM)
    w_uk_p = _slot_gather(ukv[:, :, :MLA_NOPE_DIM], nope_only)
    w_ukv_p = jnp.concatenate([w_uk_p.reshape(MLA_KV_RANK, _MLA_QK_W),
                               ukv[:, :, MLA_NOPE_DIM:].reshape(MLA_KV_RANK, MLA_WIDTH)], axis=1).astype(BF16)

    pos_view = positions.astype(F32).reshape(batch, N_ROW_TILES, N_CHUNKS, ROW_CHUNK)
    freq_a = _rope_freq_rows(A_ROT_DIM)
    freq_b = _rope_freq_rows(MLA_ROPE_DIM)
    g_emb_r, b_emb_r = row(g_emb), row(b_emb)

    outs_a = _proj_a(x, pos_view, freq_a, g_emb_r, b_emb_r, w_qkv)
    y_a = _dilated_attention(outs_a[0:3], outs_a[3:6], outs_a[6:9])

    gates, mq, qb, kb, vb = _proj_b(x, pos_view, freq_b, g_emb_r, b_emb_r, w_b, row(g_cq[0]),
                                    row(g_ckv[0]), w_uq_p, w_ukv_p)
    y_b = _latent_attention(qb, kb, vb)

    mkv = _mem_kv(mem, w_mem_kv[0].astype(BF16))
    return _output_stage(x, y_a, y_b, gates, mq, mkv, w_out[0].astype(BF16), row(g_out_a[0]),
                         row(g_out_b[0]), row(g_out_m[0]), g_emb_r, b_emb_r, row(g_post[0]), row(b_post[0]))
```

```python
import math

import jax
import jax.numpy as jnp
import numpy as np
from jax import lax
from jax.experimental import pallas as pl
from jax.experimental.pallas import tpu as pltpu

D_MODEL = 1024
SEQ = 2048
A_HEADS = 16
A_HEAD_DIM = 64
A_WIDTH = A_HEADS * A_HEAD_DIM
A_ROT_DIM = A_HEAD_DIM // 4
A_SIDE = 64
DILATIONS = (1, 4, 16)
MLA_HEADS = 8
MLA_Q_RANK = 256
MLA_KV_RANK = 128
MLA_NOPE_DIM = 64
MLA_ROPE_DIM = 32
MLA_V_DIM = 64
MLA_WIDTH = MLA_HEADS * MLA_V_DIM
N_MEM = 256
MEM_HEADS = 4
MEM_HEAD_DIM = 128
MEM_WIDTH = MEM_HEADS * MEM_HEAD_DIM
D_MIX = A_WIDTH + MLA_WIDTH + MEM_WIDTH
ROPE_THETA = 500000.0
NORM_EPS = 1e-5
NEG_INF = -1e30
DEPTH = 1
DEEPNORM_ALPHA = (2 * DEPTH) ** 0.25
IN_SPLITS = (A_WIDTH, A_WIDTH, A_WIDTH, A_WIDTH, MLA_Q_RANK, MLA_KV_RANK, MLA_ROPE_DIM, MLA_WIDTH,
             MEM_WIDTH, MEM_WIDTH)
LOG2_E = math.log2(math.e)

LANES = 128
VMEM_BYTES_V7X = 64 * 1024 * 1024
VMEM_RESERVED_BYTES = 4 << 20
VMEM_MIN_REQUEST_BYTES = 32 << 20
VMEM_SPILL_ALLOWANCE_BYTES = 8 << 20

ROW_TILE = 512
ROW_CHUNK = 128
N_CHUNKS = ROW_TILE // ROW_CHUNK
N_HALVES = 1
ROW_HALF = ROW_TILE // N_HALVES
CHUNKS_PER_HALF = N_CHUNKS // N_HALVES
N_ROW_TILES = SEQ // ROW_TILE
A_PAIRS = A_WIDTH // LANES
A_Q_TILE = 128
A_K_WIN = 2 * A_Q_TILE
C16_PITCH = SEQ // 16 + 8
MLA_SLOT = LANES
MLA_Q_TILE = 256
MLA_Q_TILES_PER_STEP = 4
MLA_PAIRS_PER_STEP = 4

F32 = jnp.float32
BF16 = jnp.bfloat16


def _vmem_limit(nbytes):
    return int(min(VMEM_BYTES_V7X - VMEM_RESERVED_BYTES,
                   max(VMEM_MIN_REQUEST_BYTES, nbytes + VMEM_SPILL_ALLOWANCE_BYTES)))


def _layer_norm_rows(x, g, b):
    mu = jnp.mean(x, axis=-1, keepdims=True)
    xc = x - mu
    var = jnp.mean(xc * xc, axis=-1, keepdims=True)
    return xc * lax.rsqrt(var + NORM_EPS) * g + b


def _rms_rows(x, g):
    ms = jnp.mean(x * x, axis=-1, keepdims=True)
    return x * lax.rsqrt(ms + NORM_EPS) * g


def _chunk(c):
    return slice(c * ROW_CHUNK, (c + 1) * ROW_CHUNK)


def _normed_chunks_to_scratch(x_ref, g_ref, b_ref, h_scr, chunks):
    for c in chunks:
        h_scr[_chunk(c), :] = _layer_norm_rows(x_ref[_chunk(c), :], g_ref[...], b_ref[...]).astype(BF16)


ROPE_FIRST_LANES = (0, 16)
ROPE_PARTNER_SHIFT = LANES // 2


def _rope_tables(pos_ref, freq_ref, spread, chunks, cos_scr, sin_scr):
    n_freq = freq_ref.shape[0]
    lane = lax.broadcasted_iota(jnp.int32, (ROW_CHUNK, LANES), 1)
    first = (lane >= ROPE_FIRST_LANES[0]) & (lane < ROPE_FIRST_LANES[1])
    second = (lane >= ROPE_FIRST_LANES[0] + ROPE_PARTNER_SHIFT) & (lane < ROPE_FIRST_LANES[1] + ROPE_PARTNER_SHIFT)
    pad = jnp.zeros((ROW_CHUNK - n_freq, LANES), F32)
    for c in chunks:
        ang = freq_ref[...] * pos_ref[c:c + 1, :]
        cos = spread(jnp.concatenate([jnp.cos(ang), pad], axis=0).T)
        sin = spread(jnp.concatenate([jnp.sin(ang), pad], axis=0).T)
        cos_scr[_chunk(c), :] = jnp.where(first | second, cos, 1.0)
        sin_scr[_chunk(c), :] = jnp.where(first, -sin, jnp.where(second, sin, 0.0))


def _spread_a(t):
    t = t + pltpu.roll(t, A_ROT_DIM // 2, 1)
    return t + pltpu.roll(t, ROPE_PARTNER_SHIFT, 1)


def _spread_b(t):
    return t + pltpu.roll(t, ROPE_PARTNER_SHIFT, 1)


def _rope_block(x, rows, cos_scr, sin_scr):
    return x * cos_scr[rows, :] + pltpu.roll(x, ROPE_PARTNER_SHIFT, 1) * sin_scr[rows, :]


def _proj_a_kernel(x_ref, pos_ref, freq_ref, g_ref, b_ref, w_ref,
                   qn_ref, kn_ref, vn_ref, q4_ref, k4_ref, v4_ref, q16_ref, k16_ref, v16_ref,
                   h_scr, acc_scr, st4_scr, cos_scr, sin_scr):
    outs = ((qn_ref, q4_ref, q16_ref), (kn_ref, k4_ref, k16_ref), (vn_ref, v4_ref, v16_ref))
    q_scale = (A_HEAD_DIM ** -0.5) * LOG2_E

    def prepare(hf):
        chunks = range(hf * CHUNKS_PER_HALF, (hf + 1) * CHUNKS_PER_HALF)
        _normed_chunks_to_scratch(x_ref, g_ref, b_ref, h_scr, chunks)
        _rope_tables(pos_ref, freq_ref, _spread_a, chunks, cos_scr, sin_scr)

    def matmul(hf, sec):
        if sec == 1 and hf + 1 < N_HALVES:
            prepare(hf + 1)
        return jnp.dot(h_scr[hf * ROW_HALF:(hf + 1) * ROW_HALF, :],
                       w_ref[:, sec * A_WIDTH:(sec + 1) * A_WIDTH], preferred_element_type=F32)

    def epilogue(hf, sec, res):
        nat_ref, r4_ref, r16_ref = outs[sec]
        acc = acc_scr.at[hf * len(outs) + sec]
        st4 = st4_scr.at[hf * len(outs) + sec]
        for cg in range(A_PAIRS):
            cols = slice(cg * LANES, (cg + 1) * LANES)
            for lc in range(CHUNKS_PER_HALF):
                rows = _chunk(hf * CHUNKS_PER_HALF + lc)
                blk = res[lc * ROW_CHUNK:(lc + 1) * ROW_CHUNK, cols]
                if sec < 2:
                    blk = _rope_block(blk, rows, cos_scr, sin_scr)
                    if sec == 0:
                        blk = blk * q_scale
                acc[cg, lc * ROW_CHUNK:(lc + 1) * ROW_CHUNK, :] = blk
                nat_ref[cg, rows, :] = blk.astype(BF16)
            n4, n16 = ROW_HALF // 4, ROW_HALF // 16
            for r in range(4):
                rows4 = acc[cg, pl.ds(r, n4, stride=4), :]
                st4[cg, r] = rows4
                r4_ref[cg, r, hf * n4:(hf + 1) * n4, :] = rows4.astype(BF16)
            for r in range(4):
                for v in range(4):
                    r16_ref[cg, r + 4 * v, hf * n16:(hf + 1) * n16, :] = (
                        st4[cg, r, pl.ds(v, n16, stride=4), :].astype(BF16))

    prepare(0)
    items =[(hf, sec) for hf in range(N_HALVES) for sec in range(len(outs))]
    pending = None
    for hf, sec in items:
        res = matmul(hf, sec)
        if pending is not None:
            epilogue(*pending)
        pending = (hf, sec, res)
    epilogue(*pending)


def _proj_a(x, pos_view, freq_a, g_emb, b_emb, w_qkv):
    batch = x.shape[0]
    nat = jax.ShapeDtypeStruct((batch, A_PAIRS, SEQ, LANES), BF16)
    r4 = jax.ShapeDtypeStruct((batch, A_PAIRS, 4, SEQ // 4, LANES), BF16)
    r16 = jax.ShapeDtypeStruct((batch, A_PAIRS, 16, SEQ // 16, LANES), BF16)
    nat_spec = pl.BlockSpec((None, A_PAIRS, ROW_TILE, LANES), lambda b, i: (b, 0, i, 0))
    r4_spec = pl.BlockSpec((None, A_PAIRS, 4, ROW_TILE // 4, LANES), lambda b, i: (b, 0, 0, i, 0))
    r16_spec = pl.BlockSpec((None, A_PAIRS, 16, ROW_TILE // 16, LANES), lambda b, i: (b, 0, 0, i, 0))
    tile_out = ROW_TILE * A_WIDTH * 2
    est = (2 * ROW_TILE * D_MODEL * 4 + 2 * w_qkv.size * 2 + 2 * 9 * tile_out
           + ROW_TILE * D_MODEL * 2 + 2 * 3 * ROW_TILE * A_WIDTH * 4 + 3 * ROW_TILE * LANES * 4
           + 2 * ROW_HALF * A_WIDTH * 4)
    return pl.pallas_call(
        _proj_a_kernel,
        grid=(batch, N_ROW_TILES),
        in_specs=[
            pl.BlockSpec((None, ROW_TILE, D_MODEL), lambda b, i: (b, i, 0)),
            pl.BlockSpec((None, None, N_CHUNKS, ROW_CHUNK), lambda b, i: (b, i, 0, 0)),
            pl.BlockSpec(freq_a.shape, lambda b, i: (0, 0)),
            pl.BlockSpec((1, D_MODEL), lambda b, i: (0, 0)),
            pl.BlockSpec((1, D_MODEL), lambda b, i: (0, 0)),
            pl.BlockSpec(w_qkv.shape, lambda b, i: (0, 0)),
        ],
        out_specs=[nat_spec] * 3 + [r4_spec] * 3 + [r16_spec] * 3,
        out_shape=[nat] * 3 + [r4] * 3 + [r16] * 3,
        scratch_shapes=[
            pltpu.VMEM((ROW_TILE, D_MODEL), BF16),
            pltpu.VMEM((N_HALVES * 3, A_PAIRS, ROW_HALF, LANES), F32),
            pltpu.VMEM((N_HALVES * 3, A_PAIRS, 4, ROW_HALF // 4, LANES), F32),
            pltpu.VMEM((ROW_TILE, LANES), F32),
            pltpu.VMEM((ROW_TILE, LANES), F32),
        ],
        compiler_params=pltpu.CompilerParams(
            dimension_semantics=("arbitrary", "arbitrary"), vmem_limit_bytes=_vmem_limit(est)),
        name="proj_a",
    )(x, pos_view, freq_a, g_emb, b_emb, w_qkv)


_GATE_W = D_MIX
_OFF_MQ = _GATE_W
_OFF_CQ = _OFF_MQ + MEM_WIDTH
_OFF_CKV = _OFF_CQ + MLA_Q_RANK
_OFF_KR = _OFF_CKV + MLA_KV_RANK
_MLA_QK_W = MLA_HEADS * MLA_SLOT


def _proj_b_kernel(x_ref, pos_ref, freq_ref, g_ref, b_ref, w_ref, gcq_ref, gckv_ref, wuq_ref,
                   wukv_ref, gate_ref, mq_ref, qb_ref, kb_ref, vb_ref,
                   h_scr, cos_scr, sin_scr):
    mla_scale = ((MLA_NOPE_DIM + MLA_ROPE_DIM) ** -0.5) * LOG2_E
    local = lambda lc: slice(lc * ROW_CHUNK, (lc + 1) * ROW_CHUNK)
    half_rows = lambda hf: slice(hf * ROW_HALF, (hf + 1) * ROW_HALF)

    def prepare(hf):
        chunks = range(hf * CHUNKS_PER_HALF, (hf + 1) * CHUNKS_PER_HALF)
        _normed_chunks_to_scratch(x_ref, g_ref, b_ref, h_scr, chunks)
        _rope_tables(pos_ref, freq_ref, _spread_b, chunks, cos_scr, sin_scr)

    def h_dot(hf, first, width):
        return jnp.dot(h_scr[half_rows(hf), :], w_ref[:, first:first + width], preferred_element_type=F32)

    def rope_slots(hf, slots, out_ref, scale=None, extra=None):
        for lc in range(CHUNKS_PER_HALF):
            rows = _chunk(hf * CHUNKS_PER_HALF + lc)
            for hd in range(MLA_HEADS):
                cols = slice(hd * MLA_SLOT, (hd + 1) * MLA_SLOT)
                blk = slots[local(lc), cols]
                if extra is not None:
                    blk = blk + extra[local(lc), :]
                blk = _rope_block(blk, rows, cos_scr, sin_scr)
                out_ref[rows, cols] = (blk if scale is None else blk * scale).astype(BF16)

    def stages(hf):
        state = {}

        def down():
            lat = h_dot(hf, _OFF_CQ, MLA_Q_RANK + MLA_KV_RANK + MLA_SLOT)
            state["cqn"] = _rms_rows(lat[:, :MLA_Q_RANK], gcq_ref[...]).astype(BF16)
            state["ckvn"] = _rms_rows(lat[:, MLA_Q_RANK:MLA_Q_RANK + MLA_KV_RANK], gckv_ref[...]).astype(BF16)
            state["kr"] = lat[:, MLA_Q_RANK + MLA_KV_RANK:]

        def queries():
            return jnp.dot(state["cqn"], wuq_ref[...], preferred_element_type=F32)

        def keys_values():
            k_all = jnp.dot(state["ckvn"], wukv_ref[:, :_MLA_QK_W], preferred_element_type=F32)
            return k_all, jnp.dot(state["ckvn"], wukv_ref[:, _MLA_QK_W:], preferred_element_type=F32)

        def store_keys_values(res):
            k_all, vb = res
            vb_ref[half_rows(hf), :] = vb.astype(BF16)
            rope_slots(hf, k_all, kb_ref, extra=state["kr"])

        def gate_block(n):
            def produce():
                if n == 0 and hf + 1 < N_HALVES:
                    prepare(hf + 1)
                return h_dot(hf, n * A_WIDTH, A_WIDTH)

            def consume(g):
                gate_ref[half_rows(hf), n * A_WIDTH:(n + 1) * A_WIDTH] = (g / (1.0 + jnp.exp(-g))).astype(BF16)

            return produce, consume

        def store_mq(mq):
            mq_ref[half_rows(hf), :] = (mq * (MEM_HEAD_DIM ** -0.5)).astype(BF16)

        return [
            (down, lambda _: None),
            (queries, lambda q_all: rope_slots(hf, q_all, qb_ref, scale=mla_scale)),
            (keys_values, store_keys_values),
            *[gate_block(n) for n in range(_GATE_W // A_WIDTH)],
            (lambda: h_dot(hf, _OFF_MQ, MEM_WIDTH), store_mq),
        ]

    prepare(0)
    pending = None
    for produce, consume in [st for hf in range(N_HALVES) for st in stages(hf)]:
        res = produce()
        if pending is not None:
            pending[0](pending[1])
        pending = (consume, res)
    pending[0](pending[1])


def _proj_b(x, pos_view, freq_b, g_emb, b_emb, w_b, g_cq, g_ckv, w_uq_p, w_ukv_p):
    batch = x.shape[0]

    def rows_spec(width):
        return pl.BlockSpec((None, ROW_TILE, width), lambda b, i: (b, i, 0))

    def full_spec(arr):
        return pl.BlockSpec(arr.shape, lambda b, i: (0,) * arr.ndim)

    widths = (_GATE_W, MEM_WIDTH, _MLA_QK_W, _MLA_QK_W, MLA_WIDTH)
    est = (2 * ROW_TILE * D_MODEL * 4 + 2 * (w_b.size + w_uq_p.size + w_ukv_p.size) * 2
           + 2 * ROW_TILE * sum(widths) * 2 + ROW_TILE * D_MODEL * 2 + ROW_TILE * A_WIDTH * 4
           + 3 * ROW_TILE * LANES * 4 + 2 * ROW_TILE * A_WIDTH * 4)
    return pl.pallas_call(
        _proj_b_kernel,
        grid=(batch, N_ROW_TILES),
        in_specs=[
            rows_spec(D_MODEL),
            pl.BlockSpec((None, None, N_CHUNKS, ROW_CHUNK), lambda b, i: (b, i, 0, 0)),
            full_spec(freq_b),
            full_spec(g_emb), full_spec(b_emb), full_spec(w_b), full_spec(g_cq), full_spec(g_ckv),
            full_spec(w_uq_p), full_spec(w_ukv_p),
        ],
        out_specs=[rows_spec(w) for w in widths],
        out_shape=[jax.ShapeDtypeStruct((batch, SEQ, w), BF16) for w in widths],
        scratch_shapes=[
            pltpu.VMEM((ROW_TILE, D_MODEL), BF16),
            pltpu.VMEM((ROW_TILE, LANES), F32),
            pltpu.VMEM((ROW_TILE, LANES), F32),
        ],
        compiler_params=pltpu.CompilerParams(
            dimension_semantics=("arbitrary", "arbitrary"), vmem_limit_bytes=_vmem_limit(est)),
        name="proj_b",
    )(x, pos_view, freq_b, g_emb, b_emb, w_b, g_cq, g_ckv, w_uq_p, w_ukv_p)


def _first_head_lanes(n_rows):
    return lax.broadcasted_iota(jnp.int32, (n_rows, LANES), 1) < A_HEAD_DIM


def _a_qk_lane_order():
    half = A_ROT_DIM // 2
    rest = A_HEAD_DIM - A_ROT_DIM
    order = ([(0, d) for d in range(half)] + [(1, d) for d in range(half)]
             + [(0, A_ROT_DIM + d) for d in range(rest)]
             + [(0, half + d) for d in range(half)] + [(1, half + d) for d in range(half)]
             + [(1, A_ROT_DIM + d) for d in range(rest)])
    assert len(order) == LANES and len(set(order)) == LANES
    return order


def _first_head_qk_lanes(n_rows):
    lane = lax.broadcasted_iota(jnp.int32, (n_rows, LANES), 1)
    half = A_ROT_DIM // 2
    return (lane < half) | ((lane >= A_ROT_DIM) & (lane < A_ROT_DIM + A_HEAD_DIM - half))


def _band_scores(q, k, bias):
    n_q = q.shape[0]
    first_head = _first_head_lanes(n_q)
    first_head_qk = _first_head_qk_lanes(n_q)
    zero = jnp.zeros_like(q)
    q2 = jnp.concatenate([jnp.where(first_head_qk, q, zero), jnp.where(first_head_qk, zero, q)], axis=0)
    s = lax.dot_general(q2, k, (((1,), (1,)), ((), ())), preferred_element_type=F32) + bias
    m = jnp.max(s, axis=-1, keepdims=True)
    p = jnp.exp2((s - m).astype(BF16))
    return p, jnp.where(first_head, m[:n_q], m[n_q:])


def _band_values(p, v_ones):
    n_q = p.shape[0] // 2
    first_head = _first_head_lanes(n_q)
    o = jnp.dot(p, v_ones, preferred_element_type=F32)
    acc = jnp.where(first_head, o[:n_q, :LANES], o[n_q:, :LANES])
    l_b = jnp.where(first_head, o[:n_q, LANES:], o[n_q:, LANES:])
    return acc, l_b


def _dilated_kernel(qn_ref, kn_ref, vn_ref, q4_ref, k4_ref, v4_ref, q16_ref, k16_ref, v16_ref,
                    o_ref, von_scr, vo4_scr, vo16_scr, bias_scr, bias16_scr, acc_scr, mx_scr, sm_scr, c16_scr):
    def window_bias(n_keys, first_key):
        row = lax.broadcasted_iota(jnp.int32, (2 * A_Q_TILE, n_keys), 0)
        col = lax.broadcasted_iota(jnp.int32, (2 * A_Q_TILE, n_keys), 1)
        q_idx = jnp.where(row >= A_Q_TILE, row - A_Q_TILE, row)
        off = col - (q_idx + first_key)
        return jnp.where(jnp.abs(off) <= A_SIDE, 0.0, NEG_INF).astype(F32)

    @pl.when((pl.program_id(0) == 0) & (pl.program_id(1) == 0))
    def _():
        for variant in range(3):
            bias_scr[variant] = window_bias(A_K_WIN, variant * A_SIDE)
        bias16_scr[...] = window_bias(A_Q_TILE, 0)
        von_scr[:, LANES:] = jnp.ones((SEQ, LANES), BF16)
        vo4_scr[:, :, LANES:] = jnp.ones((4, SEQ // 4, LANES), BF16)
        vo16_scr[:, :, LANES:] = jnp.ones((16, SEQ // 16, LANES), BF16)

    von_scr[:, :LANES] = vn_ref[...]
    vo4_scr[:, :, :LANES] = v4_ref[...]
    vo16_scr[:, :, :LANES] = v16_ref[...]

    len4 = SEQ // 4
    tiles4 = len4 // A_Q_TILE
    n_tiles = SEQ // A_Q_TILE

    def window(tile, n_seq_tiles, seq_len):
        qs = tile * A_Q_TILE
        ks = min(max(qs - A_SIDE, 0), seq_len - A_K_WIN)
        variant = 0 if tile == 0 else (2 if tile == n_seq_tiles - 1 else 1)
        return qs, ks, variant

    acc16, mx16, sm16 = (c16_scr.at[k] for k in range(3))

    def score_half(i):
        qs, ks, var = window(i, n_tiles, SEQ)
        res = i // tiles4
        qs4, ks4, var4 = window(i % tiles4, tiles4, len4)
        parts = (
            _band_scores(qn_ref[pl.ds(qs, A_Q_TILE), :], kn_ref[pl.ds(ks, A_K_WIN), :], bias_scr[var]),
            _band_scores(q4_ref[res, pl.ds(qs4, A_Q_TILE), :], k4_ref[res, pl.ds(ks4, A_K_WIN), :], bias_scr[var4]),
            _band_scores(q16_ref[i], k16_ref[i], bias16_scr[...]),
        )
        class16 = pl.ds(i * C16_PITCH, A_Q_TILE)
        dests = ((acc_scr.at[0], mx_scr.at[0], sm_scr.at[0], pl.ds(qs, A_Q_TILE)),
                 (acc_scr.at[1], mx_scr.at[1], sm_scr.at[1], pl.ds(res + 4 * qs4, A_Q_TILE, stride=4)),
                 (acc16, mx16, sm16, class16))
        v_wins = (von_scr.at[pl.ds(ks, A_K_WIN), :], vo4_scr.at[res, pl.ds(ks4, A_K_WIN), :], vo16_scr.at[i])
        for (_, mx_at, _, dst), (_, m_b) in zip(dests, parts):
            mx_at[dst, :] = m_b
        return [(p, dest, v_win) for (p, _), dest, v_win in zip(parts, dests, v_wins)]

    def value_half(pending):
        for p, (acc_at, _, sm_at, dst), v_win in pending:
            acc, l_b = _band_values(p, v_win[...])
            acc_at[dst, :] = acc
            sm_at[dst, :] = l_b

    pending = score_half(0)
    for i in range(1, n_tiles):
        ahead = score_half(i)
        value_half(pending)
        pending = ahead
    value_half(pending)

    def class16_rows(scr, c):
        groups = []
        for g in range(ROW_CHUNK // 8):
            first_class = 8 * (g % 2)
            t = (ROW_CHUNK // 16) * c + g // 2
            groups.append(scr[pl.ds(first_class * C16_PITCH + t, 8, stride=C16_PITCH), :])
        return jnp.concatenate(groups, axis=0)

    def merge(c, carry):
        rows = pl.ds(pl.multiple_of(c * ROW_CHUNK, ROW_CHUNK), ROW_CHUNK)
        parts = [(acc_scr[pat, rows, :], mx_scr[pat, rows, :], sm_scr[pat, rows, :]) for pat in range(2)]
        parts.append((class16_rows(acc16, c), class16_rows(mx16, c), class16_rows(sm16, c)))
        m_all = jnp.maximum(jnp.maximum(parts[0][1], parts[1][1]), parts[2][1])
        num = jnp.zeros((ROW_CHUNK, LANES), F32)
        den = jnp.zeros((ROW_CHUNK, LANES), F32)
        for acc, m_b, l_b in parts:
            w = jnp.exp2(m_b - m_all)
            num = num + w * acc
            den = den + w * l_b
        o_ref[rows, :] = (num / den).astype(BF16)
        return carry

    lax.fori_loop(0, SEQ // ROW_CHUNK, merge, 0, unroll=4)


def _dilated_attention(qkv_nat, qkv_4, qkv_16):
    batch = qkv_nat[0].shape[0]
    nat_spec = pl.BlockSpec((None, None, SEQ, LANES), lambda b, p: (b, p, 0, 0))
    r4_spec = pl.BlockSpec((None, None, 4, SEQ // 4, LANES), lambda b, p: (b, p, 0, 0, 0))
    r16_spec = pl.BlockSpec((None, None, 16, SEQ // 16, LANES), lambda b, p: (b, p, 0, 0, 0))
    blk = SEQ * LANES
    est = (2 * 9 * blk * 2 + 2 * blk * 2 + 3 * blk * 2 * 2 + 9 * blk * 4
           + 4 * 2 * A_Q_TILE * A_K_WIN * 4)
    return pl.pallas_call(
        _dilated_kernel,
        grid=(batch, A_PAIRS),
        in_specs=[nat_spec] * 3 + [r4_spec] * 3 + [r16_spec] * 3,
        out_specs=pl.BlockSpec((None, None, SEQ, LANES), lambda b, p: (b, p, 0, 0)),
        out_shape=jax.ShapeDtypeStruct((batch, A_PAIRS, SEQ, LANES), BF16),
        scratch_shapes=[
            pltpu.VMEM((SEQ, 2 * LANES), BF16),
            pltpu.VMEM((4, SEQ // 4, 2 * LANES), BF16),
            pltpu.VMEM((16, SEQ // 16, 2 * LANES), BF16),
            pltpu.VMEM((3, 2 * A_Q_TILE, A_K_WIN), F32),
            pltpu.VMEM((2 * A_Q_TILE, A_Q_TILE), F32),
            pltpu.VMEM((2, SEQ, LANES), F32),
            pltpu.VMEM((2, SEQ, LANES), F32),
            pltpu.VMEM((2, SEQ, LANES), F32),
            pltpu.VMEM((3, 16 * C16_PITCH, LANES), F32),
        ],
        compiler_params=pltpu.CompilerParams(
            dimension_semantics=("arbitrary", "arbitrary"), vmem_limit_bytes=_vmem_limit(est)),
        name="dilated_attn",
    )(*qkv_nat, *qkv_4, *qkv_16)


def _latent_kernel(q_ref, k_ref, v_ref, o_ref, vo_scr):
    @pl.when((pl.program_id(0) == 0) & (pl.program_id(1) == 0) & (pl.program_id(2) == 0))
    def _():
        vo_scr[:, :, LANES:] = jnp.ones((MLA_PAIRS_PER_STEP, SEQ, LANES), BF16)

    @pl.when(pl.program_id(2) == 0)
    def _():
        for pair in range(MLA_PAIRS_PER_STEP):
            vo_scr[pair, :, :LANES] = v_ref[:, pair * LANES:(pair + 1) * LANES]

    def score_half(rows, head):
        cols = slice(head * MLA_SLOT, (head + 1) * MLA_SLOT)
        s = lax.dot_general(q_ref[rows, cols], k_ref[:, cols], (((1,), (1,)), ((), ())),
                            preferred_element_type=F32)
        m = jnp.max(s, axis=-1, keepdims=True)
        return jnp.exp2((s - m).astype(BF16))

    def value_half(head, p):
        o = jnp.dot(p, vo_scr[head // 2], preferred_element_type=F32)
        return o[:, :LANES] / o[:, LANES:]

    n_heads = 2 * MLA_PAIRS_PER_STEP
    chains = [(slice(t * MLA_Q_TILE, (t + 1) * MLA_Q_TILE), head)
              for t in range(MLA_Q_TILES_PER_STEP) for head in range(n_heads)]
    lane = lax.broadcasted_iota(jnp.int32, (MLA_Q_TILE, LANES), 1)
    p_next = score_half(*chains[0])
    for n, (rows, head) in enumerate(chains):
        p = p_next
        if n + 1 < len(chains):
            p_next = score_half(*chains[n + 1])
        out = value_half(head, p)
        if head % 2 == 0:
            first_of_pair = out
        else:
            pair = head // 2
            o_ref[rows, pair * LANES:(pair + 1) * LANES] = (
                jnp.where(lane < MLA_V_DIM, first_of_pair, out).astype(BF16))


def _latent_attention(qb, kb, vb):
    batch = qb.shape[0]
    pairs = MLA_HEADS // 2 // MLA_PAIRS_PER_STEP
    qk_w = MLA_PAIRS_PER_STEP * 2 * MLA_SLOT
    v_w = MLA_PAIRS_PER_STEP * LANES
    step_rows = MLA_Q_TILES_PER_STEP * MLA_Q_TILE
    est = (2 * step_rows * qk_w * 2 + 2 * SEQ * qk_w * 2 + 2 * SEQ * v_w * 2 + SEQ * 2 * v_w * 2
           + 2 * step_rows * v_w * 2 + 4 * MLA_PAIRS_PER_STEP * MLA_Q_TILE * SEQ * 4)
    return pl.pallas_call(
        _latent_kernel,
        grid=(batch, pairs, SEQ // step_rows),
        in_specs=[
            pl.BlockSpec((None, step_rows, qk_w), lambda b, p, i: (b, i, p)),
            pl.BlockSpec((None, SEQ, qk_w), lambda b, p, i: (b, 0, p)),
            pl.BlockSpec((None, SEQ, v_w), lambda b, p, i: (b, 0, p)),
        ],
        out_specs=pl.BlockSpec((None, step_rows, v_w), lambda b, p, i: (b, i, p)),
        out_shape=jax.ShapeDtypeStruct((batch, SEQ, MLA_WIDTH), BF16),
        scratch_shapes=[pltpu.VMEM((MLA_PAIRS_PER_STEP, SEQ, 2 * LANES), BF16)],
        compiler_params=pltpu.CompilerParams(
            dimension_semantics=("arbitrary", "arbitrary", "arbitrary"),
            vmem_limit_bytes=_vmem_limit(est)),
        name="latent_attn",
    )(qb, kb, vb)


def _mem_kv_kernel(mem_ref, w_ref, o_ref):
    o_ref[...] = jnp.dot(mem_ref[...].astype(BF16), w_ref[...], preferred_element_type=F32).astype(BF16)


def _mem_kv(mem, w_mem):
    batch = mem.shape[0]
    est = 2 * N_MEM * D_MODEL * 4 + 2 * w_mem.size * 2 + 2 * N_MEM * 2 * MEM_WIDTH * 2 + N_MEM * D_MODEL * 8
    return pl.pallas_call(
        _mem_kv_kernel,
        grid=(batch,),
        in_specs=[pl.BlockSpec((None, N_MEM, D_MODEL), lambda b: (b, 0, 0)),
                  pl.BlockSpec(w_mem.shape, lambda b: (0, 0))],
        out_specs=pl.BlockSpec((None, N_MEM, 2 * MEM_WIDTH), lambda b: (b, 0, 0)),
        out_shape=jax.ShapeDtypeStruct((batch, N_MEM, 2 * MEM_WIDTH), BF16),
        compiler_params=pltpu.CompilerParams(
            dimension_semantics=("arbitrary",), vmem_limit_bytes=_vmem_limit(est)),
        name="mem_kv",
    )(mem, w_mem)


def _output_kernel(x_ref, ya_ref, yb_ref, gate_ref, mq_ref, mkv_ref, wout_ref,
                   goa_ref, gob_ref, gom_ref, gemb_ref, bemb_ref, gpost_ref, bpost_ref,
                   o_ref, y_scr):
    for hd in range(MEM_HEADS):
        cols = slice(hd * MEM_HEAD_DIM, (hd + 1) * MEM_HEAD_DIM)
        s = lax.dot_general(mq_ref[:, cols], mkv_ref[:, cols], (((1,), (1,)), ((), ())),
                            preferred_element_type=F32)
        m = jnp.max(s, axis=-1, keepdims=True)
        p = jnp.exp(s - m)
        l = jnp.sum(p, axis=-1, keepdims=True)
        o = jnp.dot(p.astype(BF16), mkv_ref[:, MEM_WIDTH + hd * MEM_HEAD_DIM:MEM_WIDTH + (hd + 1) * MEM_HEAD_DIM],
                    preferred_element_type=F32)
        y_scr[:, cols] = o / l

    off_b = A_WIDTH
    off_m = A_WIDTH + MLA_WIDTH
    ya_all = jnp.concatenate([ya_ref[cg] for cg in range(A_PAIRS)], axis=1)
    ya = _rms_rows(ya_all.astype(F32), goa_ref[...]) * gate_ref[:, :off_b].astype(F32)
    yb = _rms_rows(yb_ref[...].astype(F32), gob_ref[...]) * gate_ref[:, off_b:off_m].astype(F32)
    ym = _rms_rows(y_scr[...], gom_ref[...]) * gate_ref[:, off_m:].astype(F32)
    sub = (jnp.dot(ya.astype(BF16), wout_ref[:off_b, :], preferred_element_type=F32)
           + jnp.dot(yb.astype(BF16), wout_ref[off_b:off_m, :], preferred_element_type=F32)
           + jnp.dot(ym.astype(BF16), wout_ref[off_m:, :], preferred_element_type=F32))

    for c in range(N_CHUNKS):
        h = _layer_norm_rows(x_ref[_chunk(c), :], gemb_ref[...], bemb_ref[...])
        z = DEEPNORM_ALPHA * h + sub[c * ROW_CHUNK:(c + 1) * ROW_CHUNK, :]
        o_ref[_chunk(c), :] = _layer_norm_rows(z, gpost_ref[...], bpost_ref[...])


def _output_stage(x, ya, yb, gates, mq, mkv, w_out, g_out_a, g_out_b, g_out_m, g_emb, b_emb,
                  g_post, b_post):
    batch = x.shape[0]

    def rows_spec(width):
        return pl.BlockSpec((None, ROW_TILE, width), lambda b, i: (b, i, 0))

    def full_spec(arr):
        return pl.BlockSpec(arr.shape, lambda b, i: (0,) * arr.ndim)

    est = (4 * ROW_TILE * D_MODEL * 4 + 2 * ROW_TILE * (A_WIDTH + MLA_WIDTH + D_MIX + MEM_WIDTH) * 2
           + 2 * N_MEM * 2 * MEM_WIDTH * 2 + 2 * w_out.size * 2 + ROW_TILE * MEM_WIDTH * 4
           + 4 * ROW_TILE * D_MODEL * 4)
    return pl.pallas_call(
        _output_kernel,
        grid=(batch, N_ROW_TILES),
        in_specs=[
            rows_spec(D_MODEL),
            pl.BlockSpec((None, A_PAIRS, ROW_TILE, LANES), lambda b, i: (b, 0, i, 0)),
            rows_spec(MLA_WIDTH), rows_spec(D_MIX), rows_spec(MEM_WIDTH),
            pl.BlockSpec((None, N_MEM, 2 * MEM_WIDTH), lambda b, i: (b, 0, 0)),
            full_spec(w_out), full_spec(g_out_a), full_spec(g_out_b), full_spec(g_out_m),
            full_spec(g_emb), full_spec(b_emb), full_spec(g_post), full_spec(b_post),
        ],
        out_specs=rows_spec(D_MODEL),
        out_shape=jax.ShapeDtypeStruct(x.shape, F32),
        scratch_shapes=[pltpu.VMEM((ROW_TILE, MEM_WIDTH), F32)],
        compiler_params=pltpu.CompilerParams(
            dimension_semantics=("arbitrary", "arbitrary"), vmem_limit_bytes=_vmem_limit(est)),
        name="output_stage",
    )(x, ya, yb, gates, mq, mkv, w_out, g_out_a, g_out_b, g_out_m, g_emb, b_emb, g_post, b_post)


def _rope_freq_rows(rot_dim):
    inv_freq = (np.float32(ROPE_THETA) ** (-(np.arange(0, rot_dim, 2, dtype=np.float32) / np.float32(rot_dim)))
                ).astype(np.float32)
    return jnp.asarray(np.repeat(inv_freq[:, None], LANES, axis=1))


def _a_qk_columns():
    return np.asarray([(2 * pair + hh) * A_HEAD_DIM + d
                       for pair in range(A_PAIRS) for hh, d in _a_qk_lane_order()])


def _mla_slot_lanes():
    half = MLA_ROPE_DIM // 2
    assert ROPE_FIRST_LANES == (0, half)
    lanes = [None] * MLA_SLOT
    for i in range(half):
        lanes[i] = ("rope", i)
        lanes[ROPE_PARTNER_SHIFT + i] = ("rope", half + i)
    free = [lane for lane in range(MLA_SLOT) if lanes[lane] is None]
    for d in range(MLA_NOPE_DIM):
        lanes[free[d]] = ("nope", d)
    return lanes


def _slot_gather(cols, index_of):
    n = cols.shape[-1]
    idx = np.asarray([n if index_of(lane) is None else index_of(lane) for lane in _mla_slot_lanes()])
    return jnp.pad(cols, [(0, 0)] * (cols.ndim - 1) + [(0, 1)])[..., idx]


def kernel(x, mem, positions, g_emb, b_emb, w_in, g_cq, g_ckv, w_uq, w_ukv, w_mem_kv, g_out_a, g_out_b,
           g_out_m, w_out, g_post, b_post):
    batch = x.shape[0]
    assert x.shape == (batch, SEQ, D_MODEL) and w_in.shape[0] == DEPTH == 1
    row = lambda v: v.reshape(1, -1).astype(F32)

    splits = [int(i) for i in np.cumsum(IN_SPLITS)[:-1]]
    a_q, a_k, a_v, a_g, c_q, c_kv, b_kr, b_g, m_q, m_g = jnp.split(w_in[0].astype(BF16), splits, axis=1)
    qk_cols = _a_qk_columns()
    w_qkv = jnp.concatenate([a_q[:, qk_cols], a_k[:, qk_cols], a_v], axis=1).astype(BF16)
    rope_only = lambda lane: lane[1] if lane is not None and lane[0] == "rope" else None
    nope_only = lambda lane: lane[1] if lane is not None and lane[0] == "nope" else None
    nope_then_rope = lambda lane: None if lane is None else (lane[1] + (MLA_NOPE_DIM if lane[0] == "rope" else 0))
    kr_slot = _slot_gather(b_kr, rope_only)
    w_b = jnp.concatenate([a_g, b_g, m_g, m_q, c_q, c_kv, kr_slot], axis=1).astype(BF16)
    qk_dim = MLA_NOPE_DIM + MLA_ROPE_DIM
    w_uq_p = _slot_gather(w_uq[0].reshape(MLA_Q_RANK, MLA_HEADS, qk_dim), nope_then_rope
                          ).reshape(MLA_Q_RANK, _MLA_QK_W).astype(BF16)
    ukv = w_ukv[0].reshape(MLA_KV_RANK, MLA_HEADS, MLA_NOPE_DIM + MLA_V_DIM)
    w_uk_p = _slot_gather(ukv[:, :, :MLA_NOPE_DIM], nope_only)
    w_ukv_p = jnp.concatenate([w_uk_p.reshape(MLA_KV_RANK, _MLA_QK_W),
                               ukv[:, :, MLA_NOPE_DIM:].reshape(MLA_KV_RANK, MLA_WIDTH)], axis=1).astype(BF16)

    pos_view = positions.astype(F32).reshape(batch, N_ROW_TILES, N_CHUNKS, ROW_CHUNK)
    freq_a = _rope_freq_rows(A_ROT_DIM)
    freq_b = _rope_freq_rows(MLA_ROPE_DIM)
    g_emb_r, b_emb_r = row(g_emb), row(b_emb)

    outs_a = _proj_a(x, pos_view, freq_a, g_emb_r, b_emb_r, w_qkv)
    y_a = _dilated_attention(outs_a[0:3], outs_a[3:6], outs_a[6:9])

    gates, mq, qb, kb, vb = _proj_b(x, pos_view, freq_b, g_emb_r, b_emb_r, w_b, row(g_cq[0]),
                                    row(g_ckv[0]), w_uq_p, w_ukv_p)
    y_b = _latent_attention(qb, kb, vb)

    mkv = _mem_kv(mem, w_mem_kv[0].astype(BF16))
    return _output_stage(x, y_a, y_b, gates, mq, mkv, w_out[0].astype(BF16), row(g_out_a[0]),
                         row(g_out_b[0]), row(g_out_m[0]), g_emb_r, b_emb_r, row(g_post[0]), row(b_post[0]))
```

```python
import math

import jax
import jax.numpy as jnp
import numpy as np
from jax import lax
from jax.experimental import pallas as pl
from jax.experimental.pallas import tpu as pltpu

D_MODEL = 1024
SEQ = 2048
A_HEADS = 16
A_HEAD_DIM = 64
A_WIDTH = A_HEADS * A_HEAD_DIM
A_ROT_DIM = A_HEAD_DIM // 4
A_SIDE = 64
DILATIONS = (1, 4, 16)
MLA_HEADS = 8
MLA_Q_RANK = 256
MLA_KV_RANK = 128
MLA_NOPE_DIM = 64
MLA_ROPE_DIM = 32
MLA_V_DIM = 64
MLA_WIDTH = MLA_HEADS * MLA_V_DIM
N_MEM = 256
MEM_HEADS = 4
MEM_HEAD_DIM = 128
MEM_WIDTH = MEM_HEADS * MEM_HEAD_DIM
D_MIX = A_WIDTH + MLA_WIDTH + MEM_WIDTH
ROPE_THETA = 500000.0
NORM_EPS = 1e-5
NEG_INF = -1e30
DEPTH = 1
DEEPNORM_ALPHA = (2 * DEPTH) ** 0.25
IN_SPLITS = (A_WIDTH, A_WIDTH, A_WIDTH, A_WIDTH, MLA_Q_RANK, MLA_KV_RANK, MLA_ROPE_DIM, MLA_WIDTH,
             MEM_WIDTH, MEM_WIDTH)
LOG2_E = math.log2(math.e)

LANES = 128
VMEM_BYTES_V7X = 64 * 1024 * 1024
VMEM_RESERVED_BYTES = 4 << 20
VMEM_MIN_REQUEST_BYTES = 32 << 20
VMEM_SPILL_ALLOWANCE_BYTES = 8 << 20

ROW_TILE = 512
ROW_CHUNK = 128
N_CHUNKS = ROW_TILE // ROW_CHUNK
N_HALVES = 1
ROW_HALF = ROW_TILE // N_HALVES
CHUNKS_PER_HALF = N_CHUNKS // N_HALVES
N_ROW_TILES = SEQ // ROW_TILE
A_PAIRS = A_WIDTH // LANES
A_PAIRS_PER_STEP = 2
A_Q_TILE = 128
A_K_WIN = 2 * A_Q_TILE
C16_PITCH = SEQ // 16 + 8
MLA_SLOT = LANES
MLA_Q_TILE = 256
MLA_Q_TILES_PER_STEP = 4
MLA_PAIRS_PER_STEP = 4

F32 = jnp.float32
BF16 = jnp.bfloat16


def _vmem_limit(nbytes):
    return int(min(VMEM_BYTES_V7X - VMEM_RESERVED_BYTES,
                   max(VMEM_MIN_REQUEST_BYTES, nbytes + VMEM_SPILL_ALLOWANCE_BYTES)))


def _layer_norm_rows(x, g, b):
    mu = jnp.mean(x, axis=-1, keepdims=True)
    xc = x - mu
    var = jnp.mean(xc * xc, axis=-1, keepdims=True)
    return xc * lax.rsqrt(var + NORM_EPS) * g + b


def _rms_rows(x, g):
    ms = jnp.mean(x * x, axis=-1, keepdims=True)
    return x * lax.rsqrt(ms + NORM_EPS) * g


def _chunk(c):
    return slice(c * ROW_CHUNK, (c + 1) * ROW_CHUNK)


def _normed_chunks_to_scratch(x_ref, g_ref, b_ref, h_scr, chunks):
    for c in chunks:
        h_scr[_chunk(c), :] = _layer_norm_rows(x_ref[_chunk(c), :], g_ref[...], b_ref[...]).astype(BF16)


ROPE_FIRST_LANES = (0, 16)
ROPE_PARTNER_SHIFT = LANES // 2


def _rope_tables(pos_ref, freq_ref, spread, chunks, cos_scr, sin_scr):
    n_freq = freq_ref.shape[0]
    lane = lax.broadcasted_iota(jnp.int32, (ROW_CHUNK, LANES), 1)
    first = (lane >= ROPE_FIRST_LANES[0]) & (lane < ROPE_FIRST_LANES[1])
    second = (lane >= ROPE_FIRST_LANES[0] + ROPE_PARTNER_SHIFT) & (lane < ROPE_FIRST_LANES[1] + ROPE_PARTNER_SHIFT)
    pad = jnp.zeros((ROW_CHUNK - n_freq, LANES), F32)
    for c in chunks:
        ang = freq_ref[...] * pos_ref[c:c + 1, :]
        cos = spread(jnp.concatenate([jnp.cos(ang), pad], axis=0).T)
        sin = spread(jnp.concatenate([jnp.sin(ang), pad], axis=0).T)
        cos_scr[_chunk(c), :] = jnp.where(first | second, cos, 1.0)
        sin_scr[_chunk(c), :] = jnp.where(first, -sin, jnp.where(second, sin, 0.0))


def _spread_a(t):
    t = t + pltpu.roll(t, A_ROT_DIM // 2, 1)
    return t + pltpu.roll(t, ROPE_PARTNER_SHIFT, 1)


def _spread_b(t):
    return t + pltpu.roll(t, ROPE_PARTNER_SHIFT, 1)


def _rope_block(x, rows, cos_scr, sin_scr):
    return x * cos_scr[rows, :] + pltpu.roll(x, ROPE_PARTNER_SHIFT, 1) * sin_scr[rows, :]


def _proj_a_kernel(x_ref, pos_ref, freq_ref, g_ref, b_ref, w_ref,
                   qn_ref, kn_ref, vn_ref, q4_ref, k4_ref, v4_ref, q16_ref, k16_ref, v16_ref,
                   h_scr, acc_scr, st4_scr, cos_scr, sin_scr):
    outs = ((qn_ref, q4_ref, q16_ref), (kn_ref, k4_ref, k16_ref), (vn_ref, v4_ref, v16_ref))
    q_scale = (A_HEAD_DIM ** -0.5) * LOG2_E

    def prepare(hf):
        chunks = range(hf * CHUNKS_PER_HALF, (hf + 1) * CHUNKS_PER_HALF)
        _normed_chunks_to_scratch(x_ref, g_ref, b_ref, h_scr, chunks)
        _rope_tables(pos_ref, freq_ref, _spread_a, chunks, cos_scr, sin_scr)

    def matmul(hf, sec):
        if sec == 1 and hf + 1 < N_HALVES:
            prepare(hf + 1)
        return jnp.dot(h_scr[hf * ROW_HALF:(hf + 1) * ROW_HALF, :],
                       w_ref[:, sec * A_WIDTH:(sec + 1) * A_WIDTH], preferred_element_type=F32)

    def epilogue(hf, sec, res):
        nat_ref, r4_ref, r16_ref = outs[sec]
        acc = acc_scr.at[hf * len(outs) + sec]
        st4 = st4_scr.at[hf * len(outs) + sec]
        for cg in range(A_PAIRS):
            cols = slice(cg * LANES, (cg + 1) * LANES)
            for lc in range(CHUNKS_PER_HALF):
                rows = _chunk(hf * CHUNKS_PER_HALF + lc)
                blk = res[lc * ROW_CHUNK:(lc + 1) * ROW_CHUNK, cols]
                if sec < 2:
                    blk = _rope_block(blk, rows, cos_scr, sin_scr)
                    if sec == 0:
                        blk = blk * q_scale
                acc[cg, lc * ROW_CHUNK:(lc + 1) * ROW_CHUNK, :] = blk
                nat_ref[cg, rows, :] = blk.astype(BF16)
            n4, n16 = ROW_HALF // 4, ROW_HALF // 16
            for r in range(4):
                rows4 = acc[cg, pl.ds(r, n4, stride=4), :]
                st4[cg, r] = rows4
                r4_ref[cg, r, hf * n4:(hf + 1) * n4, :] = rows4.astype(BF16)
            for r in range(4):
                for v in range(4):
                    r16_ref[cg, r + 4 * v, hf * n16:(hf + 1) * n16, :] = (
                        st4[cg, r, pl.ds(v, n16, stride=4), :].astype(BF16))

    prepare(0)
    items =[(hf, sec) for hf in range(N_HALVES) for sec in range(len(outs))]
    pending = None
    for hf, sec in items:
        res = matmul(hf, sec)
        if pending is not None:
            epilogue(*pending)
        pending = (hf, sec, res)
    epilogue(*pending)


def _proj_a(x, pos_view, freq_a, g_emb, b_emb, w_qkv):
    batch = x.shape[0]
    nat = jax.ShapeDtypeStruct((batch, A_PAIRS, SEQ, LANES), BF16)
    r4 = jax.ShapeDtypeStruct((batch, A_PAIRS, 4, SEQ // 4, LANES), BF16)
    r16 = jax.ShapeDtypeStruct((batch, A_PAIRS, 16, SEQ // 16, LANES), BF16)
    nat_spec = pl.BlockSpec((None, A_PAIRS, ROW_TILE, LANES), lambda b, i: (b, 0, i, 0))
    r4_spec = pl.BlockSpec((None, A_PAIRS, 4, ROW_TILE // 4, LANES), lambda b, i: (b, 0, 0, i, 0))
    r16_spec = pl.BlockSpec((None, A_PAIRS, 16, ROW_TILE // 16, LANES), lambda b, i: (b, 0, 0, i, 0))
    tile_out = ROW_TILE * A_WIDTH * 2
    est = (2 * ROW_TILE * D_MODEL * 4 + 2 * w_qkv.size * 2 + 2 * 9 * tile_out
           + ROW_TILE * D_MODEL * 2 + 2 * 3 * ROW_TILE * A_WIDTH * 4 + 3 * ROW_TILE * LANES * 4
           + 2 * ROW_HALF * A_WIDTH * 4)
    return pl.pallas_call(
        _proj_a_kernel,
        grid=(batch, N_ROW_TILES),
        in_specs=[
            pl.BlockSpec((None, ROW_TILE, D_MODEL), lambda b, i: (b, i, 0)),
            pl.BlockSpec((None, None, N_CHUNKS, ROW_CHUNK), lambda b, i: (b, i, 0, 0)),
            pl.BlockSpec(freq_a.shape, lambda b, i: (0, 0)),
            pl.BlockSpec((1, D_MODEL), lambda b, i: (0, 0)),
            pl.BlockSpec((1, D_MODEL), lambda b, i: (0, 0)),
            pl.BlockSpec(w_qkv.shape, lambda b, i: (0, 0)),
        ],
        out_specs=[nat_spec] * 3 + [r4_spec] * 3 + [r16_spec] * 3,
        out_shape=[nat] * 3 + [r4] * 3 + [r16] * 3,
        scratch_shapes=[
            pltpu.VMEM((ROW_TILE, D_MODEL), BF16),
            pltpu.VMEM((N_HALVES * 3, A_PAIRS, ROW_HALF, LANES), F32),
            pltpu.VMEM((N_HALVES * 3, A_PAIRS, 4, ROW_HALF // 4, LANES), F32),
            pltpu.VMEM((ROW_TILE, LANES), F32),
            pltpu.VMEM((ROW_TILE, LANES), F32),
        ],
        compiler_params=pltpu.CompilerParams(
            dimension_semantics=("arbitrary", "arbitrary"), vmem_limit_bytes=_vmem_limit(est)),
        name="proj_a",
    )(x, pos_view, freq_a, g_emb, b_emb, w_qkv)


_GATE_W = D_MIX
_OFF_MQ = _GATE_W
_OFF_CQ = _OFF_MQ + MEM_WIDTH
_OFF_CKV = _OFF_CQ + MLA_Q_RANK
_OFF_KR = _OFF_CKV + MLA_KV_RANK
_MLA_QK_W = MLA_HEADS * MLA_SLOT


def _proj_b_kernel(x_ref, pos_ref, freq_ref, g_ref, b_ref, w_ref, gcq_ref, gckv_ref, wuq_ref,
                   wukv_ref, gate_ref, mq_ref, qb_ref, kb_ref, vb_ref,
                   h_scr, cos_scr, sin_scr):
    mla_scale = ((MLA_NOPE_DIM + MLA_ROPE_DIM) ** -0.5) * LOG2_E
    local = lambda lc: slice(lc * ROW_CHUNK, (lc + 1) * ROW_CHUNK)
    half_rows = lambda hf: slice(hf * ROW_HALF, (hf + 1) * ROW_HALF)

    def prepare(hf):
        chunks = range(hf * CHUNKS_PER_HALF, (hf + 1) * CHUNKS_PER_HALF)
        _normed_chunks_to_scratch(x_ref, g_ref, b_ref, h_scr, chunks)
        _rope_tables(pos_ref, freq_ref, _spread_b, chunks, cos_scr, sin_scr)

    def h_dot(hf, first, width):
        return jnp.dot(h_scr[half_rows(hf), :], w_ref[:, first:first + width], preferred_element_type=F32)

    def rope_slots(hf, slots, out_ref, scale=None, extra=None):
        for lc in range(CHUNKS_PER_HALF):
            rows = _chunk(hf * CHUNKS_PER_HALF + lc)
            for hd in range(MLA_HEADS):
                cols = slice(hd * MLA_SLOT, (hd + 1) * MLA_SLOT)
                blk = slots[local(lc), cols]
                if extra is not None:
                    blk = blk + extra[local(lc), :]
                blk = _rope_block(blk, rows, cos_scr, sin_scr)
                out_ref[rows, cols] = (blk if scale is None else blk * scale).astype(BF16)

    def stages(hf):
        state = {}

        def down():
            lat = h_dot(hf, _OFF_CQ, MLA_Q_RANK + MLA_KV_RANK + MLA_SLOT)
            state["cqn"] = _rms_rows(lat[:, :MLA_Q_RANK], gcq_ref[...]).astype(BF16)
            state["ckvn"] = _rms_rows(lat[:, MLA_Q_RANK:MLA_Q_RANK + MLA_KV_RANK], gckv_ref[...]).astype(BF16)
            state["kr"] = lat[:, MLA_Q_RANK + MLA_KV_RANK:]

        def queries():
            return jnp.dot(state["cqn"], wuq_ref[...], preferred_element_type=F32)

        def keys_values():
            k_all = jnp.dot(state["ckvn"], wukv_ref[:, :_MLA_QK_W], preferred_element_type=F32)
            return k_all, jnp.dot(state["ckvn"], wukv_ref[:, _MLA_QK_W:], preferred_element_type=F32)

        def store_keys_values(res):
            k_all, vb = res
            vb_ref[half_rows(hf), :] = vb.astype(BF16)
            rope_slots(hf, k_all, kb_ref, extra=state["kr"])

        def gate_block(n):
            def produce():
                if n == 0 and hf + 1 < N_HALVES:
                    prepare(hf + 1)
                return h_dot(hf, n * A_WIDTH, A_WIDTH)

            def consume(g):
                gate_ref[half_rows(hf), n * A_WIDTH:(n + 1) * A_WIDTH] = (g / (1.0 + jnp.exp(-g))).astype(BF16)

            return produce, consume

        def store_mq(mq):
            mq_ref[half_rows(hf), :] = (mq * (MEM_HEAD_DIM ** -0.5)).astype(BF16)

        return [
            (down, lambda _: None),
            (queries, lambda q_all: rope_slots(hf, q_all, qb_ref, scale=mla_scale)),
            (keys_values, store_keys_values),
            *[gate_block(n) for n in range(_GATE_W // A_WIDTH)],
            (lambda: h_dot(hf, _OFF_MQ, MEM_WIDTH), store_mq),
        ]

    prepare(0)
    pending = None
    for produce, consume in [st for hf in range(N_HALVES) for st in stages(hf)]:
        res = produce()
        if pending is not None:
            pending[0](pending[1])
        pending = (consume, res)
    pending[0](pending[1])


def _proj_b(x, pos_view, freq_b, g_emb, b_emb, w_b, g_cq, g_ckv, w_uq_p, w_ukv_p):
    batch = x.shape[0]

    def rows_spec(width):
        return pl.BlockSpec((None, ROW_TILE, width), lambda b, i: (b, i, 0))

    def full_spec(arr):
        return pl.BlockSpec(arr.shape, lambda b, i: (0,) * arr.ndim)

    widths = (_GATE_W, MEM_WIDTH, _MLA_QK_W, _MLA_QK_W, MLA_WIDTH)
    est = (2 * ROW_TILE * D_MODEL * 4 + 2 * (w_b.size + w_uq_p.size + w_ukv_p.size) * 2
           + 2 * ROW_TILE * sum(widths) * 2 + ROW_TILE * D_MODEL * 2 + ROW_TILE * A_WIDTH * 4
           + 3 * ROW_TILE * LANES * 4 + 2 * ROW_TILE * A_WIDTH * 4)
    return pl.pallas_call(
        _proj_b_kernel,
        grid=(batch, N_ROW_TILES),
        in_specs=[
            rows_spec(D_MODEL),
            pl.BlockSpec((None, None, N_CHUNKS, ROW_CHUNK), lambda b, i: (b, i, 0, 0)),
            full_spec(freq_b),
            full_spec(g_emb), full_spec(b_emb), full_spec(w_b), full_spec(g_cq), full_spec(g_ckv),
            full_spec(w_uq_p), full_spec(w_ukv_p),
        ],
        out_specs=[rows_spec(w) for w in widths],
        out_shape=[jax.ShapeDtypeStruct((batch, SEQ, w), BF16) for w in widths],
        scratch_shapes=[
            pltpu.VMEM((ROW_TILE, D_MODEL), BF16),
            pltpu.VMEM((ROW_TILE, LANES), F32),
            pltpu.VMEM((ROW_TILE, LANES), F32),
        ],
        compiler_params=pltpu.CompilerParams(
            dimension_semantics=("arbitrary", "arbitrary"), vmem_limit_bytes=_vmem_limit(est)),
        name="proj_b",
    )(x, pos_view, freq_b, g_emb, b_emb, w_b, g_cq, g_ckv, w_uq_p, w_ukv_p)


def _first_head_lanes(n_rows):
    return lax.broadcasted_iota(jnp.int32, (n_rows, LANES), 1) < A_HEAD_DIM


def _a_qk_lane_order():
    half = A_ROT_DIM // 2
    rest = A_HEAD_DIM - A_ROT_DIM
    order = ([(0, d) for d in range(half)] + [(1, d) for d in range(half)]
             + [(0, A_ROT_DIM + d) for d in range(rest)]
             + [(0, half + d) for d in range(half)] + [(1, half + d) for d in range(half)]
             + [(1, A_ROT_DIM + d) for d in range(rest)])
    assert len(order) == LANES and len(set(order)) == LANES
    return order


def _first_head_qk_lanes(n_rows):
    lane = lax.broadcasted_iota(jnp.int32, (n_rows, LANES), 1)
    half = A_ROT_DIM // 2
    return (lane < half) | ((lane >= A_ROT_DIM) & (lane < A_ROT_DIM + A_HEAD_DIM - half))


def _band_scores(q, k, bias):
    n_q = q.shape[0]
    first_head = _first_head_lanes(n_q)
    first_head_qk = _first_head_qk_lanes(n_q)
    zero = jnp.zeros_like(q)
    q2 = jnp.concatenate([jnp.where(first_head_qk, q, zero), jnp.where(first_head_qk, zero, q)], axis=0)
    s = lax.dot_general(q2, k, (((1,), (1,)), ((), ())), preferred_element_type=F32) + bias
    m = jnp.max(s, axis=-1, keepdims=True)
    p = jnp.exp2((s - m).astype(BF16))
    return p, jnp.where(first_head, m[:n_q], m[n_q:])


def _band_values(p, v_ones):
    n_q = p.shape[0] // 2
    first_head = _first_head_lanes(n_q)
    o = jnp.dot(p, v_ones, preferred_element_type=F32)
    acc = jnp.where(first_head, o[:n_q, :LANES], o[n_q:, :LANES])
    l_b = jnp.where(first_head, o[:n_q, LANES:], o[n_q:, LANES:])
    return acc, l_b


def _dilated_kernel(qn_ref, kn_ref, vn_ref, q4_ref, k4_ref, v4_ref, q16_ref, k16_ref, v16_ref,
                    o_ref, von_scr, vo4_scr, vo16_scr, bias_scr, bias16_scr, acc_scr, mx_scr, sm_scr, c16_scr):
    def window_bias(n_keys, first_key):
        row = lax.broadcasted_iota(jnp.int32, (2 * A_Q_TILE, n_keys), 0)
        col = lax.broadcasted_iota(jnp.int32, (2 * A_Q_TILE, n_keys), 1)
        q_idx = jnp.where(row >= A_Q_TILE, row - A_Q_TILE, row)
        off = col - (q_idx + first_key)
        return jnp.where(jnp.abs(off) <= A_SIDE, 0.0, NEG_INF).astype(F32)

    @pl.when((pl.program_id(0) == 0) & (pl.program_id(1) == 0))
    def _():
        for variant in range(3):
            bias_scr[variant] = window_bias(A_K_WIN, variant * A_SIDE)
        bias16_scr[...] = window_bias(A_Q_TILE, 0)
        von_scr[:, :, LANES:] = jnp.ones((A_PAIRS_PER_STEP, SEQ, LANES), BF16)
        vo4_scr[:, :, :, LANES:] = jnp.ones((A_PAIRS_PER_STEP, 4, SEQ // 4, LANES), BF16)
        vo16_scr[:, :, :, LANES:] = jnp.ones((A_PAIRS_PER_STEP, 16, SEQ // 16, LANES), BF16)

    von_scr[:, :, :LANES] = vn_ref[...]
    vo4_scr[:, :, :, :LANES] = v4_ref[...]
    vo16_scr[:, :, :, :LANES] = v16_ref[...]

    len4 = SEQ // 4
    tiles4 = len4 // A_Q_TILE
    n_tiles = SEQ // A_Q_TILE

    def window(tile, n_seq_tiles, seq_len):
        qs = tile * A_Q_TILE
        ks = min(max(qs - A_SIDE, 0), seq_len - A_K_WIN)
        variant = 0 if tile == 0 else (2 if tile == n_seq_tiles - 1 else 1)
        return qs, ks, variant

    def score_half(step):
        pp, i = step
        qs, ks, var = window(i, n_tiles, SEQ)
        res = i // tiles4
        qs4, ks4, var4 = window(i % tiles4, tiles4, len4)
        parts = (
            _band_scores(qn_ref[pp, pl.ds(qs, A_Q_TILE), :], kn_ref[pp, pl.ds(ks, A_K_WIN), :], bias_scr[var]),
            _band_scores(q4_ref[pp, res, pl.ds(qs4, A_Q_TILE), :], k4_ref[pp, res, pl.ds(ks4, A_K_WIN), :],
                         bias_scr[var4]),
            _band_scores(q16_ref[pp, i], k16_ref[pp, i], bias16_scr[...]),
        )
        class16 = pl.ds(i * C16_PITCH, A_Q_TILE)
        dests = ((acc_scr.at[pp, 0], mx_scr.at[pp, 0], sm_scr.at[pp, 0], pl.ds(qs, A_Q_TILE)),
                 (acc_scr.at[pp, 1], mx_scr.at[pp, 1], sm_scr.at[pp, 1], pl.ds(res + 4 * qs4, A_Q_TILE, stride=4)),
                 (c16_scr.at[pp, 0], c16_scr.at[pp, 1], c16_scr.at[pp, 2], class16))
        v_wins = (von_scr.at[pp, pl.ds(ks, A_K_WIN), :], vo4_scr.at[pp, res, pl.ds(ks4, A_K_WIN), :],
                  vo16_scr.at[pp, i])
        for (_, mx_at, _, dst), (_, m_b) in zip(dests, parts):
            mx_at[dst, :] = m_b
        return [(p, dest, v_win) for (p, _), dest, v_win in zip(parts, dests, v_wins)]

    def value_half(pending):
        for p, (acc_at, _, sm_at, dst), v_win in pending:
            acc, l_b = _band_values(p, v_win[...])
            acc_at[dst, :] = acc
            sm_at[dst, :] = l_b

    steps = [(pp, i) for pp in range(A_PAIRS_PER_STEP) for i in range(n_tiles)]
    pending = score_half(steps[0])
    for step in steps[1:]:
        ahead = score_half(step)
        value_half(pending)
        pending = ahead
    value_half(pending)

    def class16_rows(scr, c):
        groups = []
        for g in range(ROW_CHUNK // 8):
            first_class = 8 * (g % 2)
            t = (ROW_CHUNK // 16) * c + g // 2
            groups.append(scr[pl.ds(first_class * C16_PITCH + t, 8, stride=C16_PITCH), :])
        return jnp.concatenate(groups, axis=0)

    def merge_pair(pp):
        def merge(c, carry):
            rows = pl.ds(pl.multiple_of(c * ROW_CHUNK, ROW_CHUNK), ROW_CHUNK)
            parts = [(acc_scr[pp, pat, rows, :], mx_scr[pp, pat, rows, :], sm_scr[pp, pat, rows, :])
                     for pat in range(2)]
            parts.append(tuple(class16_rows(c16_scr.at[pp, k], c) for k in range(3)))
            m_all = jnp.maximum(jnp.maximum(parts[0][1], parts[1][1]), parts[2][1])
            num = jnp.zeros((ROW_CHUNK, LANES), F32)
            den = jnp.zeros((ROW_CHUNK, LANES), F32)
            for acc, m_b, l_b in parts:
                w = jnp.exp2(m_b - m_all)
                num = num + w * acc
                den = den + w * l_b
            o_ref[pp, rows, :] = (num / den).astype(BF16)
            return carry

        lax.fori_loop(0, SEQ // ROW_CHUNK, merge, 0, unroll=4)

    for pp in range(A_PAIRS_PER_STEP):
        merge_pair(pp)


def _dilated_attention(qkv_nat, qkv_4, qkv_16):
    batch = qkv_nat[0].shape[0]
    pps = A_PAIRS_PER_STEP
    nat_spec = pl.BlockSpec((None, pps, SEQ, LANES), lambda b, p: (b, p, 0, 0))
    r4_spec = pl.BlockSpec((None, pps, 4, SEQ // 4, LANES), lambda b, p: (b, p, 0, 0, 0))
    r16_spec = pl.BlockSpec((None, pps, 16, SEQ // 16, LANES), lambda b, p: (b, p, 0, 0, 0))
    blk = SEQ * LANES
    est = pps * (2 * 9 * blk * 2 + 2 * blk * 2 + 3 * blk * 2 * 2 + 9 * blk * 4) + 4 * 2 * A_Q_TILE * A_K_WIN * 4
    return pl.pallas_call(
        _dilated_kernel,
        grid=(batch, A_PAIRS // pps),
        in_specs=[nat_spec] * 3 + [r4_spec] * 3 + [r16_spec] * 3,
        out_specs=pl.BlockSpec((None, pps, SEQ, LANES), lambda b, p: (b, p, 0, 0)),
        out_shape=jax.ShapeDtypeStruct((batch, A_PAIRS, SEQ, LANES), BF16),
        scratch_shapes=[
            pltpu.VMEM((pps, SEQ, 2 * LANES), BF16),
            pltpu.VMEM((pps, 4, SEQ // 4, 2 * LANES), BF16),
            pltpu.VMEM((pps, 16, SEQ // 16, 2 * LANES), BF16),
            pltpu.VMEM((3, 2 * A_Q_TILE, A_K_WIN), F32),
            pltpu.VMEM((2 * A_Q_TILE, A_Q_TILE), F32),
            pltpu.VMEM((pps, 2, SEQ, LANES), F32),
            pltpu.VMEM((pps, 2, SEQ, LANES), F32),
            pltpu.VMEM((pps, 2, SEQ, LANES), F32),
            pltpu.VMEM((pps, 3, 16 * C16_PITCH, LANES), F32),
        ],
        compiler_params=pltpu.CompilerParams(
            dimension_semantics=("arbitrary", "arbitrary"), vmem_limit_bytes=_vmem_limit(est)),
        name="dilated_attn",
    )(*qkv_nat, *qkv_4, *qkv_16)


def _latent_kernel(q_ref, k_ref, v_ref, o_ref, vo_scr):
    @pl.when((pl.program_id(0) == 0) & (pl.program_id(1) == 0) & (pl.program_id(2) == 0))
    def _():
        vo_scr[:, :, LANES:] = jnp.ones((MLA_PAIRS_PER_STEP, SEQ, LANES), BF16)

    @pl.when(pl.program_id(2) == 0)
    def _():
        for pair in range(MLA_PAIRS_PER_STEP):
            vo_scr[pair, :, :LANES] = v_ref[:, pair * LANES:(pair + 1) * LANES]

    def score_half(rows, head):
        cols = slice(head * MLA_SLOT, (head + 1) * MLA_SLOT)
        s = lax.dot_general(q_ref[rows, cols], k_ref[:, cols], (((1,), (1,)), ((), ())),
                            preferred_element_type=F32)
        m = jnp.max(s, axis=-1, keepdims=True)
        return jnp.exp2((s - m).astype(BF16))

    def value_half(head, p):
        o = jnp.dot(p, vo_scr[head // 2], preferred_element_type=F32)
        return o[:, :LANES] / o[:, LANES:]

    n_heads = 2 * MLA_PAIRS_PER_STEP
    chains = [(slice(t * MLA_Q_TILE, (t + 1) * MLA_Q_TILE), head)
              for t in range(MLA_Q_TILES_PER_STEP) for head in range(n_heads)]
    lane = lax.broadcasted_iota(jnp.int32, (MLA_Q_TILE, LANES), 1)
    p_next = score_half(*chains[0])
    for n, (rows, head) in enumerate(chains):
        p = p_next
        if n + 1 < len(chains):
            p_next = score_half(*chains[n + 1])
        out = value_half(head, p)
        if head % 2 == 0:
            first_of_pair = out
        else:
            pair = head // 2
            o_ref[rows, pair * LANES:(pair + 1) * LANES] = (
                jnp.where(lane < MLA_V_DIM, first_of_pair, out).astype(BF16))


def _latent_attention(qb, kb, vb):
    batch = qb.shape[0]
    pairs = MLA_HEADS // 2 // MLA_PAIRS_PER_STEP
    qk_w = MLA_PAIRS_PER_STEP * 2 * MLA_SLOT
    v_w = MLA_PAIRS_PER_STEP * LANES
    step_rows = MLA_Q_TILES_PER_STEP * MLA_Q_TILE
    est = (2 * step_rows * qk_w * 2 + 2 * SEQ * qk_w * 2 + 2 * SEQ * v_w * 2 + SEQ * 2 * v_w * 2
           + 2 * step_rows * v_w * 2 + 4 * MLA_PAIRS_PER_STEP * MLA_Q_TILE * SEQ * 4)
    return pl.pallas_call(
        _latent_kernel,
        grid=(batch, pairs, SEQ // step_rows),
        in_specs=[
            pl.BlockSpec((None, step_rows, qk_w), lambda b, p, i: (b, i, p)),
            pl.BlockSpec((None, SEQ, qk_w), lambda b, p, i: (b, 0, p)),
            pl.BlockSpec((None, SEQ, v_w), lambda b, p, i: (b, 0, p)),
        ],
        out_specs=pl.BlockSpec((None, step_rows, v_w), lambda b, p, i: (b, i, p)),
        out_shape=jax.ShapeDtypeStruct((batch, SEQ, MLA_WIDTH), BF16),
        scratch_shapes=[pltpu.VMEM((MLA_PAIRS_PER_STEP, SEQ, 2 * LANES), BF16)],
        compiler_params=pltpu.CompilerParams(
            dimension_semantics=("arbitrary", "arbitrary", "arbitrary"),
            vmem_limit_bytes=_vmem_limit(est)),
        name="latent_attn",
    )(qb, kb, vb)


def _mem_kv_kernel(mem_ref, w_ref, o_ref):
    o_ref[...] = jnp.dot(mem_ref[...].astype(BF16), w_ref[...], preferred_element_type=F32).astype(BF16)


def _mem_kv(mem, w_mem):
    batch = mem.shape[0]
    est = 2 * N_MEM * D_MODEL * 4 + 2 * w_mem.size * 2 + 2 * N_MEM * 2 * MEM_WIDTH * 2 + N_MEM * D_MODEL * 8
    return pl.pallas_call(
        _mem_kv_kernel,
        grid=(batch,),
        in_specs=[pl.BlockSpec((None, N_MEM, D_MODEL), lambda b: (b, 0, 0)),
                  pl.BlockSpec(w_mem.shape, lambda b: (0, 0))],
        out_specs=pl.BlockSpec((None, N_MEM, 2 * MEM_WIDTH), lambda b: (b, 0, 0)),
        out_shape=jax.ShapeDtypeStruct((batch, N_MEM, 2 * MEM_WIDTH), BF16),
        compiler_params=pltpu.CompilerParams(
            dimension_semantics=("arbitrary",), vmem_limit_bytes=_vmem_limit(est)),
        name="mem_kv",
    )(mem, w_mem)


def _output_kernel(x_ref, ya_ref, yb_ref, gate_ref, mq_ref, mkv_ref, wout_ref,
                   goa_ref, gob_ref, gom_ref, gemb_ref, bemb_ref, gpost_ref, bpost_ref,
                   o_ref, y_scr):
    for hd in range(MEM_HEADS):
        cols = slice(hd * MEM_HEAD_DIM, (hd + 1) * MEM_HEAD_DIM)
        s = lax.dot_general(mq_ref[:, cols], mkv_ref[:, cols], (((1,), (1,)), ((), ())),
                            preferred_element_type=F32)
        m = jnp.max(s, axis=-1, keepdims=True)
        p = jnp.exp(s - m)
        l = jnp.sum(p, axis=-1, keepdims=True)
        o = jnp.dot(p.astype(BF16), mkv_ref[:, MEM_WIDTH + hd * MEM_HEAD_DIM:MEM_WIDTH + (hd + 1) * MEM_HEAD_DIM],
                    preferred_element_type=F32)
        y_scr[:, cols] = o / l

    off_b = A_WIDTH
    off_m = A_WIDTH + MLA_WIDTH
    ya_all = jnp.concatenate([ya_ref[cg] for cg in range(A_PAIRS)], axis=1)
    ya = _rms_rows(ya_all.astype(F32), goa_ref[...]) * gate_ref[:, :off_b].astype(F32)
    yb = _rms_rows(yb_ref[...].astype(F32), gob_ref[...]) * gate_ref[:, off_b:off_m].astype(F32)
    ym = _rms_rows(y_scr[...], gom_ref[...]) * gate_ref[:, off_m:].astype(F32)
    sub = (jnp.dot(ya.astype(BF16), wout_ref[:off_b, :], preferred_element_type=F32)
           + jnp.dot(yb.astype(BF16), wout_ref[off_b:off_m, :], preferred_element_type=F32)
           + jnp.dot(ym.astype(BF16), wout_ref[off_m:, :], preferred_element_type=F32))

    for c in range(N_CHUNKS):
        h = _layer_norm_rows(x_ref[_chunk(c), :], gemb_ref[...], bemb_ref[...])
        z = DEEPNORM_ALPHA * h + sub[c * ROW_CHUNK:(c + 1) * ROW_CHUNK, :]
        o_ref[_chunk(c), :] = _layer_norm_rows(z, gpost_ref[...], bpost_ref[...])


def _output_stage(x, ya, yb, gates, mq, mkv, w_out, g_out_a, g_out_b, g_out_m, g_emb, b_emb,
                  g_post, b_post):
    batch = x.shape[0]

    def rows_spec(width):
        return pl.BlockSpec((None, ROW_TILE, width), lambda b, i: (b, i, 0))

    def full_spec(arr):
        return pl.BlockSpec(arr.shape, lambda b, i: (0,) * arr.ndim)

    est = (4 * ROW_TILE * D_MODEL * 4 + 2 * ROW_TILE * (A_WIDTH + MLA_WIDTH + D_MIX + MEM_WIDTH) * 2
           + 2 * N_MEM * 2 * MEM_WIDTH * 2 + 2 * w_out.size * 2 + ROW_TILE * MEM_WIDTH * 4
           + 4 * ROW_TILE * D_MODEL * 4)
    return pl.pallas_call(
        _output_kernel,
        grid=(batch, N_ROW_TILES),
        in_specs=[
            rows_spec(D_MODEL),
            pl.BlockSpec((None, A_PAIRS, ROW_TILE, LANES), lambda b, i: (b, 0, i, 0)),
            rows_spec(MLA_WIDTH), rows_spec(D_MIX), rows_spec(MEM_WIDTH),
            pl.BlockSpec((None, N_MEM, 2 * MEM_WIDTH), lambda b, i: (b, 0, 0)),
            full_spec(w_out), full_spec(g_out_a), full_spec(g_out_b), full_spec(g_out_m),
            full_spec(g_emb), full_spec(b_emb), full_spec(g_post), full_spec(b_post),
        ],
        out_specs=rows_spec(D_MODEL),
        out_shape=jax.ShapeDtypeStruct(x.shape, F32),
        scratch_shapes=[pltpu.VMEM((ROW_TILE, MEM_WIDTH), F32)],
        compiler_params=pltpu.CompilerParams(
            dimension_semantics=("arbitrary", "arbitrary"), vmem_limit_bytes=_vmem_limit(est)),
        name="output_stage",
    )(x, ya, yb, gates, mq, mkv, w_out, g_out_a, g_out_b, g_out_m, g_emb, b_emb, g_post, b_post)


def _rope_freq_rows(rot_dim):
    inv_freq = (np.float32(ROPE_THETA) ** (-(np.arange(0, rot_dim, 2, dtype=np.float32) / np.float32(rot_dim)))
                ).astype(np.float32)
    return jnp.asarray(np.repeat(inv_freq[:, None], LANES, axis=1))


def _a_qk_columns():
    return np.asarray([(2 * pair + hh) * A_HEAD_DIM + d
                       for pair in range(A_PAIRS) for hh, d in _a_qk_lane_order()])


def _mla_slot_lanes():
    half = MLA_ROPE_DIM // 2
    assert ROPE_FIRST_LANES == (0, half)
    lanes = [None] * MLA_SLOT
    for i in range(half):
        lanes[i] = ("rope", i)
        lanes[ROPE_PARTNER_SHIFT + i] = ("rope", half + i)
    free = [lane for lane in range(MLA_SLOT) if lanes[lane] is None]
    for d in range(MLA_NOPE_DIM):
        lanes[free[d]] = ("nope", d)
    return lanes


def _slot_gather(cols, index_of):
    n = cols.shape[-1]
    idx = np.asarray([n if index_of(lane) is None else index_of(lane) for lane in _mla_slot_lanes()])
    return jnp.pad(cols, [(0, 0)] * (cols.ndim - 1) + [(0, 1)])[..., idx]


def kernel(x, mem, positions, g_emb, b_emb, w_in, g_cq, g_ckv, w_uq, w_ukv, w_mem_kv, g_out_a, g_out_b,
           g_out_m, w_out, g_post, b_post):
    batch = x.shape[0]
    assert x.shape == (batch, SEQ, D_MODEL) and w_in.shape[0] == DEPTH == 1
    row = lambda v: v.reshape(1, -1).astype(F32)

    splits = [int(i) for i in np.cumsum(IN_SPLITS)[:-1]]
    a_q, a_k, a_v, a_g, c_q, c_kv, b_kr, b_g, m_q, m_g = jnp.split(w_in[0].astype(BF16), splits, axis=1)
    qk_cols = _a_qk_columns()
    w_qkv = jnp.concatenate([a_q[:, qk_cols], a_k[:, qk_cols], a_v], axis=1).astype(BF16)
    rope_only = lambda lane: lane[1] if lane is not None and lane[0] == "rope" else None
    nope_only = lambda lane: lane[1] if lane is not None and lane[0] == "nope" else None
    nope_then_rope = lambda lane: None if lane is None else (lane[1] + (MLA_NOPE_DIM if lane[0] == "rope" else 0))
    kr_slot = _slot_gather(b_kr, rope_only)
    w_b = jnp.concatenate([a_g, b_g, m_g, m_q, c_q, c_kv, kr_slot], axis=1).astype(BF16)
    qk_dim = MLA_NOPE_DIM + MLA_ROPE_DIM
    w_uq_p = _slot_gather(w_uq[0].reshape(MLA_Q_RANK, MLA_HEADS, qk_dim), nope_then_rope
                          ).reshape(MLA_Q_RANK, _MLA_QK_W).astype(BF16)
    ukv = w_ukv[0].reshape(MLA_KV_RANK, MLA_HEADS, MLA_NOPE_DIM + MLA_V_DIM)
    w_uk_p = _slot_gather(ukv[:, :, :MLA_NOPE_DIM], nope_only)
    w_ukv_p = jnp.concatenate([w_uk_p.reshape(MLA_KV_RANK, _MLA_QK_W),
                               ukv[:, :, MLA_NOPE_DIM:].reshape(MLA_KV_RANK, MLA_WIDTH)], axis=1).astype(BF16)

    pos_view = positions.astype(F32).reshape(batch, N_ROW_TILES, N_CHUNKS, ROW_CHUNK)
    freq_a = _rope_freq_rows(A_ROT_DIM)
    freq_b = _rope_freq_rows(MLA_ROPE_DIM)
    g_emb_r, b_emb_r = row(g_emb), row(b_emb)

    outs_a = _proj_a(x, pos_view, freq_a, g_emb_r, b_emb_r, w_qkv)
    y_a = _dilated_attention(outs_a[0:3], outs_a[3:6], outs_a[6:9])

    gates, mq, qb, kb, vb = _proj_b(x, pos_view, freq_b, g_emb_r, b_emb_r, w_b, row(g_cq[0]),
                                    row(g_ckv[0]), w_uq_p, w_ukv_p)
    y_b = _latent_attention(qb, kb, vb)

    mkv = _mem_kv(mem, w_mem_kv[0].astype(BF16))
    return _output_stage(x, y_a, y_b, gates, mq, mkv, w_out[0].astype(BF16), row(g_out_a[0]),
                         row(g_out_b[0]), row(g_out_m[0]), g_emb_r, b_emb_r, row(g_post[0]), row(b_post[0]))
```

```python
import math

import jax
import jax.numpy as jnp
import numpy as np
from jax import lax
from jax.experimental import pallas as pl
from jax.experimental.pallas import tpu as pltpu

D_MODEL = 1024
SEQ = 2048
A_HEADS = 16
A_HEAD_DIM = 64
A_WIDTH = A_HEADS * A_HEAD_DIM
A_ROT_DIM = A_HEAD_DIM // 4
A_SIDE = 64
DILATIONS = (1, 4, 16)
MLA_HEADS = 8
MLA_Q_RANK = 256
MLA_KV_RANK = 128
MLA_NOPE_DIM = 64
MLA_ROPE_DIM = 32
MLA_V_DIM = 64
MLA_WIDTH = MLA_HEADS * MLA_V_DIM
N_MEM = 256
MEM_HEADS = 4
MEM_HEAD_DIM = 128
MEM_WIDTH = MEM_HEADS * MEM_HEAD_DIM
D_MIX = A_WIDTH + MLA_WIDTH + MEM_WIDTH
ROPE_THETA = 500000.0
NORM_EPS = 1e-5
NEG_INF = -1e30
DEPTH = 1
DEEPNORM_ALPHA = (2 * DEPTH) ** 0.25
IN_SPLITS = (A_WIDTH, A_WIDTH, A_WIDTH, A_WIDTH, MLA_Q_RANK, MLA_KV_RANK, MLA_ROPE_DIM, MLA_WIDTH,
             MEM_WIDTH, MEM_WIDTH)
LOG2_E = math.log2(math.e)

LANES = 128
VMEM_BYTES_V7X = 64 * 1024 * 1024
VMEM_RESERVED_BYTES = 4 << 20
VMEM_MIN_REQUEST_BYTES = 32 << 20
VMEM_SPILL_ALLOWANCE_BYTES = 8 << 20

ROW_TILE = 512
ROW_CHUNK = 128
N_CHUNKS = ROW_TILE // ROW_CHUNK
N_HALVES = 1
ROW_HALF = ROW_TILE // N_HALVES
CHUNKS_PER_HALF = N_CHUNKS // N_HALVES
N_ROW_TILES = SEQ // ROW_TILE
A_PAIRS = A_WIDTH // LANES
A_PAIRS_PER_STEP = 2
A_Q_TILE = 128
A_K_WIN = 2 * A_Q_TILE
C16_PITCH = SEQ // 16 + 8
MLA_SLOT = LANES
MLA_Q_TILE = 256
MLA_Q_TILES_PER_STEP = 8
MLA_PAIRS_PER_STEP = 4

F32 = jnp.float32
BF16 = jnp.bfloat16


def _vmem_limit(nbytes):
    return int(min(VMEM_BYTES_V7X - VMEM_RESERVED_BYTES,
                   max(VMEM_MIN_REQUEST_BYTES, nbytes + VMEM_SPILL_ALLOWANCE_BYTES)))


def _layer_norm_rows(x, g, b):
    mu = jnp.mean(x, axis=-1, keepdims=True)
    xc = x - mu
    var = jnp.mean(xc * xc, axis=-1, keepdims=True)
    return xc * lax.rsqrt(var + NORM_EPS) * g + b


def _rms_rows(x, g):
    ms = jnp.mean(x * x, axis=-1, keepdims=True)
    return x * lax.rsqrt(ms + NORM_EPS) * g


def _chunk(c):
    return slice(c * ROW_CHUNK, (c + 1) * ROW_CHUNK)


def _normed_chunks_to_scratch(x_ref, g_ref, b_ref, h_scr, chunks):
    for c in chunks:
        h_scr[_chunk(c), :] = _layer_norm_rows(x_ref[_chunk(c), :], g_ref[...], b_ref[...]).astype(BF16)


ROPE_FIRST_LANES = (0, 16)
ROPE_PARTNER_SHIFT = LANES // 2


def _rope_tables(pos_ref, freq_ref, spread, chunks, cos_scr, sin_scr):
    n_freq = freq_ref.shape[0]
    lane = lax.broadcasted_iota(jnp.int32, (ROW_CHUNK, LANES), 1)
    first = (lane >= ROPE_FIRST_LANES[0]) & (lane < ROPE_FIRST_LANES[1])
    second = (lane >= ROPE_FIRST_LANES[0] + ROPE_PARTNER_SHIFT) & (lane < ROPE_FIRST_LANES[1] + ROPE_PARTNER_SHIFT)
    pad = jnp.zeros((ROW_CHUNK - n_freq, LANES), F32)
    for c in chunks:
        ang = freq_ref[...] * pos_ref[c:c + 1, :]
        cos = spread(jnp.concatenate([jnp.cos(ang), pad], axis=0).T)
        sin = spread(jnp.concatenate([jnp.sin(ang), pad], axis=0).T)
        cos_scr[_chunk(c), :] = jnp.where(first | second, cos, 1.0)
        sin_scr[_chunk(c), :] = jnp.where(first, -sin, jnp.where(second, sin, 0.0))


def _spread_a(t):
    t = t + pltpu.roll(t, A_ROT_DIM // 2, 1)
    return t + pltpu.roll(t, ROPE_PARTNER_SHIFT, 1)


def _spread_b(t):
    return t + pltpu.roll(t, ROPE_PARTNER_SHIFT, 1)


def _rope_block(x, rows, cos_scr, sin_scr):
    return x * cos_scr[rows, :] + pltpu.roll(x, ROPE_PARTNER_SHIFT, 1) * sin_scr[rows, :]


def _proj_a_kernel(x_ref, pos_ref, freq_ref, g_ref, b_ref, w_ref,
                   qn_ref, kn_ref, vn_ref, q4_ref, k4_ref, v4_ref, q16_ref, k16_ref, v16_ref,
                   h_scr, acc_scr, st4_scr, cos_scr, sin_scr):
    outs = ((qn_ref, q4_ref, q16_ref), (kn_ref, k4_ref, k16_ref), (vn_ref, v4_ref, v16_ref))
    q_scale = (A_HEAD_DIM ** -0.5) * LOG2_E

    def prepare(hf):
        chunks = range(hf * CHUNKS_PER_HALF, (hf + 1) * CHUNKS_PER_HALF)
        _normed_chunks_to_scratch(x_ref, g_ref, b_ref, h_scr, chunks)
        _rope_tables(pos_ref, freq_ref, _spread_a, chunks, cos_scr, sin_scr)

    def matmul(hf, sec):
        if sec == 1 and hf + 1 < N_HALVES:
            prepare(hf + 1)
        return jnp.dot(h_scr[hf * ROW_HALF:(hf + 1) * ROW_HALF, :],
                       w_ref[:, sec * A_WIDTH:(sec + 1) * A_WIDTH], preferred_element_type=F32)

    def epilogue(hf, sec, res):
        nat_ref, r4_ref, r16_ref = outs[sec]
        acc = acc_scr.at[hf * len(outs) + sec]
        st4 = st4_scr.at[hf * len(outs) + sec]
        for cg in range(A_PAIRS):
            cols = slice(cg * LANES, (cg + 1) * LANES)
            for lc in range(CHUNKS_PER_HALF):
                rows = _chunk(hf * CHUNKS_PER_HALF + lc)
                blk = res[lc * ROW_CHUNK:(lc + 1) * ROW_CHUNK, cols]
                if sec < 2:
                    blk = _rope_block(blk, rows, cos_scr, sin_scr)
                    if sec == 0:
                        blk = blk * q_scale
                acc[cg, lc * ROW_CHUNK:(lc + 1) * ROW_CHUNK, :] = blk
                nat_ref[cg, rows, :] = blk.astype(BF16)
            n4, n16 = ROW_HALF // 4, ROW_HALF // 16
            for r in range(4):
                rows4 = acc[cg, pl.ds(r, n4, stride=4), :]
                st4[cg, r] = rows4
                r4_ref[cg, r, hf * n4:(hf + 1) * n4, :] = rows4.astype(BF16)
            for r in range(4):
                for v in range(4):
                    r16_ref[cg, r + 4 * v, hf * n16:(hf + 1) * n16, :] = (
                        st4[cg, r, pl.ds(v, n16, stride=4), :].astype(BF16))

    prepare(0)
    items =[(hf, sec) for hf in range(N_HALVES) for sec in range(len(outs))]
    pending = None
    for hf, sec in items:
        res = matmul(hf, sec)
        if pending is not None:
            epilogue(*pending)
        pending = (hf, sec, res)
    epilogue(*pending)


def _proj_a(x, pos_view, freq_a, g_emb, b_emb, w_qkv):
    batch = x.shape[0]
    nat = jax.ShapeDtypeStruct((batch, A_PAIRS, SEQ, LANES), BF16)
    r4 = jax.ShapeDtypeStruct((batch, A_PAIRS, 4, SEQ // 4, LANES), BF16)
    r16 = jax.ShapeDtypeStruct((batch, A_PAIRS, 16, SEQ // 16, LANES), BF16)
    nat_spec = pl.BlockSpec((None, A_PAIRS, ROW_TILE, LANES), lambda b, i: (b, 0, i, 0))
    r4_spec = pl.BlockSpec((None, A_PAIRS, 4, ROW_TILE // 4, LANES), lambda b, i: (b, 0, 0, i, 0))
    r16_spec = pl.BlockSpec((None, A_PAIRS, 16, ROW_TILE // 16, LANES), lambda b, i: (b, 0, 0, i, 0))
    tile_out = ROW_TILE * A_WIDTH * 2
    est = (2 * ROW_TILE * D_MODEL * 4 + 2 * w_qkv.size * 2 + 2 * 9 * tile_out
           + ROW_TILE * D_MODEL * 2 + 2 * 3 * ROW_TILE * A_WIDTH * 4 + 3 * ROW_TILE * LANES * 4
           + 2 * ROW_HALF * A_WIDTH * 4)
    return pl.pallas_call(
        _proj_a_kernel,
        grid=(batch, N_ROW_TILES),
        in_specs=[
            pl.BlockSpec((None, ROW_TILE, D_MODEL), lambda b, i: (b, i, 0)),
            pl.BlockSpec((None, None, N_CHUNKS, ROW_CHUNK), lambda b, i: (b, i, 0, 0)),
            pl.BlockSpec(freq_a.shape, lambda b, i: (0, 0)),
            pl.BlockSpec((1, D_MODEL), lambda b, i: (0, 0)),
            pl.BlockSpec((1, D_MODEL), lambda b, i: (0, 0)),
            pl.BlockSpec(w_qkv.shape, lambda b, i: (0, 0)),
        ],
        out_specs=[nat_spec] * 3 + [r4_spec] * 3 + [r16_spec] * 3,
        out_shape=[nat] * 3 + [r4] * 3 + [r16] * 3,
        scratch_shapes=[
            pltpu.VMEM((ROW_TILE, D_MODEL), BF16),
            pltpu.VMEM((N_HALVES * 3, A_PAIRS, ROW_HALF, LANES), F32),
            pltpu.VMEM((N_HALVES * 3, A_PAIRS, 4, ROW_HALF // 4, LANES), F32),
            pltpu.VMEM((ROW_TILE, LANES), F32),
            pltpu.VMEM((ROW_TILE, LANES), F32),
        ],
        compiler_params=pltpu.CompilerParams(
            dimension_semantics=("arbitrary", "arbitrary"), vmem_limit_bytes=_vmem_limit(est)),
        name="proj_a",
    )(x, pos_view, freq_a, g_emb, b_emb, w_qkv)


_GATE_W = D_MIX
_OFF_MQ = _GATE_W
_OFF_CQ = _OFF_MQ + MEM_WIDTH
_OFF_CKV = _OFF_CQ + MLA_Q_RANK
_OFF_KR = _OFF_CKV + MLA_KV_RANK
_MLA_QK_W = MLA_HEADS * MLA_SLOT


def _proj_b_kernel(x_ref, pos_ref, freq_ref, g_ref, b_ref, w_ref, gcq_ref, gckv_ref, wuq_ref,
                   wukv_ref, gate_ref, mq_ref, qb_ref, kb_ref, vb_ref,
                   h_scr, cos_scr, sin_scr):
    mla_scale = ((MLA_NOPE_DIM + MLA_ROPE_DIM) ** -0.5) * LOG2_E
    local = lambda lc: slice(lc * ROW_CHUNK, (lc + 1) * ROW_CHUNK)
    half_rows = lambda hf: slice(hf * ROW_HALF, (hf + 1) * ROW_HALF)

    def prepare(hf):
        chunks = range(hf * CHUNKS_PER_HALF, (hf + 1) * CHUNKS_PER_HALF)
        _normed_chunks_to_scratch(x_ref, g_ref, b_ref, h_scr, chunks)
        _rope_tables(pos_ref, freq_ref, _spread_b, chunks, cos_scr, sin_scr)

    def h_dot(hf, first, width):
        return jnp.dot(h_scr[half_rows(hf), :], w_ref[:, first:first + width], preferred_element_type=F32)

    def rope_slots(hf, slots, out_ref, scale=None, extra=None):
        for lc in range(CHUNKS_PER_HALF):
            rows = _chunk(hf * CHUNKS_PER_HALF + lc)
            for hd in range(MLA_HEADS):
                cols = slice(hd * MLA_SLOT, (hd + 1) * MLA_SLOT)
                blk = slots[local(lc), cols]
                if extra is not None:
                    blk = blk + extra[local(lc), :]
                blk = _rope_block(blk, rows, cos_scr, sin_scr)
                out_ref[rows, cols] = (blk if scale is None else blk * scale).astype(BF16)

    def stages(hf):
        state = {}

        def down():
            lat = h_dot(hf, _OFF_CQ, MLA_Q_RANK + MLA_KV_RANK + MLA_SLOT)
            state["cqn"] = _rms_rows(lat[:, :MLA_Q_RANK], gcq_ref[...]).astype(BF16)
            state["ckvn"] = _rms_rows(lat[:, MLA_Q_RANK:MLA_Q_RANK + MLA_KV_RANK], gckv_ref[...]).astype(BF16)
            state["kr"] = lat[:, MLA_Q_RANK + MLA_KV_RANK:]

        def queries():
            return jnp.dot(state["cqn"], wuq_ref[...], preferred_element_type=F32)

        def keys_values():
            k_all = jnp.dot(state["ckvn"], wukv_ref[:, :_MLA_QK_W], preferred_element_type=F32)
            return k_all, jnp.dot(state["ckvn"], wukv_ref[:, _MLA_QK_W:], preferred_element_type=F32)

        def store_keys_values(res):
            k_all, vb = res
            vb_ref[half_rows(hf), :] = vb.astype(BF16)
            rope_slots(hf, k_all, kb_ref, extra=state["kr"])

        def gate_block(n):
            def produce():
                if n == 0 and hf + 1 < N_HALVES:
                    prepare(hf + 1)
                return h_dot(hf, n * A_WIDTH, A_WIDTH)

            def consume(g):
                gate_ref[half_rows(hf), n * A_WIDTH:(n + 1) * A_WIDTH] = (g / (1.0 + jnp.exp(-g))).astype(BF16)

            return produce, consume

        def store_mq(mq):
            mq_ref[half_rows(hf), :] = (mq * (MEM_HEAD_DIM ** -0.5)).astype(BF16)

        return [
            (down, lambda _: None),
            (queries, lambda q_all: rope_slots(hf, q_all, qb_ref, scale=mla_scale)),
            (keys_values, store_keys_values),
            *[gate_block(n) for n in range(_GATE_W // A_WIDTH)],
            (lambda: h_dot(hf, _OFF_MQ, MEM_WIDTH), store_mq),
        ]

    prepare(0)
    pending = None
    for produce, consume in [st for hf in range(N_HALVES) for st in stages(hf)]:
        res = produce()
        if pending is not None:
            pending[0](pending[1])
        pending = (consume, res)
    pending[0](pending[1])


def _proj_b(x, pos_view, freq_b, g_emb, b_emb, w_b, g_cq, g_ckv, w_uq_p, w_ukv_p):
    batch = x.shape[0]

    def rows_spec(width):
        return pl.BlockSpec((None, ROW_TILE, width), lambda b, i: (b, i, 0))

    def full_spec(arr):
        return pl.BlockSpec(arr.shape, lambda b, i: (0,) * arr.ndim)

    widths = (_GATE_W, MEM_WIDTH, _MLA_QK_W, _MLA_QK_W, MLA_WIDTH)
    est = (2 * ROW_TILE * D_MODEL * 4 + 2 * (w_b.size + w_uq_p.size + w_ukv_p.size) * 2
           + 2 * ROW_TILE * sum(widths) * 2 + ROW_TILE * D_MODEL * 2 + ROW_TILE * A_WIDTH * 4
           + 3 * ROW_TILE * LANES * 4 + 2 * ROW_TILE * A_WIDTH * 4)
    return pl.pallas_call(
        _proj_b_kernel,
        grid=(batch, N_ROW_TILES),
        in_specs=[
            rows_spec(D_MODEL),
            pl.BlockSpec((None, None, N_CHUNKS, ROW_CHUNK), lambda b, i: (b, i, 0, 0)),
            full_spec(freq_b),
            full_spec(g_emb), full_spec(b_emb), full_spec(w_b), full_spec(g_cq), full_spec(g_ckv),
            full_spec(w_uq_p), full_spec(w_ukv_p),
        ],
        out_specs=[rows_spec(w) for w in widths],
        out_shape=[jax.ShapeDtypeStruct((batch, SEQ, w), BF16) for w in widths],
        scratch_shapes=[
            pltpu.VMEM((ROW_TILE, D_MODEL), BF16),
            pltpu.VMEM((ROW_TILE, LANES), F32),
            pltpu.VMEM((ROW_TILE, LANES), F32),
        ],
        compiler_params=pltpu.CompilerParams(
            dimension_semantics=("arbitrary", "arbitrary"), vmem_limit_bytes=_vmem_limit(est)),
        name="proj_b",
    )(x, pos_view, freq_b, g_emb, b_emb, w_b, g_cq, g_ckv, w_uq_p, w_ukv_p)


def _first_head_lanes(n_rows):
    return lax.broadcasted_iota(jnp.int32, (n_rows, LANES), 1) < A_HEAD_DIM


def _a_qk_lane_order():
    half = A_ROT_DIM // 2
    rest = A_HEAD_DIM - A_ROT_DIM
    order = ([(0, d) for d in range(half)] + [(1, d) for d in range(half)]
             + [(0, A_ROT_DIM + d) for d in range(rest)]
             + [(0, half + d) for d in range(half)] + [(1, half + d) for d in range(half)]
             + [(1, A_ROT_DIM + d) for d in range(rest)])
    assert len(order) == LANES and len(set(order)) == LANES
    return order


def _first_head_qk_lanes(n_rows):
    lane = lax.broadcasted_iota(jnp.int32, (n_rows, LANES), 1)
    half = A_ROT_DIM // 2
    return (lane < half) | ((lane >= A_ROT_DIM) & (lane < A_ROT_DIM + A_HEAD_DIM - half))


def _band_scores(q, k, bias):
    n_q = q.shape[0]
    first_head = _first_head_lanes(n_q)
    first_head_qk = _first_head_qk_lanes(n_q)
    zero = jnp.zeros_like(q)
    q2 = jnp.concatenate([jnp.where(first_head_qk, q, zero), jnp.where(first_head_qk, zero, q)], axis=0)
    s = lax.dot_general(q2, k, (((1,), (1,)), ((), ())), preferred_element_type=F32) + bias
    m = jnp.max(s, axis=-1, keepdims=True)
    p = jnp.exp2((s - m).astype(BF16))
    return p, jnp.where(first_head, m[:n_q], m[n_q:])


def _band_values(p, v_ones):
    n_q = p.shape[0] // 2
    first_head = _first_head_lanes(n_q)
    o = jnp.dot(p, v_ones, preferred_element_type=F32)
    acc = jnp.where(first_head, o[:n_q, :LANES], o[n_q:, :LANES])
    l_b = jnp.where(first_head, o[:n_q, LANES:], o[n_q:, LANES:])
    return acc, l_b


def _dilated_kernel(qn_ref, kn_ref, vn_ref, q4_ref, k4_ref, v4_ref, q16_ref, k16_ref, v16_ref,
                    o_ref, von_scr, vo4_scr, vo16_scr, bias_scr, bias16_scr, acc_scr, mx_scr, sm_scr, c16_scr):
    def window_bias(n_keys, first_key):
        row = lax.broadcasted_iota(jnp.int32, (2 * A_Q_TILE, n_keys), 0)
        col = lax.broadcasted_iota(jnp.int32, (2 * A_Q_TILE, n_keys), 1)
        q_idx = jnp.where(row >= A_Q_TILE, row - A_Q_TILE, row)
        off = col - (q_idx + first_key)
        return jnp.where(jnp.abs(off) <= A_SIDE, 0.0, NEG_INF).astype(F32)

    @pl.when((pl.program_id(0) == 0) & (pl.program_id(1) == 0))
    def _():
        for variant in range(3):
            bias_scr[variant] = window_bias(A_K_WIN, variant * A_SIDE)
        bias16_scr[...] = window_bias(A_Q_TILE, 0)
        von_scr[:, :, LANES:] = jnp.ones((A_PAIRS_PER_STEP, SEQ, LANES), BF16)
        vo4_scr[:, :, :, LANES:] = jnp.ones((A_PAIRS_PER_STEP, 4, SEQ // 4, LANES), BF16)
        vo16_scr[:, :, :, LANES:] = jnp.ones((A_PAIRS_PER_STEP, 16, SEQ // 16, LANES), BF16)

    von_scr[:, :, :LANES] = vn_ref[...]
    vo4_scr[:, :, :, :LANES] = v4_ref[...]
    vo16_scr[:, :, :, :LANES] = v16_ref[...]

    len4 = SEQ // 4
    tiles4 = len4 // A_Q_TILE
    n_tiles = SEQ // A_Q_TILE

    def window(tile, n_seq_tiles, seq_len):
        qs = tile * A_Q_TILE
        ks = min(max(qs - A_SIDE, 0), seq_len - A_K_WIN)
        variant = 0 if tile == 0 else (2 if tile == n_seq_tiles - 1 else 1)
        return qs, ks, variant

    def score_half(step):
        pp, i = step
        qs, ks, var = window(i, n_tiles, SEQ)
        res = i // tiles4
        qs4, ks4, var4 = window(i % tiles4, tiles4, len4)
        parts = (
            _band_scores(qn_ref[pp, pl.ds(qs, A_Q_TILE), :], kn_ref[pp, pl.ds(ks, A_K_WIN), :], bias_scr[var]),
            _band_scores(q4_ref[pp, res, pl.ds(qs4, A_Q_TILE), :], k4_ref[pp, res, pl.ds(ks4, A_K_WIN), :],
                         bias_scr[var4]),
            _band_scores(q16_ref[pp, i], k16_ref[pp, i], bias16_scr[...]),
        )
        class16 = pl.ds(i * C16_PITCH, A_Q_TILE)
        dests = ((acc_scr.at[pp, 0], mx_scr.at[pp, 0], sm_scr.at[pp, 0], pl.ds(qs, A_Q_TILE)),
                 (acc_scr.at[pp, 1], mx_scr.at[pp, 1], sm_scr.at[pp, 1], pl.ds(res + 4 * qs4, A_Q_TILE, stride=4)),
                 (c16_scr.at[pp, 0], c16_scr.at[pp, 1], c16_scr.at[pp, 2], class16))
        v_wins = (von_scr.at[pp, pl.ds(ks, A_K_WIN), :], vo4_scr.at[pp, res, pl.ds(ks4, A_K_WIN), :],
                  vo16_scr.at[pp, i])
        for (_, mx_at, _, dst), (_, m_b) in zip(dests, parts):
            mx_at[dst, :] = m_b
        return [(p, dest, v_win) for (p, _), dest, v_win in zip(parts, dests, v_wins)]

    def value_half(pending):
        for p, (acc_at, _, sm_at, dst), v_win in pending:
            acc, l_b = _band_values(p, v_win[...])
            acc_at[dst, :] = acc
            sm_at[dst, :] = l_b

    steps = [(pp, i) for pp in range(A_PAIRS_PER_STEP) for i in range(n_tiles)]
    pending = score_half(steps[0])
    for step in steps[1:]:
        ahead = score_half(step)
        value_half(pending)
        pending = ahead
    value_half(pending)

    def class16_rows(scr, c):
        groups = []
        for g in range(ROW_CHUNK // 8):
            first_class = 8 * (g % 2)
            t = (ROW_CHUNK // 16) * c + g // 2
            groups.append(scr[pl.ds(first_class * C16_PITCH + t, 8, stride=C16_PITCH), :])
        return jnp.concatenate(groups, axis=0)

    def merge_pair(pp):
        def merge(c, carry):
            rows = pl.ds(pl.multiple_of(c * ROW_CHUNK, ROW_CHUNK), ROW_CHUNK)
            parts = [(acc_scr[pp, pat, rows, :], mx_scr[pp, pat, rows, :], sm_scr[pp, pat, rows, :])
                     for pat in range(2)]
            parts.append(tuple(class16_rows(c16_scr.at[pp, k], c) for k in range(3)))
            m_all = jnp.maximum(jnp.maximum(parts[0][1], parts[1][1]), parts[2][1])
            num = jnp.zeros((ROW_CHUNK, LANES), F32)
            den = jnp.zeros((ROW_CHUNK, LANES), F32)
            for acc, m_b, l_b in parts:
                w = jnp.exp2(m_b - m_all)
                num = num + w * acc
                den = den + w * l_b
            o_ref[pp, rows, :] = (num / den).astype(BF16)
            return carry

        lax.fori_loop(0, SEQ // ROW_CHUNK, merge, 0, unroll=4)

    for pp in range(A_PAIRS_PER_STEP):
        merge_pair(pp)


def _dilated_attention(qkv_nat, qkv_4, qkv_16):
    batch = qkv_nat[0].shape[0]
    pps = A_PAIRS_PER_STEP
    nat_spec = pl.BlockSpec((None, pps, SEQ, LANES), lambda b, p: (b, p, 0, 0))
    r4_spec = pl.BlockSpec((None, pps, 4, SEQ // 4, LANES), lambda b, p: (b, p, 0, 0, 0))
    r16_spec = pl.BlockSpec((None, pps, 16, SEQ // 16, LANES), lambda b, p: (b, p, 0, 0, 0))
    blk = SEQ * LANES
    est = pps * (2 * 9 * blk * 2 + 2 * blk * 2 + 3 * blk * 2 * 2 + 9 * blk * 4) + 4 * 2 * A_Q_TILE * A_K_WIN * 4
    return pl.pallas_call(
        _dilated_kernel,
        grid=(batch, A_PAIRS // pps),
        in_specs=[nat_spec] * 3 + [r4_spec] * 3 + [r16_spec] * 3,
        out_specs=pl.BlockSpec((None, pps, SEQ, LANES), lambda b, p: (b, p, 0, 0)),
        out_shape=jax.ShapeDtypeStruct((batch, A_PAIRS, SEQ, LANES), BF16),
        scratch_shapes=[
            pltpu.VMEM((pps, SEQ, 2 * LANES), BF16),
            pltpu.VMEM((pps, 4, SEQ // 4, 2 * LANES), BF16),
            pltpu.VMEM((pps, 16, SEQ // 16, 2 * LANES), BF16),
            pltpu.VMEM((3, 2 * A_Q_TILE, A_K_WIN), F32),
            pltpu.VMEM((2 * A_Q_TILE, A_Q_TILE), F32),
            pltpu.VMEM((pps, 2, SEQ, LANES), F32),
            pltpu.VMEM((pps, 2, SEQ, LANES), F32),
            pltpu.VMEM((pps, 2, SEQ, LANES), F32),
            pltpu.VMEM((pps, 3, 16 * C16_PITCH, LANES), F32),
        ],
        compiler_params=pltpu.CompilerParams(
            dimension_semantics=("arbitrary", "arbitrary"), vmem_limit_bytes=_vmem_limit(est)),
        name="dilated_attn",
    )(*qkv_nat, *qkv_4, *qkv_16)


def _latent_kernel(q_ref, k_ref, v_ref, o_ref, vo_scr):
    @pl.when((pl.program_id(0) == 0) & (pl.program_id(1) == 0) & (pl.program_id(2) == 0))
    def _():
        vo_scr[:, :, LANES:] = jnp.ones((MLA_PAIRS_PER_STEP, SEQ, LANES), BF16)

    @pl.when(pl.program_id(2) == 0)
    def _():
        for pair in range(MLA_PAIRS_PER_STEP):
            vo_scr[pair, :, :LANES] = v_ref[:, pair * LANES:(pair + 1) * LANES]

    def score_half(rows, head):
        cols = slice(head * MLA_SLOT, (head + 1) * MLA_SLOT)
        s = lax.dot_general(q_ref[rows, cols], k_ref[:, cols], (((1,), (1,)), ((), ())),
                            preferred_element_type=F32)
        m = jnp.max(s, axis=-1, keepdims=True)
        return jnp.exp2((s - m).astype(BF16))

    def value_half(head, p):
        o = jnp.dot(p, vo_scr[head // 2], preferred_element_type=F32)
        return o[:, :LANES] / o[:, LANES:]

    n_heads = 2 * MLA_PAIRS_PER_STEP
    chains = [(slice(t * MLA_Q_TILE, (t + 1) * MLA_Q_TILE), head)
              for t in range(MLA_Q_TILES_PER_STEP) for head in range(n_heads)]
    lane = lax.broadcasted_iota(jnp.int32, (MLA_Q_TILE, LANES), 1)
    p_next = score_half(*chains[0])
    for n, (rows, head) in enumerate(chains):
        p = p_next
        if n + 1 < len(chains):
            p_next = score_half(*chains[n + 1])
        out = value_half(head, p)
        if head % 2 == 0:
            first_of_pair = out
        else:
            pair = head // 2
            o_ref[rows, pair * LANES:(pair + 1) * LANES] = (
                jnp.where(lane < MLA_V_DIM, first_of_pair, out).astype(BF16))


def _latent_attention(qb, kb, vb):
    batch = qb.shape[0]
    pairs = MLA_HEADS // 2 // MLA_PAIRS_PER_STEP
    qk_w = MLA_PAIRS_PER_STEP * 2 * MLA_SLOT
    v_w = MLA_PAIRS_PER_STEP * LANES
    step_rows = MLA_Q_TILES_PER_STEP * MLA_Q_TILE
    est = (2 * step_rows * qk_w * 2 + 2 * SEQ * qk_w * 2 + 2 * SEQ * v_w * 2 + SEQ * 2 * v_w * 2
           + 2 * step_rows * v_w * 2 + 4 * MLA_PAIRS_PER_STEP * MLA_Q_TILE * SEQ * 4)
    return pl.pallas_call(
        _latent_kernel,
        grid=(batch, pairs, SEQ // step_rows),
        in_specs=[
            pl.BlockSpec((None, step_rows, qk_w), lambda b, p, i: (b, i, p)),
            pl.BlockSpec((None, SEQ, qk_w), lambda b, p, i: (b, 0, p)),
            pl.BlockSpec((None, SEQ, v_w), lambda b, p, i: (b, 0, p)),
        ],
        out_specs=pl.BlockSpec((None, step_rows, v_w), lambda b, p, i: (b, i, p)),
        out_shape=jax.ShapeDtypeStruct((batch, SEQ, MLA_WIDTH), BF16),
        scratch_shapes=[pltpu.VMEM((MLA_PAIRS_PER_STEP, SEQ, 2 * LANES), BF16)],
        compiler_params=pltpu.CompilerParams(
            dimension_semantics=("arbitrary", "arbitrary", "arbitrary"),
            vmem_limit_bytes=_vmem_limit(est)),
        name="latent_attn",
    )(qb, kb, vb)


def _mem_kv_kernel(mem_ref, w_ref, o_ref):
    o_ref[...] = jnp.dot(mem_ref[...].astype(BF16), w_ref[...], preferred_element_type=F32).astype(BF16)


def _mem_kv(mem, w_mem):
    batch = mem.shape[0]
    est = 2 * N_MEM * D_MODEL * 4 + 2 * w_mem.size * 2 + 2 * N_MEM * 2 * MEM_WIDTH * 2 + N_MEM * D_MODEL * 8
    return pl.pallas_call(
        _mem_kv_kernel,
        grid=(batch,),
        in_specs=[pl.BlockSpec((None, N_MEM, D_MODEL), lambda b: (b, 0, 0)),
                  pl.BlockSpec(w_mem.shape, lambda b: (0, 0))],
        out_specs=pl.BlockSpec((None, N_MEM, 2 * MEM_WIDTH), lambda b: (b, 0, 0)),
        out_shape=jax.ShapeDtypeStruct((batch, N_MEM, 2 * MEM_WIDTH), BF16),
        compiler_params=pltpu.CompilerParams(
            dimension_semantics=("arbitrary",), vmem_limit_bytes=_vmem_limit(est)),
        name="mem_kv",
    )(mem, w_mem)


def _output_kernel(x_ref, ya_ref, yb_ref, gate_ref, mq_ref, mkv_ref, wout_ref,
                   goa_ref, gob_ref, gom_ref, gemb_ref, bemb_ref, gpost_ref, bpost_ref,
                   o_ref, y_scr):
    for hd in range(MEM_HEADS):
        cols = slice(hd * MEM_HEAD_DIM, (hd + 1) * MEM_HEAD_DIM)
        s = lax.dot_general(mq_ref[:, cols], mkv_ref[:, cols], (((1,), (1,)), ((), ())),
                            preferred_element_type=F32)
        m = jnp.max(s, axis=-1, keepdims=True)
        p = jnp.exp(s - m)
        l = jnp.sum(p, axis=-1, keepdims=True)
        o = jnp.dot(p.astype(BF16), mkv_ref[:, MEM_WIDTH + hd * MEM_HEAD_DIM:MEM_WIDTH + (hd + 1) * MEM_HEAD_DIM],
                    preferred_element_type=F32)
        y_scr[:, cols] = o / l

    off_b = A_WIDTH
    off_m = A_WIDTH + MLA_WIDTH
    ya_all = jnp.concatenate([ya_ref[cg] for cg in range(A_PAIRS)], axis=1)
    ya = _rms_rows(ya_all.astype(F32), goa_ref[...]) * gate_ref[:, :off_b].astype(F32)
    yb = _rms_rows(yb_ref[...].astype(F32), gob_ref[...]) * gate_ref[:, off_b:off_m].astype(F32)
    ym = _rms_rows(y_scr[...], gom_ref[...]) * gate_ref[:, off_m:].astype(F32)
    sub = (jnp.dot(ya.astype(BF16), wout_ref[:off_b, :], preferred_element_type=F32)
           + jnp.dot(yb.astype(BF16), wout_ref[off_b:off_m, :], preferred_element_type=F32)
           + jnp.dot(ym.astype(BF16), wout_ref[off_m:, :], preferred_element_type=F32))

    for c in range(N_CHUNKS):
        h = _layer_norm_rows(x_ref[_chunk(c), :], gemb_ref[...], bemb_ref[...])
        z = DEEPNORM_ALPHA * h + sub[c * ROW_CHUNK:(c + 1) * ROW_CHUNK, :]
        o_ref[_chunk(c), :] = _layer_norm_rows(z, gpost_ref[...], bpost_ref[...])


def _output_stage(x, ya, yb, gates, mq, mkv, w_out, g_out_a, g_out_b, g_out_m, g_emb, b_emb,
                  g_post, b_post):
    batch = x.shape[0]

    def rows_spec(width):
        return pl.BlockSpec((None, ROW_TILE, width), lambda b, i: (b, i, 0))

    def full_spec(arr):
        return pl.BlockSpec(arr.shape, lambda b, i: (0,) * arr.ndim)

    est = (4 * ROW_TILE * D_MODEL * 4 + 2 * ROW_TILE * (A_WIDTH + MLA_WIDTH + D_MIX + MEM_WIDTH) * 2
           + 2 * N_MEM * 2 * MEM_WIDTH * 2 + 2 * w_out.size * 2 + ROW_TILE * MEM_WIDTH * 4
           + 4 * ROW_TILE * D_MODEL * 4)
    return pl.pallas_call(
        _output_kernel,
        grid=(batch, N_ROW_TILES),
        in_specs=[
            rows_spec(D_MODEL),
            pl.BlockSpec((None, A_PAIRS, ROW_TILE, LANES), lambda b, i: (b, 0, i, 0)),
            rows_spec(MLA_WIDTH), rows_spec(D_MIX), rows_spec(MEM_WIDTH),
            pl.BlockSpec((None, N_MEM, 2 * MEM_WIDTH), lambda b, i: (b, 0, 0)),
            full_spec(w_out), full_spec(g_out_a), full_spec(g_out_b), full_spec(g_out_m),
            full_spec(g_emb), full_spec(b_emb), full_spec(g_post), full_spec(b_post),
        ],
        out_specs=rows_spec(D_MODEL),
        out_shape=jax.ShapeDtypeStruct(x.shape, F32),
        scratch_shapes=[pltpu.VMEM((ROW_TILE, MEM_WIDTH), F32)],
        compiler_params=pltpu.CompilerParams(
            dimension_semantics=("arbitrary", "arbitrary"), vmem_limit_bytes=_vmem_limit(est)),
        name="output_stage",
    )(x, ya, yb, gates, mq, mkv, w_out, g_out_a, g_out_b, g_out_m, g_emb, b_emb, g_post, b_post)


def _rope_freq_rows(rot_dim):
    inv_freq = (np.float32(ROPE_THETA) ** (-(np.arange(0, rot_dim, 2, dtype=np.float32) / np.float32(rot_dim)))
                ).astype(np.float32)
    return jnp.asarray(np.repeat(inv_freq[:, None], LANES, axis=1))


def _a_qk_columns():
    return np.asarray([(2 * pair + hh) * A_HEAD_DIM + d
                       for pair in range(A_PAIRS) for hh, d in _a_qk_lane_order()])


def _mla_slot_lanes():
    half = MLA_ROPE_DIM // 2
    assert ROPE_FIRST_LANES == (0, half)
    lanes = [None] * MLA_SLOT
    for i in range(half):
        lanes[i] = ("rope", i)
        lanes[ROPE_PARTNER_SHIFT + i] = ("rope", half + i)
    free = [lane for lane in range(MLA_SLOT) if lanes[lane] is None]
    for d in range(MLA_NOPE_DIM):
        lanes[free[d]] = ("nope", d)
    return lanes


def _slot_gather(cols, index_of):
    n = cols.shape[-1]
    idx = np.asarray([n if index_of(lane) is None else index_of(lane) for lane in _mla_slot_lanes()])
    return jnp.pad(cols, [(0, 0)] * (cols.ndim - 1) + [(0, 1)])[..., idx]


def kernel(x, mem, positions, g_emb, b_emb, w_in, g_cq, g_ckv, w_uq, w_ukv, w_mem_kv, g_out_a, g_out_b,
           g_out_m, w_out, g_post, b_post):
    batch = x.shape[0]
    assert x.shape == (batch, SEQ, D_MODEL) and w_in.shape[0] == DEPTH == 1
    row = lambda v: v.reshape(1, -1).astype(F32)

    splits = [int(i) for i in np.cumsum(IN_SPLITS)[:-1]]
    a_q, a_k, a_v, a_g, c_q, c_kv, b_kr, b_g, m_q, m_g = jnp.split(w_in[0].astype(BF16), splits, axis=1)
    qk_cols = _a_qk_columns()
    w_qkv = jnp.concatenate([a_q[:, qk_cols], a_k[:, qk_cols], a_v], axis=1).astype(BF16)
    rope_only = lambda lane: lane[1] if lane is not None and lane[0] == "rope" else None
    nope_only = lambda lane: lane[1] if lane is not None and lane[0] == "nope" else None
    nope_then_rope = lambda lane: None if lane is None else (lane[1] + (MLA_NOPE_DIM if lane[0] == "rope" else 0))
    kr_slot = _slot_gather(b_kr, rope_only)
    w_b = jnp.concatenate([a_g, b_g, m_g, m_q, c_q, c_kv, kr_slot], axis=1).astype(BF16)
    qk_dim = MLA_NOPE_DIM + MLA_ROPE_DIM
    w_uq_p = _slot_gather(w_uq[0].reshape(MLA_Q_RANK, MLA_HEADS, qk_dim), nope_then_rope
                          ).reshape(MLA_Q_RANK, _MLA_QK_W).astype(BF16)
    ukv = w_ukv[0].reshape(MLA_KV_RANK, MLA_HEADS, MLA_NOPE_DIM + MLA_V_DIM)
    w_uk_p = _slot_gather(ukv[:, :, :MLA_NOPE_DIM], nope_only)
    w_ukv_p = jnp.concatenate([w_uk_p.reshape(MLA_KV_RANK, _MLA_QK_W),
                               ukv[:, :, MLA_NOPE_DIM:].reshape(MLA_KV_RANK, MLA_WIDTH)], axis=1).astype(BF16)

    pos_view = positions.astype(F32).reshape(batch, N_ROW_TILES, N_CHUNKS, ROW_CHUNK)
    freq_a = _rope_freq_rows(A_ROT_DIM)
    freq_b = _rope_freq_rows(MLA_ROPE_DIM)
    g_emb_r, b_emb_r = row(g_emb), row(b_emb)

    outs_a = _proj_a(x, pos_view, freq_a, g_emb_r, b_emb_r, w_qkv)
    y_a = _dilated_attention(outs_a[0:3], outs_a[3:6], outs_a[6:9])

    gates, mq, qb, kb, vb = _proj_b(x, pos_view, freq_b, g_emb_r, b_emb_r, w_b, row(g_cq[0]),
                                    row(g_ckv[0]), w_uq_p, w_ukv_p)
    y_b = _latent_attention(qb, kb, vb)

    mkv = _mem_kv(mem, w_mem_kv[0].astype(BF16))
    return _output_stage(x, y_a, y_b, gates, mq, mkv, w_out[0].astype(BF16), row(g_out_a[0]),
                         row(g_out_b[0]), row(g_out_m[0]), g_emb_r, b_emb_r, row(g_post[0]), row(b_post[0]))
```

```python
import math

import jax
import jax.numpy as jnp
import numpy as np
from jax import lax
from jax.experimental import pallas as pl
from jax.experimental.pallas import tpu as pltpu

D_MODEL = 1024
SEQ = 2048
A_HEADS = 16
A_HEAD_DIM = 64
A_WIDTH = A_HEADS * A_HEAD_DIM
A_ROT_DIM = A_HEAD_DIM // 4
A_SIDE = 64
DILATIONS = (1, 4, 16)
MLA_HEADS = 8
MLA_Q_RANK = 256
MLA_KV_RANK = 128
MLA_NOPE_DIM = 64
MLA_ROPE_DIM = 32
MLA_V_DIM = 64
MLA_WIDTH = MLA_HEADS * MLA_V_DIM
N_MEM = 256
MEM_HEADS = 4
MEM_HEAD_DIM = 128
MEM_WIDTH = MEM_HEADS * MEM_HEAD_DIM
D_MIX = A_WIDTH + MLA_WIDTH + MEM_WIDTH
ROPE_THETA = 500000.0
NORM_EPS = 1e-5
NEG_INF = -1e30
DEPTH = 1
DEEPNORM_ALPHA = (2 * DEPTH) ** 0.25
IN_SPLITS = (A_WIDTH, A_WIDTH, A_WIDTH, A_WIDTH, MLA_Q_RANK, MLA_KV_RANK, MLA_ROPE_DIM, MLA_WIDTH,
             MEM_WIDTH, MEM_WIDTH)
LOG2_E = math.log2(math.e)

LANES = 128
VMEM_BYTES_V7X = 64 * 1024 * 1024
VMEM_RESERVED_BYTES = 4 << 20
VMEM_MIN_REQUEST_BYTES = 32 << 20
VMEM_SPILL_ALLOWANCE_BYTES = 8 << 20

ROW_TILE = 512
ROW_CHUNK = 128
N_CHUNKS = ROW_TILE // ROW_CHUNK
N_HALVES = 1
ROW_HALF = ROW_TILE // N_HALVES
CHUNKS_PER_HALF = N_CHUNKS // N_HALVES
N_ROW_TILES = SEQ // ROW_TILE
A_PAIRS = A_WIDTH // LANES
A_PAIRS_PER_STEP = 2
A_Q_TILE = 128
A_K_WIN = 2 * A_Q_TILE
C16_PITCH = SEQ // 16 + 8
MLA_SLOT = LANES
MLA_Q_TILE = 256
MLA_Q_TILES_PER_STEP = 4
MLA_PAIRS_PER_STEP = 4

F32 = jnp.float32
BF16 = jnp.bfloat16


def _vmem_limit(nbytes):
    return int(min(VMEM_BYTES_V7X - VMEM_RESERVED_BYTES,
                   max(VMEM_MIN_REQUEST_BYTES, nbytes + VMEM_SPILL_ALLOWANCE_BYTES)))


def _layer_norm_rows(x, g, b):
    mu = jnp.mean(x, axis=-1, keepdims=True)
    xc = x - mu
    var = jnp.mean(xc * xc, axis=-1, keepdims=True)
    return xc * lax.rsqrt(var + NORM_EPS) * g + b


def _rms_rows(x, g):
    ms = jnp.mean(x * x, axis=-1, keepdims=True)
    return x * lax.rsqrt(ms + NORM_EPS) * g


def _chunk(c):
    return slice(c * ROW_CHUNK, (c + 1) * ROW_CHUNK)


def _normed_chunks_to_scratch(x_ref, g_ref, b_ref, h_scr, chunks):
    for c in chunks:
        h_scr[_chunk(c), :] = _layer_norm_rows(x_ref[_chunk(c), :], g_ref[...], b_ref[...]).astype(BF16)


ROPE_FIRST_LANES = (0, 16)
ROPE_PARTNER_SHIFT = LANES // 2


def _rope_tables(pos_ref, freq_ref, spread, chunks, cos_scr, sin_scr):
    n_freq = freq_ref.shape[0]
    lane = lax.broadcasted_iota(jnp.int32, (ROW_CHUNK, LANES), 1)
    first = (lane >= ROPE_FIRST_LANES[0]) & (lane < ROPE_FIRST_LANES[1])
    second = (lane >= ROPE_FIRST_LANES[0] + ROPE_PARTNER_SHIFT) & (lane < ROPE_FIRST_LANES[1] + ROPE_PARTNER_SHIFT)
    pad = jnp.zeros((ROW_CHUNK - n_freq, LANES), F32)
    for c in chunks:
        ang = freq_ref[...] * pos_ref[c:c + 1, :]
        cos = spread(jnp.concatenate([jnp.cos(ang), pad], axis=0).T)
        sin = spread(jnp.concatenate([jnp.sin(ang), pad], axis=0).T)
        cos_scr[_chunk(c), :] = jnp.where(first | second, cos, 1.0)
        sin_scr[_chunk(c), :] = jnp.where(first, -sin, jnp.where(second, sin, 0.0))


def _spread_a(t):
    t = t + pltpu.roll(t, A_ROT_DIM // 2, 1)
    return t + pltpu.roll(t, ROPE_PARTNER_SHIFT, 1)


def _spread_b(t):
    return t + pltpu.roll(t, ROPE_PARTNER_SHIFT, 1)


def _rope_block(x, rows, cos_scr, sin_scr):
    return x * cos_scr[rows, :] + pltpu.roll(x, ROPE_PARTNER_SHIFT, 1) * sin_scr[rows, :]


def _proj_a_kernel(x_ref, pos_ref, freq_ref, g_ref, b_ref, w_ref,
                   qn_ref, kn_ref, vn_ref, q4_ref, k4_ref, v4_ref, q16_ref, k16_ref, v16_ref,
                   h_scr, acc_scr, st4_scr, cos_scr, sin_scr):
    outs = ((qn_ref, q4_ref, q16_ref), (kn_ref, k4_ref, k16_ref), (vn_ref, v4_ref, v16_ref))
    q_scale = (A_HEAD_DIM ** -0.5) * LOG2_E

    def prepare(hf):
        chunks = range(hf * CHUNKS_PER_HALF, (hf + 1) * CHUNKS_PER_HALF)
        _normed_chunks_to_scratch(x_ref, g_ref, b_ref, h_scr, chunks)
        _rope_tables(pos_ref, freq_ref, _spread_a, chunks, cos_scr, sin_scr)

    def matmul(hf, sec):
        if sec == 1 and hf + 1 < N_HALVES:
            prepare(hf + 1)
        return jnp.dot(h_scr[hf * ROW_HALF:(hf + 1) * ROW_HALF, :],
                       w_ref[:, sec * A_WIDTH:(sec + 1) * A_WIDTH], preferred_element_type=F32)

    def epilogue(hf, sec, res):
        nat_ref, r4_ref, r16_ref = outs[sec]
        acc = acc_scr.at[hf * len(outs) + sec]
        st4 = st4_scr.at[hf * len(outs) + sec]
        for cg in range(A_PAIRS):
            cols = slice(cg * LANES, (cg + 1) * LANES)
            for lc in range(CHUNKS_PER_HALF):
                rows = _chunk(hf * CHUNKS_PER_HALF + lc)
                blk = res[lc * ROW_CHUNK:(lc + 1) * ROW_CHUNK, cols]
                if sec < 2:
                    blk = _rope_block(blk, rows, cos_scr, sin_scr)
                    if sec == 0:
                        blk = blk * q_scale
                acc[cg, lc * ROW_CHUNK:(lc + 1) * ROW_CHUNK, :] = blk
                nat_ref[cg, rows, :] = blk.astype(BF16)
            n4, n16 = ROW_HALF // 4, ROW_HALF // 16
            for r in range(4):
                rows4 = acc[cg, pl.ds(r, n4, stride=4), :]
                st4[cg, r] = rows4
                r4_ref[cg, r, hf * n4:(hf + 1) * n4, :] = rows4.astype(BF16)
            for r in range(4):
                for v in range(4):
                    r16_ref[cg, r + 4 * v, hf * n16:(hf + 1) * n16, :] = (
                        st4[cg, r, pl.ds(v, n16, stride=4), :].astype(BF16))

    prepare(0)
    items =[(hf, sec) for hf in range(N_HALVES) for sec in range(len(outs))]
    pending = None
    for hf, sec in items:
        res = matmul(hf, sec)
        if pending is not None:
            epilogue(*pending)
        pending = (hf, sec, res)
    epilogue(*pending)


def _proj_a(x, pos_view, freq_a, g_emb, b_emb, w_qkv):
    batch = x.shape[0]
    nat = jax.ShapeDtypeStruct((batch, A_PAIRS, SEQ, LANES), BF16)
    r4 = jax.ShapeDtypeStruct((batch, A_PAIRS, 4, SEQ // 4, LANES), BF16)
    r16 = jax.ShapeDtypeStruct((batch, A_PAIRS, 16, SEQ // 16, LANES), BF16)
    nat_spec = pl.BlockSpec((None, A_PAIRS, ROW_TILE, LANES), lambda b, i: (b, 0, i, 0))
    r4_spec = pl.BlockSpec((None, A_PAIRS, 4, ROW_TILE // 4, LANES), lambda b, i: (b, 0, 0, i, 0))
    r16_spec = pl.BlockSpec((None, A_PAIRS, 16, ROW_TILE // 16, LANES), lambda b, i: (b, 0, 0, i, 0))
    tile_out = ROW_TILE * A_WIDTH * 2
    est = (2 * ROW_TILE * D_MODEL * 4 + 2 * w_qkv.size * 2 + 2 * 9 * tile_out
           + ROW_TILE * D_MODEL * 2 + 2 * 3 * ROW_TILE * A_WIDTH * 4 + 3 * ROW_TILE * LANES * 4
           + 2 * ROW_HALF * A_WIDTH * 4)
    return pl.pallas_call(
        _proj_a_kernel,
        grid=(batch, N_ROW_TILES),
        in_specs=[
            pl.BlockSpec((None, ROW_TILE, D_MODEL), lambda b, i: (b, i, 0)),
            pl.BlockSpec((None, None, N_CHUNKS, ROW_CHUNK), lambda b, i: (b, i, 0, 0)),
            pl.BlockSpec(freq_a.shape, lambda b, i: (0, 0)),
            pl.BlockSpec((1, D_MODEL), lambda b, i: (0, 0)),
            pl.BlockSpec((1, D_MODEL), lambda b, i: (0, 0)),
            pl.BlockSpec(w_qkv.shape, lambda b, i: (0, 0)),
        ],
        out_specs=[nat_spec] * 3 + [r4_spec] * 3 + [r16_spec] * 3,
        out_shape=[nat] * 3 + [r4] * 3 + [r16] * 3,
        scratch_shapes=[
            pltpu.VMEM((ROW_TILE, D_MODEL), BF16),
            pltpu.VMEM((N_HALVES * 3, A_PAIRS, ROW_HALF, LANES), F32),
            pltpu.VMEM((N_HALVES * 3, A_PAIRS, 4, ROW_HALF // 4, LANES), F32),
            pltpu.VMEM((ROW_TILE, LANES), F32),
            pltpu.VMEM((ROW_TILE, LANES), F32),
        ],
        compiler_params=pltpu.CompilerParams(
            dimension_semantics=("arbitrary", "arbitrary"), vmem_limit_bytes=_vmem_limit(est)),
        name="proj_a",
    )(x, pos_view, freq_a, g_emb, b_emb, w_qkv)


_GATE_W = D_MIX
_OFF_MQ = _GATE_W
_OFF_CQ = _OFF_MQ + MEM_WIDTH
_OFF_CKV = _OFF_CQ + MLA_Q_RANK
_OFF_KR = _OFF_CKV + MLA_KV_RANK
_MLA_QK_W = MLA_HEADS * MLA_SLOT


def _proj_b_kernel(x_ref, pos_ref, freq_ref, g_ref, b_ref, w_ref, gcq_ref, gckv_ref, wuq_ref,
                   wukv_ref, gate_ref, mq_ref, qb_ref, kb_ref, vb_ref,
                   h_scr, cos_scr, sin_scr):
    mla_scale = ((MLA_NOPE_DIM + MLA_ROPE_DIM) ** -0.5) * LOG2_E
    local = lambda lc: slice(lc * ROW_CHUNK, (lc + 1) * ROW_CHUNK)
    half_rows = lambda hf: slice(hf * ROW_HALF, (hf + 1) * ROW_HALF)

    def prepare(hf):
        chunks = range(hf * CHUNKS_PER_HALF, (hf + 1) * CHUNKS_PER_HALF)
        _normed_chunks_to_scratch(x_ref, g_ref, b_ref, h_scr, chunks)
        _rope_tables(pos_ref, freq_ref, _spread_b, chunks, cos_scr, sin_scr)

    def h_dot(hf, first, width):
        return jnp.dot(h_scr[half_rows(hf), :], w_ref[:, first:first + width], preferred_element_type=F32)

    def rope_slots(hf, slots, out_ref, scale=None, extra=None):
        for lc in range(CHUNKS_PER_HALF):
            rows = _chunk(hf * CHUNKS_PER_HALF + lc)
            for hd in range(MLA_HEADS):
                cols = slice(hd * MLA_SLOT, (hd + 1) * MLA_SLOT)
                blk = slots[local(lc), cols]
                if extra is not None:
                    blk = blk + extra[local(lc), :]
                blk = _rope_block(blk, rows, cos_scr, sin_scr)
                out_ref[rows, cols] = (blk if scale is None else blk * scale).astype(BF16)

    def stages(hf):
        state = {}

        def down():
            lat = h_dot(hf, _OFF_CQ, MLA_Q_RANK + MLA_KV_RANK + MLA_SLOT)
            state["cqn"] = _rms_rows(lat[:, :MLA_Q_RANK], gcq_ref[...]).astype(BF16)
            state["ckvn"] = _rms_rows(lat[:, MLA_Q_RANK:MLA_Q_RANK + MLA_KV_RANK], gckv_ref[...]).astype(BF16)
            state["kr"] = lat[:, MLA_Q_RANK + MLA_KV_RANK:]

        def queries():
            return jnp.dot(state["cqn"], wuq_ref[...], preferred_element_type=F32)

        def keys_values():
            k_all = jnp.dot(state["ckvn"], wukv_ref[:, :_MLA_QK_W], preferred_element_type=F32)
            return k_all, jnp.dot(state["ckvn"], wukv_ref[:, _MLA_QK_W:], preferred_element_type=F32)

        def store_keys_values(res):
            k_all, vb = res
            vb_ref[half_rows(hf), :] = vb.astype(BF16)
            rope_slots(hf, k_all, kb_ref, extra=state["kr"])

        def gate_block(n):
            def produce():
                if n == 0 and hf + 1 < N_HALVES:
                    prepare(hf + 1)
                return h_dot(hf, n * A_WIDTH, A_WIDTH)

            def consume(g):
                gate_ref[half_rows(hf), n * A_WIDTH:(n + 1) * A_WIDTH] = (g / (1.0 + jnp.exp(-g))).astype(BF16)

            return produce, consume

        def store_mq(mq):
            mq_ref[half_rows(hf), :] = (mq * (MEM_HEAD_DIM ** -0.5)).astype(BF16)

        return [
            (down, lambda _: None),
            (queries, lambda q_all: rope_slots(hf, q_all, qb_ref, scale=mla_scale)),
            (keys_values, store_keys_values),
            *[gate_block(n) for n in range(_GATE_W // A_WIDTH)],
            (lambda: h_dot(hf, _OFF_MQ, MEM_WIDTH), store_mq),
        ]

    prepare(0)
    pending = None
    for produce, consume in [st for hf in range(N_HALVES) for st in stages(hf)]:
        res = produce()
        if pending is not None:
            pending[0](pending[1])
        pending = (consume, res)
    pending[0](pending[1])


def _proj_b(x, pos_view, freq_b, g_emb, b_emb, w_b, g_cq, g_ckv, w_uq_p, w_ukv_p):
    batch = x.shape[0]

    def rows_spec(width):
        return pl.BlockSpec((None, ROW_TILE, width), lambda b, i: (b, i, 0))

    def full_spec(arr):
        return pl.BlockSpec(arr.shape, lambda b, i: (0,) * arr.ndim)

    widths = (_GATE_W, MEM_WIDTH, _MLA_QK_W, _MLA_QK_W, MLA_WIDTH)
    est = (2 * ROW_TILE * D_MODEL * 4 + 2 * (w_b.size + w_uq_p.size + w_ukv_p.size) * 2
           + 2 * ROW_TILE * sum(widths) * 2 + ROW_TILE * D_MODEL * 2 + ROW_TILE * A_WIDTH * 4
           + 3 * ROW_TILE * LANES * 4 + 2 * ROW_TILE * A_WIDTH * 4)
    return pl.pallas_call(
        _proj_b_kernel,
        grid=(batch, N_ROW_TILES),
        in_specs=[
            rows_spec(D_MODEL),
            pl.BlockSpec((None, None, N_CHUNKS, ROW_CHUNK), lambda b, i: (b, i, 0, 0)),
            full_spec(freq_b),
            full_spec(g_emb), full_spec(b_emb), full_spec(w_b), full_spec(g_cq), full_spec(g_ckv),
            full_spec(w_uq_p), full_spec(w_ukv_p),
        ],
        out_specs=[rows_spec(w) for w in widths],
        out_shape=[jax.ShapeDtypeStruct((batch, SEQ, w), BF16) for w in widths],
        scratch_shapes=[
            pltpu.VMEM((ROW_TILE, D_MODEL), BF16),
            pltpu.VMEM((ROW_TILE, LANES), F32),
            pltpu.VMEM((ROW_TILE, LANES), F32),
        ],
        compiler_params=pltpu.CompilerParams(
            dimension_semantics=("arbitrary", "arbitrary"), vmem_limit_bytes=_vmem_limit(est)),
        name="proj_b",
    )(x, pos_view, freq_b, g_emb, b_emb, w_b, g_cq, g_ckv, w_uq_p, w_ukv_p)


def _first_head_lanes(n_rows):
    return lax.broadcasted_iota(jnp.int32, (n_rows, LANES), 1) < A_HEAD_DIM


def _a_qk_lane_order():
    half = A_ROT_DIM // 2
    rest = A_HEAD_DIM - A_ROT_DIM
    order = ([(0, d) for d in range(half)] + [(1, d) for d in range(half)]
             + [(0, A_ROT_DIM + d) for d in range(rest)]
             + [(0, half + d) for d in range(half)] + [(1, half + d) for d in range(half)]
             + [(1, A_ROT_DIM + d) for d in range(rest)])
    assert len(order) == LANES and len(set(order)) == LANES
    return order


def _first_head_qk_lanes(n_rows):
    lane = lax.broadcasted_iota(jnp.int32, (n_rows, LANES), 1)
    half = A_ROT_DIM // 2
    return (lane < half) | ((lane >= A_ROT_DIM) & (lane < A_ROT_DIM + A_HEAD_DIM - half))


def _band_scores(q, k, bias):
    n_q = q.shape[0]
    first_head = _first_head_lanes(n_q)
    first_head_qk = _first_head_qk_lanes(n_q)
    zero = jnp.zeros_like(q)
    q2 = jnp.concatenate([jnp.where(first_head_qk, q, zero), jnp.where(first_head_qk, zero, q)], axis=0)
    s = lax.dot_general(q2, k, (((1,), (1,)), ((), ())), preferred_element_type=F32) + bias
    m = jnp.max(s, axis=-1, keepdims=True)
    p = jnp.exp2((s - m).astype(BF16))
    return p, jnp.where(first_head, m[:n_q], m[n_q:])


def _band_values(p, v_ones):
    n_q = p.shape[0] // 2
    first_head = _first_head_lanes(n_q)
    o = jnp.dot(p, v_ones, preferred_element_type=F32)
    acc = jnp.where(first_head, o[:n_q, :LANES], o[n_q:, :LANES])
    l_b = jnp.where(first_head, o[:n_q, LANES:], o[n_q:, LANES:])
    return acc, l_b


def _dilated_kernel(qn_ref, kn_ref, vn_ref, q4_ref, k4_ref, v4_ref, q16_ref, k16_ref, v16_ref,
                    o_ref, von_scr, vo4_scr, vo16_scr, bias_scr, bias16_scr, acc_scr, mx_scr, sm_scr, c16_scr):
    def window_bias(n_keys, first_key):
        row = lax.broadcasted_iota(jnp.int32, (2 * A_Q_TILE, n_keys), 0)
        col = lax.broadcasted_iota(jnp.int32, (2 * A_Q_TILE, n_keys), 1)
        q_idx = jnp.where(row >= A_Q_TILE, row - A_Q_TILE, row)
        off = col - (q_idx + first_key)
        return jnp.where(jnp.abs(off) <= A_SIDE, 0.0, NEG_INF).astype(F32)

    @pl.when((pl.program_id(0) == 0) & (pl.program_id(1) == 0))
    def _():
        for variant in range(3):
            bias_scr[variant] = window_bias(A_K_WIN, variant * A_SIDE)
        bias16_scr[...] = window_bias(A_Q_TILE, 0)
        von_scr[:, :, LANES:] = jnp.ones((A_PAIRS_PER_STEP, SEQ, LANES), BF16)
        vo4_scr[:, :, :, LANES:] = jnp.ones((A_PAIRS_PER_STEP, 4, SEQ // 4, LANES), BF16)
        vo16_scr[:, :, :, LANES:] = jnp.ones((A_PAIRS_PER_STEP, 16, SEQ // 16, LANES), BF16)

    von_scr[:, :, :LANES] = vn_ref[...]
    vo4_scr[:, :, :, :LANES] = v4_ref[...]
    vo16_scr[:, :, :, :LANES] = v16_ref[...]

    len4 = SEQ // 4
    tiles4 = len4 // A_Q_TILE
    n_tiles = SEQ // A_Q_TILE

    def window(tile, n_seq_tiles, seq_len):
        qs = tile * A_Q_TILE
        ks = min(max(qs - A_SIDE, 0), seq_len - A_K_WIN)
        variant = 0 if tile == 0 else (2 if tile == n_seq_tiles - 1 else 1)
        return qs, ks, variant

    def score_half(step):
        pp, i = step
        qs, ks, var = window(i, n_tiles, SEQ)
        res = i // tiles4
        qs4, ks4, var4 = window(i % tiles4, tiles4, len4)
        parts = (
            _band_scores(qn_ref[pp, pl.ds(qs, A_Q_TILE), :], kn_ref[pp, pl.ds(ks, A_K_WIN), :], bias_scr[var]),
            _band_scores(q4_ref[pp, res, pl.ds(qs4, A_Q_TILE), :], k4_ref[pp, res, pl.ds(ks4, A_K_WIN), :],
                         bias_scr[var4]),
            _band_scores(q16_ref[pp, i], k16_ref[pp, i], bias16_scr[...]),
        )
        class16 = pl.ds(i * C16_PITCH, A_Q_TILE)
        dests = ((acc_scr.at[pp, 0], mx_scr.at[pp, 0], sm_scr.at[pp, 0], pl.ds(qs, A_Q_TILE)),
                 (acc_scr.at[pp, 1], mx_scr.at[pp, 1], sm_scr.at[pp, 1], pl.ds(res + 4 * qs4, A_Q_TILE, stride=4)),
                 (c16_scr.at[pp, 0], c16_scr.at[pp, 1], c16_scr.at[pp, 2], class16))
        v_wins = (von_scr.at[pp, pl.ds(ks, A_K_WIN), :], vo4_scr.at[pp, res, pl.ds(ks4, A_K_WIN), :],
                  vo16_scr.at[pp, i])
        for (_, mx_at, _, dst), (_, m_b) in zip(dests, parts):
            mx_at[dst, :] = m_b
        return [(p, dest, v_win) for (p, _), dest, v_win in zip(parts, dests, v_wins)]

    def value_half(pending):
        for p, (acc_at, _, sm_at, dst), v_win in pending:
            acc, l_b = _band_values(p, v_win[...])
            acc_at[dst, :] = acc
            sm_at[dst, :] = l_b

    steps = [(pp, i) for pp in range(A_PAIRS_PER_STEP) for i in range(n_tiles)]
    pending = score_half(steps[0])
    for step in steps[1:]:
        ahead = score_half(step)
        value_half(pending)
        pending = ahead
    value_half(pending)

    def class16_rows(scr, c):
        groups = []
        for g in range(ROW_CHUNK // 8):
            first_class = 8 * (g % 2)
            t = (ROW_CHUNK // 16) * c + g // 2
            groups.append(scr[pl.ds(first_class * C16_PITCH + t, 8, stride=C16_PITCH), :])
        return jnp.concatenate(groups, axis=0)

    def merge_pair(pp):
        def merge(c, carry):
            rows = pl.ds(pl.multiple_of(c * ROW_CHUNK, ROW_CHUNK), ROW_CHUNK)
            parts = [(acc_scr[pp, pat, rows, :], mx_scr[pp, pat, rows, :], sm_scr[pp, pat, rows, :])
                     for pat in range(2)]
            parts.append(tuple(class16_rows(c16_scr.at[pp, k], c) for k in range(3)))
            m_all = jnp.maximum(jnp.maximum(parts[0][1], parts[1][1]), parts[2][1])
            num = jnp.zeros((ROW_CHUNK, LANES), F32)
            den = jnp.zeros((ROW_CHUNK, LANES), F32)
            for acc, m_b, l_b in parts:
                w = jnp.exp2(m_b - m_all)
                num = num + w * acc
                den = den + w * l_b
            o_ref[pp, rows, :] = (num / den).astype(BF16)
            return carry

        lax.fori_loop(0, SEQ // ROW_CHUNK, merge, 0, unroll=4)

    for pp in range(A_PAIRS_PER_STEP):
        merge_pair(pp)


def _dilated_attention(qkv_nat, qkv_4, qkv_16):
    batch = qkv_nat[0].shape[0]
    pps = A_PAIRS_PER_STEP
    nat_spec = pl.BlockSpec((None, pps, SEQ, LANES), lambda b, p: (b, p, 0, 0))
    r4_spec = pl.BlockSpec((None, pps, 4, SEQ // 4, LANES), lambda b, p: (b, p, 0, 0, 0))
    r16_spec = pl.BlockSpec((None, pps, 16, SEQ // 16, LANES), lambda b, p: (b, p, 0, 0, 0))
    blk = SEQ * LANES
    est = pps * (2 * 9 * blk * 2 + 2 * blk * 2 + 3 * blk * 2 * 2 + 9 * blk * 4) + 4 * 2 * A_Q_TILE * A_K_WIN * 4
    return pl.pallas_call(
        _dilated_kernel,
        grid=(batch, A_PAIRS // pps),
        in_specs=[nat_spec] * 3 + [r4_spec] * 3 + [r16_spec] * 3,
        out_specs=pl.BlockSpec((None, pps, SEQ, LANES), lambda b, p: (b, p, 0, 0)),
        out_shape=jax.ShapeDtypeStruct((batch, A_PAIRS, SEQ, LANES), BF16),
        scratch_shapes=[
            pltpu.VMEM((pps, SEQ, 2 * LANES), BF16),
            pltpu.VMEM((pps, 4, SEQ // 4, 2 * LANES), BF16),
            pltpu.VMEM((pps, 16, SEQ // 16, 2 * LANES), BF16),
            pltpu.VMEM((3, 2 * A_Q_TILE, A_K_WIN), F32),
            pltpu.VMEM((2 * A_Q_TILE, A_Q_TILE), F32),
            pltpu.VMEM((pps, 2, SEQ, LANES), F32),
            pltpu.VMEM((pps, 2, SEQ, LANES), F32),
            pltpu.VMEM((pps, 2, SEQ, LANES), F32),
            pltpu.VMEM((pps, 3, 16 * C16_PITCH, LANES), F32),
        ],
        compiler_params=pltpu.CompilerParams(
            dimension_semantics=("arbitrary", "arbitrary"), vmem_limit_bytes=_vmem_limit(est)),
        name="dilated_attn",
    )(*qkv_nat, *qkv_4, *qkv_16)


def _latent_kernel(q_ref, k_ref, v_ref, o_ref, vo_scr):
    @pl.when((pl.program_id(0) == 0) & (pl.program_id(1) == 0) & (pl.program_id(2) == 0))
    def _():
        vo_scr[:, :, LANES:] = jnp.ones((MLA_PAIRS_PER_STEP, SEQ, LANES), BF16)

    @pl.when(pl.program_id(2) == 0)
    def _():
        for pair in range(MLA_PAIRS_PER_STEP):
            vo_scr[pair, :, :LANES] = v_ref[:, pair * LANES:(pair + 1) * LANES]

    def score_half(rows, head):
        cols = slice(head * MLA_SLOT, (head + 1) * MLA_SLOT)
        s = lax.dot_general(q_ref[rows, cols], k_ref[:, cols], (((1,), (1,)), ((), ())),
                            preferred_element_type=F32)
        m = jnp.max(s, axis=-1, keepdims=True)
        return jnp.exp2((s - m).astype(BF16))

    def value_half(head, p):
        o = jnp.dot(p, vo_scr[head // 2], preferred_element_type=F32)
        return o[:, :LANES] / o[:, LANES:]

    n_heads = 2 * MLA_PAIRS_PER_STEP
    chains = [(slice(t * MLA_Q_TILE, (t + 1) * MLA_Q_TILE), head)
              for t in range(MLA_Q_TILES_PER_STEP) for head in range(n_heads)]
    lane = lax.broadcasted_iota(jnp.int32, (MLA_Q_TILE, LANES), 1)
    p_next = score_half(*chains[0])
    for n, (rows, head) in enumerate(chains):
        p = p_next
        if n + 1 < len(chains):
            p_next = score_half(*chains[n + 1])
        out = value_half(head, p)
        if head % 2 == 0:
            first_of_pair = out
        else:
            pair = head // 2
            o_ref[rows, pair * LANES:(pair + 1) * LANES] = (
                jnp.where(lane < MLA_V_DIM, first_of_pair, out).astype(BF16))


def _latent_attention(qb, kb, vb):
    batch = qb.shape[0]
    pairs = MLA_HEADS // 2 // MLA_PAIRS_PER_STEP
    qk_w = MLA_PAIRS_PER_STEP * 2 * MLA_SLOT
    v_w = MLA_PAIRS_PER_STEP * LANES
    step_rows = MLA_Q_TILES_PER_STEP * MLA_Q_TILE
    est = (2 * step_rows * qk_w * 2 + 2 * SEQ * qk_w * 2 + 2 * SEQ * v_w * 2 + SEQ * 2 * v_w * 2
           + 2 * step_rows * v_w * 2 + 4 * MLA_PAIRS_PER_STEP * MLA_Q_TILE * SEQ * 4)
    return pl.pallas_call(
        _latent_kernel,
        grid=(batch, pairs, SEQ // step_rows),
        in_specs=[
            pl.BlockSpec((None, step_rows, qk_w), lambda b, p, i: (b, i, p)),
            pl.BlockSpec((None, SEQ, qk_w), lambda b, p, i: (b, 0, p)),
            pl.BlockSpec((None, SEQ, v_w), lambda b, p, i: (b, 0, p)),
        ],
        out_specs=pl.BlockSpec((None, step_rows, v_w), lambda b, p, i: (b, i, p)),
        out_shape=jax.ShapeDtypeStruct((batch, SEQ, MLA_WIDTH), BF16),
        scratch_shapes=[pltpu.VMEM((MLA_PAIRS_PER_STEP, SEQ, 2 * LANES), BF16)],
        compiler_params=pltpu.CompilerParams(
            dimension_semantics=("arbitrary", "arbitrary", "arbitrary"),
            vmem_limit_bytes=_vmem_limit(est)),
        name="latent_attn",
    )(qb, kb, vb)


def _output_kernel(x_ref, ya_ref, yb_ref, gate_ref, mq_ref, mem_ref, wmem_ref, wout_ref,
                   goa_ref, gob_ref, gom_ref, gemb_ref, bemb_ref, gpost_ref, bpost_ref,
                   o_ref, y_scr, mkv_ref):
    @pl.when(pl.program_id(1) == 0)
    def _():
        mkv_ref[...] = jnp.dot(mem_ref[...].astype(BF16), wmem_ref[...],
                               preferred_element_type=F32).astype(BF16)

    for hd in range(MEM_HEADS):
        cols = slice(hd * MEM_HEAD_DIM, (hd + 1) * MEM_HEAD_DIM)
        s = lax.dot_general(mq_ref[:, cols], mkv_ref[:, cols], (((1,), (1,)), ((), ())),
                            preferred_element_type=F32)
        m = jnp.max(s, axis=-1, keepdims=True)
        p = jnp.exp(s - m)
        l = jnp.sum(p, axis=-1, keepdims=True)
        o = jnp.dot(p.astype(BF16), mkv_ref[:, MEM_WIDTH + hd * MEM_HEAD_DIM:MEM_WIDTH + (hd + 1) * MEM_HEAD_DIM],
                    preferred_element_type=F32)
        y_scr[:, cols] = o / l

    off_b = A_WIDTH
    off_m = A_WIDTH + MLA_WIDTH
    ya_all = jnp.concatenate([ya_ref[cg] for cg in range(A_PAIRS)], axis=1)
    ya = _rms_rows(ya_all.astype(F32), goa_ref[...]) * gate_ref[:, :off_b].astype(F32)
    yb = _rms_rows(yb_ref[...].astype(F32), gob_ref[...]) * gate_ref[:, off_b:off_m].astype(F32)
    ym = _rms_rows(y_scr[...], gom_ref[...]) * gate_ref[:, off_m:].astype(F32)
    sub = (jnp.dot(ya.astype(BF16), wout_ref[:off_b, :], preferred_element_type=F32)
           + jnp.dot(yb.astype(BF16), wout_ref[off_b:off_m, :], preferred_element_type=F32)
           + jnp.dot(ym.astype(BF16), wout_ref[off_m:, :], preferred_element_type=F32))

    for c in range(N_CHUNKS):
        h = _layer_norm_rows(x_ref[_chunk(c), :], gemb_ref[...], bemb_ref[...])
        z = DEEPNORM_ALPHA * h + sub[c * ROW_CHUNK:(c + 1) * ROW_CHUNK, :]
        o_ref[_chunk(c), :] = _layer_norm_rows(z, gpost_ref[...], bpost_ref[...])


def _output_stage(x, ya, yb, gates, mq, mem, w_mem, w_out, g_out_a, g_out_b, g_out_m, g_emb, b_emb,
                  g_post, b_post):
    batch = x.shape[0]

    def rows_spec(width):
        return pl.BlockSpec((None, ROW_TILE, width), lambda b, i: (b, i, 0))

    def full_spec(arr):
        return pl.BlockSpec(arr.shape, lambda b, i: (0,) * arr.ndim)

    def resident_spec(arr):
        return pl.BlockSpec(arr.shape, lambda b, i: (0,) * arr.ndim, pipeline_mode=pl.Buffered(1))

    est = (4 * ROW_TILE * D_MODEL * 4 + 2 * ROW_TILE * (A_WIDTH + MLA_WIDTH + D_MIX + MEM_WIDTH) * 2
           + 2 * N_MEM * D_MODEL * 4 + N_MEM * 2 * MEM_WIDTH * 2 + (w_out.size + w_mem.size) * 2
           + ROW_TILE * MEM_WIDTH * 4 + 4 * ROW_TILE * D_MODEL * 4)
    return pl.pallas_call(
        _output_kernel,
        grid=(batch, N_ROW_TILES),
        in_specs=[
            rows_spec(D_MODEL),
            pl.BlockSpec((None, A_PAIRS, ROW_TILE, LANES), lambda b, i: (b, 0, i, 0)),
            rows_spec(MLA_WIDTH), rows_spec(D_MIX), rows_spec(MEM_WIDTH),
            pl.BlockSpec((None, N_MEM, D_MODEL), lambda b, i: (b, 0, 0)),
            resident_spec(w_mem), resident_spec(w_out),
            full_spec(g_out_a), full_spec(g_out_b), full_spec(g_out_m),
            full_spec(g_emb), full_spec(b_emb), full_spec(g_post), full_spec(b_post),
        ],
        out_specs=rows_spec(D_MODEL),
        out_shape=jax.ShapeDtypeStruct(x.shape, F32),
        scratch_shapes=[pltpu.VMEM((ROW_TILE, MEM_WIDTH), F32),
                        pltpu.VMEM((N_MEM, 2 * MEM_WIDTH), BF16)],
        compiler_params=pltpu.CompilerParams(
            dimension_semantics=("arbitrary", "arbitrary"), vmem_limit_bytes=_vmem_limit(est)),
        name="output_stage",
    )(x, ya, yb, gates, mq, mem, w_mem, w_out, g_out_a, g_out_b, g_out_m, g_emb, b_emb, g_post, b_post)


def _rope_freq_rows(rot_dim):
    inv_freq = (np.float32(ROPE_THETA) ** (-(np.arange(0, rot_dim, 2, dtype=np.float32) / np.float32(rot_dim)))
                ).astype(np.float32)
    return jnp.asarray(np.repeat(inv_freq[:, None], LANES, axis=1))


def _a_qk_columns():
    return np.asarray([(2 * pair + hh) * A_HEAD_DIM + d
                       for pair in range(A_PAIRS) for hh, d in _a_qk_lane_order()])


def _mla_slot_lanes():
    half = MLA_ROPE_DIM // 2
    assert ROPE_FIRST_LANES == (0, half)
    lanes = [None] * MLA_SLOT
    for i in range(half):
        lanes[i] = ("rope", i)
        lanes[ROPE_PARTNER_SHIFT + i] = ("rope", half + i)
    free = [lane for lane in range(MLA_SLOT) if lanes[lane] is None]
    for d in range(MLA_NOPE_DIM):
        lanes[free[d]] = ("nope", d)
    return lanes


def _slot_gather(cols, index_of):
    n = cols.shape[-1]
    idx = np.asarray([n if index_of(lane) is None else index_of(lane) for lane in _mla_slot_lanes()])
    return jnp.pad(cols, [(0, 0)] * (cols.ndim - 1) + [(0, 1)])[..., idx]


def kernel(x, mem, positions, g_emb, b_emb, w_in, g_cq, g_ckv, w_uq, w_ukv, w_mem_kv, g_out_a, g_out_b,
           g_out_m, w_out, g_post, b_post):
    batch = x.shape[0]
    assert x.shape == (batch, SEQ, D_MODEL) and w_in.shape[0] == DEPTH == 1
    row = lambda v: v.reshape(1, -1).astype(F32)

    splits = [int(i) for i in np.cumsum(IN_SPLITS)[:-1]]
    a_q, a_k, a_v, a_g, c_q, c_kv, b_kr, b_g, m_q, m_g = jnp.split(w_in[0].astype(BF16), splits, axis=1)
    qk_cols = _a_qk_columns()
    w_qkv = jnp.concatenate([a_q[:, qk_cols], a_k[:, qk_cols], a_v], axis=1).astype(BF16)
    rope_only = lambda lane: lane[1] if lane is not None and lane[0] == "rope" else None
    nope_only = lambda lane: lane[1] if lane is not None and lane[0] == "nope" else None
    nope_then_rope = lambda lane: None if lane is None else (lane[1] + (MLA_NOPE_DIM if lane[0] == "rope" else 0))
    kr_slot = _slot_gather(b_kr, rope_only)
    w_b = jnp.concatenate([a_g, b_g, m_g, m_q, c_q, c_kv, kr_slot], axis=1).astype(BF16)
    qk_dim = MLA_NOPE_DIM + MLA_ROPE_DIM
    w_uq_p = _slot_gather(w_uq[0].reshape(MLA_Q_RANK, MLA_HEADS, qk_dim), nope_then_rope
                          ).reshape(MLA_Q_RANK, _MLA_QK_W).astype(BF16)
    ukv = w_ukv[0].reshape(MLA_KV_RANK, MLA_HEADS, MLA_NOPE_DIM + MLA_V_DIM)
    w_uk_p = _slot_gather(ukv[:, :, :MLA_NOPE_DIM], nope_only)
    w_ukv_p = jnp.concatenate([w_uk_p.reshape(MLA_KV_RANK, _MLA_QK_W),
                               ukv[:, :, MLA_NOPE_DIM:].reshape(MLA_KV_RANK, MLA_WIDTH)], axis=1).astype(BF16)

    pos_view = positions.astype(F32).reshape(batch, N_ROW_TILES, N_CHUNKS, ROW_CHUNK)
    freq_a = _rope_freq_rows(A_ROT_DIM)
    freq_b = _rope_freq_rows(MLA_ROPE_DIM)
    g_emb_r, b_emb_r = row(g_emb), row(b_emb)

    outs_a = _proj_a(x, pos_view, freq_a, g_emb_r, b_emb_r, w_qkv)
    y_a = _dilated_attention(outs_a[0:3], outs_a[3:6], outs_a[6:9])

    gates, mq, qb, kb, vb = _proj_b(x, pos_view, freq_b, g_emb_r, b_emb_r, w_b, row(g_cq[0]),
                                    row(g_ckv[0]), w_uq_p, w_ukv_p)
    y_b = _latent_attention(qb, kb, vb)

    return _output_stage(x, y_a, y_b, gates, mq, mem, w_mem_kv[0].astype(BF16), w_out[0].astype(BF16), row(g_out_a[0]),
                         row(g_out_b[0]), row(g_out_m[0]), g_emb_r, b_emb_r, row(g_post[0]), row(b_post[0]))
```

```python
import math

import jax
import jax.numpy as jnp
import numpy as np
from jax import lax
from jax.experimental import pallas as pl
from jax.experimental.pallas import tpu as pltpu

D_MODEL = 1024
SEQ = 2048
A_HEADS = 16
A_HEAD_DIM = 64
A_WIDTH = A_HEADS * A_HEAD_DIM
A_ROT_DIM = A_HEAD_DIM // 4
A_SIDE = 64
DILATIONS = (1, 4, 16)
MLA_HEADS = 8
MLA_Q_RANK = 256
MLA_KV_RANK = 128
MLA_NOPE_DIM = 64
MLA_ROPE_DIM = 32
MLA_V_DIM = 64
MLA_WIDTH = MLA_HEADS * MLA_V_DIM
N_MEM = 256
MEM_HEADS = 4
MEM_HEAD_DIM = 128
MEM_WIDTH = MEM_HEADS * MEM_HEAD_DIM
D_MIX = A_WIDTH + MLA_WIDTH + MEM_WIDTH
ROPE_THETA = 500000.0
NORM_EPS = 1e-5
NEG_INF = -1e30
DEPTH = 1
DEEPNORM_ALPHA = (2 * DEPTH) ** 0.25
IN_SPLITS = (A_WIDTH, A_WIDTH, A_WIDTH, A_WIDTH, MLA_Q_RANK, MLA_KV_RANK, MLA_ROPE_DIM, MLA_WIDTH,
             MEM_WIDTH, MEM_WIDTH)
LOG2_E = math.log2(math.e)

LANES = 128
VMEM_BYTES_V7X = 64 * 1024 * 1024
VMEM_RESERVED_BYTES = 4 << 20
VMEM_MIN_REQUEST_BYTES = 32 << 20
VMEM_SPILL_ALLOWANCE_BYTES = 8 << 20

ROW_TILE = 512
ROW_CHUNK = 128
N_CHUNKS = ROW_TILE // ROW_CHUNK
N_HALVES = 1
ROW_HALF = ROW_TILE // N_HALVES
CHUNKS_PER_HALF = N_CHUNKS // N_HALVES
N_ROW_TILES = SEQ // ROW_TILE
A_PAIRS = A_WIDTH // LANES
A_PAIRS_PER_STEP = 2
A_Q_TILE = 128
A_K_WIN = 2 * A_Q_TILE
C16_PITCH = SEQ // 16 + 8
MLA_SLOT = LANES
MLA_Q_TILE = 256
MLA_Q_TILES_PER_STEP = 4
MLA_PAIRS_PER_STEP = 4

F32 = jnp.float32
BF16 = jnp.bfloat16


def _vmem_limit(nbytes):
    return int(min(VMEM_BYTES_V7X - VMEM_RESERVED_BYTES,
                   max(VMEM_MIN_REQUEST_BYTES, nbytes + VMEM_SPILL_ALLOWANCE_BYTES)))


def _layer_norm_rows(x, g, b):
    mu = jnp.mean(x, axis=-1, keepdims=True)
    xc = x - mu
    var = jnp.mean(xc * xc, axis=-1, keepdims=True)
    return xc * lax.rsqrt(var + NORM_EPS) * g + b


def _rms_rows(x, g):
    ms = jnp.mean(x * x, axis=-1, keepdims=True)
    return x * lax.rsqrt(ms + NORM_EPS) * g


def _chunk(c):
    return slice(c * ROW_CHUNK, (c + 1) * ROW_CHUNK)


def _normed_chunks_to_scratch(x_ref, g_ref, b_ref, h_scr, chunks):
    for c in chunks:
        h_scr[_chunk(c), :] = _layer_norm_rows(x_ref[_chunk(c), :], g_ref[...], b_ref[...]).astype(BF16)


ROPE_FIRST_LANES = (0, 16)
ROPE_PARTNER_SHIFT = LANES // 2


def _rope_tables(pos_ref, freq_ref, spread, chunks, cos_scr, sin_scr):
    n_freq = freq_ref.shape[0]
    lane = lax.broadcasted_iota(jnp.int32, (ROW_CHUNK, LANES), 1)
    first = (lane >= ROPE_FIRST_LANES[0]) & (lane < ROPE_FIRST_LANES[1])
    second = (lane >= ROPE_FIRST_LANES[0] + ROPE_PARTNER_SHIFT) & (lane < ROPE_FIRST_LANES[1] + ROPE_PARTNER_SHIFT)
    pad = jnp.zeros((ROW_CHUNK - n_freq, LANES), F32)
    for c in chunks:
        ang = freq_ref[...] * pos_ref[c:c + 1, :]
        cos = spread(jnp.concatenate([jnp.cos(ang), pad], axis=0).T)
        sin = spread(jnp.concatenate([jnp.sin(ang), pad], axis=0).T)
        cos_scr[_chunk(c), :] = jnp.where(first | second, cos, 1.0)
        sin_scr[_chunk(c), :] = jnp.where(first, -sin, jnp.where(second, sin, 0.0))


def _spread_a(t):
    t = t + pltpu.roll(t, A_ROT_DIM // 2, 1)
    return t + pltpu.roll(t, ROPE_PARTNER_SHIFT, 1)


def _spread_b(t):
    return t + pltpu.roll(t, ROPE_PARTNER_SHIFT, 1)


def _rope_block(x, rows, cos_scr, sin_scr):
    return x * cos_scr[rows, :] + pltpu.roll(x, ROPE_PARTNER_SHIFT, 1) * sin_scr[rows, :]


def _proj_a_kernel(x_ref, pos_ref, freq_ref, g_ref, b_ref, w_ref,
                   qn_ref, kn_ref, vn_ref, q4_ref, k4_ref, v4_ref, q16_ref, k16_ref, v16_ref,
                   h_scr, acc_scr, st4_scr, cos_scr, sin_scr):
    outs = ((qn_ref, q4_ref, q16_ref), (kn_ref, k4_ref, k16_ref), (vn_ref, v4_ref, v16_ref))
    q_scale = (A_HEAD_DIM ** -0.5) * LOG2_E

    def prepare(hf):
        chunks = range(hf * CHUNKS_PER_HALF, (hf + 1) * CHUNKS_PER_HALF)
        _normed_chunks_to_scratch(x_ref, g_ref, b_ref, h_scr, chunks)
        _rope_tables(pos_ref, freq_ref, _spread_a, chunks, cos_scr, sin_scr)

    def matmul(hf, sec):
        if sec == 1 and hf + 1 < N_HALVES:
            prepare(hf + 1)
        return jnp.dot(h_scr[hf * ROW_HALF:(hf + 1) * ROW_HALF, :],
                       w_ref[:, sec * A_WIDTH:(sec + 1) * A_WIDTH], preferred_element_type=F32)

    def epilogue(hf, sec, res):
        nat_ref, r4_ref, r16_ref = outs[sec]
        acc = acc_scr.at[hf * len(outs) + sec]
        st4 = st4_scr.at[hf * len(outs) + sec]
        for cg in range(A_PAIRS):
            cols = slice(cg * LANES, (cg + 1) * LANES)
            for lc in range(CHUNKS_PER_HALF):
                rows = _chunk(hf * CHUNKS_PER_HALF + lc)
                blk = res[lc * ROW_CHUNK:(lc + 1) * ROW_CHUNK, cols]
                if sec < 2:
                    blk = _rope_block(blk, rows, cos_scr, sin_scr)
                    if sec == 0:
                        blk = blk * q_scale
                acc[cg, lc * ROW_CHUNK:(lc + 1) * ROW_CHUNK, :] = blk
                nat_ref[cg, rows, :] = blk.astype(BF16)
            n4, n16 = ROW_HALF // 4, ROW_HALF // 16
            for r in range(4):
                rows4 = acc[cg, pl.ds(r, n4, stride=4), :]
                st4[cg, r] = rows4
                r4_ref[cg, r, hf * n4:(hf + 1) * n4, :] = rows4.astype(BF16)
            for r in range(4):
                for v in range(4):
                    r16_ref[cg, r + 4 * v, hf * n16:(hf + 1) * n16, :] = (
                        st4[cg, r, pl.ds(v, n16, stride=4), :].astype(BF16))

    prepare(0)
    items =[(hf, sec) for hf in range(N_HALVES) for sec in range(len(outs))]
    pending = None
    for hf, sec in items:
        res = matmul(hf, sec)
        if pending is not None:
            epilogue(*pending)
        pending = (hf, sec, res)
    epilogue(*pending)


def _proj_a(x, pos_view, freq_a, g_emb, b_emb, w_qkv):
    batch = x.shape[0]
    nat = jax.ShapeDtypeStruct((batch, A_PAIRS, SEQ, LANES), BF16)
    r4 = jax.ShapeDtypeStruct((batch, A_PAIRS, 4, SEQ // 4, LANES), BF16)
    r16 = jax.ShapeDtypeStruct((batch, A_PAIRS, 16, SEQ // 16, LANES), BF16)
    nat_spec = pl.BlockSpec((None, A_PAIRS, ROW_TILE, LANES), lambda b, i: (b, 0, i, 0))
    r4_spec = pl.BlockSpec((None, A_PAIRS, 4, ROW_TILE // 4, LANES), lambda b, i: (b, 0, 0, i, 0))
    r16_spec = pl.BlockSpec((None, A_PAIRS, 16, ROW_TILE // 16, LANES), lambda b, i: (b, 0, 0, i, 0))
    tile_out = ROW_TILE * A_WIDTH * 2
    est = (2 * ROW_TILE * D_MODEL * 4 + 2 * w_qkv.size * 2 + 2 * 9 * tile_out
           + ROW_TILE * D_MODEL * 2 + 2 * 3 * ROW_TILE * A_WIDTH * 4 + 3 * ROW_TILE * LANES * 4
           + 2 * ROW_HALF * A_WIDTH * 4)
    return pl.pallas_call(
        _proj_a_kernel,
        grid=(batch, N_ROW_TILES),
        in_specs=[
            pl.BlockSpec((None, ROW_TILE, D_MODEL), lambda b, i: (b, i, 0)),
            pl.BlockSpec((None, None, N_CHUNKS, ROW_CHUNK), lambda b, i: (b, i, 0, 0)),
            pl.BlockSpec(freq_a.shape, lambda b, i: (0, 0)),
            pl.BlockSpec((1, D_MODEL), lambda b, i: (0, 0)),
            pl.BlockSpec((1, D_MODEL), lambda b, i: (0, 0)),
            pl.BlockSpec(w_qkv.shape, lambda b, i: (0, 0)),
        ],
        out_specs=[nat_spec] * 3 + [r4_spec] * 3 + [r16_spec] * 3,
        out_shape=[nat] * 3 + [r4] * 3 + [r16] * 3,
        scratch_shapes=[
            pltpu.VMEM((ROW_TILE, D_MODEL), BF16),
            pltpu.VMEM((N_HALVES * 3, A_PAIRS, ROW_HALF, LANES), F32),
            pltpu.VMEM((N_HALVES * 3, A_PAIRS, 4, ROW_HALF // 4, LANES), F32),
            pltpu.VMEM((ROW_TILE, LANES), F32),
            pltpu.VMEM((ROW_TILE, LANES), F32),
        ],
        compiler_params=pltpu.CompilerParams(
            dimension_semantics=("arbitrary", "arbitrary"), vmem_limit_bytes=_vmem_limit(est)),
        name="proj_a",
    )(x, pos_view, freq_a, g_emb, b_emb, w_qkv)


_GATE_W = D_MIX
_OFF_MQ = _GATE_W
_OFF_CQ = _OFF_MQ + MEM_WIDTH
_OFF_CKV = _OFF_CQ + MLA_Q_RANK
_OFF_KR = _OFF_CKV + MLA_KV_RANK
_MLA_QK_W = MLA_HEADS * MLA_SLOT


def _proj_b_kernel(x_ref, pos_ref, freq_ref, g_ref, b_ref, w_ref, gcq_ref, gckv_ref, wuq_ref,
                   wukv_ref, gate_ref, mq_ref, qb_ref, kb_ref, vb_ref,
                   h_scr, cos_scr, sin_scr):
    mla_scale = ((MLA_NOPE_DIM + MLA_ROPE_DIM) ** -0.5) * LOG2_E
    local = lambda lc: slice(lc * ROW_CHUNK, (lc + 1) * ROW_CHUNK)
    half_rows = lambda hf: slice(hf * ROW_HALF, (hf + 1) * ROW_HALF)

    def prepare(hf):
        chunks = range(hf * CHUNKS_PER_HALF, (hf + 1) * CHUNKS_PER_HALF)
        _normed_chunks_to_scratch(x_ref, g_ref, b_ref, h_scr, chunks)
        _rope_tables(pos_ref, freq_ref, _spread_b, chunks, cos_scr, sin_scr)

    def h_dot(hf, first, width):
        return jnp.dot(h_scr[half_rows(hf), :], w_ref[:, first:first + width], preferred_element_type=F32)

    def rope_slots(hf, slots, out_ref, scale=None, extra=None):
        for lc in range(CHUNKS_PER_HALF):
            rows = _chunk(hf * CHUNKS_PER_HALF + lc)
            for hd in range(MLA_HEADS):
                cols = slice(hd * MLA_SLOT, (hd + 1) * MLA_SLOT)
                blk = slots[local(lc), cols]
                if extra is not None:
                    blk = blk + extra[local(lc), :]
                blk = _rope_block(blk, rows, cos_scr, sin_scr)
                out_ref[rows, cols] = (blk if scale is None else blk * scale).astype(BF16)

    def stages(hf):
        state = {}

        def down():
            lat = h_dot(hf, _OFF_CQ, MLA_Q_RANK + MLA_KV_RANK + MLA_SLOT)
            state["cqn"] = _rms_rows(lat[:, :MLA_Q_RANK], gcq_ref[...]).astype(BF16)
            state["ckvn"] = _rms_rows(lat[:, MLA_Q_RANK:MLA_Q_RANK + MLA_KV_RANK], gckv_ref[...]).astype(BF16)
            state["kr"] = lat[:, MLA_Q_RANK + MLA_KV_RANK:]

        def queries():
            return jnp.dot(state["cqn"], wuq_ref[...], preferred_element_type=F32)

        def keys_values():
            k_all = jnp.dot(state["ckvn"], wukv_ref[:, :_MLA_QK_W], preferred_element_type=F32)
            return k_all, jnp.dot(state["ckvn"], wukv_ref[:, _MLA_QK_W:], preferred_element_type=F32)

        def store_keys_values(res):
            k_all, vb = res
            vb_ref[half_rows(hf), :] = vb.astype(BF16)
            rope_slots(hf, k_all, kb_ref, extra=state["kr"])

        def gate_block(n):
            def produce():
                if n == 0 and hf + 1 < N_HALVES:
                    prepare(hf + 1)
                return h_dot(hf, n * A_WIDTH, A_WIDTH)

            def consume(g):
                gate_ref[half_rows(hf), n * A_WIDTH:(n + 1) * A_WIDTH] = (g / (1.0 + jnp.exp(-g))).astype(BF16)

            return produce, consume

        def store_mq(mq):
            mq_ref[half_rows(hf), :] = (mq * ((MEM_HEAD_DIM ** -0.5) * LOG2_E)).astype(BF16)

        return [
            (down, lambda _: None),
            (queries, lambda q_all: rope_slots(hf, q_all, qb_ref, scale=mla_scale)),
            (keys_values, store_keys_values),
            *[gate_block(n) for n in range(_GATE_W // A_WIDTH)],
            (lambda: h_dot(hf, _OFF_MQ, MEM_WIDTH), store_mq),
        ]

    prepare(0)
    pending = None
    for produce, consume in [st for hf in range(N_HALVES) for st in stages(hf)]:
        res = produce()
        if pending is not None:
            pending[0](pending[1])
        pending = (consume, res)
    pending[0](pending[1])


def _proj_b(x, pos_view, freq_b, g_emb, b_emb, w_b, g_cq, g_ckv, w_uq_p, w_ukv_p):
    batch = x.shape[0]

    def rows_spec(width):
        return pl.BlockSpec((None, ROW_TILE, width), lambda b, i: (b, i, 0))

    def full_spec(arr):
        return pl.BlockSpec(arr.shape, lambda b, i: (0,) * arr.ndim)

    widths = (_GATE_W, MEM_WIDTH, _MLA_QK_W, _MLA_QK_W, MLA_WIDTH)
    est = (2 * ROW_TILE * D_MODEL * 4 + 2 * (w_b.size + w_uq_p.size + w_ukv_p.size) * 2
           + 2 * ROW_TILE * sum(widths) * 2 + ROW_TILE * D_MODEL * 2 + ROW_TILE * A_WIDTH * 4
           + 3 * ROW_TILE * LANES * 4 + 2 * ROW_TILE * A_WIDTH * 4)
    return pl.pallas_call(
        _proj_b_kernel,
        grid=(batch, N_ROW_TILES),
        in_specs=[
            rows_spec(D_MODEL),
            pl.BlockSpec((None, None, N_CHUNKS, ROW_CHUNK), lambda b, i: (b, i, 0, 0)),
            full_spec(freq_b),
            full_spec(g_emb), full_spec(b_emb), full_spec(w_b), full_spec(g_cq), full_spec(g_ckv),
            full_spec(w_uq_p), full_spec(w_ukv_p),
        ],
        out_specs=[rows_spec(w) for w in widths],
        out_shape=[jax.ShapeDtypeStruct((batch, SEQ, w), BF16) for w in widths],
        scratch_shapes=[
            pltpu.VMEM((ROW_TILE, D_MODEL), BF16),
            pltpu.VMEM((ROW_TILE, LANES), F32),
            pltpu.VMEM((ROW_TILE, LANES), F32),
        ],
        compiler_params=pltpu.CompilerParams(
            dimension_semantics=("arbitrary", "arbitrary"), vmem_limit_bytes=_vmem_limit(est)),
        name="proj_b",
    )(x, pos_view, freq_b, g_emb, b_emb, w_b, g_cq, g_ckv, w_uq_p, w_ukv_p)


def _first_head_lanes(n_rows):
    return lax.broadcasted_iota(jnp.int32, (n_rows, LANES), 1) < A_HEAD_DIM


def _a_qk_lane_order():
    half = A_ROT_DIM // 2
    rest = A_HEAD_DIM - A_ROT_DIM
    order = ([(0, d) for d in range(half)] + [(1, d) for d in range(half)]
             + [(0, A_ROT_DIM + d) for d in range(rest)]
             + [(0, half + d) for d in range(half)] + [(1, half + d) for d in range(half)]
             + [(1, A_ROT_DIM + d) for d in range(rest)])
    assert len(order) == LANES and len(set(order)) == LANES
    return order


def _first_head_qk_lanes(n_rows):
    lane = lax.broadcasted_iota(jnp.int32, (n_rows, LANES), 1)
    half = A_ROT_DIM // 2
    return (lane < half) | ((lane >= A_ROT_DIM) & (lane < A_ROT_DIM + A_HEAD_DIM - half))


def _band_scores(q, k, bias):
    n_q = q.shape[0]
    first_head = _first_head_lanes(n_q)
    first_head_qk = _first_head_qk_lanes(n_q)
    zero = jnp.zeros_like(q)
    q2 = jnp.concatenate([jnp.where(first_head_qk, q, zero), jnp.where(first_head_qk, zero, q)], axis=0)
    s = lax.dot_general(q2, k, (((1,), (1,)), ((), ())), preferred_element_type=F32) + bias
    m = jnp.max(s, axis=-1, keepdims=True)
    p = jnp.exp2((s - m).astype(BF16))
    return p, jnp.where(first_head, m[:n_q], m[n_q:])


def _band_values(p, v_ones):
    n_q = p.shape[0] // 2
    first_head = _first_head_lanes(n_q)
    o = jnp.dot(p, v_ones, preferred_element_type=F32)
    acc = jnp.where(first_head, o[:n_q, :LANES], o[n_q:, :LANES])
    l_b = jnp.where(first_head, o[:n_q, LANES:], o[n_q:, LANES:])
    return acc, l_b


def _dilated_kernel(qn_ref, kn_ref, vn_ref, q4_ref, k4_ref, v4_ref, q16_ref, k16_ref, v16_ref,
                    o_ref, von_scr, vo4_scr, vo16_scr, bias_scr, bias16_scr, acc_scr, mx_scr, sm_scr, c16_scr):
    def window_bias(n_keys, first_key):
        row = lax.broadcasted_iota(jnp.int32, (2 * A_Q_TILE, n_keys), 0)
        col = lax.broadcasted_iota(jnp.int32, (2 * A_Q_TILE, n_keys), 1)
        q_idx = jnp.where(row >= A_Q_TILE, row - A_Q_TILE, row)
        off = col - (q_idx + first_key)
        return jnp.where(jnp.abs(off) <= A_SIDE, 0.0, NEG_INF).astype(F32)

    @pl.when((pl.program_id(0) == 0) & (pl.program_id(1) == 0))
    def _():
        for variant in range(3):
            bias_scr[variant] = window_bias(A_K_WIN, variant * A_SIDE)
        bias16_scr[...] = window_bias(A_Q_TILE, 0)
        von_scr[:, :, LANES:] = jnp.ones((A_PAIRS_PER_STEP, SEQ, LANES), BF16)
        vo4_scr[:, :, :, LANES:] = jnp.ones((A_PAIRS_PER_STEP, 4, SEQ // 4, LANES), BF16)
        vo16_scr[:, :, :, LANES:] = jnp.ones((A_PAIRS_PER_STEP, 16, SEQ // 16, LANES), BF16)

    von_scr[:, :, :LANES] = vn_ref[...]
    vo4_scr[:, :, :, :LANES] = v4_ref[...]
    vo16_scr[:, :, :, :LANES] = v16_ref[...]

    len4 = SEQ // 4
    tiles4 = len4 // A_Q_TILE
    n_tiles = SEQ // A_Q_TILE

    def window(tile, n_seq_tiles, seq_len):
        qs = tile * A_Q_TILE
        ks = min(max(qs - A_SIDE, 0), seq_len - A_K_WIN)
        variant = 0 if tile == 0 else (2 if tile == n_seq_tiles - 1 else 1)
        return qs, ks, variant

    def score_half(step):
        pp, i = step
        qs, ks, var = window(i, n_tiles, SEQ)
        res = i // tiles4
        qs4, ks4, var4 = window(i % tiles4, tiles4, len4)
        parts = (
            _band_scores(qn_ref[pp, pl.ds(qs, A_Q_TILE), :], kn_ref[pp, pl.ds(ks, A_K_WIN), :], bias_scr[var]),
            _band_scores(q4_ref[pp, res, pl.ds(qs4, A_Q_TILE), :], k4_ref[pp, res, pl.ds(ks4, A_K_WIN), :],
                         bias_scr[var4]),
            _band_scores(q16_ref[pp, i], k16_ref[pp, i], bias16_scr[...]),
        )
        class16 = pl.ds(i * C16_PITCH, A_Q_TILE)
        dests = ((acc_scr.at[pp, 0], mx_scr.at[pp, 0], sm_scr.at[pp, 0], pl.ds(qs, A_Q_TILE)),
                 (acc_scr.at[pp, 1], mx_scr.at[pp, 1], sm_scr.at[pp, 1], pl.ds(res + 4 * qs4, A_Q_TILE, stride=4)),
                 (c16_scr.at[pp, 0], c16_scr.at[pp, 1], c16_scr.at[pp, 2], class16))
        v_wins = (von_scr.at[pp, pl.ds(ks, A_K_WIN), :], vo4_scr.at[pp, res, pl.ds(ks4, A_K_WIN), :],
                  vo16_scr.at[pp, i])
        for (_, mx_at, _, dst), (_, m_b) in zip(dests, parts):
            mx_at[dst, :] = m_b
        return [(p, dest, v_win) for (p, _), dest, v_win in zip(parts, dests, v_wins)]

    def value_half(pending):
        for p, (acc_at, _, sm_at, dst), v_win in pending:
            acc, l_b = _band_values(p, v_win[...])
            acc_at[dst, :] = acc
            sm_at[dst, :] = l_b

    steps = [(pp, i) for pp in range(A_PAIRS_PER_STEP) for i in range(n_tiles)]
    pending = score_half(steps[0])
    for step in steps[1:]:
        ahead = score_half(step)
        value_half(pending)
        pending = ahead
    value_half(pending)

    def class16_rows(scr, c):
        groups = []
        for g in range(ROW_CHUNK // 8):
            first_class = 8 * (g % 2)
            t = (ROW_CHUNK // 16) * c + g // 2
            groups.append(scr[pl.ds(first_class * C16_PITCH + t, 8, stride=C16_PITCH), :])
        return jnp.concatenate(groups, axis=0)

    def merge_pair(pp):
        def merge(c, carry):
            rows = pl.ds(pl.multiple_of(c * ROW_CHUNK, ROW_CHUNK), ROW_CHUNK)
            parts = [(acc_scr[pp, pat, rows, :], mx_scr[pp, pat, rows, :], sm_scr[pp, pat, rows, :])
                     for pat in range(2)]
            parts.append(tuple(class16_rows(c16_scr.at[pp, k], c) for k in range(3)))
            m_all = jnp.maximum(jnp.maximum(parts[0][1], parts[1][1]), parts[2][1])
            num = jnp.zeros((ROW_CHUNK, LANES), F32)
            den = jnp.zeros((ROW_CHUNK, LANES), F32)
            for acc, m_b, l_b in parts:
                w = jnp.exp2(m_b - m_all)
                num = num + w * acc
                den = den + w * l_b
            o_ref[pp, rows, :] = (num / den).astype(BF16)
            return carry

        lax.fori_loop(0, SEQ // ROW_CHUNK, merge, 0, unroll=4)

    for pp in range(A_PAIRS_PER_STEP):
        merge_pair(pp)


def _dilated_attention(qkv_nat, qkv_4, qkv_16):
    batch = qkv_nat[0].shape[0]
    pps = A_PAIRS_PER_STEP
    nat_spec = pl.BlockSpec((None, pps, SEQ, LANES), lambda b, p: (b, p, 0, 0))
    r4_spec = pl.BlockSpec((None, pps, 4, SEQ // 4, LANES), lambda b, p: (b, p, 0, 0, 0))
    r16_spec = pl.BlockSpec((None, pps, 16, SEQ // 16, LANES), lambda b, p: (b, p, 0, 0, 0))
    blk = SEQ * LANES
    est = pps * (2 * 9 * blk * 2 + 2 * blk * 2 + 3 * blk * 2 * 2 + 9 * blk * 4) + 4 * 2 * A_Q_TILE * A_K_WIN * 4
    return pl.pallas_call(
        _dilated_kernel,
        grid=(batch, A_PAIRS // pps),
        in_specs=[nat_spec] * 3 + [r4_spec] * 3 + [r16_spec] * 3,
        out_specs=pl.BlockSpec((None, pps, SEQ, LANES), lambda b, p: (b, p, 0, 0)),
        out_shape=jax.ShapeDtypeStruct((batch, A_PAIRS, SEQ, LANES), BF16),
        scratch_shapes=[
            pltpu.VMEM((pps, SEQ, 2 * LANES), BF16),
            pltpu.VMEM((pps, 4, SEQ // 4, 2 * LANES), BF16),
            pltpu.VMEM((pps, 16, SEQ // 16, 2 * LANES), BF16),
            pltpu.VMEM((3, 2 * A_Q_TILE, A_K_WIN), F32),
            pltpu.VMEM((2 * A_Q_TILE, A_Q_TILE), F32),
            pltpu.VMEM((pps, 2, SEQ, LANES), F32),
            pltpu.VMEM((pps, 2, SEQ, LANES), F32),
            pltpu.VMEM((pps, 2, SEQ, LANES), F32),
            pltpu.VMEM((pps, 3, 16 * C16_PITCH, LANES), F32),
        ],
        compiler_params=pltpu.CompilerParams(
            dimension_semantics=("arbitrary", "arbitrary"), vmem_limit_bytes=_vmem_limit(est)),
        name="dilated_attn",
    )(*qkv_nat, *qkv_4, *qkv_16)


def _latent_kernel(q_ref, k_ref, v_ref, o_ref, vo_scr):
    @pl.when((pl.program_id(0) == 0) & (pl.program_id(1) == 0) & (pl.program_id(2) == 0))
    def _():
        vo_scr[:, :, LANES:] = jnp.ones((MLA_PAIRS_PER_STEP, SEQ, LANES), BF16)

    @pl.when(pl.program_id(2) == 0)
    def _():
        for pair in range(MLA_PAIRS_PER_STEP):
            vo_scr[pair, :, :LANES] = v_ref[:, pair * LANES:(pair + 1) * LANES]

    def score_half(rows, head):
        cols = slice(head * MLA_SLOT, (head + 1) * MLA_SLOT)
        s = lax.dot_general(q_ref[rows, cols], k_ref[:, cols], (((1,), (1,)), ((), ())),
                            preferred_element_type=F32)
        m = jnp.max(s, axis=-1, keepdims=True)
        return jnp.exp2((s - m).astype(BF16))

    def value_half(head, p):
        o = jnp.dot(p, vo_scr[head // 2], preferred_element_type=F32)
        return o[:, :LANES] / o[:, LANES:]

    n_heads = 2 * MLA_PAIRS_PER_STEP
    chains = [(slice(t * MLA_Q_TILE, (t + 1) * MLA_Q_TILE), head)
              for t in range(MLA_Q_TILES_PER_STEP) for head in range(n_heads)]
    lane = lax.broadcasted_iota(jnp.int32, (MLA_Q_TILE, LANES), 1)
    p_next = score_half(*chains[0])
    for n, (rows, head) in enumerate(chains):
        p = p_next
        if n + 1 < len(chains):
            p_next = score_half(*chains[n + 1])
        out = value_half(head, p)
        if head % 2 == 0:
            first_of_pair = out
        else:
            pair = head // 2
            o_ref[rows, pair * LANES:(pair + 1) * LANES] = (
                jnp.where(lane < MLA_V_DIM, first_of_pair, out).astype(BF16))


def _latent_attention(qb, kb, vb):
    batch = qb.shape[0]
    pairs = MLA_HEADS // 2 // MLA_PAIRS_PER_STEP
    qk_w = MLA_PAIRS_PER_STEP * 2 * MLA_SLOT
    v_w = MLA_PAIRS_PER_STEP * LANES
    step_rows = MLA_Q_TILES_PER_STEP * MLA_Q_TILE
    est = (2 * step_rows * qk_w * 2 + 2 * SEQ * qk_w * 2 + 2 * SEQ * v_w * 2 + SEQ * 2 * v_w * 2
           + 2 * step_rows * v_w * 2 + 4 * MLA_PAIRS_PER_STEP * MLA_Q_TILE * SEQ * 4)
    return pl.pallas_call(
        _latent_kernel,
        grid=(batch, pairs, SEQ // step_rows),
        in_specs=[
            pl.BlockSpec((None, step_rows, qk_w), lambda b, p, i: (b, i, p)),
            pl.BlockSpec((None, SEQ, qk_w), lambda b, p, i: (b, 0, p)),
            pl.BlockSpec((None, SEQ, v_w), lambda b, p, i: (b, 0, p)),
        ],
        out_specs=pl.BlockSpec((None, step_rows, v_w), lambda b, p, i: (b, i, p)),
        out_shape=jax.ShapeDtypeStruct((batch, SEQ, MLA_WIDTH), BF16),
        scratch_shapes=[pltpu.VMEM((MLA_PAIRS_PER_STEP, SEQ, 2 * LANES), BF16)],
        compiler_params=pltpu.CompilerParams(
            dimension_semantics=("arbitrary", "arbitrary", "arbitrary"),
            vmem_limit_bytes=_vmem_limit(est)),
        name="latent_attn",
    )(qb, kb, vb)


def _output_kernel(x_ref, ya_ref, yb_ref, gate_ref, mq_ref, mem_ref, wmem_ref, wout_ref,
                   goa_ref, gob_ref, gom_ref, gemb_ref, bemb_ref, gpost_ref, bpost_ref,
                   o_ref, y_scr, mkv_ref):
    @pl.when(pl.program_id(1) == 0)
    def _():
        mkv_ref[...] = jnp.dot(mem_ref[...].astype(BF16), wmem_ref[...],
                               preferred_element_type=F32).astype(BF16)

    for hd in range(MEM_HEADS):
        cols = slice(hd * MEM_HEAD_DIM, (hd + 1) * MEM_HEAD_DIM)
        s = lax.dot_general(mq_ref[:, cols], mkv_ref[:, cols], (((1,), (1,)), ((), ())),
                            preferred_element_type=F32)
        m = jnp.max(s, axis=-1, keepdims=True)
        p = jnp.exp2(s - m)
        l = jnp.sum(p, axis=-1, keepdims=True)
        o = jnp.dot(p.astype(BF16), mkv_ref[:, MEM_WIDTH + hd * MEM_HEAD_DIM:MEM_WIDTH + (hd + 1) * MEM_HEAD_DIM],
                    preferred_element_type=F32)
        y_scr[:, cols] = o / l

    off_b = A_WIDTH
    off_m = A_WIDTH + MLA_WIDTH
    ya_all = jnp.concatenate([ya_ref[cg] for cg in range(A_PAIRS)], axis=1)
    ya = _rms_rows(ya_all.astype(F32), goa_ref[...]) * gate_ref[:, :off_b].astype(F32)
    yb = _rms_rows(yb_ref[...].astype(F32), gob_ref[...]) * gate_ref[:, off_b:off_m].astype(F32)
    ym = _rms_rows(y_scr[...], gom_ref[...]) * gate_ref[:, off_m:].astype(F32)
    sub = (jnp.dot(ya.astype(BF16), wout_ref[:off_b, :], preferred_element_type=F32)
           + jnp.dot(yb.astype(BF16), wout_ref[off_b:off_m, :], preferred_element_type=F32)
           + jnp.dot(ym.astype(BF16), wout_ref[off_m:, :], preferred_element_type=F32))

    for c in range(N_CHUNKS):
        h = _layer_norm_rows(x_ref[_chunk(c), :], gemb_ref[...], bemb_ref[...])
        z = DEEPNORM_ALPHA * h + sub[c * ROW_CHUNK:(c + 1) * ROW_CHUNK, :]
        o_ref[_chunk(c), :] = _layer_norm_rows(z, gpost_ref[...], bpost_ref[...])


def _output_stage(x, ya, yb, gates, mq, mem, w_mem, w_out, g_out_a, g_out_b, g_out_m, g_emb, b_emb,
                  g_post, b_post):
    batch = x.shape[0]

    def rows_spec(width):
        return pl.BlockSpec((None, ROW_TILE, width), lambda b, i: (b, i, 0))

    def full_spec(arr):
        return pl.BlockSpec(arr.shape, lambda b, i: (0,) * arr.ndim)

    def resident_spec(arr):
        return pl.BlockSpec(arr.shape, lambda b, i: (0,) * arr.ndim, pipeline_mode=pl.Buffered(1))

    est = (4 * ROW_TILE * D_MODEL * 4 + 2 * ROW_TILE * (A_WIDTH + MLA_WIDTH + D_MIX + MEM_WIDTH) * 2
           + 2 * N_MEM * D_MODEL * 4 + N_MEM * 2 * MEM_WIDTH * 2 + (w_out.size + w_mem.size) * 2
           + ROW_TILE * MEM_WIDTH * 4 + 4 * ROW_TILE * D_MODEL * 4)
    return pl.pallas_call(
        _output_kernel,
        grid=(batch, N_ROW_TILES),
        in_specs=[
            rows_spec(D_MODEL),
            pl.BlockSpec((None, A_PAIRS, ROW_TILE, LANES), lambda b, i: (b, 0, i, 0)),
            rows_spec(MLA_WIDTH), rows_spec(D_MIX), rows_spec(MEM_WIDTH),
            pl.BlockSpec((None, N_MEM, D_MODEL), lambda b, i: (b, 0, 0)),
            resident_spec(w_mem), resident_spec(w_out),
            full_spec(g_out_a), full_spec(g_out_b), full_spec(g_out_m),
            full_spec(g_emb), full_spec(b_emb), full_spec(g_post), full_spec(b_post),
        ],
        out_specs=rows_spec(D_MODEL),
        out_shape=jax.ShapeDtypeStruct(x.shape, F32),
        scratch_shapes=[pltpu.VMEM((ROW_TILE, MEM_WIDTH), F32),
                        pltpu.VMEM((N_MEM, 2 * MEM_WIDTH), BF16)],
        compiler_params=pltpu.CompilerParams(
            dimension_semantics=("arbitrary", "arbitrary"), vmem_limit_bytes=_vmem_limit(est)),
        name="output_stage",
    )(x, ya, yb, gates, mq, mem, w_mem, w_out, g_out_a, g_out_b, g_out_m, g_emb, b_emb, g_post, b_post)


def _rope_freq_rows(rot_dim):
    inv_freq = (np.float32(ROPE_THETA) ** (-(np.arange(0, rot_dim, 2, dtype=np.float32) / np.float32(rot_dim)))
                ).astype(np.float32)
    return jnp.asarray(np.repeat(inv_freq[:, None], LANES, axis=1))


def _a_qk_columns():
    return np.asarray([(2 * pair + hh) * A_HEAD_DIM + d
                       for pair in range(A_PAIRS) for hh, d in _a_qk_lane_order()])


def _mla_slot_lanes():
    half = MLA_ROPE_DIM // 2
    assert ROPE_FIRST_LANES == (0, half)
    lanes = [None] * MLA_SLOT
    for i in range(half):
        lanes[i] = ("rope", i)
        lanes[ROPE_PARTNER_SHIFT + i] = ("rope", half + i)
    free = [lane for lane in range(MLA_SLOT) if lanes[lane] is None]
    for d in range(MLA_NOPE_DIM):
        lanes[free[d]] = ("nope", d)
    return lanes


def _slot_gather(cols, index_of):
    n = cols.shape[-1]
    idx = np.asarray([n if index_of(lane) is None else index_of(lane) for lane in _mla_slot_lanes()])
    return jnp.pad(cols, [(0, 0)] * (cols.ndim - 1) + [(0, 1)])[..., idx]


def kernel(x, mem, positions, g_emb, b_emb, w_in, g_cq, g_ckv, w_uq, w_ukv, w_mem_kv, g_out_a, g_out_b,
           g_out_m, w_out, g_post, b_post):
    batch = x.shape[0]
    assert x.shape == (batch, SEQ, D_MODEL) and w_in.shape[0] == DEPTH == 1
    row = lambda v: v.reshape(1, -1).astype(F32)

    splits = [int(i) for i in np.cumsum(IN_SPLITS)[:-1]]
    a_q, a_k, a_v, a_g, c_q, c_kv, b_kr, b_g, m_q, m_g = jnp.split(w_in[0].astype(BF16), splits, axis=1)
    qk_cols = _a_qk_columns()
    w_qkv = jnp.concatenate([a_q[:, qk_cols], a_k[:, qk_cols], a_v], axis=1).astype(BF16)
    rope_only = lambda lane: lane[1] if lane is not None and lane[0] == "rope" else None
    nope_only = lambda lane: lane[1] if lane is not None and lane[0] == "nope" else None
    nope_then_rope = lambda lane: None if lane is None else (lane[1] + (MLA_NOPE_DIM if lane[0] == "rope" else 0))
    kr_slot = _slot_gather(b_kr, rope_only)
    w_b = jnp.concatenate([a_g, b_g, m_g, m_q, c_q, c_kv, kr_slot], axis=1).astype(BF16)
    qk_dim = MLA_NOPE_DIM + MLA_ROPE_DIM
    w_uq_p = _slot_gather(w_uq[0].reshape(MLA_Q_RANK, MLA_HEADS, qk_dim), nope_then_rope
                          ).reshape(MLA_Q_RANK, _MLA_QK_W).astype(BF16)
    ukv = w_ukv[0].reshape(MLA_KV_RANK, MLA_HEADS, MLA_NOPE_DIM + MLA_V_DIM)
    w_uk_p = _slot_gather(ukv[:, :, :MLA_NOPE_DIM], nope_only)
    w_ukv_p = jnp.concatenate([w_uk_p.reshape(MLA_KV_RANK, _MLA_QK_W),
                               ukv[:, :, MLA_NOPE_DIM:].reshape(MLA_KV_RANK, MLA_WIDTH)], axis=1).astype(BF16)

    pos_view = positions.astype(F32).reshape(batch, N_ROW_TILES, N_CHUNKS, ROW_CHUNK)
    freq_a = _rope_freq_rows(A_ROT_DIM)
    freq_b = _rope_freq_rows(MLA_ROPE_DIM)
    g_emb_r, b_emb_r = row(g_emb), row(b_emb)

    outs_a = _proj_a(x, pos_view, freq_a, g_emb_r, b_emb_r, w_qkv)
    y_a = _dilated_attention(outs_a[0:3], outs_a[3:6], outs_a[6:9])

    gates, mq, qb, kb, vb = _proj_b(x, pos_view, freq_b, g_emb_r, b_emb_r, w_b, row(g_cq[0]),
                                    row(g_ckv[0]), w_uq_p, w_ukv_p)
    y_b = _latent_attention(qb, kb, vb)

    return _output_stage(x, y_a, y_b, gates, mq, mem, w_mem_kv[0].astype(BF16), w_out[0].astype(BF16), row(g_out_a[0]),
                         row(g_out_b[0]), row(g_out_m[0]), g_emb_r, b_emb_r, row(g_post[0]), row(b_post[0]))
```
